```python
import math
import jax, jax.numpy as jnp
from jax import lax
import numpy as np

D_MODEL = 1024
BATCH = 8
SEQ = 2048
DEPTH = 1
DEC_BATCH = 128
DEC_SEQ = 4
PAST_LEN = 16384
PAGE_SIZE = 128

ML_H = 4
ML_DK = 128
ML_DV = 128
ML_W = ML_H * ML_DV
MLSTM_CHUNK = 64
GD_H = 4
GD_DK = 128
GD_DV = 128
GD_W = GD_H * GD_DV
GD_CONV_CH = 2 * GD_H * GD_DK + GD_W
CONV_W = 4
GDN_CHUNK = 64
XA_H = 4
XA_DH = 128
XA_W = XA_H * XA_DH
MEM_LEN = 256
N_GROUPS = 4
EXPERTS_PER_GROUP = 4
N_EXPERTS = N_GROUPS * EXPERTS_PER_GROUP
TOP_K_IN_GROUP = 2
D_EXPERT = 512
EPS = 1e-6

IN_SIZES = (ML_H * ML_DK, ML_H * ML_DK, ML_W, ML_H, ML_H, ML_W,
            GD_CONV_CH, GD_H, GD_H, GD_W,
            XA_W,
            D_MODEL, D_MODEL, D_MODEL)
N_IN = sum(IN_SIZES)

kernel_name = 'hybrid_mlstm_gdn_memxattn_hmoe_step'


def rmsnorm(x, g):
    xf = x.astype(jnp.float32)
    y = xf * lax.rsqrt(jnp.mean(xf * xf, axis=-1, keepdims=True) + EPS)
    return (y * g.astype(jnp.float32)).astype(x.dtype)


def head_rmsnorm(h, g):
    return h * lax.rsqrt(jnp.mean(h * h, axis=-1, keepdims=True) + EPS) * g


def l2norm(x):
    return x * lax.rsqrt(jnp.sum(x * x, axis=-1, keepdims=True) + EPS)


def mlstm_chunked(q, k, v, i_pre, log_f, C0, n0, m0):
    B, S, H, DK = q.shape
    L = math.gcd(S, MLSTM_CHUNK)
    nc = S // L

    def to_chunks(t):
        return jnp.moveaxis(t.reshape((B, nc, L) + t.shape[2:]), 1, 0)

    causal = jnp.tril(jnp.ones((L, L), dtype=bool))

    def step(carry, inp):
        C, n, m = carry
        qc, kc, vc, ic, fc = inp
        bt = jnp.swapaxes(jnp.cumsum(fc, axis=1), 1, 2)
        it = jnp.swapaxes(ic, 1, 2)
        log_d = bt[..., :, None] - bt[..., None, :] + it[..., None, :]
        log_d = jnp.where(causal, log_d, -jnp.inf)
        inter = bt + m[..., None]
        m_t = jnp.maximum(inter, jnp.max(log_d, axis=-1))
        d = jnp.exp(log_d - m_t[..., None])
        inter_w = jnp.exp(inter - m_t)
        s = jnp.einsum('blhd,bshd->bhls', qc, kc) * d
        num = (jnp.einsum('bhls,bshv->blhv', s, vc)
               + jnp.einsum('blhd,bhdv->blhv', qc, C) * jnp.swapaxes(inter_w, 1, 2)[..., None])
        den = jnp.sum(s, axis=-1) + inter_w * jnp.einsum('blhd,bhd->bhl', qc, n)
        denom = jnp.maximum(jnp.abs(den), jnp.exp(-m_t))
        h = num / jnp.swapaxes(denom, 1, 2)[..., None]
        m_new = m_t[..., -1]
        w = jnp.exp(bt[..., -1:] - bt + it - m_new[..., None])
        decay = jnp.exp(bt[..., -1] + m - m_new)
        kw = kc * jnp.swapaxes(w, 1, 2)[..., None]
        C_new = decay[..., None, None] * C + jnp.einsum('bshd,bshv->bhdv', kw, vc)
        n_new = decay[..., None] * n + jnp.sum(kw, axis=1)
        return (C_new, n_new, m_new), h

    xs = (to_chunks(q), to_chunks(k), to_chunks(v), to_chunks(i_pre), to_chunks(log_f))
    (C1, n1, m1), hs = lax.scan(step, (C0, n0, m0), xs)
    h = jnp.moveaxis(hs, 0, 1).reshape(B, S, H, v.shape[-1])
    return h, C1, n1, m1


def gated_delta_chunked(q, k, v, g, beta, S0):
    B, S, H, DK = q.shape
    DV = v.shape[-1]
    L = math.gcd(S, GDN_CHUNK)
    nc = S // L

    def heads_chunks(t):
        t = t.reshape((B, nc, L) + t.shape[2:])
        return jnp.moveaxis(jnp.moveaxis(t, 3, 2), 1, 0)

    qc, kc, vc, gc, bc = map(heads_chunks, (q, k, v, g, beta))
    gam = jnp.cumsum(gc, axis=-1)
    incl = jnp.tril(jnp.ones((L, L), dtype=bool))
    strict = jnp.tril(jnp.ones((L, L), dtype=bool), -1)
    diff = gam[..., :, None] - gam[..., None, :]
    decay_mat = jnp.where(incl, jnp.exp(jnp.where(incl, diff, 0.0)), 0.0)
    kk = jnp.einsum('...ld,...sd->...ls', kc, kc)
    a_mat = jnp.where(strict, bc[..., :, None] * kk * decay_mat, 0.0)
    rhs = jnp.concatenate([bc[..., None] * vc, bc[..., None] * kc * jnp.exp(gam)[..., None]], axis=-1)
    sol = lax.linalg.triangular_solve(a_mat + jnp.eye(L, dtype=a_mat.dtype), rhs,
                                      left_side=True, lower=True, unit_diagonal=True)
    u, w = sol[..., :DV], sol[..., DV:]
    qk = jnp.einsum('...ld,...sd->...ls', qc, kc) * decay_mat
    q_dec = qc * jnp.exp(gam)[..., None]
    k_dec = kc * jnp.exp(gam[..., -1:] - gam)[..., None]
    g_last = jnp.exp(gam[..., -1])

    def step(Sm, inp):
        u_c, w_c, qk_c, qd_c, kd_c, gl_c = inp
        v_new = u_c - jnp.einsum('bhld,bhdv->bhlv', w_c, Sm)
        o = jnp.einsum('bhld,bhdv->bhlv', qd_c, Sm) + jnp.einsum('bhls,bhsv->bhlv', qk_c, v_new)
        Sm = gl_c[..., None, None] * Sm + jnp.einsum('bhld,bhlv->bhdv', kd_c, v_new)
        return Sm, o

    S1, o = lax.scan(step, S0, (u, w, qk, q_dec, k_dec, g_last))
    o = jnp.moveaxis(jnp.moveaxis(o, 0, 1), 2, 3).reshape(B, S, H, DV)
    return o, S1


def memory_kv(mem, mem_norm, w_mem_kv):
    B, M, _ = mem.shape
    kv = rmsnorm(mem, mem_norm) @ w_mem_kv
    k, v = jnp.split(kv, 2, axis=-1)
    return k.reshape(B, M, XA_H, XA_DH), v.reshape(B, M, XA_H, XA_DH)


def memory_attention(q, mk, mv):
    s = jnp.einsum('bshd,bmhd->bhsm', q.astype(jnp.float32), mk.astype(jnp.float32)) * (XA_DH ** -0.5)
    p = jax.nn.softmax(s, axis=-1)
    return jnp.einsum('bhsm,bmhd->bshd', p, mv.astype(jnp.float32))


def hier_moe(h, w_router_group, b_router_group, w_router_expert, b_router_expert,
             w_exp_gate, w_exp_up, w_exp_down):
    f32 = jnp.float32
    B, S, D = h.shape
    t = h.reshape(-1, D)
    lg = (t @ w_router_group).astype(f32) + b_router_group.astype(f32)
    pg = jax.nn.softmax(lg, axis=-1)
    _, gi = lax.top_k(lg, 1)
    grp_1h = jax.nn.one_hot(gi[:, 0], N_GROUPS, dtype=f32)
    p_grp = jnp.sum(pg * grp_1h, axis=-1)
    le = ((t @ w_router_expert).astype(f32) + b_router_expert.astype(f32)).reshape(-1, N_GROUPS, EXPERTS_PER_GROUP)
    le_sel = jnp.sum(le * grp_1h[:, :, None], axis=1)
    top_v, top_i = lax.top_k(le_sel, TOP_K_IN_GROUP)
    w_top = jax.nn.softmax(top_v, axis=-1)
    within = jnp.sum(w_top[..., None] * jax.nn.one_hot(top_i, EXPERTS_PER_GROUP, dtype=f32), axis=1)
    comb = (p_grp[:, None, None] * grp_1h[:, :, None] * within[:, None, :]).astype(t.dtype)
    out = jnp.zeros_like(t)
    for gidx in range(N_GROUPS):
        e0 = gidx * EXPERTS_PER_GROUP
        e1 = e0 + EXPERTS_PER_GROUP
        a = jnp.einsum('td,edf->tef', t, w_exp_gate[e0:e1])
        u = jnp.einsum('td,edf->tef', t, w_exp_up[e0:e1])
        act = jax.nn.silu(a) * u * comb[:, gidx, :, None]
        out = out + jnp.einsum('tef,efd->td', act, w_exp_down[e0:e1])
    return out.reshape(B, S, D)


def decoder_layer(x, mem_k, mem_v, C0, n0, m0, S0, conv0,
                  norm_mix, w_in, b_igate, b_fgate, mlstm_norm, conv_w, a_log, dt_bias, gdn_norm,
                  w_br_mlstm, w_br_gdn, w_br_xattn, w_out, norm_ffn,
                  w_router_group, b_router_group, w_router_expert, b_router_expert,
                  w_exp_gate, w_exp_up, w_exp_down):
    f32 = jnp.float32
    B, S, _ = x.shape
    h = rmsnorm(x, norm_mix)
    points = np.cumsum(IN_SIZES)[:-1].tolist()
    (m_q, m_k, m_v, m_i, m_f, m_o, g_qkv, g_a, g_b, g_out, x_q,
     gate_a, gate_b, gate_c) = jnp.split(h @ w_in, points, axis=-1)

    q = m_q.reshape(B, S, ML_H, ML_DK).astype(f32)
    k = m_k.reshape(B, S, ML_H, ML_DK).astype(f32) * (ML_DK ** -0.5)
    v = m_v.reshape(B, S, ML_H, ML_DV).astype(f32)
    i_pre = m_i.astype(f32) + b_igate.astype(f32)
    log_f = jax.nn.log_sigmoid(m_f.astype(f32) + b_fgate.astype(f32))
    h_a, C1, n1, m1 = mlstm_chunked(q, k, v, i_pre, log_f, C0.astype(f32), n0.astype(f32), m0.astype(f32))
    h_a = head_rmsnorm(h_a, mlstm_norm.reshape(ML_H, ML_DV).astype(f32)) * jax.nn.sigmoid(m_o.reshape(B, S, ML_H, ML_DV).astype(f32))
    h_a = h_a.reshape(B, S, ML_W).astype(x.dtype)

    conv_in = jnp.concatenate([conv0.astype(x.dtype), g_qkv], axis=1)
    acc = conv_in[:, 0:S] * conv_w[0]
    for j in range(1, CONV_W):
        acc = acc + conv_in[:, j:j + S] * conv_w[j]
    conv_out = jax.nn.silu(acc.astype(f32))
    conv1 = conv_in[:, S:]
    gq, gk, gv = jnp.split(conv_out, [GD_H * GD_DK, 2 * GD_H * GD_DK], axis=-1)
    gq = l2norm(gq.reshape(B, S, GD_H, GD_DK)) * (GD_DK ** -0.5)
    gk = l2norm(gk.reshape(B, S, GD_H, GD_DK))
    gv = gv.reshape(B, S, GD_H, GD_DV)
    g = -jnp.exp(a_log.astype(f32)) * jax.nn.softplus(g_a.astype(f32) + dt_bias.astype(f32))
    beta = jax.nn.sigmoid(g_b.astype(f32))
    h_b, S1 = gated_delta_chunked(gq, gk, gv, g, beta, S0.astype(f32))
    h_b = head_rmsnorm(h_b, gdn_norm.astype(f32)) * jax.nn.silu(g_out.reshape(B, S, GD_H, GD_DV).astype(f32))
    h_b = h_b.reshape(B, S, GD_W).astype(x.dtype)

    h_c = memory_attention(x_q.reshape(B, S, XA_H, XA_DH), mem_k, mem_v).reshape(B, S, XA_W).astype(x.dtype)

    mixed = (jax.nn.sigmoid(gate_a) * (h_a @ w_br_mlstm)
             + jax.nn.sigmoid(gate_b) * (h_b @ w_br_gdn)
             + jax.nn.sigmoid(gate_c) * (h_c @ w_br_xattn))
    x = x + mixed @ w_out

    x = x + hier_moe(rmsnorm(x, norm_ffn), w_router_group, b_router_group, w_router_expert,
                     b_router_expert, w_exp_gate, w_exp_up, w_exp_down)
    dt = x.dtype
    return x, C1.astype(dt), n1.astype(dt), m1.astype(dt), S1.astype(dt), conv1.astype(dt)


def setup_inputs(seed: int = 0) -> dict:
    key = jax.random.key(seed)
    ks = iter(jax.random.split(key, 48))
    f32 = jnp.float32
    D = D_MODEL

    def nrm(shape, scale=1.0):
        return jax.random.normal(next(ks), shape, f32) * scale

    def gain(shape):
        return 1.0 + nrm(shape, 0.02)

    x_prompt = nrm((BATCH, SEQ, D))
    x_sample = nrm((DEC_BATCH, DEC_SEQ, D))
    mem_prompt = nrm((BATCH, MEM_LEN, D))
    cache_mem_k = nrm((DEPTH, DEC_BATCH, MEM_LEN, XA_H, XA_DH))
    cache_mem_v = nrm((DEPTH, DEC_BATCH, MEM_LEN, XA_H, XA_DH))
    state_mlstm_C = nrm((DEPTH, DEC_BATCH, ML_H, ML_DK, ML_DV), 0.5)
    state_mlstm_n = nrm((DEPTH, DEC_BATCH, ML_H, ML_DK))
    state_mlstm_m = nrm((DEPTH, DEC_BATCH, ML_H))
    state_gdn_S = nrm((DEPTH, DEC_BATCH, GD_H, GD_DK, GD_DV), 0.1)
    state_gdn_conv = nrm((DEPTH, DEC_BATCH, CONV_W - 1, GD_CONV_CH))

    norm_mix = gain((DEPTH, D))
    w_in = nrm((DEPTH, D, N_IN), D ** -0.5)
    b_igate = nrm((DEPTH, ML_H), 0.5)
    b_fgate = 3.0 + jnp.linspace(0.0, 3.0, ML_H, dtype=f32)[None, :] + nrm((DEPTH, ML_H), 0.1)
    mlstm_norm = gain((DEPTH, ML_W))
    conv_w = nrm((DEPTH, CONV_W, GD_CONV_CH), CONV_W ** -0.5)
    a_log = jnp.log(jax.random.uniform(next(ks), (DEPTH, GD_H), f32, 1.0, 16.0))
    dt0 = jnp.exp(jax.random.uniform(next(ks), (DEPTH, GD_H), f32, math.log(1e-3), math.log(1e-1)))
    dt_bias = jnp.log(jnp.expm1(dt0))
    gdn_norm = gain((DEPTH, GD_DV))
    mem_norm = gain((DEPTH, D))
    w_mem_kv = nrm((DEPTH, D, 2 * XA_W), D ** -0.5)
    w_br_mlstm = nrm((DEPTH, ML_W, D), ML_W ** -0.5)
    w_br_gdn = nrm((DEPTH, GD_W, D), GD_W ** -0.5)
    w_br_xattn = nrm((DEPTH, XA_W, D), XA_W ** -0.5)
    w_out = nrm((DEPTH, D, D), D ** -0.5)
    norm_ffn = gain((DEPTH, D))
    w_router_group = nrm((DEPTH, D, N_GROUPS), D ** -0.5)
    b_router_group = nrm((DEPTH, N_GROUPS), 0.01)
    w_router_expert = nrm((DEPTH, D, N_EXPERTS), D ** -0.5)
    b_router_expert = nrm((DEPTH, N_EXPERTS), 0.01)
    w_exp_gate = nrm((DEPTH, N_EXPERTS, D, D_EXPERT), D ** -0.5)
    w_exp_up = nrm((DEPTH, N_EXPERTS, D, D_EXPERT), D ** -0.5)
    w_exp_down = nrm((DEPTH, N_EXPERTS, D_EXPERT, D), D_EXPERT ** -0.5)
    final_norm = gain((D,))
    return {
        'x_prompt': x_prompt, 'x_sample': x_sample, 'mem_prompt': mem_prompt,
        'cache_mem_k': cache_mem_k, 'cache_mem_v': cache_mem_v,
        'state_mlstm_C': state_mlstm_C, 'state_mlstm_n': state_mlstm_n, 'state_mlstm_m': state_mlstm_m,
        'state_gdn_S': state_gdn_S, 'state_gdn_conv': state_gdn_conv,
        'norm_mix': norm_mix, 'w_in': w_in, 'b_igate': b_igate, 'b_fgate': b_fgate,
        'mlstm_norm': mlstm_norm, 'conv_w': conv_w, 'a_log': a_log, 'dt_bias': dt_bias,
        'gdn_norm': gdn_norm, 'mem_norm': mem_norm, 'w_mem_kv': w_mem_kv,
        'w_br_mlstm': w_br_mlstm, 'w_br_gdn': w_br_gdn, 'w_br_xattn': w_br_xattn,
        'w_out': w_out, 'norm_ffn': norm_ffn,
        'w_router_group': w_router_group, 'b_router_group': b_router_group,
        'w_router_expert': w_router_expert, 'b_router_expert': b_router_expert,
        'w_exp_gate': w_exp_gate, 'w_exp_up': w_exp_up, 'w_exp_down': w_exp_down,
        'final_norm': final_norm,
    }


def reference(x_prompt, x_sample, mem_prompt, cache_mem_k, cache_mem_v,
              state_mlstm_C, state_mlstm_n, state_mlstm_m, state_gdn_S, state_gdn_conv,
              norm_mix, w_in, b_igate, b_fgate, mlstm_norm, conv_w, a_log, dt_bias, gdn_norm,
              mem_norm, w_mem_kv, w_br_mlstm, w_br_gdn, w_br_xattn, w_out, norm_ffn,
              w_router_group, b_router_group, w_router_expert, b_router_expert,
              w_exp_gate, w_exp_up, w_exp_down, final_norm):
    f32 = jnp.float32
    dt = x_prompt.dtype
    bp = x_prompt.shape[0]
    xp, xs = x_prompt, x_sample
    mk_l, mv_l = [], []
    cp_l, np_l, mp_l, sp_l, vp_l = [], [], [], [], []
    cs_l, ns_l, ms_l, ss_l, vs_l = [], [], [], [], []
    for l in range(DEPTH):
        lw = (norm_mix[l], w_in[l], b_igate[l], b_fgate[l], mlstm_norm[l], conv_w[l], a_log[l],
              dt_bias[l], gdn_norm[l], w_br_mlstm[l], w_br_gdn[l], w_br_xattn[l], w_out[l],
              norm_ffn[l], w_router_group[l], b_router_group[l], w_router_expert[l],
              b_router_expert[l], w_exp_gate[l], w_exp_up[l], w_exp_down[l])
        mk_p, mv_p = memory_kv(mem_prompt, mem_norm[l], w_mem_kv[l])
        xp, c_p, n_p, m_p, s_p, v_p = decoder_layer(
            xp, mk_p, mv_p,
            jnp.zeros((bp, ML_H, ML_DK, ML_DV), f32), jnp.zeros((bp, ML_H, ML_DK), f32),
            jnp.zeros((bp, ML_H), f32), jnp.zeros((bp, GD_H, GD_DK, GD_DV), f32),
            jnp.zeros((bp, CONV_W - 1, GD_CONV_CH), dt), *lw)
        xs, c_s, n_s, m_s, s_s, v_s = decoder_layer(
            xs, cache_mem_k[l], cache_mem_v[l], state_mlstm_C[l], state_mlstm_n[l],
            state_mlstm_m[l], state_gdn_S[l], state_gdn_conv[l], *lw)
        mk_l.append(mk_p.astype(dt)); mv_l.append(mv_p.astype(dt))
        cp_l.append(c_p); np_l.append(n_p); mp_l.append(m_p); sp_l.append(s_p); vp_l.append(v_p)
        cs_l.append(c_s); ns_l.append(n_s); ms_l.append(m_s); ss_l.append(s_s); vs_l.append(v_s)
    y_prompt = rmsnorm(xp, final_norm)
    y_sample = rmsnorm(xs, final_norm)
    return (y_prompt, y_sample,
            jnp.stack(mk_l), jnp.stack(mv_l),
            jnp.stack(cp_l), jnp.stack(np_l), jnp.stack(mp_l), jnp.stack(sp_l), jnp.stack(vp_l),
            jnp.stack(cs_l), jnp.stack(ns_l), jnp.stack(ms_l), jnp.stack(ss_l), jnp.stack(vs_l))
```

```python
import functools

import jax
import jax.numpy as jnp
from jax import lax
from jax.experimental import pallas as pl
from jax.experimental.pallas import tpu as pltpu

F32 = jnp.float32
BF16 = jnp.bfloat16
EPS = 1e-6

N_HEADS = 4
HEAD_DIM = 128
BRANCH_W = N_HEADS * HEAD_DIM
CONV_W = 4
N_GROUPS = 4
EXPERTS_PER_GROUP = 4
N_EXPERTS = N_GROUPS * EXPERTS_PER_GROUP
N_PAIRS = 6
N_CLASSES = N_GROUPS * N_PAIRS
PAIR_LO = (0, 0, 0, 1, 1, 2)
PAIR_HI = (1, 2, 3, 2, 3, 3)
NEG_BIG = -1e30
LANES = 128
SUBLANES = 8
ROUTE_W = LANES
VMEM_LIMIT = 48 * 1024 * 1024

COL_GATE = 0
COL_MQ, COL_MK, COL_MV, COL_MO = 6, 7, 8, 9
COL_GQ, COL_GK, COL_GV, COL_GO, COL_XQ = 10, 11, 12, 13, 14
N_PROJ = 15 * BRANCH_W


def _dot(a, b):
    return jnp.dot(a.astype(BF16), b.astype(BF16), preferred_element_type=F32)


def _dot_nt(a, b):
    return lax.dot_general(a.astype(BF16), b.astype(BF16), (((1,), (1,)), ((), ())),
                           preferred_element_type=F32)


def _dot_tn(a, b):
    return lax.dot_general(a.astype(BF16), b.astype(BF16), (((0,), (0,)), ((), ())),
                           preferred_element_type=F32)


def _dot_exact(a, b):
    return jnp.dot(a, b, preferred_element_type=F32, precision=lax.Precision.HIGHEST)


def _lanes_to_rows(x, lane0):
    r = lax.broadcasted_iota(jnp.int32, (SUBLANES, LANES), 0)
    c = lax.broadcasted_iota(jnp.int32, (SUBLANES, LANES), 1)
    sel = (c == r + lane0).astype(F32)
    return lax.dot_general(sel, x, (((1,), (1,)), ((), ())), preferred_element_type=F32,
                           precision=lax.Precision.HIGHEST)


def _rms(x, g):
    return x * lax.rsqrt(jnp.mean(x * x, axis=-1, keepdims=True) + EPS) * g


def _softplus(x):
    return jnp.maximum(x, 0.0) + jnp.log1p(jnp.exp(-jnp.abs(x)))


def _sigmoid(x):
    return 1.0 / (1.0 + jnp.exp(-x))


def _silu(x):
    return x * _sigmoid(x)


def _tri(n, strict=False, upper=False):
    r = lax.broadcasted_iota(jnp.int32, (n, n), 0)
    c = lax.broadcasted_iota(jnp.int32, (n, n), 1)
    if upper:
        r, c = c, r
    return (c < r) if strict else (c <= r)


def _params(sem):
    return pltpu.CompilerParams(dimension_semantics=sem, vmem_limit_bytes=VMEM_LIMIT)


def _two_source_specs(n_first, block, width):
    first = pl.BlockSpec((block, width), lambda i, *_: (jnp.minimum(i, n_first - 1), 0))
    second = pl.BlockSpec((block, width), lambda i, *_: (jnp.maximum(i - n_first, 0), 0))
    return first, second


def _norm_proj_kernel(xa_ref, xb_ref, g_ref, w_ref, ws_ref, o_ref, os_ref, hb_ref, *, n_first):
    i = pl.program_id(0)
    j = pl.program_id(1)

    @pl.when(j == 0)
    def _():
        @pl.when(i < n_first)
        def _():
            hb_ref[...] = _rms(xa_ref[...], g_ref[...]).astype(BF16)

        @pl.when(i >= n_first)
        def _():
            hb_ref[...] = _rms(xb_ref[...], g_ref[...]).astype(BF16)

        os_ref[...] = jnp.dot(hb_ref[...], ws_ref[...], preferred_element_type=F32)

    o_ref[...] = jnp.dot(hb_ref[...], w_ref[...], preferred_element_type=F32)


def _norm_proj(xa, xb, g, w, ws, *, tm, tn):
    d = xa.shape[1]
    n_first = xa.shape[0] // tm
    n_rows = xa.shape[0] + (0 if xb is None else xb.shape[0])
    if xb is None:
        xb = xa
    n = w.shape[1]
    sa, sb = _two_source_specs(n_first, tm, d)
    return pl.pallas_call(
        functools.partial(_norm_proj_kernel, n_first=n_first),
        out_shape=(jax.ShapeDtypeStruct((n_rows, n), F32),
                   jax.ShapeDtypeStruct((n_rows, ws.shape[1]), F32)),
        grid=(n_rows // tm, n // tn),
        in_specs=[sa, sb,
                  pl.BlockSpec((1, d), lambda i, j: (0, 0)),
                  pl.BlockSpec((d, tn), lambda i, j: (0, j)),
                  pl.BlockSpec((d, ws.shape[1]), lambda i, j: (0, 0))],
        out_specs=(pl.BlockSpec((tm, tn), lambda i, j: (i, j)),
                   pl.BlockSpec((tm, ws.shape[1]), lambda i, j: (i, 0))),
        scratch_shapes=[pltpu.VMEM((tm, d), BF16)],
        compiler_params=_params(("parallel", "arbitrary")),
        name="norm_proj",
    )(xa, xb, g, w, ws)


def _pad_rows(shape, t, n_valid):
    row = lax.broadcasted_iota(jnp.int32, shape, 0) + t * shape[0]
    return row >= n_valid


def _mlstm_kernel(*refs, L, s_valid, s_total, has_init):
    if has_init:
        (q_ref, k_ref, v_ref, og_ref, gs_ref, bias_ref, norm_ref, c0_ref, n0_ref, m0_ref,
         h_ref, c_ref, n_ref, m_ref) = refs
    else:
        (q_ref, k_ref, v_ref, og_ref, gs_ref, bias_ref, norm_ref,
         h_ref, c_ref, n_ref, m_ref) = refs
    t = pl.program_id(1)

    @pl.when(t == 0)
    def _():
        if has_init:
            c_ref[...] = c0_ref[...]
            n_ref[...] = n0_ref[...]
            m_ref[...] = m0_ref[...]
        else:
            c_ref[...] = jnp.zeros_like(c_ref)
            n_ref[...] = jnp.zeros_like(n_ref)
            m_ref[...] = jnp.zeros_like(m_ref)

    z = gs_ref[...] + bias_ref[...]
    lane = lax.broadcasted_iota(jnp.int32, z.shape, 1)
    log_f = jnp.minimum(z, 0.0) - jnp.log1p(jnp.exp(-jnp.abs(z)))
    g = jnp.where((lane >= N_HEADS) & (lane < 2 * N_HEADS), log_f, z)
    if s_valid < s_total:
        pad = _pad_rows(z.shape, t, s_valid)
        g = jnp.where(pad, jnp.where(lane < N_HEADS, NEG_BIG, 0.0), g)
    g_t = _lanes_to_rows(g, 0)
    bt_cols = _dot_exact(_tri(L).astype(F32), g)
    bt_rows = _dot_exact(g_t, _tri(L, upper=True).astype(F32))
    causal = _tri(L)
    scale = HEAD_DIM ** -0.5

    for h in range(N_HEADS):
        sl = slice(h * HEAD_DIM, (h + 1) * HEAD_DIM)
        qh = q_ref[:, sl]
        kh = k_ref[:, sl] * scale
        vh = v_ref[:, sl]
        bt_c = bt_cols[:, N_HEADS + h:N_HEADS + h + 1]
        it_c = g[:, h:h + 1]
        bt_r = bt_rows[N_HEADS + h:N_HEADS + h + 1, :]
        it_r = g_t[h:h + 1, :]
        m_prev = m_ref[0, h:h + 1, 0:1]
        c_prev = c_ref[0, h]
        n_prev = n_ref[0, h:h + 1, :]

        log_d = jnp.where(causal, bt_c - bt_r + it_r, -jnp.inf)
        inter = bt_c + m_prev
        m_t = jnp.maximum(inter, jnp.max(log_d, axis=-1, keepdims=True))
        d = jnp.exp(log_d - m_t)
        inter_w = jnp.exp(inter - m_t)
        s = _dot_nt(qh, kh) * d
        num = _dot(s, vh) + _dot(qh, c_prev) * inter_w
        den = (jnp.sum(s, axis=-1, keepdims=True)
               + inter_w * jnp.sum(qh * n_prev, axis=-1, keepdims=True))
        denom = jnp.maximum(jnp.abs(den), jnp.exp(-m_t))
        hh = num / denom
        bt_last = bt_c[L - 1:L, :]
        m_new = m_t[L - 1:L, :]
        w = jnp.exp(bt_last - bt_c + it_c - m_new)
        decay = jnp.exp(bt_last + m_prev - m_new)
        kw = kh * w
        c_ref[0, h] = decay * c_prev + _dot_tn(kw, vh)
        n_ref[0, h:h + 1, :] = decay * n_prev + jnp.sum(kw, axis=0, keepdims=True)
        m_ref[0, h:h + 1, :] = jnp.broadcast_to(m_new, (1, LANES))

        hn = hh * lax.rsqrt(jnp.mean(hh * hh, axis=-1, keepdims=True) + EPS) * norm_ref[:, sl]
        h_ref[:, sl] = hn * _sigmoid(og_ref[:, sl])


def _mlstm(proj, gs, bias_row, norm_row, init, *, n_batch, seq, s_valid, row_off, L):
    nt = seq // L
    has_init = init is not None

    def col(c):
        return pl.BlockSpec((L, BRANCH_W), lambda b, t: (row_off + b * nt + t, c))

    in_specs = [col(COL_MQ), col(COL_MK), col(COL_MV), col(COL_MO),
                pl.BlockSpec((L, LANES), lambda b, t: (row_off + b * nt + t, 0)),
                pl.BlockSpec((1, LANES), lambda b, t: (0, 0)),
                pl.BlockSpec((1, BRANCH_W), lambda b, t: (0, 0))]
    args = [proj, proj, proj, proj, gs, bias_row, norm_row]
    c_spec = pl.BlockSpec((1, N_HEADS, HEAD_DIM, HEAD_DIM), lambda b, t: (b, 0, 0, 0))
    n_spec = pl.BlockSpec((1, N_HEADS, HEAD_DIM), lambda b, t: (b, 0, 0))
    m_spec = pl.BlockSpec((1, N_HEADS, LANES), lambda b, t: (b, 0, 0))
    if has_init:
        in_specs += [c_spec, n_spec, m_spec]
        args += list(init)
    return pl.pallas_call(
        functools.partial(_mlstm_kernel, L=L, s_valid=s_valid, s_total=seq, has_init=has_init),
        out_shape=(jax.ShapeDtypeStruct((n_batch * seq, BRANCH_W), F32),
                   jax.ShapeDtypeStruct((n_batch, N_HEADS, HEAD_DIM, HEAD_DIM), F32),
                   jax.ShapeDtypeStruct((n_batch, N_HEADS, HEAD_DIM), F32),
                   jax.ShapeDtypeStruct((n_batch, N_HEADS, LANES), F32)),
        grid=(n_batch, nt),
        in_specs=in_specs,
        out_specs=(pl.BlockSpec((L, BRANCH_W), lambda b, t: (b * nt + t, 0)),
                   c_spec, n_spec, m_spec),
        compiler_params=_params(("parallel", "arbitrary")),
        name="mlstm",
    )(*args)


def _unit_lower_inverse(a, L):
    eye = (lax.broadcasted_iota(jnp.int32, (L, L), 0)
           == lax.broadcasted_iota(jnp.int32, (L, L), 1)).astype(F32)
    p = -a
    x = eye + p
    n = 2
    while n < L:
        p = _dot(p, p)
        x = x + _dot(x, p)
        n *= 2
    return x


def _gdn_kernel(*refs, L, s_valid, s_total, has_init):
    if has_init:
        (q_ref, k_ref, v_ref, og_ref, gs_ref, cw_ref, arow_ref, dtrow_ref, norm_ref,
         conv0_ref, s0_ref, o_ref, s_ref, conv1_ref, ext_ref) = refs
    else:
        (q_ref, k_ref, v_ref, og_ref, gs_ref, cw_ref, arow_ref, dtrow_ref, norm_ref,
         o_ref, s_ref, conv1_ref, ext_ref) = refs
    t = pl.program_id(1)
    nt = s_total // L
    W = BRANCH_W

    @pl.when(t == 0)
    def _():
        if has_init:
            s_ref[...] = s0_ref[...]
            ext_ref[0:SUBLANES, :] = conv0_ref[0]
        else:
            s_ref[...] = jnp.zeros_like(s_ref)
            ext_ref[0:SUBLANES, :] = jnp.zeros((SUBLANES, 3 * W), F32)

    @pl.when(t > 0)
    def _():
        ext_ref[0:SUBLANES, :] = ext_ref[L:L + SUBLANES, :]

    ext_ref[SUBLANES:SUBLANES + L, 0:W] = q_ref[...]
    ext_ref[SUBLANES:SUBLANES + L, W:2 * W] = k_ref[...]
    ext_ref[SUBLANES:SUBLANES + L, 2 * W:3 * W] = v_ref[...]

    base = SUBLANES - (CONV_W - 1)
    acc = ext_ref[base:base + L, :] * cw_ref[0:1, :]
    for j in range(1, CONV_W):
        acc = acc + ext_ref[base + j:base + j + L, :] * cw_ref[j:j + 1, :]
    conv = _silu(acc)

    @pl.when(t == nt - 1)
    def _():
        v_last = s_valid - (nt - 1) * L
        conv1_ref[0] = ext_ref[v_last:v_last + SUBLANES, :]

    z = gs_ref[...]
    lane = lax.broadcasted_iota(jnp.int32, z.shape, 1)
    gdec = arow_ref[...] * _softplus(z + dtrow_ref[...])
    gb = jnp.where((lane >= 2 * N_HEADS) & (lane < 3 * N_HEADS), gdec,
                   jnp.where((lane >= 3 * N_HEADS) & (lane < 4 * N_HEADS), _sigmoid(z), 0.0))
    if s_valid < s_total:
        gb = jnp.where(_pad_rows(z.shape, t, s_valid), 0.0, gb)
    gb_t = _lanes_to_rows(gb, 2 * N_HEADS)
    gam_cols = _dot_exact(_tri(L).astype(F32), gb)
    gam_rows = _dot_exact(gb_t, _tri(L, upper=True).astype(F32))
    incl = _tri(L)
    strict = _tri(L, strict=True)

    for h in range(N_HEADS):
        sl = slice(h * HEAD_DIM, (h + 1) * HEAD_DIM)
        qh = conv[:, h * HEAD_DIM:(h + 1) * HEAD_DIM]
        kh = conv[:, W + h * HEAD_DIM:W + (h + 1) * HEAD_DIM]
        vh = conv[:, 2 * W + h * HEAD_DIM:2 * W + (h + 1) * HEAD_DIM]
        qh = qh * lax.rsqrt(jnp.sum(qh * qh, axis=-1, keepdims=True) + EPS) * (HEAD_DIM ** -0.5)
        kh = kh * lax.rsqrt(jnp.sum(kh * kh, axis=-1, keepdims=True) + EPS)
        gam_c = gam_cols[:, 2 * N_HEADS + h:2 * N_HEADS + h + 1]
        gam_r = gam_rows[h:h + 1, :]
        beta_c = gb[:, 3 * N_HEADS + h:3 * N_HEADS + h + 1]

        decay = jnp.where(incl, jnp.exp(jnp.where(incl, gam_c - gam_r, 0.0)), 0.0)
        kk = _dot_nt(kh, kh)
        a_mat = jnp.where(strict, beta_c * kk * decay, 0.0)
        e_gam = jnp.exp(gam_c)
        rhs = jnp.concatenate([beta_c * vh, beta_c * kh * e_gam], axis=-1)
        sol = _dot(_unit_lower_inverse(a_mat, L), rhs)
        u = sol[:, :HEAD_DIM]
        w = sol[:, HEAD_DIM:]
        qk = _dot_nt(qh, kh) * decay
        gam_last = gam_c[L - 1:L, :]
        q_dec = qh * e_gam
        k_dec = kh * jnp.exp(gam_last - gam_c)
        s_prev = s_ref[0, h]
        v_new = u - _dot(w, s_prev)
        o = _dot(q_dec, s_prev) + _dot(qk, v_new)
        s_ref[0, h] = jnp.exp(gam_last) * s_prev + _dot_tn(k_dec, v_new)

        on = o * lax.rsqrt(jnp.mean(o * o, axis=-1, keepdims=True) + EPS) * norm_ref[...]
        o_ref[:, sl] = on * _silu(og_ref[:, sl])


def _gdn(proj, gs, conv_w, a_row, dt_row, norm_row, init, *, n_batch, seq, s_valid, row_off, L):
    nt = seq // L
    has_init = init is not None

    def col(c):
        return pl.BlockSpec((L, BRANCH_W), lambda b, t: (row_off + b * nt + t, c))

    def const(shape):
        return pl.BlockSpec(shape, lambda b, t: (0,) * len(shape))

    in_specs = [col(COL_GQ), col(COL_GK), col(COL_GV), col(COL_GO),
                pl.BlockSpec((L, LANES), lambda b, t: (row_off + b * nt + t, 0)),
                const((CONV_W, 3 * BRANCH_W)), const((1, LANES)), const((1, LANES)),
                const((1, HEAD_DIM))]
    args = [proj, proj, proj, proj, gs, conv_w, a_row, dt_row, norm_row]
    s_spec = pl.BlockSpec((1, N_HEADS, HEAD_DIM, HEAD_DIM), lambda b, t: (b, 0, 0, 0))
    conv_spec = pl.BlockSpec((1, SUBLANES, 3 * BRANCH_W), lambda b, t: (b, 0, 0))
    if has_init:
        in_specs += [conv_spec, s_spec]
        args += list(init)
    return pl.pallas_call(
        functools.partial(_gdn_kernel, L=L, s_valid=s_valid, s_total=seq, has_init=has_init),
        out_shape=(jax.ShapeDtypeStruct((n_batch * seq, BRANCH_W), F32),
                   jax.ShapeDtypeStruct((n_batch, N_HEADS, HEAD_DIM, HEAD_DIM), F32),
                   jax.ShapeDtypeStruct((n_batch, SUBLANES, 3 * BRANCH_W), F32)),
        grid=(n_batch, nt),
        in_specs=in_specs,
        out_specs=(pl.BlockSpec((L, BRANCH_W), lambda b, t: (b * nt + t, 0)),
                   s_spec, conv_spec),
        scratch_shapes=[pltpu.VMEM((L + SUBLANES, 3 * BRANCH_W), F32)],
        compiler_params=_params(("parallel", "arbitrary")),
        name="gdn",
    )(*args)


def _xattn_kernel(q_ref, mk_ref, mv_ref, o_ref):
    scale = HEAD_DIM ** -0.5
    for h in range(N_HEADS):
        sl = slice(h * HEAD_DIM, (h + 1) * HEAD_DIM)
        s = _dot_nt(q_ref[:, sl], mk_ref[:, sl]) * scale
        s = s - jnp.max(s, axis=-1, keepdims=True)
        e = jnp.exp(s)
        p = e / jnp.sum(e, axis=-1, keepdims=True)
        o_ref[:, sl] = _dot(p, mv_ref[:, sl])


def _xattn(proj, mk, mv, *, n_batch, seq, row_off, lt, mem_len, k_col, v_col):
    nt = seq // lt
    return pl.pallas_call(
        _xattn_kernel,
        out_shape=jax.ShapeDtypeStruct((n_batch * seq, BRANCH_W), F32),
        grid=(n_batch, nt),
        in_specs=[pl.BlockSpec((lt, BRANCH_W), lambda b, t: (row_off + b * nt + t, COL_XQ)),
                  pl.BlockSpec((mem_len, BRANCH_W), lambda b, t: (b, k_col)),
                  pl.BlockSpec((mem_len, BRANCH_W), lambda b, t: (b, v_col))],
        out_specs=pl.BlockSpec((lt, BRANCH_W), lambda b, t: (b * nt + t, 0)),
        compiler_params=_params(("parallel", "arbitrary")),
        name="xattn",
    )(proj, mk, mv)


def _lane_first_max(x, mask, lane_f):
    xm = jnp.where(mask, x, -jnp.inf)
    mx = jnp.max(xm, axis=-1, keepdims=True)
    idx = jnp.min(jnp.where(xm == mx, lane_f, float(LANES)), axis=-1, keepdims=True)
    return mx, idx


def _merge_kernel(hap_ref, has_ref, hbp_ref, hbs_ref, hcp_ref, hcs_ref, xp_ref, xs_ref,
                  ga_ref, gb_ref, gc_ref, wa_ref, wb_ref, wc_ref, wo_ref, nf_ref, wr_ref, br_ref,
                  x1_ref, *, n_first):
    i = pl.program_id(0)
    d = xp_ref.shape[1]

    def body(ha_ref, hb_ref, hc_ref, x_ref):
        mixed = (_sigmoid(ga_ref[...]) * _dot(ha_ref[...], wa_ref[...])
                 + _sigmoid(gb_ref[...]) * _dot(hb_ref[...], wb_ref[...])
                 + _sigmoid(gc_ref[...]) * _dot(hc_ref[...], wc_ref[...]))
        x1 = x_ref[...] + _dot(mixed, wo_ref[...])
        h2 = _rms(x1, nf_ref[...])
        logits = _dot(h2, wr_ref[...]) + br_ref[...]
        lane = lax.broadcasted_iota(jnp.int32, logits.shape, 1)
        lane_f = lane.astype(F32)
        is_g = lane < N_GROUPS
        g_max, g_idx = _lane_first_max(logits, is_g, lane_f)
        g_den = jnp.sum(jnp.where(is_g, jnp.exp(logits - g_max), 0.0), axis=-1, keepdims=True)
        p_grp = 1.0 / g_den
        e_lo = N_GROUPS + EXPERTS_PER_GROUP * g_idx
        in_grp = (lane_f >= e_lo) & (lane_f < e_lo + EXPERTS_PER_GROUP)
        v1, i1 = _lane_first_max(logits, in_grp, lane_f)
        v2, i2 = _lane_first_max(logits, in_grp & (lane_f != i1), lane_f)
        e2 = jnp.exp(v2 - v1)
        w1 = 1.0 / (1.0 + e2)
        w2 = e2 / (1.0 + e2)
        comb = p_grp * (jnp.where(lane_f == i1, w1, 0.0) + jnp.where(lane_f == i2, w2, 0.0))
        lo = jnp.minimum(i1, i2) - e_lo
        hi = jnp.maximum(i1, i2) - e_lo
        pair = lo * (7.0 - lo) * 0.5 + (hi - lo - 1.0)
        cls = g_idx * float(N_PAIRS) + pair
        x1_ref[:, 0:d] = x1
        x1_ref[:, d:d + ROUTE_W] = jnp.where(lane == 0, cls, comb)

    @pl.when(i < n_first)
    def _():
        body(hap_ref, hbp_ref, hcp_ref, xp_ref)

    @pl.when(i >= n_first)
    def _():
        body(has_ref, hbs_ref, hcs_ref, xs_ref)


def _merge(ha, hb, hc, xs, proj, wa, wb, wc, wo, nf, wr, br, *, tm):
    d = xs[0].shape[1]
    n_first = xs[0].shape[0] // tm
    n_rows = xs[0].shape[0] + xs[1].shape[0]
    specs = []
    for pair, width in ((ha, BRANCH_W), (hb, BRANCH_W), (hc, BRANCH_W), (xs, d)):
        specs += list(_two_source_specs(n_first, tm, width))

    def const(a):
        return pl.BlockSpec(a.shape, lambda i: (0,) * a.ndim)

    specs += [pl.BlockSpec((tm, d), lambda i, c=c: (i, COL_GATE + c)) for c in range(3)]
    weights = [wa, wb, wc, wo, nf, wr, br]
    specs += [const(a) for a in weights]
    return pl.pallas_call(
        functools.partial(_merge_kernel, n_first=n_first),
        out_shape=jax.ShapeDtypeStruct((n_rows, d + ROUTE_W), F32),
        grid=(n_rows // tm,),
        in_specs=specs,
        out_specs=pl.BlockSpec((tm, d + ROUTE_W), lambda i: (i, 0)),
        compiler_params=_params(("parallel",)),
        name="merge",
    )(*ha, *hb, *hc, *xs, proj, proj, proj, *weights)


def _row_gather_start(src_ref, dst_ref, sem, ids_ref, base, n_rows):
    def issue(r, carry):
        pltpu.make_async_copy(src_ref.at[pl.ds(ids_ref[base + r], 1)],
                              dst_ref.at[pl.ds(r, 1)], sem).start()
        return carry
    lax.fori_loop(0, n_rows, issue, 0)


def _row_gather_wait(src_ref, dst_ref, sem, n_rows):
    pltpu.make_async_copy(src_ref.at[pl.ds(0, n_rows)], dst_ref, sem).wait()


def _moe_kernel(ids_ref, ea_ref, eb_ref, nused_ref, x_hbm, wga_ref, wua_ref, wda_ref,
                wgb_ref, wub_ref, wdb_ref, nf_ref, fin_ref, y_ref, xbuf, sem, *, tm, d):
    i = pl.program_id(0)
    n_used = nused_ref[0]
    slot = i % 2

    @pl.when(i == 0)
    def _():
        _row_gather_start(x_hbm, xbuf.at[0], sem.at[0], ids_ref, 0, tm)

    @pl.when(i + 1 < n_used)
    def _():
        _row_gather_start(x_hbm, xbuf.at[1 - slot], sem.at[1 - slot], ids_ref, (i + 1) * tm, tm)

    @pl.when(i < n_used)
    def _():
        _row_gather_wait(x_hbm, xbuf.at[slot], sem.at[slot], tm)
        x1 = xbuf[slot, :, 0:d]
        route = xbuf[slot, :, d:d + ROUTE_W]
        lane = lax.broadcasted_iota(jnp.int32, route.shape, 1)
        h2 = _rms(x1, nf_ref[...]).astype(BF16)
        acc = jnp.zeros((tm, d), F32)
        for e_ref, wg_ref, wu_ref, wd_ref in ((ea_ref, wga_ref, wua_ref, wda_ref),
                                              (eb_ref, wgb_ref, wub_ref, wdb_ref)):
            comb = jnp.sum(jnp.where(lane == N_GROUPS + e_ref[i], route, 0.0),
                           axis=-1, keepdims=True)
            a = jnp.dot(h2, wg_ref[0], preferred_element_type=F32)
            u = jnp.dot(h2, wu_ref[0], preferred_element_type=F32)
            act = _silu(a) * u * comb
            acc = acc + _dot(act, wd_ref[0])
        y_ref[...] = _rms(x1 + acc, fin_ref[...])

    @pl.when(i >= n_used)
    def _():
        y_ref[...] = jnp.zeros_like(y_ref)


def _moe(ids, tile_ea, tile_eb, n_used, x1aug, wg, wu, wd, nf, fin, *, tm, n_tiles):
    d = nf.shape[1]
    f = wg.shape[2]

    def expert(shape, which):
        return pl.BlockSpec((1,) + shape, lambda i, ids, ea, eb, nu: ((ea, eb)[which][i], 0, 0))

    def const(a):
        return pl.BlockSpec(a.shape, lambda i, *_: (0,) * a.ndim)

    grid_spec = pltpu.PrefetchScalarGridSpec(
        num_scalar_prefetch=4,
        grid=(n_tiles,),
        in_specs=[pl.BlockSpec(memory_space=pl.ANY),
                  expert((d, f), 0), expert((d, f), 0), expert((f, d), 0),
                  expert((d, f), 1), expert((d, f), 1), expert((f, d), 1),
                  const(nf), const(fin)],
        out_specs=pl.BlockSpec((tm, d), lambda i, *_: (i, 0)),
        scratch_shapes=[pltpu.VMEM((2, tm, d + ROUTE_W), F32),
                        pltpu.SemaphoreType.DMA((2,))],
    )
    return pl.pallas_call(
        functools.partial(_moe_kernel, tm=tm, d=d),
        out_shape=jax.ShapeDtypeStruct((n_tiles * tm, d), F32),
        grid_spec=grid_spec,
        compiler_params=_params(("arbitrary",)),
        name="moe",
    )(ids, tile_ea, tile_eb, n_used, x1aug, wg, wu, wd, wg, wu, wd, nf, fin)


def _unsort_kernel(pos_ref, y_hbm, o_ref, sem, *, tm, base):
    i = pl.program_id(0)
    _row_gather_start(y_hbm, o_ref, sem, pos_ref, base + i * tm, tm)
    _row_gather_wait(y_hbm, o_ref, sem, tm)


def _unsort(pos, y_sorted, *, base, n_rows, tm):
    d = y_sorted.shape[1]
    grid_spec = pltpu.PrefetchScalarGridSpec(
        num_scalar_prefetch=1,
        grid=(n_rows // tm,),
        in_specs=[pl.BlockSpec(memory_space=pl.ANY)],
        out_specs=pl.BlockSpec((tm, d), lambda i, *_: (i, 0)),
        scratch_shapes=[pltpu.SemaphoreType.DMA(())],
    )
    return pl.pallas_call(
        functools.partial(_unsort_kernel, tm=tm, base=base),
        out_shape=jax.ShapeDtypeStruct((n_rows, d), F32),
        grid_spec=grid_spec,
        compiler_params=_params(("arbitrary",)),
        name="unsort",
    )(pos, y_sorted)


def _sort_plan(cls, *, tm, n_tiles):
    n = cls.shape[0]
    onehot = (cls[:, None] == jnp.arange(N_CLASSES, dtype=jnp.int32)[None, :]).astype(jnp.int32)
    csum = jnp.cumsum(onehot, axis=0)
    rank = jnp.sum((csum - onehot) * onehot, axis=1)
    counts = csum[-1]
    tiles_per = (counts + tm - 1) // tm
    tile_end = jnp.cumsum(tiles_per)
    tile_start = tile_end - tiles_per
    pos = (jnp.sum(tile_start[None, :] * onehot, axis=1) * tm + rank).astype(jnp.int32)
    n_used = tile_end[-1].astype(jnp.int32)
    ids = jnp.zeros((n_tiles * tm,), jnp.int32).at[pos].set(jnp.arange(n, dtype=jnp.int32))
    tile = jnp.minimum(jnp.arange(n_tiles, dtype=jnp.int32), n_used - 1)
    tile_cls = jnp.sum((tile[:, None] >= tile_end[None, :]).astype(jnp.int32), axis=1)
    grp = tile_cls // N_PAIRS
    pair = tile_cls % N_PAIRS
    lo = jnp.asarray(PAIR_LO, jnp.int32)[pair]
    hi = jnp.asarray(PAIR_HI, jnp.int32)[pair]
    ea = (grp * EXPERTS_PER_GROUP + lo).astype(jnp.int32)
    eb = (grp * EXPERTS_PER_GROUP + hi).astype(jnp.int32)
    return pos, ids, ea, eb, n_used.reshape(1)


def _lane_row(values, start):
    row = jnp.zeros((LANES,), F32)
    return row.at[start:start + values.shape[0]].set(values.astype(F32)).reshape(1, LANES)


def _layer(x_prompt, x_sample, mem_prompt, cache_mem_k, cache_mem_v,
           state_mlstm_C, state_mlstm_n, state_mlstm_m, state_gdn_S, state_gdn_conv,
           norm_mix, w_in, b_igate, b_fgate, mlstm_norm, conv_w, a_log, dt_bias, gdn_norm,
           mem_norm, w_mem_kv, w_br_mlstm, w_br_gdn, w_br_xattn, w_out, norm_ffn,
           w_router_group, b_router_group, w_router_expert, b_router_expert,
           w_exp_gate, w_exp_up, w_exp_down, final_norm, *, chunk_m, chunk_g, tm, tm_moe):
    bp, sp, d = x_prompt.shape
    bs, ss, _ = x_sample.shape
    mem_len = mem_prompt.shape[1]
    ss_pad = SUBLANES
    assert ss <= ss_pad and d % LANES == 0
    n_p = bp * sp
    n_s = bs * ss_pad
    W = BRANCH_W

    sizes = (W, W, W, N_HEADS, N_HEADS, W, 3 * W, N_HEADS, N_HEADS, W, W, d, d, d)
    offs = [0]
    for s in sizes:
        offs.append(offs[-1] + s)
    (m_q, m_k, m_v, m_i, m_f, m_o, g_qkv, g_a, g_b, g_out, x_q, gate_a, gate_b, gate_c) = [
        w_in[:, offs[j]:offs[j + 1]] for j in range(len(sizes))]
    w_big = jnp.concatenate([gate_a, gate_b, gate_c, m_q, m_k, m_v, m_o, g_qkv, g_out, x_q],
                            axis=1).astype(BF16)
    w_small = jnp.concatenate(
        [m_i, m_f, g_a, g_b, jnp.zeros((d, LANES - 4 * N_HEADS), w_in.dtype)], axis=1).astype(BF16)
    w_router = jnp.concatenate(
        [w_router_group, w_router_expert,
         jnp.zeros((d, LANES - N_GROUPS - N_EXPERTS), w_in.dtype)], axis=1).astype(BF16)
    b_router = _lane_row(jnp.concatenate([b_router_group, b_router_expert]), 0)
    gate_bias = _lane_row(jnp.concatenate([b_igate, b_fgate]), 0)
    a_row = _lane_row(-jnp.exp(a_log.astype(F32)), 2 * N_HEADS)
    dt_row = _lane_row(dt_bias, 2 * N_HEADS)

    xp2 = x_prompt.reshape(n_p, d)
    xs2 = jnp.pad(x_sample, ((0, 0), (0, ss_pad - ss), (0, 0))).reshape(n_s, d)

    proj, gs = _norm_proj(xp2, xs2, norm_mix.reshape(1, d), w_big, w_small, tm=tm, tn=W)

    mem2 = mem_prompt.reshape(bp * mem_len, d)
    mem_kv, _ = _norm_proj(mem2, None, mem_norm.reshape(1, d), w_mem_kv.astype(BF16),
                           jnp.zeros((d, LANES), BF16), tm=mem_len, tn=W)

    mnorm = mlstm_norm.reshape(1, W)
    gnorm = gdn_norm.reshape(1, HEAD_DIM)
    ha_p, c_p, nn_p, mm_p = _mlstm(proj, gs, gate_bias, mnorm, None, n_batch=bp, seq=sp,
                                   s_valid=sp, row_off=0, L=chunk_m)
    m0 = jnp.broadcast_to(state_mlstm_m[:, :, None], (bs, N_HEADS, LANES))
    ha_s, c_s, nn_s, mm_s = _mlstm(proj, gs, gate_bias, mnorm,
                                   (state_mlstm_C, state_mlstm_n, m0), n_batch=bs, seq=ss_pad,
                                   s_valid=ss, row_off=n_p // ss_pad, L=ss_pad)
    hb_p, s_p, cv_p = _gdn(proj, gs, conv_w, a_row, dt_row, gnorm, None, n_batch=bp, seq=sp,
                           s_valid=sp, row_off=0, L=chunk_g)
    conv0 = jnp.pad(state_gdn_conv, ((0, 0), (SUBLANES - (CONV_W - 1), 0), (0, 0)))
    hb_s, s_s, cv_s = _gdn(proj, gs, conv_w, a_row, dt_row, gnorm, (conv0, state_gdn_S),
                           n_batch=bs, seq=ss_pad, s_valid=ss, row_off=n_p // ss_pad, L=ss_pad)
    lt = min(sp, 256)
    hc_p = _xattn(proj, mem_kv, mem_kv, n_batch=bp, seq=sp, row_off=0, lt=lt,
                  mem_len=mem_len, k_col=0, v_col=1)
    ck = cache_mem_k.reshape(bs * mem_len, W)
    cv = cache_mem_v.reshape(bs * mem_len, W)
    hc_s = _xattn(proj, ck, cv, n_batch=bs, seq=ss_pad, row_off=n_p // ss_pad, lt=ss_pad,
                  mem_len=mem_len, k_col=0, v_col=0)

    x1aug = _merge((ha_p, ha_s), (hb_p, hb_s), (hc_p, hc_s), (xp2, xs2), proj,
                   w_br_mlstm.astype(BF16), w_br_gdn.astype(BF16), w_br_xattn.astype(BF16),
                   w_out.astype(BF16), norm_ffn.reshape(1, d), w_router, b_router, tm=tm)

    n_all = n_p + n_s
    n_tiles = (n_all + N_CLASSES * (tm_moe - 1)) // tm_moe + 1
    cls = x1aug[:, d].astype(jnp.int32)
    pos, ids, ea, eb, n_used = _sort_plan(cls, tm=tm_moe, n_tiles=n_tiles)
    y_sorted = _moe(ids, ea, eb, n_used, x1aug, w_exp_gate.astype(BF16), w_exp_up.astype(BF16),
                    w_exp_down.astype(BF16), norm_ffn.reshape(1, d), final_norm.reshape(1, d),
                    tm=tm_moe, n_tiles=n_tiles)
    y_p = _unsort(pos, y_sorted, base=0, n_rows=n_p, tm=tm)
    y_s = _unsort(pos, y_sorted, base=n_p, n_rows=n_s, tm=tm)

    y_prompt = y_p.reshape(bp, sp, d)
    y_sample = y_s.reshape(bs, ss_pad, d)[:, :ss]
    mk_p = mem_kv[:, :W].reshape(bp, mem_len, N_HEADS, HEAD_DIM)
    mv_p = mem_kv[:, W:].reshape(bp, mem_len, N_HEADS, HEAD_DIM)
    tail = slice(SUBLANES - (CONV_W - 1), SUBLANES)
    return (y_prompt, y_sample, mk_p, mv_p,
            c_p, nn_p, mm_p[:, :, 0], s_p, cv_p[:, tail],
            c_s, nn_s, mm_s[:, :, 0], s_s, cv_s[:, tail])


def kernel(x_prompt, x_sample, mem_prompt, cache_mem_k, cache_mem_v, state_mlstm_C, state_mlstm_n, state_mlstm_m, state_gdn_S, state_gdn_conv, norm_mix, w_in, b_igate, b_fgate, mlstm_norm, conv_w, a_log, dt_bias, gdn_norm, mem_norm, w_mem_kv, w_br_mlstm, w_br_gdn, w_br_xattn, w_out, norm_ffn, w_router_group, b_router_group, w_router_expert, b_router_expert, w_exp_gate, w_exp_up, w_exp_down, final_norm):
    depth = w_in.shape[0]
    assert depth == 1, "one decoder layer"
    outs = _layer(
        x_prompt, x_sample, mem_prompt, cache_mem_k[0], cache_mem_v[0],
        state_mlstm_C[0], state_mlstm_n[0], state_mlstm_m[0], state_gdn_S[0], state_gdn_conv[0],
        norm_mix[0], w_in[0], b_igate[0], b_fgate[0], mlstm_norm[0], conv_w[0], a_log[0],
        dt_bias[0], gdn_norm[0], mem_norm[0], w_mem_kv[0], w_br_mlstm[0], w_br_gdn[0],
        w_br_xattn[0], w_out[0], norm_ffn[0], w_router_group[0], b_router_group[0],
        w_router_expert[0], b_router_expert[0], w_exp_gate[0], w_exp_up[0], w_exp_down[0],
        final_norm, chunk_m=128, chunk_g=64, tm=256, tm_moe=256)
    y_prompt, y_sample = outs[0], outs[1]
    return (y_prompt, y_sample) + tuple(o[None] for o in outs[2:])
```

```python
import functools

import jax
import jax.numpy as jnp
from jax import lax
from jax.experimental import pallas as pl
from jax.experimental.pallas import tpu as pltpu

F32 = jnp.float32
BF16 = jnp.bfloat16
EPS = 1e-6

N_HEADS = 4
HEAD_DIM = 128
BRANCH_W = N_HEADS * HEAD_DIM
CONV_W = 4
N_GROUPS = 4
EXPERTS_PER_GROUP = 4
N_EXPERTS = N_GROUPS * EXPERTS_PER_GROUP
N_PAIRS = 6
N_CLASSES = N_GROUPS * N_PAIRS
PAIR_LO = (0, 0, 0, 1, 1, 2)
PAIR_HI = (1, 2, 3, 2, 3, 3)
NEG_BIG = -1e30
LANES = 128
SUBLANES = 8
ROUTE_W = LANES
VMEM_LIMIT = 48 * 1024 * 1024

COL_GATE = 0
COL_MQ, COL_MK, COL_MV, COL_MO = 6, 7, 8, 9
COL_GQ, COL_GK, COL_GV, COL_GO, COL_XQ = 10, 11, 12, 13, 14
N_PROJ = 15 * BRANCH_W


def _dot(a, b):
    return jnp.dot(a.astype(BF16), b.astype(BF16), preferred_element_type=F32)


def _dot_nt(a, b):
    return lax.dot_general(a.astype(BF16), b.astype(BF16), (((1,), (1,)), ((), ())),
                           preferred_element_type=F32)


def _dot_tn(a, b):
    return lax.dot_general(a.astype(BF16), b.astype(BF16), (((0,), (0,)), ((), ())),
                           preferred_element_type=F32)


def _dot_exact(a, b):
    return jnp.dot(a, b, preferred_element_type=F32, precision=lax.Precision.HIGHEST)


def _lanes_to_rows(x, lane0):
    r = lax.broadcasted_iota(jnp.int32, (SUBLANES, LANES), 0)
    c = lax.broadcasted_iota(jnp.int32, (SUBLANES, LANES), 1)
    sel = (c == r + lane0).astype(F32)
    return lax.dot_general(sel, x, (((1,), (1,)), ((), ())), preferred_element_type=F32,
                           precision=lax.Precision.HIGHEST)


def _rms(x, g):
    return x * lax.rsqrt(jnp.mean(x * x, axis=-1, keepdims=True) + EPS) * g


def _softplus(x):
    return jnp.maximum(x, 0.0) + jnp.log1p(jnp.exp(-jnp.abs(x)))


def _sigmoid(x):
    return 1.0 / (1.0 + jnp.exp(-x))


def _silu(x):
    return x * _sigmoid(x)


def _tri(n, strict=False):
    r = lax.broadcasted_iota(jnp.int32, (n, n), 0)
    c = lax.broadcasted_iota(jnp.int32, (n, n), 1)
    return (c < r) if strict else (c <= r)


def _tri_blocks(n, block, upper=False):
    shift = block.bit_length() - 1
    assert block == 1 << shift
    r = lax.broadcasted_iota(jnp.int32, (n, n), 0)
    c = lax.broadcasted_iota(jnp.int32, (n, n), 1)
    same = jnp.right_shift(r, shift) == jnp.right_shift(c, shift)
    return same & ((r <= c) if upper else (c <= r))


def _pad_rows(shape, t, seq_rows, n_valid):
    assert seq_rows & (seq_rows - 1) == 0
    row = lax.broadcasted_iota(jnp.int32, shape, 0)
    return (row & (seq_rows - 1)) + t * seq_rows >= n_valid


def _params(sem):
    return pltpu.CompilerParams(dimension_semantics=sem, vmem_limit_bytes=VMEM_LIMIT)


def _two_source_specs(n_first, block, width):
    first = pl.BlockSpec((block, width), lambda i, *_: (jnp.minimum(i, n_first - 1), 0))
    second = pl.BlockSpec((block, width), lambda i, *_: (jnp.maximum(i - n_first, 0), 0))
    return first, second


def _norm_proj_kernel(xa_ref, xb_ref, g_ref, w_ref, ws_ref, o_ref, os_ref, hb_ref, *, n_first):
    i = pl.program_id(0)
    j = pl.program_id(1)

    @pl.when(j == 0)
    def _():
        @pl.when(i < n_first)
        def _():
            hb_ref[...] = _rms(xa_ref[...], g_ref[...]).astype(BF16)

        @pl.when(i >= n_first)
        def _():
            hb_ref[...] = _rms(xb_ref[...], g_ref[...]).astype(BF16)

        os_ref[...] = jnp.dot(hb_ref[...], ws_ref[...], preferred_element_type=F32)

    o_ref[...] = jnp.dot(hb_ref[...], w_ref[0], preferred_element_type=F32)


def _col_blocks(w, tn):
    d, n = w.shape
    return w.reshape(d, n // tn, tn).transpose(1, 0, 2)


def _norm_proj(xa, xb, g, w, ws, *, tm):
    d = xa.shape[1]
    n_first = xa.shape[0] // tm
    n_rows = xa.shape[0] + (0 if xb is None else xb.shape[0])
    if xb is None:
        xb = xa
    tn = w.shape[2]
    n = w.shape[0] * tn
    sa, sb = _two_source_specs(n_first, tm, d)
    return pl.pallas_call(
        functools.partial(_norm_proj_kernel, n_first=n_first),
        out_shape=(jax.ShapeDtypeStruct((n_rows, n), F32),
                   jax.ShapeDtypeStruct((n_rows, ws.shape[1]), F32)),
        grid=(n_rows // tm, n // tn),
        in_specs=[sa, sb,
                  pl.BlockSpec((1, d), lambda i, j: (0, 0)),
                  pl.BlockSpec((1, d, tn), lambda i, j: (j, 0, 0)),
                  pl.BlockSpec((d, ws.shape[1]), lambda i, j: (0, 0))],
        out_specs=(pl.BlockSpec((tm, tn), lambda i, j: (i, j)),
                   pl.BlockSpec((tm, ws.shape[1]), lambda i, j: (i, 0))),
        scratch_shapes=[pltpu.VMEM((tm, d), BF16)],
        compiler_params=_params(("parallel", "arbitrary")),
        name="norm_proj",
    )(xa, xb, g, w, ws)


def _seq_tile_specs(G, Lt, nt, row_off):
    assert G == 1 or nt == 1

    def rows(b, t):
        return row_off + b * nt + t

    return rows


def _mlstm_kernel(*refs, L, n_sub, G, s_valid, s_total, has_init):
    if has_init:
        (q_ref, k_ref, v_ref, og_ref, gs_ref, bias_ref, norm_ref, c0_ref, n0_ref, m0_ref,
         h_ref, c_ref, n_ref, m_ref) = refs
    else:
        (q_ref, k_ref, v_ref, og_ref, gs_ref, bias_ref, norm_ref,
         h_ref, c_ref, n_ref, m_ref) = refs
    t = pl.program_id(1)
    Lt = n_sub * L
    rows = G * Lt

    @pl.when(t == 0)
    def _():
        if has_init:
            c_ref[...] = c0_ref[...]
            n_ref[...] = n0_ref[...]
            m_ref[...] = m0_ref[...]
        else:
            c_ref[...] = jnp.zeros_like(c_ref)
            n_ref[...] = jnp.zeros_like(n_ref)
            m_ref[...] = jnp.zeros_like(m_ref)

    z = gs_ref[...] + bias_ref[...]
    lane = lax.broadcasted_iota(jnp.int32, z.shape, 1)
    log_f = jnp.minimum(z, 0.0) - jnp.log1p(jnp.exp(-jnp.abs(z)))
    g = jnp.where((lane >= N_HEADS) & (lane < 2 * N_HEADS), log_f, z)
    if s_valid < s_total:
        pad = _pad_rows(z.shape, t, Lt, s_valid)
        g = jnp.where(pad, jnp.where(lane < N_HEADS, NEG_BIG, 0.0), g)
    g_t = _lanes_to_rows(g, 0)
    bt_cols = _dot_exact(_tri_blocks(rows, L).astype(F32), g)
    bt_rows = _dot_exact(g_t, _tri_blocks(rows, L, upper=True).astype(F32))
    causal = _tri(L)
    scale = HEAD_DIM ** -0.5
    probs = [(s, h) for s in range(G) for h in range(N_HEADS)]

    for c in range(n_sub):
        P = []
        for s, h in probs:
            r = slice(s * Lt + c * L, s * Lt + (c + 1) * L)
            sl = slice(h * HEAD_DIM, (h + 1) * HEAD_DIM)
            p = dict(s=s, h=h, r=r, sl=sl, q=q_ref[r, sl], k=k_ref[r, sl] * scale, v=v_ref[r, sl],
                     bt_c=bt_cols[r, N_HEADS + h:N_HEADS + h + 1], it_c=g[r, h:h + 1])
            bt_r = bt_rows[N_HEADS + h:N_HEADS + h + 1, r]
            it_r = g_t[h:h + 1, r]
            p["log_d"] = jnp.where(causal, p["bt_c"] - bt_r + it_r, -jnp.inf)
            p["c_prev"] = c_ref[s, h]
            p["n_prev"] = n_ref[s, h:h + 1, :]
            p["m_prev"] = m_ref[s, h:h + 1, 0:1]
            P.append(p)
        for p in P:
            p["qk"] = _dot_nt(p["q"], p["k"])
        for p in P:
            p["qc"] = _dot(p["q"], p["c_prev"])
        for p in P:
            inter = p["bt_c"] + p["m_prev"]
            p["m_t"] = jnp.maximum(inter, jnp.max(p["log_d"], axis=-1, keepdims=True))
            p["inter_w"] = jnp.exp(inter - p["m_t"])
            p["sd"] = p["qk"] * jnp.exp(p["log_d"] - p["m_t"])
        for p in P:
            p["sv"] = _dot(p["sd"], p["v"])
        for p in P:
            bt_last = p["bt_c"][L - 1:L, :]
            p["m_new"] = p["m_t"][L - 1:L, :]
            w = jnp.exp(bt_last - p["bt_c"] + p["it_c"] - p["m_new"])
            p["decay"] = jnp.exp(bt_last + p["m_prev"] - p["m_new"])
            p["kw"] = p["k"] * w
        for p in P:
            p["kv"] = _dot_tn(p["kw"], p["v"])
        for p in P:
            num = p["sv"] + p["qc"] * p["inter_w"]
            den = (jnp.sum(p["sd"], axis=-1, keepdims=True)
                   + p["inter_w"] * jnp.sum(p["q"] * p["n_prev"], axis=-1, keepdims=True))
            hh = num / jnp.maximum(jnp.abs(den), jnp.exp(-p["m_t"]))
            hn = hh * lax.rsqrt(jnp.mean(hh * hh, axis=-1, keepdims=True) + EPS) * norm_ref[:, p["sl"]]
            p["h_out"] = hn * _sigmoid(og_ref[p["r"], p["sl"]])
            p["n_new"] = p["decay"] * p["n_prev"] + jnp.sum(p["kw"], axis=0, keepdims=True)
        for p in P:
            s, h = p["s"], p["h"]
            h_ref[p["r"], p["sl"]] = p["h_out"]
            n_ref[s, h:h + 1, :] = p["n_new"]
            m_ref[s, h:h + 1, :] = jnp.broadcast_to(p["m_new"], (1, LANES))
            c_ref[s, h] = p["decay"] * p["c_prev"] + p["kv"]


def _mlstm(proj, gs, bias_row, norm_row, init, *, n_batch, seq, s_valid, row_off, L, n_sub, G):
    Lt = n_sub * L
    nt = seq // Lt
    rows = G * Lt
    has_init = init is not None
    blk = _seq_tile_specs(G, Lt, nt, row_off)

    def col(c):
        return pl.BlockSpec((rows, BRANCH_W), lambda b, t: (blk(b, t), c))

    in_specs = [col(COL_MQ), col(COL_MK), col(COL_MV), col(COL_MO),
                pl.BlockSpec((rows, LANES), lambda b, t: (blk(b, t), 0)),
                pl.BlockSpec((1, LANES), lambda b, t: (0, 0)),
                pl.BlockSpec((1, BRANCH_W), lambda b, t: (0, 0))]
    args = [proj, proj, proj, proj, gs, bias_row, norm_row]
    c_spec = pl.BlockSpec((G, N_HEADS, HEAD_DIM, HEAD_DIM), lambda b, t: (b, 0, 0, 0))
    n_spec = pl.BlockSpec((G, N_HEADS, HEAD_DIM), lambda b, t: (b, 0, 0))
    m_spec = pl.BlockSpec((G, N_HEADS, LANES), lambda b, t: (b, 0, 0))
    if has_init:
        in_specs += [c_spec, n_spec, m_spec]
        args += list(init)
    return pl.pallas_call(
        functools.partial(_mlstm_kernel, L=L, n_sub=n_sub, G=G, s_valid=s_valid, s_total=seq,
                          has_init=has_init),
        out_shape=(jax.ShapeDtypeStruct((n_batch * seq, BRANCH_W), F32),
                   jax.ShapeDtypeStruct((n_batch, N_HEADS, HEAD_DIM, HEAD_DIM), F32),
                   jax.ShapeDtypeStruct((n_batch, N_HEADS, HEAD_DIM), F32),
                   jax.ShapeDtypeStruct((n_batch, N_HEADS, LANES), F32)),
        grid=(n_batch // G, nt),
        in_specs=in_specs,
        out_specs=(pl.BlockSpec((rows, BRANCH_W), lambda b, t: (b * nt + t, 0)),
                   c_spec, n_spec, m_spec),
        compiler_params=_params(("parallel", "arbitrary")),
        name="mlstm",
    )(*args)


def _gdn_kernel(*refs, L, n_sub, G, s_valid, s_total, has_init):
    if has_init:
        (q_ref, k_ref, v_ref, og_ref, gs_ref, cw_ref, arow_ref, dtrow_ref, norm_ref,
         conv0_ref, s0_ref, o_ref, s_ref, conv1_ref, ext_ref, conv_ref) = refs
    else:
        (q_ref, k_ref, v_ref, og_ref, gs_ref, cw_ref, arow_ref, dtrow_ref, norm_ref,
         o_ref, s_ref, conv1_ref, ext_ref, conv_ref) = refs
    t = pl.program_id(1)
    Lt = n_sub * L
    rows = G * Lt
    nt = s_total // Lt
    W = BRANCH_W

    @pl.when(t == 0)
    def _():
        if has_init:
            s_ref[...] = s0_ref[...]
            ext_ref[:, 0:SUBLANES, :] = conv0_ref[...]
        else:
            s_ref[...] = jnp.zeros_like(s_ref)
            ext_ref[:, 0:SUBLANES, :] = jnp.zeros((G, SUBLANES, 3 * W), F32)

    @pl.when(t > 0)
    def _():
        ext_ref[:, 0:SUBLANES, :] = ext_ref[:, Lt:Lt + SUBLANES, :]

    base = SUBLANES - (CONV_W - 1)
    for s in range(G):
        rs = slice(s * Lt, (s + 1) * Lt)
        ext_ref[s, SUBLANES:SUBLANES + Lt, 0:W] = q_ref[rs, :]
        ext_ref[s, SUBLANES:SUBLANES + Lt, W:2 * W] = k_ref[rs, :]
        ext_ref[s, SUBLANES:SUBLANES + Lt, 2 * W:3 * W] = v_ref[rs, :]
        acc = ext_ref[s, base:base + Lt, :] * cw_ref[0:1, :]
        for j in range(1, CONV_W):
            acc = acc + ext_ref[s, base + j:base + j + Lt, :] * cw_ref[j:j + 1, :]
        conv_ref[rs, :] = _silu(acc)

    @pl.when(t == nt - 1)
    def _():
        v_last = s_valid - (nt - 1) * Lt
        conv1_ref[...] = ext_ref[:, v_last:v_last + SUBLANES, :]

    z = gs_ref[...]
    lane = lax.broadcasted_iota(jnp.int32, z.shape, 1)
    gdec = arow_ref[...] * _softplus(z + dtrow_ref[...])
    gb = jnp.where((lane >= 2 * N_HEADS) & (lane < 3 * N_HEADS), gdec,
                   jnp.where((lane >= 3 * N_HEADS) & (lane < 4 * N_HEADS), _sigmoid(z), 0.0))
    if s_valid < s_total:
        gb = jnp.where(_pad_rows(z.shape, t, Lt, s_valid), 0.0, gb)
    gb_t = _lanes_to_rows(gb, 2 * N_HEADS)
    gam_cols = _dot_exact(_tri_blocks(rows, L).astype(F32), gb)
    gam_rows = _dot_exact(gb_t, _tri_blocks(rows, L, upper=True).astype(F32))
    incl = _tri(L)
    strict = _tri(L, strict=True)
    eye = (lax.broadcasted_iota(jnp.int32, (L, L), 0)
           == lax.broadcasted_iota(jnp.int32, (L, L), 1)).astype(F32)

    P = []
    for s in range(G):
        for c in range(n_sub):
            for h in range(N_HEADS):
                r = slice(s * Lt + c * L, s * Lt + (c + 1) * L)
                sl = slice(h * HEAD_DIM, (h + 1) * HEAD_DIM)
                q = conv_ref[r, h * HEAD_DIM:(h + 1) * HEAD_DIM]
                k = conv_ref[r, W + h * HEAD_DIM:W + (h + 1) * HEAD_DIM]
                v = conv_ref[r, 2 * W + h * HEAD_DIM:2 * W + (h + 1) * HEAD_DIM]
                q = q * lax.rsqrt(jnp.sum(q * q, axis=-1, keepdims=True) + EPS) * (HEAD_DIM ** -0.5)
                k = k * lax.rsqrt(jnp.sum(k * k, axis=-1, keepdims=True) + EPS)
                gam_c = gam_cols[r, 2 * N_HEADS + h:2 * N_HEADS + h + 1]
                gam_r = gam_rows[h:h + 1, r]
                beta_c = gb[r, 3 * N_HEADS + h:3 * N_HEADS + h + 1]
                decay = jnp.where(incl, jnp.exp(jnp.where(incl, gam_c - gam_r, 0.0)), 0.0)
                P.append(dict(s=s, c=c, h=h, r=r, sl=sl, q=q, k=k, v=v, gam_c=gam_c,
                              beta_c=beta_c, decay=decay))
    for p in P:
        p["kk"] = _dot_nt(p["k"], p["k"])
    for p in P:
        p["qk"] = _dot_nt(p["q"], p["k"]) * p["decay"]
    for p in P:
        e_gam = jnp.exp(p["gam_c"])
        gam_last = p["gam_c"][L - 1:L, :]
        p["pw"] = -jnp.where(strict, p["beta_c"] * p["kk"] * p["decay"], 0.0)
        p["x"] = eye + p["pw"]
        p["rhs"] = jnp.concatenate([p["beta_c"] * p["v"], p["beta_c"] * p["k"] * e_gam], axis=-1)
        p["q_dec"] = p["q"] * e_gam
        p["k_dec"] = p["k"] * jnp.exp(gam_last - p["gam_c"])
        p["e_last"] = jnp.exp(gam_last)
    n = 2
    while n < L:
        for p in P:
            p["pw"] = _dot(p["pw"], p["pw"])
        for p in P:
            p["x"] = p["x"] + _dot(p["x"], p["pw"])
        n *= 2
    for p in P:
        sol = _dot(p["x"], p["rhs"])
        p["u"] = sol[:, :HEAD_DIM]
        p["w"] = sol[:, HEAD_DIM:]

    for c in range(n_sub):
        Pc = [p for p in P if p["c"] == c]
        for p in Pc:
            p["s_prev"] = s_ref[p["s"], p["h"]]
            p["ws"] = _dot(p["w"], p["s_prev"])
        for p in Pc:
            p["qs"] = _dot(p["q_dec"], p["s_prev"])
        for p in Pc:
            p["v_new"] = p["u"] - p["ws"]
        for p in Pc:
            p["o"] = p["qs"] + _dot(p["qk"], p["v_new"])
        for p in Pc:
            p["kv"] = _dot_tn(p["k_dec"], p["v_new"])
        for p in Pc:
            o = p["o"]
            on = o * lax.rsqrt(jnp.mean(o * o, axis=-1, keepdims=True) + EPS) * norm_ref[...]
            p["o_out"] = on * _silu(og_ref[p["r"], p["sl"]])
        for p in Pc:
            s_ref[p["s"], p["h"]] = p["e_last"] * p["s_prev"] + p["kv"]
            o_ref[p["r"], p["sl"]] = p["o_out"]


def _gdn(proj, gs, conv_w, a_row, dt_row, norm_row, init, *, n_batch, seq, s_valid, row_off,
         L, n_sub, G):
    Lt = n_sub * L
    nt = seq // Lt
    rows = G * Lt
    has_init = init is not None
    blk = _seq_tile_specs(G, Lt, nt, row_off)

    def col(c):
        return pl.BlockSpec((rows, BRANCH_W), lambda b, t: (blk(b, t), c))

    def const(shape):
        return pl.BlockSpec(shape, lambda b, t: (0,) * len(shape))

    in_specs = [col(COL_GQ), col(COL_GK), col(COL_GV), col(COL_GO),
                pl.BlockSpec((rows, LANES), lambda b, t: (blk(b, t), 0)),
                const((CONV_W, 3 * BRANCH_W)), const((1, LANES)), const((1, LANES)),
                const((1, HEAD_DIM))]
    args = [proj, proj, proj, proj, gs, conv_w, a_row, dt_row, norm_row]
    s_spec = pl.BlockSpec((G, N_HEADS, HEAD_DIM, HEAD_DIM), lambda b, t: (b, 0, 0, 0))
    conv_spec = pl.BlockSpec((G, SUBLANES, 3 * BRANCH_W), lambda b, t: (b, 0, 0))
    if has_init:
        in_specs += [conv_spec, s_spec]
        args += list(init)
    return pl.pallas_call(
        functools.partial(_gdn_kernel, L=L, n_sub=n_sub, G=G, s_valid=s_valid, s_total=seq,
                          has_init=has_init),
        out_shape=(jax.ShapeDtypeStruct((n_batch * seq, BRANCH_W), F32),
                   jax.ShapeDtypeStruct((n_batch, N_HEADS, HEAD_DIM, HEAD_DIM), F32),
                   jax.ShapeDtypeStruct((n_batch, SUBLANES, 3 * BRANCH_W), F32)),
        grid=(n_batch // G, nt),
        in_specs=in_specs,
        out_specs=(pl.BlockSpec((rows, BRANCH_W), lambda b, t: (b * nt + t, 0)),
                   s_spec, conv_spec),
        scratch_shapes=[pltpu.VMEM((G, Lt + SUBLANES, 3 * BRANCH_W), F32),
                        pltpu.VMEM((rows, 3 * BRANCH_W), F32)],
        compiler_params=_params(("parallel", "arbitrary")),
        name="gdn",
    )(*args)


def _xattn_kernel(q_ref, mk_ref, mv_ref, o_ref, *, G, lt, mem_len):
    scale = HEAD_DIM ** -0.5
    P = []
    for s in range(G):
        for h in range(N_HEADS):
            P.append(dict(r=slice(s * lt, (s + 1) * lt), m=slice(s * mem_len, (s + 1) * mem_len),
                          sl=slice(h * HEAD_DIM, (h + 1) * HEAD_DIM)))
    for p in P:
        p["s"] = _dot_nt(q_ref[p["r"], p["sl"]], mk_ref[p["m"], p["sl"]]) * scale
    for p in P:
        e = jnp.exp(p["s"] - jnp.max(p["s"], axis=-1, keepdims=True))
        p["p"] = e / jnp.sum(e, axis=-1, keepdims=True)
    for p in P:
        o_ref[p["r"], p["sl"]] = _dot(p["p"], mv_ref[p["m"], p["sl"]])


def _xattn(proj, mk, mv, *, n_batch, seq, row_off, lt, G, mem_len, k_col, v_col):
    nt = seq // lt
    blk = _seq_tile_specs(G, lt, nt, row_off)
    return pl.pallas_call(
        functools.partial(_xattn_kernel, G=G, lt=lt, mem_len=mem_len),
        out_shape=jax.ShapeDtypeStruct((n_batch * seq, BRANCH_W), F32),
        grid=(n_batch // G, nt),
        in_specs=[pl.BlockSpec((G * lt, BRANCH_W), lambda b, t: (blk(b, t), COL_XQ)),
                  pl.BlockSpec((G * mem_len, BRANCH_W), lambda b, t: (b, k_col)),
                  pl.BlockSpec((G * mem_len, BRANCH_W), lambda b, t: (b, v_col))],
        out_specs=pl.BlockSpec((G * lt, BRANCH_W), lambda b, t: (b * nt + t, 0)),
        compiler_params=_params(("parallel", "arbitrary")),
        name="xattn",
    )(proj, mk, mv)


def _lane_first_max(x, mask, lane_f):
    xm = jnp.where(mask, x, -jnp.inf)
    mx = jnp.max(xm, axis=-1, keepdims=True)
    idx = jnp.min(jnp.where(xm == mx, lane_f, float(LANES)), axis=-1, keepdims=True)
    return mx, idx


def _merge_kernel(hap_ref, has_ref, hbp_ref, hbs_ref, hcp_ref, hcs_ref, xp_ref, xs_ref,
                  ga_ref, gb_ref, gc_ref, wa_ref, wb_ref, wc_ref, wo_ref, nf_ref, wr_ref, br_ref,
                  x1_ref, *, n_first):
    i = pl.program_id(0)
    d = xp_ref.shape[1]

    def body(ha_ref, hb_ref, hc_ref, x_ref):
        mixed = (_sigmoid(ga_ref[...]) * _dot(ha_ref[...], wa_ref[...])
                 + _sigmoid(gb_ref[...]) * _dot(hb_ref[...], wb_ref[...])
                 + _sigmoid(gc_ref[...]) * _dot(hc_ref[...], wc_ref[...]))
        x1 = x_ref[...] + _dot(mixed, wo_ref[...])
        h2 = _rms(x1, nf_ref[...])
        logits = _dot(h2, wr_ref[...]) + br_ref[...]
        lane = lax.broadcasted_iota(jnp.int32, logits.shape, 1)
        lane_f = lane.astype(F32)
        is_g = lane < N_GROUPS
        g_max, g_idx = _lane_first_max(logits, is_g, lane_f)
        g_den = jnp.sum(jnp.where(is_g, jnp.exp(logits - g_max), 0.0), axis=-1, keepdims=True)
        p_grp = 1.0 / g_den
        e_lo = N_GROUPS + EXPERTS_PER_GROUP * g_idx
        in_grp = (lane_f >= e_lo) & (lane_f < e_lo + EXPERTS_PER_GROUP)
        v1, i1 = _lane_first_max(logits, in_grp, lane_f)
        v2, i2 = _lane_first_max(logits, in_grp & (lane_f != i1), lane_f)
        e2 = jnp.exp(v2 - v1)
        w1 = 1.0 / (1.0 + e2)
        w2 = e2 / (1.0 + e2)
        comb = p_grp * (jnp.where(lane_f == i1, w1, 0.0) + jnp.where(lane_f == i2, w2, 0.0))
        lo = jnp.minimum(i1, i2) - e_lo
        hi = jnp.maximum(i1, i2) - e_lo
        pair = lo * (7.0 - lo) * 0.5 + (hi - lo - 1.0)
        cls = g_idx * float(N_PAIRS) + pair
        x1_ref[:, 0:d] = x1
        x1_ref[:, d:d + ROUTE_W] = jnp.where(lane == 0, cls, comb)

    @pl.when(i < n_first)
    def _():
        body(hap_ref, hbp_ref, hcp_ref, xp_ref)

    @pl.when(i >= n_first)
    def _():
        body(has_ref, hbs_ref, hcs_ref, xs_ref)


def _merge(ha, hb, hc, xs, proj, wa, wb, wc, wo, nf, wr, br, *, tm):
    d = xs[0].shape[1]
    n_first = xs[0].shape[0] // tm
    n_rows = xs[0].shape[0] + xs[1].shape[0]
    specs = []
    for width in (BRANCH_W, BRANCH_W, BRANCH_W, d):
        specs += list(_two_source_specs(n_first, tm, width))

    def const(a):
        return pl.BlockSpec(a.shape, lambda i: (0,) * a.ndim)

    specs += [pl.BlockSpec((tm, d), lambda i, c=c: (i, COL_GATE + c)) for c in range(3)]
    weights = [wa, wb, wc, wo, nf, wr, br]
    specs += [const(a) for a in weights]
    return pl.pallas_call(
        functools.partial(_merge_kernel, n_first=n_first),
        out_shape=jax.ShapeDtypeStruct((n_rows, d + ROUTE_W), F32),
        grid=(n_rows // tm,),
        in_specs=specs,
        out_specs=pl.BlockSpec((tm, d + ROUTE_W), lambda i: (i, 0)),
        compiler_params=_params(("parallel",)),
        name="merge",
    )(*ha, *hb, *hc, *xs, proj, proj, proj, *weights)


GATHER_UNROLL = 8


def _row_gather_start(src_ref, dst_ref, sem, ids_ref, base, n_rows):
    def issue(r, carry):
        pltpu.make_async_copy(src_ref.at[pl.ds(ids_ref[base + r], 1)],
                              dst_ref.at[pl.ds(r, 1)], sem).start()
        return carry
    lax.fori_loop(0, n_rows, issue, 0, unroll=GATHER_UNROLL)


def _row_gather_wait(src_ref, dst_ref, sem, n_rows):
    pltpu.make_async_copy(src_ref.at[pl.ds(0, n_rows)], dst_ref, sem).wait()


def _moe_kernel(ids_ref, ea_ref, eb_ref, nused_ref, x_hbm, wga_ref, wua_ref, wda_ref,
                wgb_ref, wub_ref, wdb_ref, nf_ref, fin_ref, y_ref, xbuf, sem, *, tm, d):
    i = pl.program_id(0)
    n_used = nused_ref[0]
    slot = i % 2

    @pl.when(i == 0)
    def _():
        _row_gather_start(x_hbm, xbuf.at[0], sem.at[0], ids_ref, 0, tm)

    @pl.when(i + 1 < n_used)
    def _():
        _row_gather_start(x_hbm, xbuf.at[1 - slot], sem.at[1 - slot], ids_ref, (i + 1) * tm, tm)

    @pl.when(i < n_used)
    def _():
        _row_gather_wait(x_hbm, xbuf.at[slot], sem.at[slot], tm)
        x1 = xbuf[slot, :, 0:d]
        route = xbuf[slot, :, d:d + ROUTE_W]
        lane = lax.broadcasted_iota(jnp.int32, route.shape, 1)
        h2 = _rms(x1, nf_ref[...]).astype(BF16)
        acc = jnp.zeros((tm, d), F32)
        for e_ref, wg_ref, wu_ref, wd_ref in ((ea_ref, wga_ref, wua_ref, wda_ref),
                                              (eb_ref, wgb_ref, wub_ref, wdb_ref)):
            comb = jnp.sum(jnp.where(lane == N_GROUPS + e_ref[i], route, 0.0),
                           axis=-1, keepdims=True)
            a = jnp.dot(h2, wg_ref[0], preferred_element_type=F32)
            u = jnp.dot(h2, wu_ref[0], preferred_element_type=F32)
            act = _silu(a) * u * comb
            acc = acc + _dot(act, wd_ref[0])
        y_ref[...] = _rms(x1 + acc, fin_ref[...])

    @pl.when(i >= n_used)
    def _():
        y_ref[...] = jnp.zeros_like(y_ref)


def _moe(ids, tile_ea, tile_eb, n_used, x1aug, wg, wu, wd, nf, fin, *, tm, n_tiles):
    d = nf.shape[1]
    f = wg.shape[2]

    def expert(shape, which):
        return pl.BlockSpec((1,) + shape, lambda i, ids, ea, eb, nu: ((ea, eb)[which][i], 0, 0))

    def const(a):
        return pl.BlockSpec(a.shape, lambda i, *_: (0,) * a.ndim)

    grid_spec = pltpu.PrefetchScalarGridSpec(
        num_scalar_prefetch=4,
        grid=(n_tiles,),
        in_specs=[pl.BlockSpec(memory_space=pl.ANY),
                  expert((d, f), 0), expert((d, f), 0), expert((f, d), 0),
                  expert((d, f), 1), expert((d, f), 1), expert((f, d), 1),
                  const(nf), const(fin)],
        out_specs=pl.BlockSpec((tm, d), lambda i, *_: (i, 0)),
        scratch_shapes=[pltpu.VMEM((2, tm, d + ROUTE_W), F32),
                        pltpu.SemaphoreType.DMA((2,))],
    )
    return pl.pallas_call(
        functools.partial(_moe_kernel, tm=tm, d=d),
        out_shape=jax.ShapeDtypeStruct((n_tiles * tm, d), F32),
        grid_spec=grid_spec,
        compiler_params=_params(("arbitrary",)),
        name="moe",
    )(ids, tile_ea, tile_eb, n_used, x1aug, wg, wu, wd, wg, wu, wd, nf, fin)


def _unsort_kernel(pos_ref, y_hbm, o_ref, sem, *, tm, base):
    i = pl.program_id(0)
    _row_gather_start(y_hbm, o_ref, sem, pos_ref, base + i * tm, tm)
    _row_gather_wait(y_hbm, o_ref, sem, tm)


def _unsort(pos, y_sorted, *, base, n_rows, tm):
    d = y_sorted.shape[1]
    grid_spec = pltpu.PrefetchScalarGridSpec(
        num_scalar_prefetch=1,
        grid=(n_rows // tm,),
        in_specs=[pl.BlockSpec(memory_space=pl.ANY)],
        out_specs=pl.BlockSpec((tm, d), lambda i, *_: (i, 0)),
        scratch_shapes=[pltpu.SemaphoreType.DMA(())],
    )
    return pl.pallas_call(
        functools.partial(_unsort_kernel, tm=tm, base=base),
        out_shape=jax.ShapeDtypeStruct((n_rows, d), F32),
        grid_spec=grid_spec,
        compiler_params=_params(("arbitrary",)),
        name="unsort",
    )(pos, y_sorted)


def _sort_plan(cls, *, tm, n_tiles):
    n = cls.shape[0]
    onehot = (cls[:, None] == jnp.arange(N_CLASSES, dtype=jnp.int32)[None, :]).astype(jnp.int32)
    csum = jnp.cumsum(onehot, axis=0)
    rank = jnp.sum((csum - onehot) * onehot, axis=1)
    counts = csum[-1]
    tiles_per = (counts + tm - 1) // tm
    tile_end = jnp.cumsum(tiles_per)
    tile_start = tile_end - tiles_per
    pos = (jnp.sum(tile_start[None, :] * onehot, axis=1) * tm + rank).astype(jnp.int32)
    n_used = tile_end[-1].astype(jnp.int32)
    ids = jnp.zeros((n_tiles * tm,), jnp.int32).at[pos].set(jnp.arange(n, dtype=jnp.int32))
    tile = jnp.minimum(jnp.arange(n_tiles, dtype=jnp.int32), n_used - 1)
    tile_cls = jnp.sum((tile[:, None] >= tile_end[None, :]).astype(jnp.int32), axis=1)
    grp = tile_cls // N_PAIRS
    pair = tile_cls % N_PAIRS
    lo = jnp.asarray(PAIR_LO, jnp.int32)[pair]
    hi = jnp.asarray(PAIR_HI, jnp.int32)[pair]
    ea = (grp * EXPERTS_PER_GROUP + lo).astype(jnp.int32)
    eb = (grp * EXPERTS_PER_GROUP + hi).astype(jnp.int32)
    return pos, ids, ea, eb, n_used.reshape(1)


def _lane_row(values, start):
    row = jnp.zeros((LANES,), F32)
    return row.at[start:start + values.shape[0]].set(values.astype(F32)).reshape(1, LANES)


def _layer(x_prompt, x_sample, mem_prompt, cache_mem_k, cache_mem_v,
           state_mlstm_C, state_mlstm_n, state_mlstm_m, state_gdn_S, state_gdn_conv,
           norm_mix, w_in, b_igate, b_fgate, mlstm_norm, conv_w, a_log, dt_bias, gdn_norm,
           mem_norm, w_mem_kv, w_br_mlstm, w_br_gdn, w_br_xattn, w_out, norm_ffn,
           w_router_group, b_router_group, w_router_expert, b_router_expert,
           w_exp_gate, w_exp_up, w_exp_down, final_norm, *, cfg):
    bp, sp, d = x_prompt.shape
    bs, ss, _ = x_sample.shape
    mem_len = mem_prompt.shape[1]
    ss_pad = SUBLANES
    assert ss <= ss_pad and d % LANES == 0
    n_p = bp * sp
    n_s = bs * ss_pad
    W = BRANCH_W
    gs_ = cfg["sample_group"]

    sizes = (W, W, W, N_HEADS, N_HEADS, W, 3 * W, N_HEADS, N_HEADS, W, W, d, d, d)
    offs = [0]
    for s in sizes:
        offs.append(offs[-1] + s)
    (m_q, m_k, m_v, m_i, m_f, m_o, g_qkv, g_a, g_b, g_out, x_q, gate_a, gate_b, gate_c) = [
        w_in[:, offs[j]:offs[j + 1]] for j in range(len(sizes))]
    w_big = jnp.stack(
        [gate_a[:, :W], gate_a[:, W:], gate_b[:, :W], gate_b[:, W:], gate_c[:, :W], gate_c[:, W:],
         m_q, m_k, m_v, m_o, g_qkv[:, :W], g_qkv[:, W:2 * W], g_qkv[:, 2 * W:], g_out, x_q],
        axis=0).astype(BF16)
    w_small = jnp.concatenate(
        [m_i, m_f, g_a, g_b, jnp.zeros((d, LANES - 4 * N_HEADS), w_in.dtype)], axis=1).astype(BF16)
    w_router = jnp.concatenate(
        [w_router_group, w_router_expert,
         jnp.zeros((d, LANES - N_GROUPS - N_EXPERTS), w_in.dtype)], axis=1).astype(BF16)
    b_router = _lane_row(jnp.concatenate([b_router_group, b_router_expert]), 0)
    gate_bias = _lane_row(jnp.concatenate([b_igate, b_fgate]), 0)
    a_row = _lane_row(-jnp.exp(a_log.astype(F32)), 2 * N_HEADS)
    dt_row = _lane_row(dt_bias, 2 * N_HEADS)

    xp2 = x_prompt.reshape(n_p, d)
    xs2 = jnp.pad(x_sample, ((0, 0), (0, ss_pad - ss), (0, 0))).reshape(n_s, d)

    proj, gs = _norm_proj(xp2, xs2, norm_mix.reshape(1, d), w_big, w_small, tm=cfg["tm_proj"])

    mem2 = mem_prompt.reshape(bp * mem_len, d)
    mem_kv, _ = _norm_proj(mem2, None, mem_norm.reshape(1, d),
                           _col_blocks(w_mem_kv.astype(BF16), W),
                           jnp.zeros((d, LANES), BF16), tm=mem_len)

    mnorm = mlstm_norm.reshape(1, W)
    gnorm = gdn_norm.reshape(1, HEAD_DIM)
    s_off = n_p // (gs_ * ss_pad)
    ha_p, c_p, nn_p, mm_p = _mlstm(proj, gs, gate_bias, mnorm, None, n_batch=bp, seq=sp,
                                   s_valid=sp, row_off=0, L=cfg["chunk_m"], n_sub=cfg["sub_m"], G=1)
    m0 = jnp.broadcast_to(state_mlstm_m[:, :, None], (bs, N_HEADS, LANES))
    ha_s, c_s, nn_s, mm_s = _mlstm(proj, gs, gate_bias, mnorm,
                                   (state_mlstm_C, state_mlstm_n, m0), n_batch=bs, seq=ss_pad,
                                   s_valid=ss, row_off=s_off, L=ss_pad, n_sub=1, G=gs_)
    hb_p, s_p, cv_p = _gdn(proj, gs, conv_w, a_row, dt_row, gnorm, None, n_batch=bp, seq=sp,
                           s_valid=sp, row_off=0, L=cfg["chunk_g"], n_sub=cfg["sub_g"], G=1)
    conv0 = jnp.pad(state_gdn_conv, ((0, 0), (SUBLANES - (CONV_W - 1), 0), (0, 0)))
    hb_s, s_s, cv_s = _gdn(proj, gs, conv_w, a_row, dt_row, gnorm, (conv0, state_gdn_S),
                           n_batch=bs, seq=ss_pad, s_valid=ss, row_off=s_off, L=ss_pad,
                           n_sub=1, G=gs_)
    lt = min(sp, cfg["lt_x"])
    hc_p = _xattn(proj, mem_kv, mem_kv, n_batch=bp, seq=sp, row_off=0, lt=lt, G=1,
                  mem_len=mem_len, k_col=0, v_col=1)
    ck = cache_mem_k.reshape(bs * mem_len, W)
    cv = cache_mem_v.reshape(bs * mem_len, W)
    hc_s = _xattn(proj, ck, cv, n_batch=bs, seq=ss_pad, row_off=s_off, lt=ss_pad, G=gs_,
                  mem_len=mem_len, k_col=0, v_col=0)

    tm = cfg["tm"]
    x1aug = _merge((ha_p, ha_s), (hb_p, hb_s), (hc_p, hc_s), (xp2, xs2), proj,
                   w_br_mlstm.astype(BF16), w_br_gdn.astype(BF16), w_br_xattn.astype(BF16),
                   w_out.astype(BF16), norm_ffn.reshape(1, d), w_router, b_router, tm=tm)

    tm_moe = cfg["tm_moe"]
    n_all = n_p + n_s
    n_tiles = (n_all + N_CLASSES * (tm_moe - 1)) // tm_moe + 1
    cls = x1aug[:, d].astype(jnp.int32)
    pos, ids, ea, eb, n_used = _sort_plan(cls, tm=tm_moe, n_tiles=n_tiles)
    y_sorted = _moe(ids, ea, eb, n_used, x1aug, w_exp_gate.astype(BF16), w_exp_up.astype(BF16),
                    w_exp_down.astype(BF16), norm_ffn.reshape(1, d), final_norm.reshape(1, d),
                    tm=tm_moe, n_tiles=n_tiles)
    y_p = _unsort(pos, y_sorted, base=0, n_rows=n_p, tm=tm)
    y_s = _unsort(pos, y_sorted, base=n_p, n_rows=n_s, tm=tm)

    y_prompt = y_p.reshape(bp, sp, d)
    y_sample = y_s.reshape(bs, ss_pad, d)[:, :ss]
    mk_p = mem_kv[:, :W].reshape(bp, mem_len, N_HEADS, HEAD_DIM)
    mv_p = mem_kv[:, W:].reshape(bp, mem_len, N_HEADS, HEAD_DIM)
    tail = slice(SUBLANES - (CONV_W - 1), SUBLANES)
    return (y_prompt, y_sample, mk_p, mv_p,
            c_p, nn_p, mm_p[:, :, 0], s_p, cv_p[:, tail],
            c_s, nn_s, mm_s[:, :, 0], s_s, cv_s[:, tail])


CONFIG = dict(tm_proj=1024, tm=512, tm_moe=256, chunk_m=128, sub_m=2, chunk_g=64, sub_g=4,
              lt_x=512, sample_group=8)


def kernel(x_prompt, x_sample, mem_prompt, cache_mem_k, cache_mem_v, state_mlstm_C, state_mlstm_n, state_mlstm_m, state_gdn_S, state_gdn_conv, norm_mix, w_in, b_igate, b_fgate, mlstm_norm, conv_w, a_log, dt_bias, gdn_norm, mem_norm, w_mem_kv, w_br_mlstm, w_br_gdn, w_br_xattn, w_out, norm_ffn, w_router_group, b_router_group, w_router_expert, b_router_expert, w_exp_gate, w_exp_up, w_exp_down, final_norm):
    depth = w_in.shape[0]
    assert depth == 1, "one decoder layer"
    outs = _layer(
        x_prompt, x_sample, mem_prompt, cache_mem_k[0], cache_mem_v[0],
        state_mlstm_C[0], state_mlstm_n[0], state_mlstm_m[0], state_gdn_S[0], state_gdn_conv[0],
        norm_mix[0], w_in[0], b_igate[0], b_fgate[0], mlstm_norm[0], conv_w[0], a_log[0],
        dt_bias[0], gdn_norm[0], mem_norm[0], w_mem_kv[0], w_br_mlstm[0], w_br_gdn[0],
        w_br_xattn[0], w_out[0], norm_ffn[0], w_router_group[0], b_router_group[0],
        w_router_expert[0], b_router_expert[0], w_exp_gate[0], w_exp_up[0], w_exp_down[0],
        final_norm, cfg=CONFIG)
    y_prompt, y_sample = outs[0], outs[1]
    return (y_prompt, y_sample) + tuple(o[None] for o in outs[2:])
```

```python
import functools

import jax
import jax.numpy as jnp
from jax import lax
from jax.experimental import pallas as pl
from jax.experimental.pallas import tpu as pltpu

F32 = jnp.float32
BF16 = jnp.bfloat16
EPS = 1e-6

N_HEADS = 4
HEAD_DIM = 128
BRANCH_W = N_HEADS * HEAD_DIM
CONV_W = 4
N_GROUPS = 4
EXPERTS_PER_GROUP = 4
N_EXPERTS = N_GROUPS * EXPERTS_PER_GROUP
N_PAIRS = 6
N_CLASSES = N_GROUPS * N_PAIRS
PAIR_LO = (0, 0, 0, 1, 1, 2)
PAIR_HI = (1, 2, 3, 2, 3, 3)
NEG_BIG = -1e30
LANES = 128
SUBLANES = 8
TOKEN_ROWS = 16
ROUTE_ROW = 8
VMEM_LIMIT = 48 * 1024 * 1024

COL_GATE = 0
COL_MQ, COL_MK, COL_MV, COL_MO = 6, 7, 8, 9
COL_GQ, COL_GK, COL_GV, COL_GO, COL_XQ = 10, 11, 12, 13, 14
N_PROJ = 15 * BRANCH_W


def _dot(a, b):
    return jnp.dot(a.astype(BF16), b.astype(BF16), preferred_element_type=F32)


def _dot_nt(a, b):
    return lax.dot_general(a.astype(BF16), b.astype(BF16), (((1,), (1,)), ((), ())),
                           preferred_element_type=F32)


def _dot_tn(a, b):
    return lax.dot_general(a.astype(BF16), b.astype(BF16), (((0,), (0,)), ((), ())),
                           preferred_element_type=F32)


def _dot_exact(a, b):
    return jnp.dot(a, b, preferred_element_type=F32, precision=lax.Precision.HIGHEST)


def _lanes_to_rows(x, lane0):
    r = lax.broadcasted_iota(jnp.int32, (SUBLANES, LANES), 0)
    c = lax.broadcasted_iota(jnp.int32, (SUBLANES, LANES), 1)
    sel = (c == r + lane0).astype(F32)
    return lax.dot_general(sel, x, (((1,), (1,)), ((), ())), preferred_element_type=F32,
                           precision=lax.Precision.HIGHEST)


def _rms(x, g):
    return x * lax.rsqrt(jnp.mean(x * x, axis=-1, keepdims=True) + EPS) * g


def _softplus(x):
    return jnp.maximum(x, 0.0) + jnp.log1p(jnp.exp(-jnp.abs(x)))


def _sigmoid(x):
    return 1.0 / (1.0 + jnp.exp(-x))


def _silu(x):
    return x * _sigmoid(x)


def _tri(n, strict=False):
    r = lax.broadcasted_iota(jnp.int32, (n, n), 0)
    c = lax.broadcasted_iota(jnp.int32, (n, n), 1)
    return (c < r) if strict else (c <= r)


def _tri_blocks(n, block, upper=False):
    shift = block.bit_length() - 1
    assert block == 1 << shift
    r = lax.broadcasted_iota(jnp.int32, (n, n), 0)
    c = lax.broadcasted_iota(jnp.int32, (n, n), 1)
    same = jnp.right_shift(r, shift) == jnp.right_shift(c, shift)
    return same & ((r <= c) if upper else (c <= r))


def _pad_rows(shape, t, seq_rows, n_valid):
    assert seq_rows & (seq_rows - 1) == 0
    row = lax.broadcasted_iota(jnp.int32, shape, 0)
    return (row & (seq_rows - 1)) + t * seq_rows >= n_valid


def _params(sem):
    return pltpu.CompilerParams(dimension_semantics=sem, vmem_limit_bytes=VMEM_LIMIT)


def _two_source_specs(n_first, block, width):
    first = pl.BlockSpec((block, width), lambda i, *_: (jnp.minimum(i, n_first - 1), 0))
    second = pl.BlockSpec((block, width), lambda i, *_: (jnp.maximum(i - n_first, 0), 0))
    return first, second


def _norm_proj_kernel(xa_ref, xb_ref, g_ref, w_ref, ws_ref, o_ref, os_ref, hb_ref, *, n_first):
    i = pl.program_id(0)
    j = pl.program_id(1)

    @pl.when(j == 0)
    def _():
        @pl.when(i < n_first)
        def _():
            hb_ref[...] = _rms(xa_ref[...], g_ref[...]).astype(BF16)

        @pl.when(i >= n_first)
        def _():
            hb_ref[...] = _rms(xb_ref[...], g_ref[...]).astype(BF16)

        os_ref[...] = jnp.dot(hb_ref[...], ws_ref[...], preferred_element_type=F32)

    o_ref[...] = jnp.dot(hb_ref[...], w_ref[0], preferred_element_type=F32)


def _col_blocks(w, tn):
    d, n = w.shape
    return w.reshape(d, n // tn, tn).transpose(1, 0, 2)


def _norm_proj(xa, xb, g, w, ws, *, tm):
    d = xa.shape[1]
    n_first = xa.shape[0] // tm
    n_rows = xa.shape[0] + (0 if xb is None else xb.shape[0])
    if xb is None:
        xb = xa
    tn = w.shape[2]
    n = w.shape[0] * tn
    sa, sb = _two_source_specs(n_first, tm, d)
    return pl.pallas_call(
        functools.partial(_norm_proj_kernel, n_first=n_first),
        out_shape=(jax.ShapeDtypeStruct((n // tn, n_rows, tn), F32),
                   jax.ShapeDtypeStruct((n_rows, ws.shape[1]), F32)),
        grid=(n_rows // tm, n // tn),
        in_specs=[sa, sb,
                  pl.BlockSpec((1, d), lambda i, j: (0, 0)),
                  pl.BlockSpec((1, d, tn), lambda i, j: (j, 0, 0)),
                  pl.BlockSpec((d, ws.shape[1]), lambda i, j: (0, 0))],
        out_specs=(pl.BlockSpec((None, tm, tn), lambda i, j: (j, i, 0)),
                   pl.BlockSpec((tm, ws.shape[1]), lambda i, j: (i, 0))),
        scratch_shapes=[pltpu.VMEM((tm, d), BF16)],
        compiler_params=_params(("parallel", "arbitrary")),
        name="norm_proj",
    )(xa, xb, g, w, ws)


def _seq_tile_specs(G, Lt, nt, row_off):
    assert G == 1 or nt == 1

    def rows(b, t):
        return row_off + b * nt + t

    return rows


def _mlstm_kernel(*refs, L, n_sub, G, s_valid, s_total, has_init):
    if has_init:
        (q_ref, k_ref, v_ref, og_ref, gs_ref, bias_ref, norm_ref, c0_ref, n0_ref, m0_ref,
         h_ref, c_ref, n_ref, m_ref) = refs
    else:
        (q_ref, k_ref, v_ref, og_ref, gs_ref, bias_ref, norm_ref,
         h_ref, c_ref, n_ref, m_ref) = refs
    t = pl.program_id(1)
    Lt = n_sub * L
    rows = G * Lt

    @pl.when(t == 0)
    def _():
        if has_init:
            c_ref[...] = c0_ref[...]
            n_ref[...] = n0_ref[...]
            m_ref[...] = m0_ref[...]
        else:
            c_ref[...] = jnp.zeros_like(c_ref)
            n_ref[...] = jnp.zeros_like(n_ref)
            m_ref[...] = jnp.zeros_like(m_ref)

    z = gs_ref[...] + bias_ref[...]
    lane = lax.broadcasted_iota(jnp.int32, z.shape, 1)
    log_f = jnp.minimum(z, 0.0) - jnp.log1p(jnp.exp(-jnp.abs(z)))
    g = jnp.where((lane >= N_HEADS) & (lane < 2 * N_HEADS), log_f, z)
    if s_valid < s_total:
        pad = _pad_rows(z.shape, t, Lt, s_valid)
        g = jnp.where(pad, jnp.where(lane < N_HEADS, NEG_BIG, 0.0), g)
    g_t = _lanes_to_rows(g, 0)
    bt_cols = _dot_exact(_tri_blocks(rows, L).astype(F32), g)
    bt_rows = _dot_exact(g_t, _tri_blocks(rows, L, upper=True).astype(F32))
    causal = _tri(L)
    scale = HEAD_DIM ** -0.5
    probs = [(s, h) for s in range(G) for h in range(N_HEADS)]

    for c in range(n_sub):
        P = []
        for s, h in probs:
            r = slice(s * Lt + c * L, s * Lt + (c + 1) * L)
            sl = slice(h * HEAD_DIM, (h + 1) * HEAD_DIM)
            p = dict(s=s, h=h, r=r, sl=sl, q=q_ref[r, sl], k=k_ref[r, sl] * scale, v=v_ref[r, sl],
                     bt_c=bt_cols[r, N_HEADS + h:N_HEADS + h + 1], it_c=g[r, h:h + 1])
            bt_r = bt_rows[N_HEADS + h:N_HEADS + h + 1, r]
            it_r = g_t[h:h + 1, r]
            p["log_d"] = jnp.where(causal, p["bt_c"] - bt_r + it_r, -jnp.inf)
            p["c_prev"] = c_ref[s, h]
            p["n_prev"] = n_ref[s, h:h + 1, :]
            p["m_prev"] = m_ref[s, h:h + 1, 0:1]
            P.append(p)
        for p in P:
            p["qk"] = _dot_nt(p["q"], p["k"])
        for p in P:
            p["qc"] = _dot(p["q"], p["c_prev"])
        for p in P:
            inter = p["bt_c"] + p["m_prev"]
            p["m_t"] = jnp.maximum(inter, jnp.max(p["log_d"], axis=-1, keepdims=True))
            p["inter_w"] = jnp.exp(inter - p["m_t"])
            p["sd"] = p["qk"] * jnp.exp(p["log_d"] - p["m_t"])
        for p in P:
            p["sv"] = _dot(p["sd"], p["v"])
        for p in P:
            bt_last = p["bt_c"][L - 1:L, :]
            p["m_new"] = p["m_t"][L - 1:L, :]
            w = jnp.exp(bt_last - p["bt_c"] + p["it_c"] - p["m_new"])
            p["decay"] = jnp.exp(bt_last + p["m_prev"] - p["m_new"])
            p["kw"] = p["k"] * w
        for p in P:
            p["kv"] = _dot_tn(p["kw"], p["v"])
        for p in P:
            num = p["sv"] + p["qc"] * p["inter_w"]
            den = (jnp.sum(p["sd"], axis=-1, keepdims=True)
                   + p["inter_w"] * jnp.sum(p["q"] * p["n_prev"], axis=-1, keepdims=True))
            hh = num / jnp.maximum(jnp.abs(den), jnp.exp(-p["m_t"]))
            hn = hh * lax.rsqrt(jnp.mean(hh * hh, axis=-1, keepdims=True) + EPS) * norm_ref[:, p["sl"]]
            p["h_out"] = hn * _sigmoid(og_ref[p["r"], p["sl"]])
            p["n_new"] = p["decay"] * p["n_prev"] + jnp.sum(p["kw"], axis=0, keepdims=True)
        for p in P:
            s, h = p["s"], p["h"]
            h_ref[p["r"], p["sl"]] = p["h_out"]
            n_ref[s, h:h + 1, :] = p["n_new"]
            m_ref[s, h:h + 1, :] = jnp.broadcast_to(p["m_new"], (1, LANES))
            c_ref[s, h] = p["decay"] * p["c_prev"] + p["kv"]


def _mlstm(proj, gs, bias_row, norm_row, init, *, n_batch, seq, s_valid, row_off, L, n_sub, G):
    Lt = n_sub * L
    nt = seq // Lt
    rows = G * Lt
    has_init = init is not None
    blk = _seq_tile_specs(G, Lt, nt, row_off)

    def col(c):
        return pl.BlockSpec((None, rows, BRANCH_W), lambda b, t: (c, blk(b, t), 0))

    in_specs = [col(COL_MQ), col(COL_MK), col(COL_MV), col(COL_MO),
                pl.BlockSpec((rows, LANES), lambda b, t: (blk(b, t), 0)),
                pl.BlockSpec((1, LANES), lambda b, t: (0, 0)),
                pl.BlockSpec((1, BRANCH_W), lambda b, t: (0, 0))]
    args = [proj, proj, proj, proj, gs, bias_row, norm_row]
    c_spec = pl.BlockSpec((G, N_HEADS, HEAD_DIM, HEAD_DIM), lambda b, t: (b, 0, 0, 0))
    n_spec = pl.BlockSpec((G, N_HEADS, HEAD_DIM), lambda b, t: (b, 0, 0))
    m_spec = pl.BlockSpec((G, N_HEADS, LANES), lambda b, t: (b, 0, 0))
    if has_init:
        in_specs += [c_spec, n_spec, m_spec]
        args += list(init)
    return pl.pallas_call(
        functools.partial(_mlstm_kernel, L=L, n_sub=n_sub, G=G, s_valid=s_valid, s_total=seq,
                          has_init=has_init),
        out_shape=(jax.ShapeDtypeStruct((n_batch * seq, BRANCH_W), F32),
                   jax.ShapeDtypeStruct((n_batch, N_HEADS, HEAD_DIM, HEAD_DIM), F32),
                   jax.ShapeDtypeStruct((n_batch, N_HEADS, HEAD_DIM), F32),
                   jax.ShapeDtypeStruct((n_batch, N_HEADS, LANES), F32)),
        grid=(n_batch // G, nt),
        in_specs=in_specs,
        out_specs=(pl.BlockSpec((rows, BRANCH_W), lambda b, t: (b * nt + t, 0)),
                   c_spec, n_spec, m_spec),
        compiler_params=_params(("parallel", "arbitrary")),
        name="mlstm",
    )(*args)


def _gdn_kernel(*refs, L, n_sub, G, s_valid, s_total, has_init):
    if has_init:
        (q_ref, k_ref, v_ref, og_ref, gs_ref, cw_ref, arow_ref, dtrow_ref, norm_ref,
         conv0_ref, s0_ref, o_ref, s_ref, conv1_ref, ext_ref, conv_ref) = refs
    else:
        (q_ref, k_ref, v_ref, og_ref, gs_ref, cw_ref, arow_ref, dtrow_ref, norm_ref,
         o_ref, s_ref, conv1_ref, ext_ref, conv_ref) = refs
    t = pl.program_id(1)
    Lt = n_sub * L
    rows = G * Lt
    nt = s_total // Lt
    W = BRANCH_W

    @pl.when(t == 0)
    def _():
        if has_init:
            s_ref[...] = s0_ref[...]
            ext_ref[:, 0:SUBLANES, :] = conv0_ref[...]
        else:
            s_ref[...] = jnp.zeros_like(s_ref)
            ext_ref[:, 0:SUBLANES, :] = jnp.zeros((G, SUBLANES, 3 * W), F32)

    @pl.when(t > 0)
    def _():
        ext_ref[:, 0:SUBLANES, :] = ext_ref[:, Lt:Lt + SUBLANES, :]

    base = SUBLANES - (CONV_W - 1)
    for s in range(G):
        rs = slice(s * Lt, (s + 1) * Lt)
        ext_ref[s, SUBLANES:SUBLANES + Lt, 0:W] = q_ref[rs, :]
        ext_ref[s, SUBLANES:SUBLANES + Lt, W:2 * W] = k_ref[rs, :]
        ext_ref[s, SUBLANES:SUBLANES + Lt, 2 * W:3 * W] = v_ref[rs, :]
        acc = ext_ref[s, base:base + Lt, :] * cw_ref[0:1, :]
        for j in range(1, CONV_W):
            acc = acc + ext_ref[s, base + j:base + j + Lt, :] * cw_ref[j:j + 1, :]
        conv_ref[rs, :] = _silu(acc)

    @pl.when(t == nt - 1)
    def _():
        v_last = s_valid - (nt - 1) * Lt
        conv1_ref[...] = ext_ref[:, v_last:v_last + SUBLANES, :]

    z = gs_ref[...]
    lane = lax.broadcasted_iota(jnp.int32, z.shape, 1)
    gdec = arow_ref[...] * _softplus(z + dtrow_ref[...])
    gb = jnp.where((lane >= 2 * N_HEADS) & (lane < 3 * N_HEADS), gdec,
                   jnp.where((lane >= 3 * N_HEADS) & (lane < 4 * N_HEADS), _sigmoid(z), 0.0))
    if s_valid < s_total:
        gb = jnp.where(_pad_rows(z.shape, t, Lt, s_valid), 0.0, gb)
    gb_t = _lanes_to_rows(gb, 2 * N_HEADS)
    gam_cols = _dot_exact(_tri_blocks(rows, L).astype(F32), gb)
    gam_rows = _dot_exact(gb_t, _tri_blocks(rows, L, upper=True).astype(F32))
    incl = _tri(L)
    strict = _tri(L, strict=True)
    eye = (lax.broadcasted_iota(jnp.int32, (L, L), 0)
           == lax.broadcasted_iota(jnp.int32, (L, L), 1)).astype(F32)

    P = []
    for s in range(G):
        for c in range(n_sub):
            for h in range(N_HEADS):
                r = slice(s * Lt + c * L, s * Lt + (c + 1) * L)
                sl = slice(h * HEAD_DIM, (h + 1) * HEAD_DIM)
                q = conv_ref[r, h * HEAD_DIM:(h + 1) * HEAD_DIM]
                k = conv_ref[r, W + h * HEAD_DIM:W + (h + 1) * HEAD_DIM]
                v = conv_ref[r, 2 * W + h * HEAD_DIM:2 * W + (h + 1) * HEAD_DIM]
                q = q * lax.rsqrt(jnp.sum(q * q, axis=-1, keepdims=True) + EPS) * (HEAD_DIM ** -0.5)
                k = k * lax.rsqrt(jnp.sum(k * k, axis=-1, keepdims=True) + EPS)
                gam_c = gam_cols[r, 2 * N_HEADS + h:2 * N_HEADS + h + 1]
                gam_r = gam_rows[h:h + 1, r]
                beta_c = gb[r, 3 * N_HEADS + h:3 * N_HEADS + h + 1]
                decay = jnp.where(incl, jnp.exp(jnp.where(incl, gam_c - gam_r, 0.0)), 0.0)
                P.append(dict(s=s, c=c, h=h, r=r, sl=sl, q=q, k=k, v=v, gam_c=gam_c,
                              beta_c=beta_c, decay=decay))
    for p in P:
        p["kk"] = _dot_nt(p["k"], p["k"])
    for p in P:
        p["qk"] = _dot_nt(p["q"], p["k"]) * p["decay"]
    for p in P:
        e_gam = jnp.exp(p["gam_c"])
        gam_last = p["gam_c"][L - 1:L, :]
        p["pw"] = -jnp.where(strict, p["beta_c"] * p["kk"] * p["decay"], 0.0)
        p["x"] = eye + p["pw"]
        p["rhs"] = jnp.concatenate([p["beta_c"] * p["v"], p["beta_c"] * p["k"] * e_gam], axis=-1)
        p["q_dec"] = p["q"] * e_gam
        p["k_dec"] = p["k"] * jnp.exp(gam_last - p["gam_c"])
        p["e_last"] = jnp.exp(gam_last)
    n = 2
    while n < L:
        for p in P:
            p["pw"] = _dot(p["pw"], p["pw"])
        for p in P:
            p["x"] = p["x"] + _dot(p["x"], p["pw"])
        n *= 2
    for p in P:
        sol = _dot(p["x"], p["rhs"])
        p["u"] = sol[:, :HEAD_DIM]
        p["w"] = sol[:, HEAD_DIM:]

    for c in range(n_sub):
        Pc = [p for p in P if p["c"] == c]
        for p in Pc:
            p["s_prev"] = s_ref[p["s"], p["h"]]
            p["ws"] = _dot(p["w"], p["s_prev"])
        for p in Pc:
            p["qs"] = _dot(p["q_dec"], p["s_prev"])
        for p in Pc:
            p["v_new"] = p["u"] - p["ws"]
        for p in Pc:
            p["o"] = p["qs"] + _dot(p["qk"], p["v_new"])
        for p in Pc:
            p["kv"] = _dot_tn(p["k_dec"], p["v_new"])
        for p in Pc:
            o = p["o"]
            on = o * lax.rsqrt(jnp.mean(o * o, axis=-1, keepdims=True) + EPS) * norm_ref[...]
            p["o_out"] = on * _silu(og_ref[p["r"], p["sl"]])
        for p in Pc:
            s_ref[p["s"], p["h"]] = p["e_last"] * p["s_prev"] + p["kv"]
            o_ref[p["r"], p["sl"]] = p["o_out"]


def _gdn(proj, gs, conv_w, a_row, dt_row, norm_row, init, *, n_batch, seq, s_valid, row_off,
         L, n_sub, G):
    Lt = n_sub * L
    nt = seq // Lt
    rows = G * Lt
    has_init = init is not None
    blk = _seq_tile_specs(G, Lt, nt, row_off)

    def col(c):
        return pl.BlockSpec((None, rows, BRANCH_W), lambda b, t: (c, blk(b, t), 0))

    def const(shape):
        return pl.BlockSpec(shape, lambda b, t: (0,) * len(shape))

    in_specs = [col(COL_GQ), col(COL_GK), col(COL_GV), col(COL_GO),
                pl.BlockSpec((rows, LANES), lambda b, t: (blk(b, t), 0)),
                const((CONV_W, 3 * BRANCH_W)), const((1, LANES)), const((1, LANES)),
                const((1, HEAD_DIM))]
    args = [proj, proj, proj, proj, gs, conv_w, a_row, dt_row, norm_row]
    s_spec = pl.BlockSpec((G, N_HEADS, HEAD_DIM, HEAD_DIM), lambda b, t: (b, 0, 0, 0))
    conv_spec = pl.BlockSpec((G, SUBLANES, 3 * BRANCH_W), lambda b, t: (b, 0, 0))
    if has_init:
        in_specs += [conv_spec, s_spec]
        args += list(init)
    return pl.pallas_call(
        functools.partial(_gdn_kernel, L=L, n_sub=n_sub, G=G, s_valid=s_valid, s_total=seq,
                          has_init=has_init),
        out_shape=(jax.ShapeDtypeStruct((n_batch * seq, BRANCH_W), F32),
                   jax.ShapeDtypeStruct((n_batch, N_HEADS, HEAD_DIM, HEAD_DIM), F32),
                   jax.ShapeDtypeStruct((n_batch, SUBLANES, 3 * BRANCH_W), F32)),
        grid=(n_batch // G, nt),
        in_specs=in_specs,
        out_specs=(pl.BlockSpec((rows, BRANCH_W), lambda b, t: (b * nt + t, 0)),
                   s_spec, conv_spec),
        scratch_shapes=[pltpu.VMEM((G, Lt + SUBLANES, 3 * BRANCH_W), F32),
                        pltpu.VMEM((rows, 3 * BRANCH_W), F32)],
        compiler_params=_params(("parallel", "arbitrary")),
        name="gdn",
    )(*args)


def _xattn_kernel(q_ref, mk_ref, mv_ref, o_ref, *, G, lt, mem_len):
    scale = HEAD_DIM ** -0.5
    P = []
    for s in range(G):
        for h in range(N_HEADS):
            P.append(dict(r=slice(s * lt, (s + 1) * lt), m=slice(s * mem_len, (s + 1) * mem_len),
                          sl=slice(h * HEAD_DIM, (h + 1) * HEAD_DIM)))
    for p in P:
        p["s"] = _dot_nt(q_ref[p["r"], p["sl"]], mk_ref[p["m"], p["sl"]]) * scale
    for p in P:
        e = jnp.exp(p["s"] - jnp.max(p["s"], axis=-1, keepdims=True))
        p["p"] = e / jnp.sum(e, axis=-1, keepdims=True)
    for p in P:
        o_ref[p["r"], p["sl"]] = _dot(p["p"], mv_ref[p["m"], p["sl"]])


def _xattn(proj, mk, mv, *, n_batch, seq, row_off, lt, G, mem_len, k_col, v_col):
    nt = seq // lt
    blk = _seq_tile_specs(G, lt, nt, row_off)
    return pl.pallas_call(
        functools.partial(_xattn_kernel, G=G, lt=lt, mem_len=mem_len),
        out_shape=jax.ShapeDtypeStruct((n_batch * seq, BRANCH_W), F32),
        grid=(n_batch // G, nt),
        in_specs=[pl.BlockSpec((None, G * lt, BRANCH_W), lambda b, t: (COL_XQ, blk(b, t), 0)),
                  pl.BlockSpec((None, G * mem_len, BRANCH_W), lambda b, t: (k_col, b, 0)),
                  pl.BlockSpec((None, G * mem_len, BRANCH_W), lambda b, t: (v_col, b, 0))],
        out_specs=pl.BlockSpec((G * lt, BRANCH_W), lambda b, t: (b * nt + t, 0)),
        compiler_params=_params(("parallel", "arbitrary")),
        name="xattn",
    )(proj, mk, mv)


def _lane_first_max(x, mask, lane_f):
    xm = jnp.where(mask, x, -jnp.inf)
    mx = jnp.max(xm, axis=-1, keepdims=True)
    idx = jnp.min(jnp.where(xm == mx, lane_f, float(LANES)), axis=-1, keepdims=True)
    return mx, idx


def _store_token_tiles(ref, x, tok_rows):
    n = x.shape[0]
    for j in range(x.shape[1] // LANES):
        ref[pl.ds(j, n, stride=tok_rows), :] = x[:, j * LANES:(j + 1) * LANES]


def _load_token_tiles(ref, n, tok_rows, n_chunks=SUBLANES):
    return jnp.concatenate([ref[pl.ds(j, n, stride=tok_rows), :] for j in range(n_chunks)], axis=1)


def _merge_kernel(hap_ref, has_ref, hbp_ref, hbs_ref, hcp_ref, hcs_ref, xp_ref, xs_ref,
                  ga0_ref, ga1_ref, gb0_ref, gb1_ref, gc0_ref, gc1_ref,
                  wa_ref, wb_ref, wc_ref, wo_ref, nf_ref, wr_ref, br_ref,
                  x1_ref, *, n_first):
    i = pl.program_id(0)
    d = xp_ref.shape[1]
    tm = xp_ref.shape[0]

    def gate(lo_ref, hi_ref):
        return jnp.concatenate([_sigmoid(lo_ref[...]), _sigmoid(hi_ref[...])], axis=1)

    def body(ha_ref, hb_ref, hc_ref, x_ref):
        mixed = (gate(ga0_ref, ga1_ref) * _dot(ha_ref[...], wa_ref[...])
                 + gate(gb0_ref, gb1_ref) * _dot(hb_ref[...], wb_ref[...])
                 + gate(gc0_ref, gc1_ref) * _dot(hc_ref[...], wc_ref[...]))
        x1 = x_ref[...] + _dot(mixed, wo_ref[...])
        h2 = _rms(x1, nf_ref[...])
        logits = _dot(h2, wr_ref[...]) + br_ref[...]
        lane = lax.broadcasted_iota(jnp.int32, logits.shape, 1)
        lane_f = lane.astype(F32)
        is_g = lane < N_GROUPS
        g_max, g_idx = _lane_first_max(logits, is_g, lane_f)
        g_den = jnp.sum(jnp.where(is_g, jnp.exp(logits - g_max), 0.0), axis=-1, keepdims=True)
        p_grp = 1.0 / g_den
        e_lo = N_GROUPS + EXPERTS_PER_GROUP * g_idx
        in_grp = (lane_f >= e_lo) & (lane_f < e_lo + EXPERTS_PER_GROUP)
        v1, i1 = _lane_first_max(logits, in_grp, lane_f)
        v2, i2 = _lane_first_max(logits, in_grp & (lane_f != i1), lane_f)
        e2 = jnp.exp(v2 - v1)
        w1 = 1.0 / (1.0 + e2)
        w2 = e2 / (1.0 + e2)
        comb = p_grp * (jnp.where(lane_f == i1, w1, 0.0) + jnp.where(lane_f == i2, w2, 0.0))
        lo = jnp.minimum(i1, i2) - e_lo
        hi = jnp.maximum(i1, i2) - e_lo
        pair = lo * (7.0 - lo) * 0.5 + (hi - lo - 1.0)
        cls = g_idx * float(N_PAIRS) + pair
        _store_token_tiles(x1_ref, x1, TOKEN_ROWS)
        route = jnp.where(lane == 0, cls, comb)
        for j in range(ROUTE_ROW, TOKEN_ROWS):
            x1_ref[pl.ds(j, tm, stride=TOKEN_ROWS), :] = route

    @pl.when(i < n_first)
    def _():
        body(hap_ref, hbp_ref, hcp_ref, xp_ref)

    @pl.when(i >= n_first)
    def _():
        body(has_ref, hbs_ref, hcs_ref, xs_ref)


def _merge(ha, hb, hc, xs, proj, wa, wb, wc, wo, nf, wr, br, *, tm):
    d = xs[0].shape[1]
    n_first = xs[0].shape[0] // tm
    n_rows = xs[0].shape[0] + xs[1].shape[0]
    specs = []
    for width in (BRANCH_W, BRANCH_W, BRANCH_W, d):
        specs += list(_two_source_specs(n_first, tm, width))

    def const(a):
        return pl.BlockSpec(a.shape, lambda i: (0,) * a.ndim)

    n_gate = 3 * d // BRANCH_W
    specs += [pl.BlockSpec((None, tm, BRANCH_W), lambda i, c=c: (COL_GATE + c, i, 0))
              for c in range(n_gate)]
    weights = [wa, wb, wc, wo, nf, wr, br]
    specs += [const(a) for a in weights]
    return pl.pallas_call(
        functools.partial(_merge_kernel, n_first=n_first),
        out_shape=jax.ShapeDtypeStruct((n_rows * TOKEN_ROWS, LANES), F32),
        grid=(n_rows // tm,),
        in_specs=specs,
        out_specs=pl.BlockSpec((tm * TOKEN_ROWS, LANES), lambda i: (i, 0)),
        compiler_params=_params(("parallel",)),
        name="merge",
    )(*ha, *hb, *hc, *xs, *([proj] * n_gate), *weights)


GATHER_UNROLL = 8


def _record_gather_start(src_ref, dst_ref, sem, ids_ref, base, n, rec):
    def issue(r, carry):
        src = pl.multiple_of(ids_ref[base + r] * rec, rec)
        dst = pl.multiple_of(r * rec, rec)
        pltpu.make_async_copy(src_ref.at[pl.ds(src, rec)], dst_ref.at[pl.ds(dst, rec)], sem).start()
        return carry
    lax.fori_loop(0, n, issue, 0, unroll=GATHER_UNROLL)


def _record_gather_wait(src_ref, dst_ref, sem):
    pltpu.make_async_copy(src_ref.at[pl.ds(0, dst_ref.shape[0])], dst_ref, sem).wait()


def _moe_kernel(ids_ref, ea_ref, eb_ref, nused_ref, x_hbm, wga_ref, wua_ref, wda_ref,
                wgb_ref, wub_ref, wdb_ref, nf_ref, fin_ref, y_ref, xbuf, sem, *, tm, d):
    i = pl.program_id(0)
    n_used = nused_ref[0]
    slot = i % 2

    @pl.when(i == 0)
    def _():
        _record_gather_start(x_hbm, xbuf.at[0], sem.at[0], ids_ref, 0, tm, TOKEN_ROWS)

    @pl.when(i + 1 < n_used)
    def _():
        _record_gather_start(x_hbm, xbuf.at[1 - slot], sem.at[1 - slot], ids_ref, (i + 1) * tm, tm,
                             TOKEN_ROWS)

    @pl.when(i < n_used)
    def _():
        _record_gather_wait(x_hbm, xbuf.at[slot], sem.at[slot])
        x1 = _load_token_tiles(xbuf.at[slot], tm, TOKEN_ROWS)
        route = xbuf[slot, pl.ds(ROUTE_ROW, tm, stride=TOKEN_ROWS), :]
        lane = lax.broadcasted_iota(jnp.int32, route.shape, 1)
        h2 = _rms(x1, nf_ref[...]).astype(BF16)
        acc = jnp.zeros((tm, d), F32)
        for e_ref, wg_ref, wu_ref, wd_ref in ((ea_ref, wga_ref, wua_ref, wda_ref),
                                              (eb_ref, wgb_ref, wub_ref, wdb_ref)):
            comb = jnp.sum(jnp.where(lane == N_GROUPS + e_ref[i], route, 0.0),
                           axis=-1, keepdims=True)
            a = jnp.dot(h2, wg_ref[0], preferred_element_type=F32)
            u = jnp.dot(h2, wu_ref[0], preferred_element_type=F32)
            act = _silu(a) * u * comb
            acc = acc + _dot(act, wd_ref[0])
        _store_token_tiles(y_ref, _rms(x1 + acc, fin_ref[...]), SUBLANES)

    @pl.when(i >= n_used)
    def _():
        y_ref[...] = jnp.zeros_like(y_ref)


def _moe(ids, tile_ea, tile_eb, n_used, x1rec, wg, wu, wd, nf, fin, *, tm, n_tiles):
    d = nf.shape[1]
    f = wg.shape[2]
    assert d == SUBLANES * LANES

    def expert(shape, which):
        return pl.BlockSpec((1,) + shape, lambda i, ids, ea, eb, nu: ((ea, eb)[which][i], 0, 0))

    def const(a):
        return pl.BlockSpec(a.shape, lambda i, *_: (0,) * a.ndim)

    grid_spec = pltpu.PrefetchScalarGridSpec(
        num_scalar_prefetch=4,
        grid=(n_tiles,),
        in_specs=[pl.BlockSpec(memory_space=pl.ANY),
                  expert((d, f), 0), expert((d, f), 0), expert((f, d), 0),
                  expert((d, f), 1), expert((d, f), 1), expert((f, d), 1),
                  const(nf), const(fin)],
        out_specs=pl.BlockSpec((tm * SUBLANES, LANES), lambda i, *_: (i, 0)),
        scratch_shapes=[pltpu.VMEM((2, tm * TOKEN_ROWS, LANES), F32),
                        pltpu.SemaphoreType.DMA((2,))],
    )
    return pl.pallas_call(
        functools.partial(_moe_kernel, tm=tm, d=d),
        out_shape=jax.ShapeDtypeStruct((n_tiles * tm * SUBLANES, LANES), F32),
        grid_spec=grid_spec,
        compiler_params=_params(("arbitrary",)),
        name="moe",
    )(ids, tile_ea, tile_eb, n_used, x1rec, wg, wu, wd, wg, wu, wd, nf, fin)


def _unsort_kernel(pos_ref, y_hbm, o_ref, buf, sem, *, tm, base):
    i = pl.program_id(0)
    slot = i % 2

    @pl.when(i == 0)
    def _():
        _record_gather_start(y_hbm, buf.at[0], sem.at[0], pos_ref, base, tm, SUBLANES)

    @pl.when(i + 1 < pl.num_programs(0))
    def _():
        _record_gather_start(y_hbm, buf.at[1 - slot], sem.at[1 - slot], pos_ref,
                             base + (i + 1) * tm, tm, SUBLANES)

    _record_gather_wait(y_hbm, buf.at[slot], sem.at[slot])
    o_ref[...] = _load_token_tiles(buf.at[slot], tm, SUBLANES)


def _unsort(pos, y_rec, *, base, n_rows, tm):
    d = SUBLANES * LANES
    grid_spec = pltpu.PrefetchScalarGridSpec(
        num_scalar_prefetch=1,
        grid=(n_rows // tm,),
        in_specs=[pl.BlockSpec(memory_space=pl.ANY)],
        out_specs=pl.BlockSpec((tm, d), lambda i, *_: (i, 0)),
        scratch_shapes=[pltpu.VMEM((2, tm * SUBLANES, LANES), F32),
                        pltpu.SemaphoreType.DMA((2,))],
    )
    return pl.pallas_call(
        functools.partial(_unsort_kernel, tm=tm, base=base),
        out_shape=jax.ShapeDtypeStruct((n_rows, d), F32),
        grid_spec=grid_spec,
        compiler_params=_params(("arbitrary",)),
        name="unsort",
    )(pos, y_rec)


def _sort_plan(cls, *, tm, n_tiles):
    n = cls.shape[0]
    onehot = (cls[:, None] == jnp.arange(N_CLASSES, dtype=jnp.int32)[None, :]).astype(jnp.int32)
    csum = jnp.cumsum(onehot, axis=0)
    rank = jnp.sum((csum - onehot) * onehot, axis=1)
    counts = csum[-1]
    tiles_per = (counts + tm - 1) // tm
    tile_end = jnp.cumsum(tiles_per)
    tile_start = tile_end - tiles_per
    pos = (jnp.sum(tile_start[None, :] * onehot, axis=1) * tm + rank).astype(jnp.int32)
    n_used = tile_end[-1].astype(jnp.int32)
    ids = jnp.zeros((n_tiles * tm,), jnp.int32).at[pos].set(jnp.arange(n, dtype=jnp.int32))
    tile = jnp.minimum(jnp.arange(n_tiles, dtype=jnp.int32), n_used - 1)
    tile_cls = jnp.sum((tile[:, None] >= tile_end[None, :]).astype(jnp.int32), axis=1)
    grp = tile_cls // N_PAIRS
    pair = tile_cls % N_PAIRS
    lo = jnp.asarray(PAIR_LO, jnp.int32)[pair]
    hi = jnp.asarray(PAIR_HI, jnp.int32)[pair]
    ea = (grp * EXPERTS_PER_GROUP + lo).astype(jnp.int32)
    eb = (grp * EXPERTS_PER_GROUP + hi).astype(jnp.int32)
    return pos, ids, ea, eb, n_used.reshape(1)


def _lane_row(values, start):
    row = jnp.zeros((LANES,), F32)
    return row.at[start:start + values.shape[0]].set(values.astype(F32)).reshape(1, LANES)


def _layer(x_prompt, x_sample, mem_prompt, cache_mem_k, cache_mem_v,
           state_mlstm_C, state_mlstm_n, state_mlstm_m, state_gdn_S, state_gdn_conv,
           norm_mix, w_in, b_igate, b_fgate, mlstm_norm, conv_w, a_log, dt_bias, gdn_norm,
           mem_norm, w_mem_kv, w_br_mlstm, w_br_gdn, w_br_xattn, w_out, norm_ffn,
           w_router_group, b_router_group, w_router_expert, b_router_expert,
           w_exp_gate, w_exp_up, w_exp_down, final_norm, *, cfg):
    bp, sp, d = x_prompt.shape
    bs, ss, _ = x_sample.shape
    mem_len = mem_prompt.shape[1]
    ss_pad = SUBLANES
    assert ss <= ss_pad and d % LANES == 0
    n_p = bp * sp
    n_s = bs * ss_pad
    W = BRANCH_W
    gs_ = cfg["sample_group"]

    sizes = (W, W, W, N_HEADS, N_HEADS, W, 3 * W, N_HEADS, N_HEADS, W, W, d, d, d)
    offs = [0]
    for s in sizes:
        offs.append(offs[-1] + s)
    (m_q, m_k, m_v, m_i, m_f, m_o, g_qkv, g_a, g_b, g_out, x_q, gate_a, gate_b, gate_c) = [
        w_in[:, offs[j]:offs[j + 1]] for j in range(len(sizes))]
    w_big = jnp.stack(
        [gate_a[:, :W], gate_a[:, W:], gate_b[:, :W], gate_b[:, W:], gate_c[:, :W], gate_c[:, W:],
         m_q, m_k, m_v, m_o, g_qkv[:, :W], g_qkv[:, W:2 * W], g_qkv[:, 2 * W:], g_out, x_q],
        axis=0).astype(BF16)
    w_small = jnp.concatenate(
        [m_i, m_f, g_a, g_b, jnp.zeros((d, LANES - 4 * N_HEADS), w_in.dtype)], axis=1).astype(BF16)
    w_router = jnp.concatenate(
        [w_router_group, w_router_expert,
         jnp.zeros((d, LANES - N_GROUPS - N_EXPERTS), w_in.dtype)], axis=1).astype(BF16)
    b_router = _lane_row(jnp.concatenate([b_router_group, b_router_expert]), 0)
    gate_bias = _lane_row(jnp.concatenate([b_igate, b_fgate]), 0)
    a_row = _lane_row(-jnp.exp(a_log.astype(F32)), 2 * N_HEADS)
    dt_row = _lane_row(dt_bias, 2 * N_HEADS)

    xp2 = x_prompt.reshape(n_p, d)
    xs2 = jnp.pad(x_sample, ((0, 0), (0, ss_pad - ss), (0, 0))).reshape(n_s, d)

    proj, gs = _norm_proj(xp2, xs2, norm_mix.reshape(1, d), w_big, w_small, tm=cfg["tm_proj"])

    mem2 = mem_prompt.reshape(bp * mem_len, d)
    mem_kv, _ = _norm_proj(mem2, None, mem_norm.reshape(1, d),
                           _col_blocks(w_mem_kv.astype(BF16), W),
                           jnp.zeros((d, LANES), BF16), tm=mem_len)

    mnorm = mlstm_norm.reshape(1, W)
    gnorm = gdn_norm.reshape(1, HEAD_DIM)
    s_off = n_p // (gs_ * ss_pad)
    ha_p, c_p, nn_p, mm_p = _mlstm(proj, gs, gate_bias, mnorm, None, n_batch=bp, seq=sp,
                                   s_valid=sp, row_off=0, L=cfg["chunk_m"], n_sub=cfg["sub_m"], G=1)
    m0 = jnp.broadcast_to(state_mlstm_m[:, :, None], (bs, N_HEADS, LANES))
    ha_s, c_s, nn_s, mm_s = _mlstm(proj, gs, gate_bias, mnorm,
                                   (state_mlstm_C, state_mlstm_n, m0), n_batch=bs, seq=ss_pad,
                                   s_valid=ss, row_off=s_off, L=ss_pad, n_sub=1, G=gs_)
    hb_p, s_p, cv_p = _gdn(proj, gs, conv_w, a_row, dt_row, gnorm, None, n_batch=bp, seq=sp,
                           s_valid=sp, row_off=0, L=cfg["chunk_g"], n_sub=cfg["sub_g"], G=1)
    conv0 = jnp.pad(state_gdn_conv, ((0, 0), (SUBLANES - (CONV_W - 1), 0), (0, 0)))
    hb_s, s_s, cv_s = _gdn(proj, gs, conv_w, a_row, dt_row, gnorm, (conv0, state_gdn_S),
                           n_batch=bs, seq=ss_pad, s_valid=ss, row_off=s_off, L=ss_pad,
                           n_sub=1, G=gs_)
    lt = min(sp, cfg["lt_x"])
    hc_p = _xattn(proj, mem_kv, mem_kv, n_batch=bp, seq=sp, row_off=0, lt=lt, G=1,
                  mem_len=mem_len, k_col=0, v_col=1)
    ck = cache_mem_k.reshape(1, bs * mem_len, W)
    cv = cache_mem_v.reshape(1, bs * mem_len, W)
    hc_s = _xattn(proj, ck, cv, n_batch=bs, seq=ss_pad, row_off=s_off, lt=ss_pad, G=gs_,
                  mem_len=mem_len, k_col=0, v_col=0)

    tm = cfg["tm"]
    x1aug = _merge((ha_p, ha_s), (hb_p, hb_s), (hc_p, hc_s), (xp2, xs2), proj,
                   w_br_mlstm.astype(BF16), w_br_gdn.astype(BF16), w_br_xattn.astype(BF16),
                   w_out.astype(BF16), norm_ffn.reshape(1, d), w_router, b_router, tm=tm)

    tm_moe = cfg["tm_moe"]
    n_all = n_p + n_s
    n_tiles = (n_all + N_CLASSES * (tm_moe - 1)) // tm_moe + 1
    cls = x1aug[ROUTE_ROW::TOKEN_ROWS, 0].astype(jnp.int32)
    pos, ids, ea, eb, n_used = _sort_plan(cls, tm=tm_moe, n_tiles=n_tiles)
    y_sorted = _moe(ids, ea, eb, n_used, x1aug, w_exp_gate.astype(BF16), w_exp_up.astype(BF16),
                    w_exp_down.astype(BF16), norm_ffn.reshape(1, d), final_norm.reshape(1, d),
                    tm=tm_moe, n_tiles=n_tiles)
    y_p = _unsort(pos, y_sorted, base=0, n_rows=n_p, tm=tm)
    y_s = _unsort(pos, y_sorted, base=n_p, n_rows=n_s, tm=tm)

    y_prompt = y_p.reshape(bp, sp, d)
    y_sample = y_s.reshape(bs, ss_pad, d)[:, :ss]
    mk_p = mem_kv[0].reshape(bp, mem_len, N_HEADS, HEAD_DIM)
    mv_p = mem_kv[1].reshape(bp, mem_len, N_HEADS, HEAD_DIM)
    tail = slice(SUBLANES - (CONV_W - 1), SUBLANES)
    return (y_prompt, y_sample, mk_p, mv_p,
            c_p, nn_p, mm_p[:, :, 0], s_p, cv_p[:, tail],
            c_s, nn_s, mm_s[:, :, 0], s_s, cv_s[:, tail])


CONFIG = dict(tm_proj=1024, tm=512, tm_moe=256, chunk_m=128, sub_m=2, chunk_g=64, sub_g=4,
              lt_x=512, sample_group=8)


def kernel(x_prompt, x_sample, mem_prompt, cache_mem_k, cache_mem_v, state_mlstm_C, state_mlstm_n, state_mlstm_m, state_gdn_S, state_gdn_conv, norm_mix, w_in, b_igate, b_fgate, mlstm_norm, conv_w, a_log, dt_bias, gdn_norm, mem_norm, w_mem_kv, w_br_mlstm, w_br_gdn, w_br_xattn, w_out, norm_ffn, w_router_group, b_router_group, w_router_expert, b_router_expert, w_exp_gate, w_exp_up, w_exp_down, final_norm):
    depth = w_in.shape[0]
    assert depth == 1, "one decoder layer"
    outs = _layer(
        x_prompt, x_sample, mem_prompt, cache_mem_k[0], cache_mem_v[0],
        state_mlstm_C[0], state_mlstm_n[0], state_mlstm_m[0], state_gdn_S[0], state_gdn_conv[0],
        norm_mix[0], w_in[0], b_igate[0], b_fgate[0], mlstm_norm[0], conv_w[0], a_log[0],
        dt_bias[0], gdn_norm[0], mem_norm[0], w_mem_kv[0], w_br_mlstm[0], w_br_gdn[0],
        w_br_xattn[0], w_out[0], norm_ffn[0], w_router_group[0], b_router_group[0],
        w_router_expert[0], b_router_expert[0], w_exp_gate[0], w_exp_up[0], w_exp_down[0],
        final_norm, cfg=CONFIG)
    y_prompt, y_sample = outs[0], outs[1]
    return (y_prompt, y_sample) + tuple(o[None] for o in outs[2:])
```

```python
import functools

import jax
import jax.numpy as jnp
from jax import lax
from jax.experimental import pallas as pl
from jax.experimental.pallas import tpu as pltpu

F32 = jnp.float32
BF16 = jnp.bfloat16
EPS = 1e-6

N_HEADS = 4
HEAD_DIM = 128
BRANCH_W = N_HEADS * HEAD_DIM
CONV_W = 4
N_GROUPS = 4
EXPERTS_PER_GROUP = 4
N_EXPERTS = N_GROUPS * EXPERTS_PER_GROUP
N_PAIRS = 6
N_CLASSES = N_GROUPS * N_PAIRS
PAIR_LO = (0, 0, 0, 1, 1, 2)
PAIR_HI = (1, 2, 3, 2, 3, 3)
NEG_BIG = -1e30
LANES = 128
SUBLANES = 8
TOKEN_ROWS = 8
PROJ_GROUP = 3
VMEM_LIMIT = 48 * 1024 * 1024

COL_GATE = 0
COL_MQ, COL_MK, COL_MV, COL_MO = 6, 7, 8, 9
COL_GQ, COL_GK, COL_GV, COL_GO, COL_XQ = 10, 11, 12, 13, 14
N_PROJ = 15 * BRANCH_W


def _dot(a, b):
    return jnp.dot(a.astype(BF16), b.astype(BF16), preferred_element_type=F32)


def _dot_nt(a, b):
    return lax.dot_general(a.astype(BF16), b.astype(BF16), (((1,), (1,)), ((), ())),
                           preferred_element_type=F32)


def _dot_tn(a, b):
    return lax.dot_general(a.astype(BF16), b.astype(BF16), (((0,), (0,)), ((), ())),
                           preferred_element_type=F32)


def _dot_exact(a, b):
    return jnp.dot(a, b, preferred_element_type=F32, precision=lax.Precision.HIGHEST)


def _lanes_to_rows(x, lane0):
    r = lax.broadcasted_iota(jnp.int32, (SUBLANES, LANES), 0)
    c = lax.broadcasted_iota(jnp.int32, (SUBLANES, LANES), 1)
    sel = (c == r + lane0).astype(F32)
    return lax.dot_general(sel, x, (((1,), (1,)), ((), ())), preferred_element_type=F32,
                           precision=lax.Precision.HIGHEST)


def _rms(x, g):
    return x * lax.rsqrt(jnp.mean(x * x, axis=-1, keepdims=True) + EPS) * g


def _softplus(x):
    return jnp.maximum(x, 0.0) + jnp.log1p(jnp.exp(-jnp.abs(x)))


def _sigmoid(x):
    return 1.0 / (1.0 + jnp.exp(-x))


def _silu(x):
    return x * _sigmoid(x)


def _tri(n, strict=False):
    r = lax.broadcasted_iota(jnp.int32, (n, n), 0)
    c = lax.broadcasted_iota(jnp.int32, (n, n), 1)
    return (c < r) if strict else (c <= r)


def _tri_blocks(n, block, upper=False):
    shift = block.bit_length() - 1
    assert block == 1 << shift
    r = lax.broadcasted_iota(jnp.int32, (n, n), 0)
    c = lax.broadcasted_iota(jnp.int32, (n, n), 1)
    same = jnp.right_shift(r, shift) == jnp.right_shift(c, shift)
    return same & ((r <= c) if upper else (c <= r))


def _pad_rows(shape, t, seq_rows, n_valid):
    assert seq_rows & (seq_rows - 1) == 0
    row = lax.broadcasted_iota(jnp.int32, shape, 0)
    return (row & (seq_rows - 1)) + t * seq_rows >= n_valid


def _params(sem):
    return pltpu.CompilerParams(dimension_semantics=sem, vmem_limit_bytes=VMEM_LIMIT)


def _two_source_specs(n_first, block, width):
    first = pl.BlockSpec((block, width), lambda i, *_: (jnp.minimum(i, n_first - 1), 0))
    second = pl.BlockSpec((block, width), lambda i, *_: (jnp.maximum(i - n_first, 0), 0))
    return first, second


def _norm_proj_kernel(xa_ref, xb_ref, g_ref, w_ref, ws_ref, o_ref, os_ref, hb_ref, *, n_first):
    i = pl.program_id(0)
    j = pl.program_id(1)

    @pl.when(j == 0)
    def _():
        @pl.when(i < n_first)
        def _():
            hb_ref[...] = _rms(xa_ref[...], g_ref[...]).astype(BF16)

        @pl.when(i >= n_first)
        def _():
            hb_ref[...] = _rms(xb_ref[...], g_ref[...]).astype(BF16)

        os_ref[...] = jnp.dot(hb_ref[...], ws_ref[...], preferred_element_type=F32)

    o_ref[...] = jnp.dot(hb_ref[...], w_ref[0], preferred_element_type=F32)


def _proj_spec(rows, c, row_block):
    return pl.BlockSpec((None, rows, BRANCH_W),
                        lambda *g: (c // PROJ_GROUP, row_block(*g), c % PROJ_GROUP))


def _col_blocks(w, tn):
    d, n = w.shape
    return w.reshape(d, n // tn, tn).transpose(1, 0, 2)


def _norm_proj(xa, xb, g, w, ws, *, tm):
    d = xa.shape[1]
    n_first = xa.shape[0] // tm
    n_rows = xa.shape[0] + (0 if xb is None else xb.shape[0])
    if xb is None:
        xb = xa
    tn = w.shape[2]
    n = w.shape[0] * tn
    sa, sb = _two_source_specs(n_first, tm, d)
    return pl.pallas_call(
        functools.partial(_norm_proj_kernel, n_first=n_first),
        out_shape=(jax.ShapeDtypeStruct((n // tn, n_rows, tn), F32),
                   jax.ShapeDtypeStruct((n_rows, ws.shape[1]), F32)),
        grid=(n_rows // tm, n // tn),
        in_specs=[sa, sb,
                  pl.BlockSpec((1, d), lambda i, j: (0, 0)),
                  pl.BlockSpec((1, d, tn), lambda i, j: (j, 0, 0)),
                  pl.BlockSpec((d, ws.shape[1]), lambda i, j: (0, 0))],
        out_specs=(pl.BlockSpec((None, tm, tn), lambda i, j: (j, i, 0)),
                   pl.BlockSpec((tm, ws.shape[1]), lambda i, j: (i, 0))),
        scratch_shapes=[pltpu.VMEM((tm, d), BF16)],
        compiler_params=_params(("parallel", "arbitrary")),
        name="norm_proj",
    )(xa, xb, g, w, ws)


def _seq_tile_specs(G, Lt, nt, row_off):
    assert G == 1 or nt == 1

    def rows(b, t):
        return row_off + b * nt + t

    return rows


def _mlstm_kernel(*refs, L, n_sub, G, s_valid, s_total, has_init):
    if has_init:
        (q_ref, k_ref, v_ref, og_ref, gs_ref, bias_ref, norm_ref, c0_ref, n0_ref, m0_ref,
         h_ref, c_ref, n_ref, m_ref) = refs
    else:
        (q_ref, k_ref, v_ref, og_ref, gs_ref, bias_ref, norm_ref,
         h_ref, c_ref, n_ref, m_ref) = refs
    t = pl.program_id(1)
    Lt = n_sub * L
    rows = G * Lt

    @pl.when(t == 0)
    def _():
        if has_init:
            c_ref[...] = c0_ref[...]
            n_ref[...] = n0_ref[...]
            m_ref[...] = m0_ref[...]
        else:
            c_ref[...] = jnp.zeros_like(c_ref)
            n_ref[...] = jnp.zeros_like(n_ref)
            m_ref[...] = jnp.zeros_like(m_ref)

    z = gs_ref[...] + bias_ref[...]
    lane = lax.broadcasted_iota(jnp.int32, z.shape, 1)
    log_f = jnp.minimum(z, 0.0) - jnp.log1p(jnp.exp(-jnp.abs(z)))
    g = jnp.where((lane >= N_HEADS) & (lane < 2 * N_HEADS), log_f, z)
    if s_valid < s_total:
        pad = _pad_rows(z.shape, t, Lt, s_valid)
        g = jnp.where(pad, jnp.where(lane < N_HEADS, NEG_BIG, 0.0), g)
    g_t = _lanes_to_rows(g, 0)
    bt_cols = _dot_exact(_tri_blocks(rows, L).astype(F32), g)
    bt_rows = _dot_exact(g_t, _tri_blocks(rows, L, upper=True).astype(F32))
    causal = _tri(L)
    scale = HEAD_DIM ** -0.5
    probs = [(s, h) for s in range(G) for h in range(N_HEADS)]

    for c in range(n_sub):
        P = []
        for s, h in probs:
            r = slice(s * Lt + c * L, s * Lt + (c + 1) * L)
            sl = slice(h * HEAD_DIM, (h + 1) * HEAD_DIM)
            p = dict(s=s, h=h, r=r, sl=sl, q=q_ref[r, sl], k=k_ref[r, sl] * scale, v=v_ref[r, sl],
                     bt_c=bt_cols[r, N_HEADS + h:N_HEADS + h + 1], it_c=g[r, h:h + 1])
            bt_r = bt_rows[N_HEADS + h:N_HEADS + h + 1, r]
            it_r = g_t[h:h + 1, r]
            p["log_d"] = jnp.where(causal, p["bt_c"] - bt_r + it_r, -jnp.inf)
            p["c_prev"] = c_ref[s, h]
            p["n_prev"] = n_ref[s, h:h + 1, :]
            p["m_prev"] = m_ref[s, h:h + 1, 0:1]
            P.append(p)
        for p in P:
            p["qk"] = _dot_nt(p["q"], p["k"])
        for p in P:
            p["qc"] = _dot(p["q"], p["c_prev"])
        for p in P:
            inter = p["bt_c"] + p["m_prev"]
            p["m_t"] = jnp.maximum(inter, jnp.max(p["log_d"], axis=-1, keepdims=True))
            p["inter_w"] = jnp.exp(inter - p["m_t"])
            p["sd"] = p["qk"] * jnp.exp(p["log_d"] - p["m_t"])
        for p in P:
            p["sv"] = _dot(p["sd"], p["v"])
        for p in P:
            bt_last = p["bt_c"][L - 1:L, :]
            p["m_new"] = p["m_t"][L - 1:L, :]
            w = jnp.exp(bt_last - p["bt_c"] + p["it_c"] - p["m_new"])
            p["decay"] = jnp.exp(bt_last + p["m_prev"] - p["m_new"])
            p["kw"] = p["k"] * w
        for p in P:
            p["kv"] = _dot_tn(p["kw"], p["v"])
        for p in P:
            num = p["sv"] + p["qc"] * p["inter_w"]
            den = (jnp.sum(p["sd"], axis=-1, keepdims=True)
                   + p["inter_w"] * jnp.sum(p["q"] * p["n_prev"], axis=-1, keepdims=True))
            hh = num / jnp.maximum(jnp.abs(den), jnp.exp(-p["m_t"]))
            hn = hh * lax.rsqrt(jnp.mean(hh * hh, axis=-1, keepdims=True) + EPS) * norm_ref[:, p["sl"]]
            p["h_out"] = hn * _sigmoid(og_ref[p["r"], p["sl"]])
            p["n_new"] = p["decay"] * p["n_prev"] + jnp.sum(p["kw"], axis=0, keepdims=True)
        for p in P:
            s, h = p["s"], p["h"]
            h_ref[p["r"], p["sl"]] = p["h_out"]
            n_ref[s, h:h + 1, :] = p["n_new"]
            m_ref[s, h:h + 1, :] = jnp.broadcast_to(p["m_new"], (1, LANES))
            c_ref[s, h] = p["decay"] * p["c_prev"] + p["kv"]


def _mlstm(proj, gs, bias_row, norm_row, init, *, n_batch, seq, s_valid, row_off, L, n_sub, G):
    Lt = n_sub * L
    nt = seq // Lt
    rows = G * Lt
    has_init = init is not None
    blk = _seq_tile_specs(G, Lt, nt, row_off)

    def col(c):
        return _proj_spec(rows, c, blk)

    in_specs = [col(COL_MQ), col(COL_MK), col(COL_MV), col(COL_MO),
                pl.BlockSpec((rows, LANES), lambda b, t: (blk(b, t), 0)),
                pl.BlockSpec((1, LANES), lambda b, t: (0, 0)),
                pl.BlockSpec((1, BRANCH_W), lambda b, t: (0, 0))]
    args = [proj, proj, proj, proj, gs, bias_row, norm_row]
    c_spec = pl.BlockSpec((G, N_HEADS, HEAD_DIM, HEAD_DIM), lambda b, t: (b, 0, 0, 0))
    n_spec = pl.BlockSpec((G, N_HEADS, HEAD_DIM), lambda b, t: (b, 0, 0))
    m_spec = pl.BlockSpec((G, N_HEADS, LANES), lambda b, t: (b, 0, 0))
    if has_init:
        in_specs += [c_spec, n_spec, m_spec]
        args += list(init)
    return pl.pallas_call(
        functools.partial(_mlstm_kernel, L=L, n_sub=n_sub, G=G, s_valid=s_valid, s_total=seq,
                          has_init=has_init),
        out_shape=(jax.ShapeDtypeStruct((n_batch * seq, BRANCH_W), F32),
                   jax.ShapeDtypeStruct((n_batch, N_HEADS, HEAD_DIM, HEAD_DIM), F32),
                   jax.ShapeDtypeStruct((n_batch, N_HEADS, HEAD_DIM), F32),
                   jax.ShapeDtypeStruct((n_batch, N_HEADS, LANES), F32)),
        grid=(n_batch // G, nt),
        in_specs=in_specs,
        out_specs=(pl.BlockSpec((rows, BRANCH_W), lambda b, t: (b * nt + t, 0)),
                   c_spec, n_spec, m_spec),
        compiler_params=_params(("parallel", "arbitrary")),
        name="mlstm",
    )(*args)


def _gdn_kernel(*refs, L, n_sub, G, s_valid, s_total, has_init):
    if has_init:
        (q_ref, k_ref, v_ref, og_ref, gs_ref, cw_ref, arow_ref, dtrow_ref, norm_ref,
         conv0_ref, s0_ref, o_ref, s_ref, conv1_ref, ext_ref, conv_ref) = refs
    else:
        (q_ref, k_ref, v_ref, og_ref, gs_ref, cw_ref, arow_ref, dtrow_ref, norm_ref,
         o_ref, s_ref, conv1_ref, ext_ref, conv_ref) = refs
    t = pl.program_id(1)
    Lt = n_sub * L
    rows = G * Lt
    nt = s_total // Lt
    W = BRANCH_W

    @pl.when(t == 0)
    def _():
        if has_init:
            s_ref[...] = s0_ref[...]
            ext_ref[:, 0:SUBLANES, :] = conv0_ref[...]
        else:
            s_ref[...] = jnp.zeros_like(s_ref)
            ext_ref[:, 0:SUBLANES, :] = jnp.zeros((G, SUBLANES, 3 * W), F32)

    @pl.when(t > 0)
    def _():
        ext_ref[:, 0:SUBLANES, :] = ext_ref[:, Lt:Lt + SUBLANES, :]

    base = SUBLANES - (CONV_W - 1)
    for s in range(G):
        rs = slice(s * Lt, (s + 1) * Lt)
        ext_ref[s, SUBLANES:SUBLANES + Lt, 0:W] = q_ref[rs, :]
        ext_ref[s, SUBLANES:SUBLANES + Lt, W:2 * W] = k_ref[rs, :]
        ext_ref[s, SUBLANES:SUBLANES + Lt, 2 * W:3 * W] = v_ref[rs, :]
        acc = ext_ref[s, base:base + Lt, :] * cw_ref[0:1, :]
        for j in range(1, CONV_W):
            acc = acc + ext_ref[s, base + j:base + j + Lt, :] * cw_ref[j:j + 1, :]
        conv_ref[rs, :] = _silu(acc)

    @pl.when(t == nt - 1)
    def _():
        v_last = s_valid - (nt - 1) * Lt
        conv1_ref[...] = ext_ref[:, v_last:v_last + SUBLANES, :]

    z = gs_ref[...]
    lane = lax.broadcasted_iota(jnp.int32, z.shape, 1)
    gdec = arow_ref[...] * _softplus(z + dtrow_ref[...])
    gb = jnp.where((lane >= 2 * N_HEADS) & (lane < 3 * N_HEADS), gdec,
                   jnp.where((lane >= 3 * N_HEADS) & (lane < 4 * N_HEADS), _sigmoid(z), 0.0))
    if s_valid < s_total:
        gb = jnp.where(_pad_rows(z.shape, t, Lt, s_valid), 0.0, gb)
    gb_t = _lanes_to_rows(gb, 2 * N_HEADS)
    gam_cols = _dot_exact(_tri_blocks(rows, L).astype(F32), gb)
    gam_rows = _dot_exact(gb_t, _tri_blocks(rows, L, upper=True).astype(F32))
    incl = _tri(L)
    strict = _tri(L, strict=True)
    eye = (lax.broadcasted_iota(jnp.int32, (L, L), 0)
           == lax.broadcasted_iota(jnp.int32, (L, L), 1)).astype(F32)

    P = []
    for s in range(G):
        for c in range(n_sub):
            for h in range(N_HEADS):
                r = slice(s * Lt + c * L, s * Lt + (c + 1) * L)
                sl = slice(h * HEAD_DIM, (h + 1) * HEAD_DIM)
                q = conv_ref[r, h * HEAD_DIM:(h + 1) * HEAD_DIM]
                k = conv_ref[r, W + h * HEAD_DIM:W + (h + 1) * HEAD_DIM]
                v = conv_ref[r, 2 * W + h * HEAD_DIM:2 * W + (h + 1) * HEAD_DIM]
                q = q * lax.rsqrt(jnp.sum(q * q, axis=-1, keepdims=True) + EPS) * (HEAD_DIM ** -0.5)
                k = k * lax.rsqrt(jnp.sum(k * k, axis=-1, keepdims=True) + EPS)
                gam_c = gam_cols[r, 2 * N_HEADS + h:2 * N_HEADS + h + 1]
                gam_r = gam_rows[h:h + 1, r]
                beta_c = gb[r, 3 * N_HEADS + h:3 * N_HEADS + h + 1]
                decay = jnp.where(incl, jnp.exp(jnp.where(incl, gam_c - gam_r, 0.0)), 0.0)
                P.append(dict(s=s, c=c, h=h, r=r, sl=sl, q=q, k=k, v=v, gam_c=gam_c,
                              beta_c=beta_c, decay=decay))
    for p in P:
        p["kk"] = _dot_nt(p["k"], p["k"])
    for p in P:
        p["qk"] = _dot_nt(p["q"], p["k"]) * p["decay"]
    for p in P:
        e_gam = jnp.exp(p["gam_c"])
        gam_last = p["gam_c"][L - 1:L, :]
        p["pw"] = -jnp.where(strict, p["beta_c"] * p["kk"] * p["decay"], 0.0)
        p["x"] = eye + p["pw"]
        p["rhs"] = jnp.concatenate([p["beta_c"] * p["v"], p["beta_c"] * p["k"] * e_gam], axis=-1)
        p["q_dec"] = p["q"] * e_gam
        p["k_dec"] = p["k"] * jnp.exp(gam_last - p["gam_c"])
        p["e_last"] = jnp.exp(gam_last)
    n = 2
    while n < L:
        for p in P:
            p["pw"] = _dot(p["pw"], p["pw"])
        for p in P:
            p["x"] = p["x"] + _dot(p["x"], p["pw"])
        n *= 2
    for p in P:
        sol = _dot(p["x"], p["rhs"])
        p["u"] = sol[:, :HEAD_DIM]
        p["w"] = sol[:, HEAD_DIM:]

    for c in range(n_sub):
        Pc = [p for p in P if p["c"] == c]
        for p in Pc:
            p["s_prev"] = s_ref[p["s"], p["h"]]
            p["ws"] = _dot(p["w"], p["s_prev"])
        for p in Pc:
            p["qs"] = _dot(p["q_dec"], p["s_prev"])
        for p in Pc:
            p["v_new"] = p["u"] - p["ws"]
        for p in Pc:
            p["o"] = p["qs"] + _dot(p["qk"], p["v_new"])
        for p in Pc:
            p["kv"] = _dot_tn(p["k_dec"], p["v_new"])
        for p in Pc:
            o = p["o"]
            on = o * lax.rsqrt(jnp.mean(o * o, axis=-1, keepdims=True) + EPS) * norm_ref[...]
            p["o_out"] = on * _silu(og_ref[p["r"], p["sl"]])
        for p in Pc:
            s_ref[p["s"], p["h"]] = p["e_last"] * p["s_prev"] + p["kv"]
            o_ref[p["r"], p["sl"]] = p["o_out"]


def _gdn(proj, gs, conv_w, a_row, dt_row, norm_row, init, *, n_batch, seq, s_valid, row_off,
         L, n_sub, G):
    Lt = n_sub * L
    nt = seq // Lt
    rows = G * Lt
    has_init = init is not None
    blk = _seq_tile_specs(G, Lt, nt, row_off)

    def col(c):
        return _proj_spec(rows, c, blk)

    def const(shape):
        return pl.BlockSpec(shape, lambda b, t: (0,) * len(shape))

    in_specs = [col(COL_GQ), col(COL_GK), col(COL_GV), col(COL_GO),
                pl.BlockSpec((rows, LANES), lambda b, t: (blk(b, t), 0)),
                const((CONV_W, 3 * BRANCH_W)), const((1, LANES)), const((1, LANES)),
                const((1, HEAD_DIM))]
    args = [proj, proj, proj, proj, gs, conv_w, a_row, dt_row, norm_row]
    s_spec = pl.BlockSpec((G, N_HEADS, HEAD_DIM, HEAD_DIM), lambda b, t: (b, 0, 0, 0))
    conv_spec = pl.BlockSpec((G, SUBLANES, 3 * BRANCH_W), lambda b, t: (b, 0, 0))
    if has_init:
        in_specs += [conv_spec, s_spec]
        args += list(init)
    return pl.pallas_call(
        functools.partial(_gdn_kernel, L=L, n_sub=n_sub, G=G, s_valid=s_valid, s_total=seq,
                          has_init=has_init),
        out_shape=(jax.ShapeDtypeStruct((n_batch * seq, BRANCH_W), F32),
                   jax.ShapeDtypeStruct((n_batch, N_HEADS, HEAD_DIM, HEAD_DIM), F32),
                   jax.ShapeDtypeStruct((n_batch, SUBLANES, 3 * BRANCH_W), F32)),
        grid=(n_batch // G, nt),
        in_specs=in_specs,
        out_specs=(pl.BlockSpec((rows, BRANCH_W), lambda b, t: (b * nt + t, 0)),
                   s_spec, conv_spec),
        scratch_shapes=[pltpu.VMEM((G, Lt + SUBLANES, 3 * BRANCH_W), F32),
                        pltpu.VMEM((rows, 3 * BRANCH_W), F32)],
        compiler_params=_params(("parallel", "arbitrary")),
        name="gdn",
    )(*args)


def _xattn_kernel(q_ref, mk_ref, mv_ref, o_ref, *, G, lt, mem_len, head_rows):
    scale = HEAD_DIM ** -0.5

    def head(ref, s, h):
        if head_rows:
            return ref[pl.ds(s * mem_len * N_HEADS + h, mem_len, stride=N_HEADS), :]
        return ref[s * mem_len:(s + 1) * mem_len, h * HEAD_DIM:(h + 1) * HEAD_DIM]

    P = []
    for s in range(G):
        for h in range(N_HEADS):
            P.append(dict(s=s, h=h, r=slice(s * lt, (s + 1) * lt),
                          sl=slice(h * HEAD_DIM, (h + 1) * HEAD_DIM)))
    for p in P:
        p["sc"] = _dot_nt(q_ref[p["r"], p["sl"]], head(mk_ref, p["s"], p["h"])) * scale
    for p in P:
        e = jnp.exp(p["sc"] - jnp.max(p["sc"], axis=-1, keepdims=True))
        p["p"] = e / jnp.sum(e, axis=-1, keepdims=True)
    for p in P:
        o_ref[p["r"], p["sl"]] = _dot(p["p"], head(mv_ref, p["s"], p["h"]))


def _xattn(proj, mk, mv, *, n_batch, seq, row_off, lt, G, mem_len, kv_specs, head_rows):
    nt = seq // lt
    blk = _seq_tile_specs(G, lt, nt, row_off)
    return pl.pallas_call(
        functools.partial(_xattn_kernel, G=G, lt=lt, mem_len=mem_len, head_rows=head_rows),
        out_shape=jax.ShapeDtypeStruct((n_batch * seq, BRANCH_W), F32),
        grid=(n_batch // G, nt),
        in_specs=[_proj_spec(G * lt, COL_XQ, blk)] + list(kv_specs),
        out_specs=pl.BlockSpec((G * lt, BRANCH_W), lambda b, t: (b * nt + t, 0)),
        compiler_params=_params(("parallel", "arbitrary")),
        name="xattn",
    )(proj, mk, mv)


def _lane_first_max(x, mask, lane_f):
    xm = jnp.where(mask, x, -jnp.inf)
    mx = jnp.max(xm, axis=-1, keepdims=True)
    idx = jnp.min(jnp.where(xm == mx, lane_f, float(LANES)), axis=-1, keepdims=True)
    return mx, idx


def _store_token_tiles(ref, x, tok_rows):
    n = x.shape[0]
    for j in range(x.shape[1] // LANES):
        ref[pl.ds(j, n, stride=tok_rows), :] = x[:, j * LANES:(j + 1) * LANES]


def _load_token_tiles(ref, n, tok_rows, n_chunks=SUBLANES):
    return jnp.concatenate([ref[pl.ds(j, n, stride=tok_rows), :] for j in range(n_chunks)], axis=1)


def _merge_kernel(hap_ref, has_ref, hbp_ref, hbs_ref, hcp_ref, hcs_ref, xp_ref, xs_ref,
                  ga0_ref, ga1_ref, gb0_ref, gb1_ref, gc0_ref, gc1_ref,
                  wa_ref, wb_ref, wc_ref, wo_ref, nf_ref, wr_ref, br_ref,
                  x1_ref, route_ref, *, n_first):
    i = pl.program_id(0)

    def gate(lo_ref, hi_ref):
        return jnp.concatenate([_sigmoid(lo_ref[...]), _sigmoid(hi_ref[...])], axis=1)

    def body(ha_ref, hb_ref, hc_ref, x_ref):
        mixed = (gate(ga0_ref, ga1_ref) * _dot(ha_ref[...], wa_ref[...])
                 + gate(gb0_ref, gb1_ref) * _dot(hb_ref[...], wb_ref[...])
                 + gate(gc0_ref, gc1_ref) * _dot(hc_ref[...], wc_ref[...]))
        x1 = x_ref[...] + _dot(mixed, wo_ref[...])
        h2 = _rms(x1, nf_ref[...])
        logits = _dot(h2, wr_ref[...]) + br_ref[...]
        lane = lax.broadcasted_iota(jnp.int32, logits.shape, 1)
        lane_f = lane.astype(F32)
        is_g = lane < N_GROUPS
        g_max, g_idx = _lane_first_max(logits, is_g, lane_f)
        g_den = jnp.sum(jnp.where(is_g, jnp.exp(logits - g_max), 0.0), axis=-1, keepdims=True)
        p_grp = 1.0 / g_den
        e_lo = N_GROUPS + EXPERTS_PER_GROUP * g_idx
        in_grp = (lane_f >= e_lo) & (lane_f < e_lo + EXPERTS_PER_GROUP)
        v1, i1 = _lane_first_max(logits, in_grp, lane_f)
        v2, i2 = _lane_first_max(logits, in_grp & (lane_f != i1), lane_f)
        e2 = jnp.exp(v2 - v1)
        w1 = 1.0 / (1.0 + e2)
        w2 = e2 / (1.0 + e2)
        lo = jnp.minimum(i1, i2) - e_lo
        hi = jnp.maximum(i1, i2) - e_lo
        pair = lo * (7.0 - lo) * 0.5 + (hi - lo - 1.0)
        cls = g_idx * float(N_PAIRS) + pair
        first_is_lo = i1 < i2
        w_lo = p_grp * jnp.where(first_is_lo, w1, w2)
        w_hi = p_grp * jnp.where(first_is_lo, w2, w1)
        _store_token_tiles(x1_ref, x1, TOKEN_ROWS)
        route_ref[...] = jnp.where(lane == ROUTE_CLS, cls,
                                   jnp.where(lane == ROUTE_W_LO, w_lo,
                                             jnp.where(lane == ROUTE_W_HI, w_hi, 0.0)))

    @pl.when(i < n_first)
    def _():
        body(hap_ref, hbp_ref, hcp_ref, xp_ref)

    @pl.when(i >= n_first)
    def _():
        body(has_ref, hbs_ref, hcs_ref, xs_ref)


def _merge(ha, hb, hc, xs, proj, wa, wb, wc, wo, nf, wr, br, *, tm):
    d = xs[0].shape[1]
    n_first = xs[0].shape[0] // tm
    n_rows = xs[0].shape[0] + xs[1].shape[0]
    specs = []
    for width in (BRANCH_W, BRANCH_W, BRANCH_W, d):
        specs += list(_two_source_specs(n_first, tm, width))

    def const(a):
        return pl.BlockSpec(a.shape, lambda i: (0,) * a.ndim)

    n_gate = 3 * d // BRANCH_W
    specs += [_proj_spec(tm, COL_GATE + c, lambda i: i) for c in range(n_gate)]
    weights = [wa, wb, wc, wo, nf, wr, br]
    specs += [const(a) for a in weights]
    return pl.pallas_call(
        functools.partial(_merge_kernel, n_first=n_first),
        out_shape=(jax.ShapeDtypeStruct((n_rows * TOKEN_ROWS, LANES), F32),
                   jax.ShapeDtypeStruct((n_rows, LANES), F32)),
        grid=(n_rows // tm,),
        in_specs=specs,
        out_specs=(pl.BlockSpec((tm * TOKEN_ROWS, LANES), lambda i: (i, 0)),
                   pl.BlockSpec((tm, LANES), lambda i: (i, 0))),
        compiler_params=_params(("parallel",)),
        name="merge",
    )(*ha, *hb, *hc, *xs, *([proj] * n_gate), *weights)


GATHER_UNROLL = 8
MOE_SLOTS = 2
ROUTE_CLS, ROUTE_W_LO, ROUTE_W_HI = 0, 1, 2


def _record_gather_start(src_ref, dst_ref, sem, ids_ref, base, n, rec):
    def issue(r, carry):
        src = pl.multiple_of(ids_ref[base + r] * rec, rec)
        dst = pl.multiple_of(r * rec, rec)
        pltpu.make_async_copy(src_ref.at[pl.ds(src, rec)], dst_ref.at[pl.ds(dst, rec)], sem).start()
        return carry
    lax.fori_loop(0, n, issue, 0, unroll=GATHER_UNROLL)


def _record_gather_wait(src_ref, dst_ref, sem):
    pltpu.make_async_copy(src_ref.at[pl.ds(0, dst_ref.shape[0])], dst_ref, sem).wait()


def _moe_kernel(ids_ref, ea_ref, eb_ref, nused_ref, x_hbm, route_ref, wga_ref, wua_ref, wda_ref,
                wgb_ref, wub_ref, wdb_ref, nf_ref, fin_ref, y_ref, xbuf, sem, *, tm, d):
    i = pl.program_id(0)
    n_used = nused_ref[0]
    n_slots = xbuf.shape[0]
    ahead = n_slots - 1
    slot = i % n_slots

    def gather(tile):
        s = tile % n_slots
        _record_gather_start(x_hbm, xbuf.at[s], sem.at[s], ids_ref, tile * tm, tm, TOKEN_ROWS)

    @pl.when(i == 0)
    def _():
        for k in range(ahead):
            @pl.when(k < n_used)
            def _():
                gather(jnp.int32(k))

    @pl.when(i + ahead < n_used)
    def _():
        gather(i + ahead)

    @pl.when(i < n_used)
    def _():
        _record_gather_wait(x_hbm, xbuf.at[slot], sem.at[slot])
        x1 = _load_token_tiles(xbuf.at[slot], tm, TOKEN_ROWS)
        h2 = _rms(x1, nf_ref[...]).astype(BF16)
        acc = jnp.zeros((tm, d), F32)
        for w_lane, wg_ref, wu_ref, wd_ref in ((ROUTE_W_LO, wga_ref, wua_ref, wda_ref),
                                               (ROUTE_W_HI, wgb_ref, wub_ref, wdb_ref)):
            comb = route_ref[:, w_lane:w_lane + 1]
            a = jnp.dot(h2, wg_ref[0], preferred_element_type=F32)
            u = jnp.dot(h2, wu_ref[0], preferred_element_type=F32)
            act = _silu(a) * u * comb
            acc = acc + _dot(act, wd_ref[0])
        _store_token_tiles(y_ref, _rms(x1 + acc, fin_ref[...]), SUBLANES)

    @pl.when(i >= n_used)
    def _():
        y_ref[...] = jnp.zeros_like(y_ref)


def _moe(ids, tile_ea, tile_eb, n_used, x1rec, route_sorted, wg, wu, wd, nf, fin, *, tm, n_tiles):
    d = nf.shape[1]
    f = wg.shape[2]
    assert d == SUBLANES * LANES

    def expert(shape, which):
        return pl.BlockSpec((1,) + shape, lambda i, ids, ea, eb, nu: ((ea, eb)[which][i], 0, 0))

    def const(a):
        return pl.BlockSpec(a.shape, lambda i, *_: (0,) * a.ndim)

    grid_spec = pltpu.PrefetchScalarGridSpec(
        num_scalar_prefetch=4,
        grid=(n_tiles,),
        in_specs=[pl.BlockSpec(memory_space=pl.ANY),
                  pl.BlockSpec((tm, LANES), lambda i, *_: (i, 0)),
                  expert((d, f), 0), expert((d, f), 0), expert((f, d), 0),
                  expert((d, f), 1), expert((d, f), 1), expert((f, d), 1),
                  const(nf), const(fin)],
        out_specs=pl.BlockSpec((tm * SUBLANES, LANES), lambda i, *_: (i, 0)),
        scratch_shapes=[pltpu.VMEM((MOE_SLOTS, tm * TOKEN_ROWS, LANES), F32),
                        pltpu.SemaphoreType.DMA((MOE_SLOTS,))],
    )
    return pl.pallas_call(
        functools.partial(_moe_kernel, tm=tm, d=d),
        out_shape=jax.ShapeDtypeStruct((n_tiles * tm * SUBLANES, LANES), F32),
        grid_spec=grid_spec,
        compiler_params=_params(("arbitrary",)),
        name="moe",
    )(ids, tile_ea, tile_eb, n_used, x1rec, route_sorted, wg, wu, wd, wg, wu, wd, nf, fin)


def _unsort_kernel(pos_ref, y_hbm, o_ref, buf, sem, *, tm, base):
    i = pl.program_id(0)
    slot = i % 2

    @pl.when(i == 0)
    def _():
        _record_gather_start(y_hbm, buf.at[0], sem.at[0], pos_ref, base, tm, SUBLANES)

    @pl.when(i + 1 < pl.num_programs(0))
    def _():
        _record_gather_start(y_hbm, buf.at[1 - slot], sem.at[1 - slot], pos_ref,
                             base + (i + 1) * tm, tm, SUBLANES)

    _record_gather_wait(y_hbm, buf.at[slot], sem.at[slot])
    o_ref[...] = _load_token_tiles(buf.at[slot], tm, SUBLANES)


def _unsort(pos, y_rec, *, base, n_rows, tm):
    d = SUBLANES * LANES
    grid_spec = pltpu.PrefetchScalarGridSpec(
        num_scalar_prefetch=1,
        grid=(n_rows // tm,),
        in_specs=[pl.BlockSpec(memory_space=pl.ANY)],
        out_specs=pl.BlockSpec((tm, d), lambda i, *_: (i, 0)),
        scratch_shapes=[pltpu.VMEM((2, tm * SUBLANES, LANES), F32),
                        pltpu.SemaphoreType.DMA((2,))],
    )
    return pl.pallas_call(
        functools.partial(_unsort_kernel, tm=tm, base=base),
        out_shape=jax.ShapeDtypeStruct((n_rows, d), F32),
        grid_spec=grid_spec,
        compiler_params=_params(("arbitrary",)),
        name="unsort",
    )(pos, y_rec)


def _sort_plan(cls, *, tm, n_tiles):
    n = cls.shape[0]
    onehot = (cls[:, None] == jnp.arange(N_CLASSES, dtype=jnp.int32)[None, :]).astype(jnp.int32)
    csum = jnp.cumsum(onehot, axis=0)
    rank = jnp.sum((csum - onehot) * onehot, axis=1)
    counts = csum[-1]
    tiles_per = (counts + tm - 1) // tm
    tile_end = jnp.cumsum(tiles_per)
    tile_start = tile_end - tiles_per
    pos = (jnp.sum(tile_start[None, :] * onehot, axis=1) * tm + rank).astype(jnp.int32)
    n_used = tile_end[-1].astype(jnp.int32)
    ids = jnp.zeros((n_tiles * tm,), jnp.int32).at[pos].set(jnp.arange(n, dtype=jnp.int32))
    tile = jnp.minimum(jnp.arange(n_tiles, dtype=jnp.int32), n_used - 1)
    tile_cls = jnp.sum((tile[:, None] >= tile_end[None, :]).astype(jnp.int32), axis=1)
    grp = tile_cls // N_PAIRS
    pair = tile_cls % N_PAIRS
    lo = jnp.asarray(PAIR_LO, jnp.int32)[pair]
    hi = jnp.asarray(PAIR_HI, jnp.int32)[pair]
    ea = (grp * EXPERTS_PER_GROUP + lo).astype(jnp.int32)
    eb = (grp * EXPERTS_PER_GROUP + hi).astype(jnp.int32)
    return pos, ids, ea, eb, n_used.reshape(1)


def _lane_row(values, start):
    row = jnp.zeros((LANES,), F32)
    return row.at[start:start + values.shape[0]].set(values.astype(F32)).reshape(1, LANES)


def _layer(x_prompt, x_sample, mem_prompt, cache_mem_k, cache_mem_v,
           state_mlstm_C, state_mlstm_n, state_mlstm_m, state_gdn_S, state_gdn_conv,
           norm_mix, w_in, b_igate, b_fgate, mlstm_norm, conv_w, a_log, dt_bias, gdn_norm,
           mem_norm, w_mem_kv, w_br_mlstm, w_br_gdn, w_br_xattn, w_out, norm_ffn,
           w_router_group, b_router_group, w_router_expert, b_router_expert,
           w_exp_gate, w_exp_up, w_exp_down, final_norm, *, cfg):
    bp, sp, d = x_prompt.shape
    bs, ss, _ = x_sample.shape
    mem_len = mem_prompt.shape[1]
    ss_pad = SUBLANES
    assert ss <= ss_pad and d % LANES == 0
    n_p = bp * sp
    n_s = bs * ss_pad
    W = BRANCH_W
    gs_ = cfg["sample_group"]

    sizes = (W, W, W, N_HEADS, N_HEADS, W, 3 * W, N_HEADS, N_HEADS, W, W, d, d, d)
    offs = [0]
    for s in sizes:
        offs.append(offs[-1] + s)
    (m_q, m_k, m_v, m_i, m_f, m_o, g_qkv, g_a, g_b, g_out, x_q, gate_a, gate_b, gate_c) = [
        w_in[:, offs[j]:offs[j + 1]] for j in range(len(sizes))]
    w_big = _col_blocks(
        jnp.concatenate([gate_a, gate_b, gate_c, m_q, m_k, m_v, m_o, g_qkv, g_out, x_q],
                        axis=1).astype(BF16), PROJ_GROUP * W)
    w_small = jnp.concatenate(
        [m_i, m_f, g_a, g_b, jnp.zeros((d, LANES - 4 * N_HEADS), w_in.dtype)], axis=1).astype(BF16)
    w_router = jnp.concatenate(
        [w_router_group, w_router_expert,
         jnp.zeros((d, LANES - N_GROUPS - N_EXPERTS), w_in.dtype)], axis=1).astype(BF16)
    b_router = _lane_row(jnp.concatenate([b_router_group, b_router_expert]), 0)
    gate_bias = _lane_row(jnp.concatenate([b_igate, b_fgate]), 0)
    a_row = _lane_row(-jnp.exp(a_log.astype(F32)), 2 * N_HEADS)
    dt_row = _lane_row(dt_bias, 2 * N_HEADS)

    xp2 = x_prompt.reshape(n_p, d)
    xs2 = jnp.pad(x_sample, ((0, 0), (0, ss_pad - ss), (0, 0))).reshape(n_s, d)

    proj, gs = _norm_proj(xp2, xs2, norm_mix.reshape(1, d), w_big, w_small, tm=cfg["tm_proj"])

    mem2 = mem_prompt.reshape(bp * mem_len, d)
    mem_kv, _ = _norm_proj(mem2, None, mem_norm.reshape(1, d),
                           _col_blocks(w_mem_kv.astype(BF16), W),
                           jnp.zeros((d, LANES), BF16), tm=mem_len)

    mnorm = mlstm_norm.reshape(1, W)
    gnorm = gdn_norm.reshape(1, HEAD_DIM)
    s_off = n_p // (gs_ * ss_pad)
    ha_p, c_p, nn_p, mm_p = _mlstm(proj, gs, gate_bias, mnorm, None, n_batch=bp, seq=sp,
                                   s_valid=sp, row_off=0, L=cfg["chunk_m"], n_sub=cfg["sub_m"], G=1)
    m0 = jnp.broadcast_to(state_mlstm_m[:, :, None], (bs, N_HEADS, LANES))
    ha_s, c_s, nn_s, mm_s = _mlstm(proj, gs, gate_bias, mnorm,
                                   (state_mlstm_C, state_mlstm_n, m0), n_batch=bs, seq=ss_pad,
                                   s_valid=ss, row_off=s_off, L=ss_pad, n_sub=1, G=gs_)
    hb_p, s_p, cv_p = _gdn(proj, gs, conv_w, a_row, dt_row, gnorm, None, n_batch=bp, seq=sp,
                           s_valid=sp, row_off=0, L=cfg["chunk_g"], n_sub=cfg["sub_g"], G=1)
    conv0 = jnp.pad(state_gdn_conv, ((0, 0), (SUBLANES - (CONV_W - 1), 0), (0, 0)))
    hb_s, s_s, cv_s = _gdn(proj, gs, conv_w, a_row, dt_row, gnorm, (conv0, state_gdn_S),
                           n_batch=bs, seq=ss_pad, s_valid=ss, row_off=s_off, L=ss_pad,
                           n_sub=1, G=gs_)
    lt = min(sp, cfg["lt_x"])
    hc_p = _xattn(proj, mem_kv, mem_kv, n_batch=bp, seq=sp, row_off=0, lt=lt, G=1,
                  mem_len=mem_len, head_rows=False,
                  kv_specs=[pl.BlockSpec((None, mem_len, W), lambda b, t, c=c: (c, b, 0))
                            for c in range(2)])
    kv_rows = gs_ * mem_len * N_HEADS
    hc_s = _xattn(proj, cache_mem_k.reshape(bs * mem_len * N_HEADS, HEAD_DIM),
                  cache_mem_v.reshape(bs * mem_len * N_HEADS, HEAD_DIM),
                  n_batch=bs, seq=ss_pad, row_off=s_off, lt=ss_pad, G=gs_, mem_len=mem_len,
                  head_rows=True,
                  kv_specs=[pl.BlockSpec((kv_rows, HEAD_DIM), lambda b, t: (b, 0))] * 2)

    tm = cfg["tm"]
    x1rec, route = _merge((ha_p, ha_s), (hb_p, hb_s), (hc_p, hc_s), (xp2, xs2), proj,
                          w_br_mlstm.astype(BF16), w_br_gdn.astype(BF16), w_br_xattn.astype(BF16),
                          w_out.astype(BF16), norm_ffn.reshape(1, d), w_router, b_router, tm=tm)

    tm_moe = cfg["tm_moe"]
    n_all = n_p + n_s
    n_tiles = (n_all + N_CLASSES * (tm_moe - 1)) // tm_moe + 1
    cls = route[:, ROUTE_CLS].astype(jnp.int32)
    pos, ids, ea, eb, n_used = _sort_plan(cls, tm=tm_moe, n_tiles=n_tiles)
    route_sorted = jnp.take(route, ids, axis=0)
    y_sorted = _moe(ids, ea, eb, n_used, x1rec, route_sorted, w_exp_gate.astype(BF16),
                    w_exp_up.astype(BF16), w_exp_down.astype(BF16), norm_ffn.reshape(1, d),
                    final_norm.reshape(1, d), tm=tm_moe, n_tiles=n_tiles)
    y_p = _unsort(pos, y_sorted, base=0, n_rows=n_p, tm=tm)
    y_s = _unsort(pos, y_sorted, base=n_p, n_rows=n_s, tm=tm)

    y_prompt = y_p.reshape(bp, sp, d)
    y_sample = y_s.reshape(bs, ss_pad, d)[:, :ss]
    mk_p = mem_kv[0].reshape(bp, mem_len, N_HEADS, HEAD_DIM)
    mv_p = mem_kv[1].reshape(bp, mem_len, N_HEADS, HEAD_DIM)
    tail = slice(SUBLANES - (CONV_W - 1), SUBLANES)
    return (y_prompt, y_sample, mk_p, mv_p,
            c_p, nn_p, mm_p[:, :, 0], s_p, cv_p[:, tail],
            c_s, nn_s, mm_s[:, :, 0], s_s, cv_s[:, tail])


CONFIG = dict(tm_proj=1024, tm=512, tm_moe=256, chunk_m=128, sub_m=2, chunk_g=64, sub_g=4,
              lt_x=512, sample_group=8)


def kernel(x_prompt, x_sample, mem_prompt, cache_mem_k, cache_mem_v, state_mlstm_C, state_mlstm_n, state_mlstm_m, state_gdn_S, state_gdn_conv, norm_mix, w_in, b_igate, b_fgate, mlstm_norm, conv_w, a_log, dt_bias, gdn_norm, mem_norm, w_mem_kv, w_br_mlstm, w_br_gdn, w_br_xattn, w_out, norm_ffn, w_router_group, b_router_group, w_router_expert, b_router_expert, w_exp_gate, w_exp_up, w_exp_down, final_norm):
    depth = w_in.shape[0]
    assert depth == 1, "one decoder layer"
    outs = _layer(
        x_prompt, x_sample, mem_prompt, cache_mem_k[0], cache_mem_v[0],
        state_mlstm_C[0], state_mlstm_n[0], state_mlstm_m[0], state_gdn_S[0], state_gdn_conv[0],
        norm_mix[0], w_in[0], b_igate[0], b_fgate[0], mlstm_norm[0], conv_w[0], a_log[0],
        dt_bias[0], gdn_norm[0], mem_norm[0], w_mem_kv[0], w_br_mlstm[0], w_br_gdn[0],
        w_br_xattn[0], w_out[0], norm_ffn[0], w_router_group[0], b_router_group[0],
        w_router_expert[0], b_router_expert[0], w_exp_gate[0], w_exp_up[0], w_exp_down[0],
        final_norm, cfg=CONFIG)
    y_prompt, y_sample = outs[0], outs[1]
    return (y_prompt, y_sample) + tuple(o[None] for o in outs[2:])
```

```python
import functools

import jax
import jax.numpy as jnp
from jax import lax
from jax.experimental import pallas as pl
from jax.experimental.pallas import tpu as pltpu

F32 = jnp.float32
BF16 = jnp.bfloat16
EPS = 1e-6

N_HEADS = 4
HEAD_DIM = 128
BRANCH_W = N_HEADS * HEAD_DIM
CONV_W = 4
N_GROUPS = 4
EXPERTS_PER_GROUP = 4
N_EXPERTS = N_GROUPS * EXPERTS_PER_GROUP
N_PAIRS = 6
N_CLASSES = N_GROUPS * N_PAIRS
PAIR_LO = (0, 0, 0, 1, 1, 2)
PAIR_HI = (1, 2, 3, 2, 3, 3)
NEG_BIG = -1e30
LANES = 128
SUBLANES = 8
TOKEN_ROWS = 8
PROJ_GROUP = 3
VMEM_LIMIT = 48 * 1024 * 1024

COL_GATE = 0
COL_MQ, COL_MK, COL_MV, COL_MO = 6, 7, 8, 9
COL_GQ, COL_GK, COL_GV, COL_GO, COL_XQ = 10, 11, 12, 13, 14
N_PROJ = 15 * BRANCH_W


def _dot(a, b):
    return jnp.dot(a.astype(BF16), b.astype(BF16), preferred_element_type=F32)


def _dot_nt(a, b):
    return lax.dot_general(a.astype(BF16), b.astype(BF16), (((1,), (1,)), ((), ())),
                           preferred_element_type=F32)


def _dot_tn(a, b):
    return lax.dot_general(a.astype(BF16), b.astype(BF16), (((0,), (0,)), ((), ())),
                           preferred_element_type=F32)


def _dot_exact(a, b):
    return jnp.dot(a, b, preferred_element_type=F32, precision=lax.Precision.HIGHEST)


def _lanes_to_rows(x, lane0):
    r = lax.broadcasted_iota(jnp.int32, (SUBLANES, LANES), 0)
    c = lax.broadcasted_iota(jnp.int32, (SUBLANES, LANES), 1)
    sel = (c == r + lane0).astype(F32)
    return lax.dot_general(sel, x, (((1,), (1,)), ((), ())), preferred_element_type=F32,
                           precision=lax.Precision.HIGHEST)


def _rms(x, g):
    return x * lax.rsqrt(jnp.mean(x * x, axis=-1, keepdims=True) + EPS) * g


def _softplus(x):
    return jnp.maximum(x, 0.0) + jnp.log1p(jnp.exp(-jnp.abs(x)))


def _sigmoid(x):
    return 1.0 / (1.0 + jnp.exp(-x))


def _silu(x):
    return x * _sigmoid(x)


def _tri(n, strict=False):
    r = lax.broadcasted_iota(jnp.int32, (n, n), 0)
    c = lax.broadcasted_iota(jnp.int32, (n, n), 1)
    return (c < r) if strict else (c <= r)


def _tri_blocks(n, block, upper=False):
    shift = block.bit_length() - 1
    assert block == 1 << shift
    r = lax.broadcasted_iota(jnp.int32, (n, n), 0)
    c = lax.broadcasted_iota(jnp.int32, (n, n), 1)
    same = jnp.right_shift(r, shift) == jnp.right_shift(c, shift)
    return same & ((r <= c) if upper else (c <= r))


def _pad_rows(shape, t, seq_rows, n_valid):
    assert seq_rows & (seq_rows - 1) == 0
    row = lax.broadcasted_iota(jnp.int32, shape, 0)
    return (row & (seq_rows - 1)) + t * seq_rows >= n_valid


def _params(sem):
    return pltpu.CompilerParams(dimension_semantics=sem, vmem_limit_bytes=VMEM_LIMIT)


def _two_source_specs(n_first, block, width):
    first = pl.BlockSpec((block, width), lambda i, *_: (jnp.minimum(i, n_first - 1), 0))
    second = pl.BlockSpec((block, width), lambda i, *_: (jnp.maximum(i - n_first, 0), 0))
    return first, second


def _norm_proj_kernel(xa_ref, xb_ref, g_ref, w_ref, ws_ref, o_ref, os_ref, hb_ref, *, n_first):
    i = pl.program_id(0)
    j = pl.program_id(1)

    @pl.when(j == 0)
    def _():
        @pl.when(i < n_first)
        def _():
            hb_ref[...] = _rms(xa_ref[...], g_ref[...]).astype(BF16)

        @pl.when(i >= n_first)
        def _():
            hb_ref[...] = _rms(xb_ref[...], g_ref[...]).astype(BF16)

        os_ref[...] = jnp.dot(hb_ref[...], ws_ref[...], preferred_element_type=F32)

    o_ref[...] = jnp.dot(hb_ref[...], w_ref[0], preferred_element_type=F32)


def _proj_spec(rows, c, row_block):
    return pl.BlockSpec((None, rows, BRANCH_W),
                        lambda *g: (c // PROJ_GROUP, row_block(*g), c % PROJ_GROUP))


def _col_blocks(w, tn):
    d, n = w.shape
    return w.reshape(d, n // tn, tn).transpose(1, 0, 2)


def _norm_proj(xa, xb, g, w, ws, *, tm):
    d = xa.shape[1]
    n_first = xa.shape[0] // tm
    n_rows = xa.shape[0] + (0 if xb is None else xb.shape[0])
    if xb is None:
        xb = xa
    tn = w.shape[2]
    n = w.shape[0] * tn
    sa, sb = _two_source_specs(n_first, tm, d)
    return pl.pallas_call(
        functools.partial(_norm_proj_kernel, n_first=n_first),
        out_shape=(jax.ShapeDtypeStruct((n // tn, n_rows, tn), F32),
                   jax.ShapeDtypeStruct((n_rows, ws.shape[1]), F32)),
        grid=(n_rows // tm, n // tn),
        in_specs=[sa, sb,
                  pl.BlockSpec((1, d), lambda i, j: (0, 0)),
                  pl.BlockSpec((1, d, tn), lambda i, j: (j, 0, 0)),
                  pl.BlockSpec((d, ws.shape[1]), lambda i, j: (0, 0))],
        out_specs=(pl.BlockSpec((None, tm, tn), lambda i, j: (j, i, 0)),
                   pl.BlockSpec((tm, ws.shape[1]), lambda i, j: (i, 0))),
        scratch_shapes=[pltpu.VMEM((tm, d), BF16)],
        compiler_params=_params(("parallel", "arbitrary")),
        name="norm_proj",
    )(xa, xb, g, w, ws)


def _seq_tile_specs(G, Lt, nt, row_off):
    assert G == 1 or nt == 1

    def rows(b, t):
        return row_off + b * nt + t

    return rows


def _mlstm_kernel(*refs, L, n_sub, G, s_valid, s_total, has_init):
    if has_init:
        (q_ref, k_ref, v_ref, og_ref, gs_ref, bias_ref, norm_ref, c0_ref, n0_ref, m0_ref,
         h_ref, c_ref, n_ref, m_ref) = refs
    else:
        (q_ref, k_ref, v_ref, og_ref, gs_ref, bias_ref, norm_ref,
         h_ref, c_ref, n_ref, m_ref) = refs
    t = pl.program_id(1)
    Lt = n_sub * L
    rows = G * Lt

    @pl.when(t == 0)
    def _():
        if has_init:
            c_ref[...] = c0_ref[...]
            n_ref[...] = n0_ref[...]
            m_ref[...] = m0_ref[...]
        else:
            c_ref[...] = jnp.zeros_like(c_ref)
            n_ref[...] = jnp.zeros_like(n_ref)
            m_ref[...] = jnp.zeros_like(m_ref)

    z = gs_ref[...] + bias_ref[...]
    lane = lax.broadcasted_iota(jnp.int32, z.shape, 1)
    log_f = jnp.minimum(z, 0.0) - jnp.log1p(jnp.exp(-jnp.abs(z)))
    g = jnp.where((lane >= N_HEADS) & (lane < 2 * N_HEADS), log_f, z)
    if s_valid < s_total:
        pad = _pad_rows(z.shape, t, Lt, s_valid)
        g = jnp.where(pad, jnp.where(lane < N_HEADS, NEG_BIG, 0.0), g)
    g_t = _lanes_to_rows(g, 0)
    bt_cols = _dot_exact(_tri_blocks(rows, L).astype(F32), g)
    bt_rows = _dot_exact(g_t, _tri_blocks(rows, L, upper=True).astype(F32))
    causal = _tri(L)
    scale = HEAD_DIM ** -0.5
    probs = [(s, h) for s in range(G) for h in range(N_HEADS)]

    for c in range(n_sub):
        P = []
        for s, h in probs:
            r = slice(s * Lt + c * L, s * Lt + (c + 1) * L)
            sl = slice(h * HEAD_DIM, (h + 1) * HEAD_DIM)
            p = dict(s=s, h=h, r=r, sl=sl, q=q_ref[r, sl], k=k_ref[r, sl] * scale, v=v_ref[r, sl],
                     bt_c=bt_cols[r, N_HEADS + h:N_HEADS + h + 1], it_c=g[r, h:h + 1])
            bt_r = bt_rows[N_HEADS + h:N_HEADS + h + 1, r]
            it_r = g_t[h:h + 1, r]
            p["log_d"] = jnp.where(causal, p["bt_c"] - bt_r + it_r, -jnp.inf)
            p["c_prev"] = c_ref[s, h]
            p["n_prev"] = n_ref[s, h:h + 1, :]
            p["m_prev"] = m_ref[s, h:h + 1, 0:1]
            P.append(p)
        for p in P:
            p["qk"] = _dot_nt(p["q"], p["k"])
        for p in P:
            p["qc"] = _dot(p["q"], p["c_prev"])
        for p in P:
            inter = p["bt_c"] + p["m_prev"]
            p["m_t"] = jnp.maximum(inter, jnp.max(p["log_d"], axis=-1, keepdims=True))
            p["inter_w"] = jnp.exp(inter - p["m_t"])
            p["sd"] = p["qk"] * jnp.exp(p["log_d"] - p["m_t"])
        for p in P:
            p["sv"] = _dot(p["sd"], p["v"])
        for p in P:
            bt_last = p["bt_c"][L - 1:L, :]
            p["m_new"] = p["m_t"][L - 1:L, :]
            w = jnp.exp(bt_last - p["bt_c"] + p["it_c"] - p["m_new"])
            p["decay"] = jnp.exp(bt_last + p["m_prev"] - p["m_new"])
            p["kw"] = p["k"] * w
        for p in P:
            p["kv"] = _dot_tn(p["kw"], p["v"])
        for p in P:
            num = p["sv"] + p["qc"] * p["inter_w"]
            den = (jnp.sum(p["sd"], axis=-1, keepdims=True)
                   + p["inter_w"] * jnp.sum(p["q"] * p["n_prev"], axis=-1, keepdims=True))
            hh = num / jnp.maximum(jnp.abs(den), jnp.exp(-p["m_t"]))
            hn = hh * lax.rsqrt(jnp.mean(hh * hh, axis=-1, keepdims=True) + EPS) * norm_ref[:, p["sl"]]
            p["h_out"] = hn * _sigmoid(og_ref[p["r"], p["sl"]])
            p["n_new"] = p["decay"] * p["n_prev"] + jnp.sum(p["kw"], axis=0, keepdims=True)
        for p in P:
            s, h = p["s"], p["h"]
            h_ref[p["r"], p["sl"]] = p["h_out"]
            n_ref[s, h:h + 1, :] = p["n_new"]
            m_ref[s, h:h + 1, :] = jnp.broadcast_to(p["m_new"], (1, LANES))
            c_ref[s, h] = p["decay"] * p["c_prev"] + p["kv"]


def _mlstm(proj, gs, bias_row, norm_row, init, *, n_batch, seq, s_valid, row_off, L, n_sub, G):
    Lt = n_sub * L
    nt = seq // Lt
    rows = G * Lt
    has_init = init is not None
    blk = _seq_tile_specs(G, Lt, nt, row_off)

    def col(c):
        return _proj_spec(rows, c, blk)

    in_specs = [col(COL_MQ), col(COL_MK), col(COL_MV), col(COL_MO),
                pl.BlockSpec((rows, LANES), lambda b, t: (blk(b, t), 0)),
                pl.BlockSpec((1, LANES), lambda b, t: (0, 0)),
                pl.BlockSpec((1, BRANCH_W), lambda b, t: (0, 0))]
    args = [proj, proj, proj, proj, gs, bias_row, norm_row]
    c_spec = pl.BlockSpec((G, N_HEADS, HEAD_DIM, HEAD_DIM), lambda b, t: (b, 0, 0, 0))
    n_spec = pl.BlockSpec((G, N_HEADS, HEAD_DIM), lambda b, t: (b, 0, 0))
    m_spec = pl.BlockSpec((G, N_HEADS, LANES), lambda b, t: (b, 0, 0))
    if has_init:
        in_specs += [c_spec, n_spec, m_spec]
        args += list(init)
    return pl.pallas_call(
        functools.partial(_mlstm_kernel, L=L, n_sub=n_sub, G=G, s_valid=s_valid, s_total=seq,
                          has_init=has_init),
        out_shape=(jax.ShapeDtypeStruct((n_batch * seq, BRANCH_W), F32),
                   jax.ShapeDtypeStruct((n_batch, N_HEADS, HEAD_DIM, HEAD_DIM), F32),
                   jax.ShapeDtypeStruct((n_batch, N_HEADS, HEAD_DIM), F32),
                   jax.ShapeDtypeStruct((n_batch, N_HEADS, LANES), F32)),
        grid=(n_batch // G, nt),
        in_specs=in_specs,
        out_specs=(pl.BlockSpec((rows, BRANCH_W), lambda b, t: (b * nt + t, 0)),
                   c_spec, n_spec, m_spec),
        compiler_params=_params(("parallel", "arbitrary")),
        name="mlstm",
    )(*args)


def _gdn_kernel(*refs, L, n_sub, G, s_valid, s_total, has_init):
    if has_init:
        (q_ref, k_ref, v_ref, og_ref, gs_ref, cw_ref, arow_ref, dtrow_ref, norm_ref,
         conv0_ref, s0_ref, o_ref, s_ref, conv1_ref, ext_ref, conv_ref) = refs
    else:
        (q_ref, k_ref, v_ref, og_ref, gs_ref, cw_ref, arow_ref, dtrow_ref, norm_ref,
         o_ref, s_ref, conv1_ref, ext_ref, conv_ref) = refs
    t = pl.program_id(1)
    Lt = n_sub * L
    rows = G * Lt
    nt = s_total // Lt
    W = BRANCH_W

    @pl.when(t == 0)
    def _():
        if has_init:
            s_ref[...] = s0_ref[...]
            ext_ref[:, 0:SUBLANES, :] = conv0_ref[...]
        else:
            s_ref[...] = jnp.zeros_like(s_ref)
            ext_ref[:, 0:SUBLANES, :] = jnp.zeros((G, SUBLANES, 3 * W), F32)

    @pl.when(t > 0)
    def _():
        ext_ref[:, 0:SUBLANES, :] = ext_ref[:, Lt:Lt + SUBLANES, :]

    base = SUBLANES - (CONV_W - 1)
    for s in range(G):
        rs = slice(s * Lt, (s + 1) * Lt)
        ext_ref[s, SUBLANES:SUBLANES + Lt, 0:W] = q_ref[rs, :]
        ext_ref[s, SUBLANES:SUBLANES + Lt, W:2 * W] = k_ref[rs, :]
        ext_ref[s, SUBLANES:SUBLANES + Lt, 2 * W:3 * W] = v_ref[rs, :]
        acc = ext_ref[s, base:base + Lt, :] * cw_ref[0:1, :]
        for j in range(1, CONV_W):
            acc = acc + ext_ref[s, base + j:base + j + Lt, :] * cw_ref[j:j + 1, :]
        conv_ref[rs, :] = _silu(acc)

    @pl.when(t == nt - 1)
    def _():
        v_last = s_valid - (nt - 1) * Lt
        conv1_ref[...] = ext_ref[:, v_last:v_last + SUBLANES, :]

    z = gs_ref[...]
    lane = lax.broadcasted_iota(jnp.int32, z.shape, 1)
    gdec = arow_ref[...] * _softplus(z + dtrow_ref[...])
    gb = jnp.where((lane >= 2 * N_HEADS) & (lane < 3 * N_HEADS), gdec,
                   jnp.where((lane >= 3 * N_HEADS) & (lane < 4 * N_HEADS), _sigmoid(z), 0.0))
    if s_valid < s_total:
        gb = jnp.where(_pad_rows(z.shape, t, Lt, s_valid), 0.0, gb)
    gb_t = _lanes_to_rows(gb, 2 * N_HEADS)
    gam_cols = _dot_exact(_tri_blocks(rows, L).astype(F32), gb)
    gam_rows = _dot_exact(gb_t, _tri_blocks(rows, L, upper=True).astype(F32))
    incl = _tri(L)
    strict = _tri(L, strict=True)
    eye = (lax.broadcasted_iota(jnp.int32, (L, L), 0)
           == lax.broadcasted_iota(jnp.int32, (L, L), 1)).astype(F32)

    P = []
    for s in range(G):
        for c in range(n_sub):
            for h in range(N_HEADS):
                r = slice(s * Lt + c * L, s * Lt + (c + 1) * L)
                sl = slice(h * HEAD_DIM, (h + 1) * HEAD_DIM)
                q = conv_ref[r, h * HEAD_DIM:(h + 1) * HEAD_DIM]
                k = conv_ref[r, W + h * HEAD_DIM:W + (h + 1) * HEAD_DIM]
                v = conv_ref[r, 2 * W + h * HEAD_DIM:2 * W + (h + 1) * HEAD_DIM]
                q = q * lax.rsqrt(jnp.sum(q * q, axis=-1, keepdims=True) + EPS) * (HEAD_DIM ** -0.5)
                k = k * lax.rsqrt(jnp.sum(k * k, axis=-1, keepdims=True) + EPS)
                gam_c = gam_cols[r, 2 * N_HEADS + h:2 * N_HEADS + h + 1]
                gam_r = gam_rows[h:h + 1, r]
                beta_c = gb[r, 3 * N_HEADS + h:3 * N_HEADS + h + 1]
                decay = jnp.where(incl, jnp.exp(jnp.where(incl, gam_c - gam_r, 0.0)), 0.0)
                P.append(dict(s=s, c=c, h=h, r=r, sl=sl, q=q, k=k, v=v, gam_c=gam_c,
                              beta_c=beta_c, decay=decay))
    for p in P:
        p["kk"] = _dot_nt(p["k"], p["k"])
    for p in P:
        p["qk"] = _dot_nt(p["q"], p["k"]) * p["decay"]
    for p in P:
        e_gam = jnp.exp(p["gam_c"])
        gam_last = p["gam_c"][L - 1:L, :]
        p["pw"] = -jnp.where(strict, p["beta_c"] * p["kk"] * p["decay"], 0.0)
        p["x"] = eye + p["pw"]
        p["rhs"] = jnp.concatenate([p["beta_c"] * p["v"], p["beta_c"] * p["k"] * e_gam], axis=-1)
        p["q_dec"] = p["q"] * e_gam
        p["k_dec"] = p["k"] * jnp.exp(gam_last - p["gam_c"])
        p["e_last"] = jnp.exp(gam_last)
    n = 2
    while n < L:
        for p in P:
            p["pw"] = _dot(p["pw"], p["pw"])
        for p in P:
            p["x"] = p["x"] + _dot(p["x"], p["pw"])
        n *= 2
    for p in P:
        sol = _dot(p["x"], p["rhs"])
        p["u"] = sol[:, :HEAD_DIM]
        p["w"] = sol[:, HEAD_DIM:]

    for c in range(n_sub):
        Pc = [p for p in P if p["c"] == c]
        for p in Pc:
            p["s_prev"] = s_ref[p["s"], p["h"]]
            p["ws"] = _dot(p["w"], p["s_prev"])
        for p in Pc:
            p["qs"] = _dot(p["q_dec"], p["s_prev"])
        for p in Pc:
            p["v_new"] = p["u"] - p["ws"]
        for p in Pc:
            p["o"] = p["qs"] + _dot(p["qk"], p["v_new"])
        for p in Pc:
            p["kv"] = _dot_tn(p["k_dec"], p["v_new"])
        for p in Pc:
            o = p["o"]
            on = o * lax.rsqrt(jnp.mean(o * o, axis=-1, keepdims=True) + EPS) * norm_ref[...]
            p["o_out"] = on * _silu(og_ref[p["r"], p["sl"]])
        for p in Pc:
            s_ref[p["s"], p["h"]] = p["e_last"] * p["s_prev"] + p["kv"]
            o_ref[p["r"], p["sl"]] = p["o_out"]


def _gdn(proj, gs, conv_w, a_row, dt_row, norm_row, init, *, n_batch, seq, s_valid, row_off,
         L, n_sub, G):
    Lt = n_sub * L
    nt = seq // Lt
    rows = G * Lt
    has_init = init is not None
    blk = _seq_tile_specs(G, Lt, nt, row_off)

    def col(c):
        return _proj_spec(rows, c, blk)

    def const(shape):
        return pl.BlockSpec(shape, lambda b, t: (0,) * len(shape))

    in_specs = [col(COL_GQ), col(COL_GK), col(COL_GV), col(COL_GO),
                pl.BlockSpec((rows, LANES), lambda b, t: (blk(b, t), 0)),
                const((CONV_W, 3 * BRANCH_W)), const((1, LANES)), const((1, LANES)),
                const((1, HEAD_DIM))]
    args = [proj, proj, proj, proj, gs, conv_w, a_row, dt_row, norm_row]
    s_spec = pl.BlockSpec((G, N_HEADS, HEAD_DIM, HEAD_DIM), lambda b, t: (b, 0, 0, 0))
    conv_spec = pl.BlockSpec((G, SUBLANES, 3 * BRANCH_W), lambda b, t: (b, 0, 0))
    if has_init:
        in_specs += [conv_spec, s_spec]
        args += list(init)
    return pl.pallas_call(
        functools.partial(_gdn_kernel, L=L, n_sub=n_sub, G=G, s_valid=s_valid, s_total=seq,
                          has_init=has_init),
        out_shape=(jax.ShapeDtypeStruct((n_batch * seq, BRANCH_W), F32),
                   jax.ShapeDtypeStruct((n_batch, N_HEADS, HEAD_DIM, HEAD_DIM), F32),
                   jax.ShapeDtypeStruct((n_batch, SUBLANES, 3 * BRANCH_W), F32)),
        grid=(n_batch // G, nt),
        in_specs=in_specs,
        out_specs=(pl.BlockSpec((rows, BRANCH_W), lambda b, t: (b * nt + t, 0)),
                   s_spec, conv_spec),
        scratch_shapes=[pltpu.VMEM((G, Lt + SUBLANES, 3 * BRANCH_W), F32),
                        pltpu.VMEM((rows, 3 * BRANCH_W), F32)],
        compiler_params=_params(("parallel", "arbitrary")),
        name="gdn",
    )(*args)


def _xattn_kernel(q_ref, mk_ref, mv_ref, o_ref, *, G, lt, mem_len, head_rows):
    scale = HEAD_DIM ** -0.5

    def head(ref, s, h):
        if head_rows:
            return ref[pl.ds(s * mem_len * N_HEADS + h, mem_len, stride=N_HEADS), :]
        return ref[s * mem_len:(s + 1) * mem_len, h * HEAD_DIM:(h + 1) * HEAD_DIM]

    P = []
    for s in range(G):
        for h in range(N_HEADS):
            P.append(dict(s=s, h=h, r=slice(s * lt, (s + 1) * lt),
                          sl=slice(h * HEAD_DIM, (h + 1) * HEAD_DIM)))
    for p in P:
        p["sc"] = _dot_nt(q_ref[p["r"], p["sl"]], head(mk_ref, p["s"], p["h"])) * scale
    for p in P:
        e = jnp.exp(p["sc"] - jnp.max(p["sc"], axis=-1, keepdims=True))
        p["p"] = e / jnp.sum(e, axis=-1, keepdims=True)
    for p in P:
        o_ref[p["r"], p["sl"]] = _dot(p["p"], head(mv_ref, p["s"], p["h"]))


def _xattn(proj, mk, mv, *, n_batch, seq, row_off, lt, G, mem_len, kv_specs, head_rows):
    nt = seq // lt
    blk = _seq_tile_specs(G, lt, nt, row_off)
    return pl.pallas_call(
        functools.partial(_xattn_kernel, G=G, lt=lt, mem_len=mem_len, head_rows=head_rows),
        out_shape=jax.ShapeDtypeStruct((n_batch * seq, BRANCH_W), F32),
        grid=(n_batch // G, nt),
        in_specs=[_proj_spec(G * lt, COL_XQ, blk)] + list(kv_specs),
        out_specs=pl.BlockSpec((G * lt, BRANCH_W), lambda b, t: (b * nt + t, 0)),
        compiler_params=_params(("parallel", "arbitrary")),
        name="xattn",
    )(proj, mk, mv)


def _lane_first_max(x, mask, lane_f):
    xm = jnp.where(mask, x, -jnp.inf)
    mx = jnp.max(xm, axis=-1, keepdims=True)
    idx = jnp.min(jnp.where(xm == mx, lane_f, float(LANES)), axis=-1, keepdims=True)
    return mx, idx


def _store_token_tiles(ref, x, tok_rows):
    n = x.shape[0]
    for j in range(x.shape[1] // LANES):
        ref[pl.ds(j, n, stride=tok_rows), :] = x[:, j * LANES:(j + 1) * LANES]


def _load_token_tiles(ref, n, tok_rows, n_chunks=SUBLANES):
    return jnp.concatenate([ref[pl.ds(j, n, stride=tok_rows), :] for j in range(n_chunks)], axis=1)


def _merge_kernel(hap_ref, has_ref, hbp_ref, hbs_ref, hcp_ref, hcs_ref, xp_ref, xs_ref,
                  ga0_ref, ga1_ref, gb0_ref, gb1_ref, gc0_ref, gc1_ref,
                  wa_ref, wb_ref, wc_ref, wo_ref, nf_ref, wr_ref, br_ref,
                  x1_ref, route_ref, *, n_first):
    i = pl.program_id(0)

    def gate(lo_ref, hi_ref):
        return jnp.concatenate([_sigmoid(lo_ref[...]), _sigmoid(hi_ref[...])], axis=1)

    def body(ha_ref, hb_ref, hc_ref, x_ref):
        mixed = (gate(ga0_ref, ga1_ref) * _dot(ha_ref[...], wa_ref[...])
                 + gate(gb0_ref, gb1_ref) * _dot(hb_ref[...], wb_ref[...])
                 + gate(gc0_ref, gc1_ref) * _dot(hc_ref[...], wc_ref[...]))
        x1 = x_ref[...] + _dot(mixed, wo_ref[...])
        h2 = _rms(x1, nf_ref[...])
        logits = _dot(h2, wr_ref[...]) + br_ref[...]
        lane = lax.broadcasted_iota(jnp.int32, logits.shape, 1)
        lane_f = lane.astype(F32)
        is_g = lane < N_GROUPS
        _, g_idx = _lane_first_max(logits, is_g, lane_f)
        e_lo = N_GROUPS + EXPERTS_PER_GROUP * g_idx
        in_grp = (lane_f >= e_lo) & (lane_f < e_lo + EXPERTS_PER_GROUP)
        _, i1 = _lane_first_max(logits, in_grp, lane_f)
        _, i2 = _lane_first_max(logits, in_grp & (lane_f != i1), lane_f)
        lo = jnp.minimum(i1, i2) - e_lo
        hi = jnp.maximum(i1, i2) - e_lo
        pair = lo * (7.0 - lo) * 0.5 + (hi - lo - 1.0)
        cls = g_idx * float(N_PAIRS) + pair
        _store_token_tiles(x1_ref, x1, TOKEN_ROWS)
        route_ref[...] = jnp.where(lane == ROUTE_CLS, cls, 0.0)

    @pl.when(i < n_first)
    def _():
        body(hap_ref, hbp_ref, hcp_ref, xp_ref)

    @pl.when(i >= n_first)
    def _():
        body(has_ref, hbs_ref, hcs_ref, xs_ref)


def _merge(ha, hb, hc, xs, proj, wa, wb, wc, wo, nf, wr, br, *, tm):
    d = xs[0].shape[1]
    n_first = xs[0].shape[0] // tm
    n_rows = xs[0].shape[0] + xs[1].shape[0]
    specs = []
    for width in (BRANCH_W, BRANCH_W, BRANCH_W, d):
        specs += list(_two_source_specs(n_first, tm, width))

    def const(a):
        return pl.BlockSpec(a.shape, lambda i: (0,) * a.ndim)

    n_gate = 3 * d // BRANCH_W
    specs += [_proj_spec(tm, COL_GATE + c, lambda i: i) for c in range(n_gate)]
    weights = [wa, wb, wc, wo, nf, wr, br]
    specs += [const(a) for a in weights]
    return pl.pallas_call(
        functools.partial(_merge_kernel, n_first=n_first),
        out_shape=(jax.ShapeDtypeStruct((n_rows * TOKEN_ROWS, LANES), F32),
                   jax.ShapeDtypeStruct((n_rows, LANES), F32)),
        grid=(n_rows // tm,),
        in_specs=specs,
        out_specs=(pl.BlockSpec((tm * TOKEN_ROWS, LANES), lambda i: (i, 0)),
                   pl.BlockSpec((tm, LANES), lambda i: (i, 0))),
        compiler_params=_params(("parallel",)),
        name="merge",
    )(*ha, *hb, *hc, *xs, *([proj] * n_gate), *weights)


GATHER_UNROLL = 8
MOE_SLOTS = 2
ROUTE_CLS = 0


def _record_gather_start(src_ref, dst_ref, sem, ids_ref, base, n, rec):
    def issue(r, carry):
        src = pl.multiple_of(ids_ref[base + r] * rec, rec)
        dst = pl.multiple_of(r * rec, rec)
        pltpu.make_async_copy(src_ref.at[pl.ds(src, rec)], dst_ref.at[pl.ds(dst, rec)], sem).start()
        return carry
    lax.fori_loop(0, n, issue, 0, unroll=GATHER_UNROLL)


def _record_gather_wait(src_ref, dst_ref, sem):
    pltpu.make_async_copy(src_ref.at[pl.ds(0, dst_ref.shape[0])], dst_ref, sem).wait()


def _moe_kernel(ids_ref, ea_ref, eb_ref, nused_ref, x_hbm, wga_ref, wua_ref, wda_ref,
                wgb_ref, wub_ref, wdb_ref, nf_ref, fin_ref, wr_ref, br_ref, y_ref, xbuf, sem,
                *, tm, d):
    i = pl.program_id(0)
    n_used = nused_ref[0]
    n_slots = xbuf.shape[0]
    ahead = n_slots - 1
    slot = i % n_slots

    def gather(tile):
        s = tile % n_slots
        _record_gather_start(x_hbm, xbuf.at[s], sem.at[s], ids_ref, tile * tm, tm, TOKEN_ROWS)

    @pl.when(i == 0)
    def _():
        for k in range(ahead):
            @pl.when(k < n_used)
            def _():
                gather(jnp.int32(k))

    @pl.when(i + ahead < n_used)
    def _():
        gather(i + ahead)

    @pl.when(i < n_used)
    def _():
        _record_gather_wait(x_hbm, xbuf.at[slot], sem.at[slot])
        x1 = _load_token_tiles(xbuf.at[slot], tm, TOKEN_ROWS)
        h2 = _rms(x1, nf_ref[...]).astype(BF16)
        logits = jnp.dot(h2, wr_ref[...], preferred_element_type=F32) + br_ref[...]
        lane = lax.broadcasted_iota(jnp.int32, logits.shape, 1)

        def pick(l):
            return jnp.sum(jnp.where(lane == l, logits, 0.0), axis=-1, keepdims=True)

        ea, eb = ea_ref[i], eb_ref[i]
        is_g = lane < N_GROUPS
        g_max = jnp.max(jnp.where(is_g, logits, -jnp.inf), axis=-1, keepdims=True)
        g_den = jnp.sum(jnp.where(is_g, jnp.exp(logits - g_max), 0.0), axis=-1, keepdims=True)
        p_grp = jnp.exp(pick(ea // EXPERTS_PER_GROUP) - g_max) / g_den
        v_a, v_b = pick(N_GROUPS + ea), pick(N_GROUPS + eb)
        combs = (p_grp / (1.0 + jnp.exp(v_b - v_a)), p_grp / (1.0 + jnp.exp(v_a - v_b)))
        acc = jnp.zeros((tm, d), F32)
        for comb, wg_ref, wu_ref, wd_ref in ((combs[0], wga_ref, wua_ref, wda_ref),
                                             (combs[1], wgb_ref, wub_ref, wdb_ref)):
            a = jnp.dot(h2, wg_ref[0], preferred_element_type=F32)
            u = jnp.dot(h2, wu_ref[0], preferred_element_type=F32)
            act = _silu(a) * u * comb
            acc = acc + _dot(act, wd_ref[0])
        _store_token_tiles(y_ref, _rms(x1 + acc, fin_ref[...]), SUBLANES)

    @pl.when(i >= n_used)
    def _():
        y_ref[...] = jnp.zeros_like(y_ref)


def _moe(ids, tile_ea, tile_eb, n_used, x1rec, wg, wu, wd, nf, fin, wr, br, *, tm, n_tiles):
    d = nf.shape[1]
    f = wg.shape[2]
    assert d == SUBLANES * LANES

    def expert(shape, which):
        return pl.BlockSpec((1,) + shape, lambda i, ids, ea, eb, nu: ((ea, eb)[which][i], 0, 0))

    def const(a):
        return pl.BlockSpec(a.shape, lambda i, *_: (0,) * a.ndim)

    grid_spec = pltpu.PrefetchScalarGridSpec(
        num_scalar_prefetch=4,
        grid=(n_tiles,),
        in_specs=[pl.BlockSpec(memory_space=pl.ANY),
                  expert((d, f), 0), expert((d, f), 0), expert((f, d), 0),
                  expert((d, f), 1), expert((d, f), 1), expert((f, d), 1),
                  const(nf), const(fin), const(wr), const(br)],
        out_specs=pl.BlockSpec((tm * SUBLANES, LANES), lambda i, *_: (i, 0)),
        scratch_shapes=[pltpu.VMEM((MOE_SLOTS, tm * TOKEN_ROWS, LANES), F32),
                        pltpu.SemaphoreType.DMA((MOE_SLOTS,))],
    )
    return pl.pallas_call(
        functools.partial(_moe_kernel, tm=tm, d=d),
        out_shape=jax.ShapeDtypeStruct((n_tiles * tm * SUBLANES, LANES), F32),
        grid_spec=grid_spec,
        compiler_params=_params(("arbitrary",)),
        name="moe",
    )(ids, tile_ea, tile_eb, n_used, x1rec, wg, wu, wd, wg, wu, wd, nf, fin, wr, br)


def _unsort_kernel(pos_ref, y_hbm, o_ref, buf, sem, *, tm, base):
    i = pl.program_id(0)
    slot = i % 2

    @pl.when(i == 0)
    def _():
        _record_gather_start(y_hbm, buf.at[0], sem.at[0], pos_ref, base, tm, SUBLANES)

    @pl.when(i + 1 < pl.num_programs(0))
    def _():
        _record_gather_start(y_hbm, buf.at[1 - slot], sem.at[1 - slot], pos_ref,
                             base + (i + 1) * tm, tm, SUBLANES)

    _record_gather_wait(y_hbm, buf.at[slot], sem.at[slot])
    o_ref[...] = _load_token_tiles(buf.at[slot], tm, SUBLANES)


def _unsort(pos, y_rec, *, base, n_rows, tm):
    d = SUBLANES * LANES
    grid_spec = pltpu.PrefetchScalarGridSpec(
        num_scalar_prefetch=1,
        grid=(n_rows // tm,),
        in_specs=[pl.BlockSpec(memory_space=pl.ANY)],
        out_specs=pl.BlockSpec((tm, d), lambda i, *_: (i, 0)),
        scratch_shapes=[pltpu.VMEM((2, tm * SUBLANES, LANES), F32),
                        pltpu.SemaphoreType.DMA((2,))],
    )
    return pl.pallas_call(
        functools.partial(_unsort_kernel, tm=tm, base=base),
        out_shape=jax.ShapeDtypeStruct((n_rows, d), F32),
        grid_spec=grid_spec,
        compiler_params=_params(("arbitrary",)),
        name="unsort",
    )(pos, y_rec)


def _sort_plan(cls, *, tm, n_tiles):
    n = cls.shape[0]
    onehot = (cls[:, None] == jnp.arange(N_CLASSES, dtype=jnp.int32)[None, :]).astype(jnp.int32)
    csum = jnp.cumsum(onehot, axis=0)
    rank = jnp.sum((csum - onehot) * onehot, axis=1)
    counts = csum[-1]
    tiles_per = (counts + tm - 1) // tm
    tile_end = jnp.cumsum(tiles_per)
    tile_start = tile_end - tiles_per
    pos = (jnp.sum(tile_start[None, :] * onehot, axis=1) * tm + rank).astype(jnp.int32)
    n_used = tile_end[-1].astype(jnp.int32)
    ids = jnp.zeros((n_tiles * tm,), jnp.int32).at[pos].set(jnp.arange(n, dtype=jnp.int32))
    tile = jnp.minimum(jnp.arange(n_tiles, dtype=jnp.int32), n_used - 1)
    tile_cls = jnp.sum((tile[:, None] >= tile_end[None, :]).astype(jnp.int32), axis=1)
    grp = tile_cls // N_PAIRS
    pair = tile_cls % N_PAIRS
    lo = jnp.asarray(PAIR_LO, jnp.int32)[pair]
    hi = jnp.asarray(PAIR_HI, jnp.int32)[pair]
    ea = (grp * EXPERTS_PER_GROUP + lo).astype(jnp.int32)
    eb = (grp * EXPERTS_PER_GROUP + hi).astype(jnp.int32)
    return pos, ids, ea, eb, n_used.reshape(1)


def _lane_row(values, start):
    row = jnp.zeros((LANES,), F32)
    return row.at[start:start + values.shape[0]].set(values.astype(F32)).reshape(1, LANES)


def _layer(x_prompt, x_sample, mem_prompt, cache_mem_k, cache_mem_v,
           state_mlstm_C, state_mlstm_n, state_mlstm_m, state_gdn_S, state_gdn_conv,
           norm_mix, w_in, b_igate, b_fgate, mlstm_norm, conv_w, a_log, dt_bias, gdn_norm,
           mem_norm, w_mem_kv, w_br_mlstm, w_br_gdn, w_br_xattn, w_out, norm_ffn,
           w_router_group, b_router_group, w_router_expert, b_router_expert,
           w_exp_gate, w_exp_up, w_exp_down, final_norm, *, cfg):
    bp, sp, d = x_prompt.shape
    bs, ss, _ = x_sample.shape
    mem_len = mem_prompt.shape[1]
    ss_pad = SUBLANES
    assert ss <= ss_pad and d % LANES == 0
    n_p = bp * sp
    n_s = bs * ss_pad
    W = BRANCH_W
    gs_ = cfg["sample_group"]

    sizes = (W, W, W, N_HEADS, N_HEADS, W, 3 * W, N_HEADS, N_HEADS, W, W, d, d, d)
    offs = [0]
    for s in sizes:
        offs.append(offs[-1] + s)
    (m_q, m_k, m_v, m_i, m_f, m_o, g_qkv, g_a, g_b, g_out, x_q, gate_a, gate_b, gate_c) = [
        w_in[:, offs[j]:offs[j + 1]] for j in range(len(sizes))]
    w_big = _col_blocks(
        jnp.concatenate([gate_a, gate_b, gate_c, m_q, m_k, m_v, m_o, g_qkv, g_out, x_q],
                        axis=1).astype(BF16), PROJ_GROUP * W)
    w_small = jnp.concatenate(
        [m_i, m_f, g_a, g_b, jnp.zeros((d, LANES - 4 * N_HEADS), w_in.dtype)], axis=1).astype(BF16)
    w_router = jnp.concatenate(
        [w_router_group, w_router_expert,
         jnp.zeros((d, LANES - N_GROUPS - N_EXPERTS), w_in.dtype)], axis=1).astype(BF16)
    b_router = _lane_row(jnp.concatenate([b_router_group, b_router_expert]), 0)
    gate_bias = _lane_row(jnp.concatenate([b_igate, b_fgate]), 0)
    a_row = _lane_row(-jnp.exp(a_log.astype(F32)), 2 * N_HEADS)
    dt_row = _lane_row(dt_bias, 2 * N_HEADS)

    xp2 = x_prompt.reshape(n_p, d)
    xs2 = jnp.pad(x_sample, ((0, 0), (0, ss_pad - ss), (0, 0))).reshape(n_s, d)

    proj, gs = _norm_proj(xp2, xs2, norm_mix.reshape(1, d), w_big, w_small, tm=cfg["tm_proj"])

    mem2 = mem_prompt.reshape(bp * mem_len, d)
    mem_kv, _ = _norm_proj(mem2, None, mem_norm.reshape(1, d),
                           _col_blocks(w_mem_kv.astype(BF16), W),
                           jnp.zeros((d, LANES), BF16), tm=mem_len)

    mnorm = mlstm_norm.reshape(1, W)
    gnorm = gdn_norm.reshape(1, HEAD_DIM)
    s_off = n_p // (gs_ * ss_pad)
    ha_p, c_p, nn_p, mm_p = _mlstm(proj, gs, gate_bias, mnorm, None, n_batch=bp, seq=sp,
                                   s_valid=sp, row_off=0, L=cfg["chunk_m"], n_sub=cfg["sub_m"], G=1)
    m0 = jnp.broadcast_to(state_mlstm_m[:, :, None], (bs, N_HEADS, LANES))
    ha_s, c_s, nn_s, mm_s = _mlstm(proj, gs, gate_bias, mnorm,
                                   (state_mlstm_C, state_mlstm_n, m0), n_batch=bs, seq=ss_pad,
                                   s_valid=ss, row_off=s_off, L=ss_pad, n_sub=1, G=gs_)
    hb_p, s_p, cv_p = _gdn(proj, gs, conv_w, a_row, dt_row, gnorm, None, n_batch=bp, seq=sp,
                           s_valid=sp, row_off=0, L=cfg["chunk_g"], n_sub=cfg["sub_g"], G=1)
    conv0 = jnp.pad(state_gdn_conv, ((0, 0), (SUBLANES - (CONV_W - 1), 0), (0, 0)))
    hb_s, s_s, cv_s = _gdn(proj, gs, conv_w, a_row, dt_row, gnorm, (conv0, state_gdn_S),
                           n_batch=bs, seq=ss_pad, s_valid=ss, row_off=s_off, L=ss_pad,
                           n_sub=1, G=gs_)
    lt = min(sp, cfg["lt_x"])
    hc_p = _xattn(proj, mem_kv, mem_kv, n_batch=bp, seq=sp, row_off=0, lt=lt, G=1,
                  mem_len=mem_len, head_rows=False,
                  kv_specs=[pl.BlockSpec((None, mem_len, W), lambda b, t, c=c: (c, b, 0))
                            for c in range(2)])
    kv_rows = gs_ * mem_len * N_HEADS
    hc_s = _xattn(proj, cache_mem_k.reshape(bs * mem_len * N_HEADS, HEAD_DIM),
                  cache_mem_v.reshape(bs * mem_len * N_HEADS, HEAD_DIM),
                  n_batch=bs, seq=ss_pad, row_off=s_off, lt=ss_pad, G=gs_, mem_len=mem_len,
                  head_rows=True,
                  kv_specs=[pl.BlockSpec((kv_rows, HEAD_DIM), lambda b, t: (b, 0))] * 2)

    tm = cfg["tm"]
    x1rec, route = _merge((ha_p, ha_s), (hb_p, hb_s), (hc_p, hc_s), (xp2, xs2), proj,
                          w_br_mlstm.astype(BF16), w_br_gdn.astype(BF16), w_br_xattn.astype(BF16),
                          w_out.astype(BF16), norm_ffn.reshape(1, d), w_router, b_router, tm=tm)

    tm_moe = cfg["tm_moe"]
    n_all = n_p + n_s
    n_tiles = (n_all + N_CLASSES * (tm_moe - 1)) // tm_moe + 1
    cls = route[:, ROUTE_CLS].astype(jnp.int32)
    pos, ids, ea, eb, n_used = _sort_plan(cls, tm=tm_moe, n_tiles=n_tiles)
    y_sorted = _moe(ids, ea, eb, n_used, x1rec, w_exp_gate.astype(BF16),
                    w_exp_up.astype(BF16), w_exp_down.astype(BF16), norm_ffn.reshape(1, d),
                    final_norm.reshape(1, d), w_router, b_router, tm=tm_moe, n_tiles=n_tiles)
    y_p = _unsort(pos, y_sorted, base=0, n_rows=n_p, tm=tm)
    y_s = _unsort(pos, y_sorted, base=n_p, n_rows=n_s, tm=tm)

    y_prompt = y_p.reshape(bp, sp, d)
    y_sample = y_s.reshape(bs, ss_pad, d)[:, :ss]
    mk_p = mem_kv[0].reshape(bp, mem_len, N_HEADS, HEAD_DIM)
    mv_p = mem_kv[1].reshape(bp, mem_len, N_HEADS, HEAD_DIM)
    tail = slice(SUBLANES - (CONV_W - 1), SUBLANES)
    return (y_prompt, y_sample, mk_p, mv_p,
            c_p, nn_p, mm_p[:, :, 0], s_p, cv_p[:, tail],
            c_s, nn_s, mm_s[:, :, 0], s_s, cv_s[:, tail])


CONFIG = dict(tm_proj=1024, tm=512, tm_moe=256, chunk_m=128, sub_m=2, chunk_g=64, sub_g=4,
              lt_x=512, sample_group=8)


def kernel(x_prompt, x_sample, mem_prompt, cache_mem_k, cache_mem_v, state_mlstm_C, state_mlstm_n, state_mlstm_m, state_gdn_S, state_gdn_conv, norm_mix, w_in, b_igate, b_fgate, mlstm_norm, conv_w, a_log, dt_bias, gdn_norm, mem_norm, w_mem_kv, w_br_mlstm, w_br_gdn, w_br_xattn, w_out, norm_ffn, w_router_group, b_router_group, w_router_expert, b_router_expert, w_exp_gate, w_exp_up, w_exp_down, final_norm):
    depth = w_in.shape[0]
    assert depth == 1, "one decoder layer"
    outs = _layer(
        x_prompt, x_sample, mem_prompt, cache_mem_k[0], cache_mem_v[0],
        state_mlstm_C[0], state_mlstm_n[0], state_mlstm_m[0], state_gdn_S[0], state_gdn_conv[0],
        norm_mix[0], w_in[0], b_igate[0], b_fgate[0], mlstm_norm[0], conv_w[0], a_log[0],
        dt_bias[0], gdn_norm[0], mem_norm[0], w_mem_kv[0], w_br_mlstm[0], w_br_gdn[0],
        w_br_xattn[0], w_out[0], norm_ffn[0], w_router_group[0], b_router_group[0],
        w_router_expert[0], b_router_expert[0], w_exp_gate[0], w_exp_up[0], w_exp_down[0],
        final_norm, cfg=CONFIG)
    y_prompt, y_sample = outs[0], outs[1]
    return (y_prompt, y_sample) + tuple(o[None] for o in outs[2:])
```

```python
import functools

import jax
import jax.numpy as jnp
from jax import lax
from jax.experimental import pallas as pl
from jax.experimental.pallas import tpu as pltpu

F32 = jnp.float32
BF16 = jnp.bfloat16
EPS = 1e-6

N_HEADS = 4
HEAD_DIM = 128
BRANCH_W = N_HEADS * HEAD_DIM
CONV_W = 4
N_GROUPS = 4
EXPERTS_PER_GROUP = 4
N_EXPERTS = N_GROUPS * EXPERTS_PER_GROUP
N_PAIRS = 6
N_CLASSES = N_GROUPS * N_PAIRS
PAIR_LO = (0, 0, 0, 1, 1, 2)
PAIR_HI = (1, 2, 3, 2, 3, 3)
NEG_BIG = -1e30
LANES = 128
SUBLANES = 8
TOKEN_ROWS = 8
PROJ_GROUP = 3
VMEM_LIMIT = 48 * 1024 * 1024

COL_GATE = 0
COL_MQ, COL_MK, COL_MV, COL_MO = 6, 7, 8, 9
COL_GQ, COL_GK, COL_GV, COL_GO, COL_XQ = 10, 11, 12, 13, 14
N_PROJ = 15 * BRANCH_W


def _dot(a, b):
    return jnp.dot(a.astype(BF16), b.astype(BF16), preferred_element_type=F32)


def _dot_nt(a, b):
    return lax.dot_general(a.astype(BF16), b.astype(BF16), (((1,), (1,)), ((), ())),
                           preferred_element_type=F32)


def _dot_tn(a, b):
    return lax.dot_general(a.astype(BF16), b.astype(BF16), (((0,), (0,)), ((), ())),
                           preferred_element_type=F32)


def _dot_exact(a, b):
    return jnp.dot(a, b, preferred_element_type=F32, precision=lax.Precision.HIGHEST)


def _lanes_to_rows(x, lane0):
    r = lax.broadcasted_iota(jnp.int32, (SUBLANES, LANES), 0)
    c = lax.broadcasted_iota(jnp.int32, (SUBLANES, LANES), 1)
    sel = (c == r + lane0).astype(F32)
    return lax.dot_general(sel, x, (((1,), (1,)), ((), ())), preferred_element_type=F32,
                           precision=lax.Precision.HIGHEST)


def _rms(x, g):
    return x * lax.rsqrt(jnp.mean(x * x, axis=-1, keepdims=True) + EPS) * g


def _softplus(x):
    return jnp.maximum(x, 0.0) + jnp.log1p(jnp.exp(-jnp.abs(x)))


def _sigmoid(x):
    return 1.0 / (1.0 + jnp.exp(-x))


def _silu(x):
    return x * _sigmoid(x)


def _tri(n, strict=False):
    r = lax.broadcasted_iota(jnp.int32, (n, n), 0)
    c = lax.broadcasted_iota(jnp.int32, (n, n), 1)
    return (c < r) if strict else (c <= r)


def _tri_blocks(n, block, upper=False):
    shift = block.bit_length() - 1
    assert block == 1 << shift
    r = lax.broadcasted_iota(jnp.int32, (n, n), 0)
    c = lax.broadcasted_iota(jnp.int32, (n, n), 1)
    same = jnp.right_shift(r, shift) == jnp.right_shift(c, shift)
    return same & ((r <= c) if upper else (c <= r))


def _pad_rows(shape, t, seq_rows, n_valid):
    assert seq_rows & (seq_rows - 1) == 0
    row = lax.broadcasted_iota(jnp.int32, shape, 0)
    return (row & (seq_rows - 1)) + t * seq_rows >= n_valid


def _params(sem):
    return pltpu.CompilerParams(dimension_semantics=sem, vmem_limit_bytes=VMEM_LIMIT)


def _two_source_specs(n_first, block, width):
    first = pl.BlockSpec((block, width), lambda i, *_: (jnp.minimum(i, n_first - 1), 0))
    second = pl.BlockSpec((block, width), lambda i, *_: (jnp.maximum(i - n_first, 0), 0))
    return first, second


def _norm_proj_kernel(xa_ref, xb_ref, g_ref, w_ref, ws_ref, o_ref, os_ref, hb_ref, *, n_first):
    i = pl.program_id(0)
    j = pl.program_id(1)

    @pl.when(j == 0)
    def _():
        @pl.when(i < n_first)
        def _():
            hb_ref[...] = _rms(xa_ref[...], g_ref[...]).astype(BF16)

        @pl.when(i >= n_first)
        def _():
            hb_ref[...] = _rms(xb_ref[...], g_ref[...]).astype(BF16)

        os_ref[...] = jnp.dot(hb_ref[...], ws_ref[...], preferred_element_type=F32)

    o_ref[...] = jnp.dot(hb_ref[...], w_ref[0], preferred_element_type=F32)


def _proj_spec(rows, c, row_block):
    return pl.BlockSpec((None, rows, BRANCH_W),
                        lambda *g: (c // PROJ_GROUP, row_block(*g), c % PROJ_GROUP))


def _col_blocks(w, tn):
    d, n = w.shape
    return w.reshape(d, n // tn, tn).transpose(1, 0, 2)


def _norm_proj(xa, xb, g, w, ws, *, tm):
    d = xa.shape[1]
    n_first = xa.shape[0] // tm
    n_rows = xa.shape[0] + (0 if xb is None else xb.shape[0])
    if xb is None:
        xb = xa
    tn = w.shape[2]
    n = w.shape[0] * tn
    sa, sb = _two_source_specs(n_first, tm, d)
    return pl.pallas_call(
        functools.partial(_norm_proj_kernel, n_first=n_first),
        out_shape=(jax.ShapeDtypeStruct((n // tn, n_rows, tn), F32),
                   jax.ShapeDtypeStruct((n_rows, ws.shape[1]), F32)),
        grid=(n_rows // tm, n // tn),
        in_specs=[sa, sb,
                  pl.BlockSpec((1, d), lambda i, j: (0, 0)),
                  pl.BlockSpec((1, d, tn), lambda i, j: (j, 0, 0)),
                  pl.BlockSpec((d, ws.shape[1]), lambda i, j: (0, 0))],
        out_specs=(pl.BlockSpec((None, tm, tn), lambda i, j: (j, i, 0)),
                   pl.BlockSpec((tm, ws.shape[1]), lambda i, j: (i, 0))),
        scratch_shapes=[pltpu.VMEM((tm, d), BF16)],
        compiler_params=_params(("parallel", "arbitrary")),
        name="norm_proj",
    )(xa, xb, g, w, ws)


def _seq_tile_specs(G, Lt, nt, row_off):
    assert G == 1 or nt == 1

    def rows(b, t):
        return row_off + b * nt + t

    return rows


def _mlstm_kernel(*refs, L, n_sub, G, s_valid, s_total, has_init):
    if has_init:
        (q_ref, k_ref, v_ref, og_ref, gs_ref, bias_ref, norm_ref, c0_ref, n0_ref, m0_ref,
         h_ref, c_ref, n_ref, m_ref) = refs
    else:
        (q_ref, k_ref, v_ref, og_ref, gs_ref, bias_ref, norm_ref,
         h_ref, c_ref, n_ref, m_ref) = refs
    t = pl.program_id(1)
    Lt = n_sub * L
    rows = G * Lt

    @pl.when(t == 0)
    def _():
        if has_init:
            c_ref[...] = c0_ref[...]
            n_ref[...] = n0_ref[...]
            m_ref[...] = m0_ref[...]
        else:
            c_ref[...] = jnp.zeros_like(c_ref)
            n_ref[...] = jnp.zeros_like(n_ref)
            m_ref[...] = jnp.zeros_like(m_ref)

    z = gs_ref[...] + bias_ref[...]
    lane = lax.broadcasted_iota(jnp.int32, z.shape, 1)
    log_f = jnp.minimum(z, 0.0) - jnp.log1p(jnp.exp(-jnp.abs(z)))
    g = jnp.where((lane >= N_HEADS) & (lane < 2 * N_HEADS), log_f, z)
    if s_valid < s_total:
        pad = _pad_rows(z.shape, t, Lt, s_valid)
        g = jnp.where(pad, jnp.where(lane < N_HEADS, NEG_BIG, 0.0), g)
    g_t = _lanes_to_rows(g, 0)
    bt_cols = _dot_exact(_tri_blocks(rows, L).astype(F32), g)
    bt_rows = _dot_exact(g_t, _tri_blocks(rows, L, upper=True).astype(F32))
    causal = _tri(L)
    scale = HEAD_DIM ** -0.5
    probs = [(s, h) for s in range(G) for h in range(N_HEADS)]

    for c in range(n_sub):
        P = []
        for s, h in probs:
            r = slice(s * Lt + c * L, s * Lt + (c + 1) * L)
            sl = slice(h * HEAD_DIM, (h + 1) * HEAD_DIM)
            p = dict(s=s, h=h, r=r, sl=sl, q=q_ref[r, sl], k=k_ref[r, sl] * scale, v=v_ref[r, sl],
                     bt_c=bt_cols[r, N_HEADS + h:N_HEADS + h + 1], it_c=g[r, h:h + 1])
            bt_r = bt_rows[N_HEADS + h:N_HEADS + h + 1, r]
            it_r = g_t[h:h + 1, r]
            p["log_d"] = jnp.where(causal, p["bt_c"] - bt_r + it_r, -jnp.inf)
            p["c_prev"] = c_ref[s, h]
            p["n_prev"] = n_ref[s, h:h + 1, :]
            p["m_prev"] = m_ref[s, h:h + 1, 0:1]
            P.append(p)
        for p in P:
            p["qk"] = _dot_nt(p["q"], p["k"])
        for p in P:
            p["qc"] = _dot(p["q"], p["c_prev"])
        for p in P:
            inter = p["bt_c"] + p["m_prev"]
            p["m_t"] = jnp.maximum(inter, jnp.max(p["log_d"], axis=-1, keepdims=True))
            p["inter_w"] = jnp.exp(inter - p["m_t"])
            p["sd"] = p["qk"] * jnp.exp(p["log_d"] - p["m_t"])
        for p in P:
            p["sv"] = _dot(p["sd"], p["v"])
        for p in P:
            bt_last = p["bt_c"][L - 1:L, :]
            p["m_new"] = p["m_t"][L - 1:L, :]
            w = jnp.exp(bt_last - p["bt_c"] + p["it_c"] - p["m_new"])
            p["decay"] = jnp.exp(bt_last + p["m_prev"] - p["m_new"])
            p["kw"] = p["k"] * w
        for p in P:
            p["kv"] = _dot_tn(p["kw"], p["v"])
        for p in P:
            num = p["sv"] + p["qc"] * p["inter_w"]
            den = (jnp.sum(p["sd"], axis=-1, keepdims=True)
                   + p["inter_w"] * jnp.sum(p["q"] * p["n_prev"], axis=-1, keepdims=True))
            hh = num / jnp.maximum(jnp.abs(den), jnp.exp(-p["m_t"]))
            hn = hh * lax.rsqrt(jnp.mean(hh * hh, axis=-1, keepdims=True) + EPS) * norm_ref[:, p["sl"]]
            p["h_out"] = hn * _sigmoid(og_ref[p["r"], p["sl"]])
            p["n_new"] = p["decay"] * p["n_prev"] + jnp.sum(p["kw"], axis=0, keepdims=True)
        for p in P:
            s, h = p["s"], p["h"]
            h_ref[p["r"], p["sl"]] = p["h_out"]
            n_ref[s, h:h + 1, :] = p["n_new"]
            m_ref[s, h:h + 1, :] = jnp.broadcast_to(p["m_new"], (1, LANES))
            c_ref[s, h] = p["decay"] * p["c_prev"] + p["kv"]


def _mlstm(proj, gs, bias_row, norm_row, init, *, n_batch, seq, s_valid, row_off, L, n_sub, G):
    Lt = n_sub * L
    nt = seq // Lt
    rows = G * Lt
    has_init = init is not None
    blk = _seq_tile_specs(G, Lt, nt, row_off)

    def col(c):
        return _proj_spec(rows, c, blk)

    in_specs = [col(COL_MQ), col(COL_MK), col(COL_MV), col(COL_MO),
                pl.BlockSpec((rows, LANES), lambda b, t: (blk(b, t), 0)),
                pl.BlockSpec((1, LANES), lambda b, t: (0, 0)),
                pl.BlockSpec((1, BRANCH_W), lambda b, t: (0, 0))]
    args = [proj, proj, proj, proj, gs, bias_row, norm_row]
    c_spec = pl.BlockSpec((G, N_HEADS, HEAD_DIM, HEAD_DIM), lambda b, t: (b, 0, 0, 0))
    n_spec = pl.BlockSpec((G, N_HEADS, HEAD_DIM), lambda b, t: (b, 0, 0))
    m_spec = pl.BlockSpec((G, N_HEADS, LANES), lambda b, t: (b, 0, 0))
    if has_init:
        in_specs += [c_spec, n_spec, m_spec]
        args += list(init)
    return pl.pallas_call(
        functools.partial(_mlstm_kernel, L=L, n_sub=n_sub, G=G, s_valid=s_valid, s_total=seq,
                          has_init=has_init),
        out_shape=(jax.ShapeDtypeStruct((n_batch * seq, BRANCH_W), F32),
                   jax.ShapeDtypeStruct((n_batch, N_HEADS, HEAD_DIM, HEAD_DIM), F32),
                   jax.ShapeDtypeStruct((n_batch, N_HEADS, HEAD_DIM), F32),
                   jax.ShapeDtypeStruct((n_batch, N_HEADS, LANES), F32)),
        grid=(n_batch // G, nt),
        in_specs=in_specs,
        out_specs=(pl.BlockSpec((rows, BRANCH_W), lambda b, t: (b * nt + t, 0)),
                   c_spec, n_spec, m_spec),
        compiler_params=_params(("parallel", "arbitrary")),
        name="mlstm",
    )(*args)


def _gdn_kernel(*refs, L, n_sub, G, s_valid, s_total, has_init):
    if has_init:
        (q_ref, k_ref, v_ref, og_ref, gs_ref, cw_ref, arow_ref, dtrow_ref, norm_ref,
         conv0_ref, s0_ref, o_ref, s_ref, conv1_ref, ext_ref, conv_ref) = refs
    else:
        (q_ref, k_ref, v_ref, og_ref, gs_ref, cw_ref, arow_ref, dtrow_ref, norm_ref,
         o_ref, s_ref, conv1_ref, ext_ref, conv_ref) = refs
    t = pl.program_id(1)
    Lt = n_sub * L
    rows = G * Lt
    nt = s_total // Lt
    W = BRANCH_W

    @pl.when(t == 0)
    def _():
        if has_init:
            s_ref[...] = s0_ref[...]
            ext_ref[:, 0:SUBLANES, :] = conv0_ref[...]
        else:
            s_ref[...] = jnp.zeros_like(s_ref)
            ext_ref[:, 0:SUBLANES, :] = jnp.zeros((G, SUBLANES, 3 * W), F32)

    @pl.when(t > 0)
    def _():
        ext_ref[:, 0:SUBLANES, :] = ext_ref[:, Lt:Lt + SUBLANES, :]

    base = SUBLANES - (CONV_W - 1)
    for s in range(G):
        rs = slice(s * Lt, (s + 1) * Lt)
        ext_ref[s, SUBLANES:SUBLANES + Lt, 0:W] = q_ref[rs, :]
        ext_ref[s, SUBLANES:SUBLANES + Lt, W:2 * W] = k_ref[rs, :]
        ext_ref[s, SUBLANES:SUBLANES + Lt, 2 * W:3 * W] = v_ref[rs, :]
        acc = ext_ref[s, base:base + Lt, :] * cw_ref[0:1, :]
        for j in range(1, CONV_W):
            acc = acc + ext_ref[s, base + j:base + j + Lt, :] * cw_ref[j:j + 1, :]
        conv_ref[rs, :] = _silu(acc)

    @pl.when(t == nt - 1)
    def _():
        v_last = s_valid - (nt - 1) * Lt
        conv1_ref[...] = ext_ref[:, v_last:v_last + SUBLANES, :]

    z = gs_ref[...]
    lane = lax.broadcasted_iota(jnp.int32, z.shape, 1)
    gdec = arow_ref[...] * _softplus(z + dtrow_ref[...])
    gb = jnp.where((lane >= 2 * N_HEADS) & (lane < 3 * N_HEADS), gdec,
                   jnp.where((lane >= 3 * N_HEADS) & (lane < 4 * N_HEADS), _sigmoid(z), 0.0))
    if s_valid < s_total:
        gb = jnp.where(_pad_rows(z.shape, t, Lt, s_valid), 0.0, gb)
    gb_t = _lanes_to_rows(gb, 2 * N_HEADS)
    gam_cols = _dot_exact(_tri_blocks(rows, L).astype(F32), gb)
    gam_rows = _dot_exact(gb_t, _tri_blocks(rows, L, upper=True).astype(F32))
    incl = _tri(L)
    strict = _tri(L, strict=True)
    eye = (lax.broadcasted_iota(jnp.int32, (L, L), 0)
           == lax.broadcasted_iota(jnp.int32, (L, L), 1)).astype(F32)

    P = []
    for s in range(G):
        for c in range(n_sub):
            for h in range(N_HEADS):
                r = slice(s * Lt + c * L, s * Lt + (c + 1) * L)
                sl = slice(h * HEAD_DIM, (h + 1) * HEAD_DIM)
                q = conv_ref[r, h * HEAD_DIM:(h + 1) * HEAD_DIM]
                k = conv_ref[r, W + h * HEAD_DIM:W + (h + 1) * HEAD_DIM]
                v = conv_ref[r, 2 * W + h * HEAD_DIM:2 * W + (h + 1) * HEAD_DIM]
                q = q * lax.rsqrt(jnp.sum(q * q, axis=-1, keepdims=True) + EPS) * (HEAD_DIM ** -0.5)
                k = k * lax.rsqrt(jnp.sum(k * k, axis=-1, keepdims=True) + EPS)
                gam_c = gam_cols[r, 2 * N_HEADS + h:2 * N_HEADS + h + 1]
                gam_r = gam_rows[h:h + 1, r]
                beta_c = gb[r, 3 * N_HEADS + h:3 * N_HEADS + h + 1]
                decay = jnp.where(incl, jnp.exp(jnp.where(incl, gam_c - gam_r, 0.0)), 0.0)
                P.append(dict(s=s, c=c, h=h, r=r, sl=sl, q=q, k=k, v=v, gam_c=gam_c,
                              beta_c=beta_c, decay=decay))
    for p in P:
        p["kk"] = _dot_nt(p["k"], p["k"])
    for p in P:
        p["qk"] = _dot_nt(p["q"], p["k"]) * p["decay"]
    for p in P:
        e_gam = jnp.exp(p["gam_c"])
        gam_last = p["gam_c"][L - 1:L, :]
        p["pw"] = -jnp.where(strict, p["beta_c"] * p["kk"] * p["decay"], 0.0)
        p["x"] = eye + p["pw"]
        p["rhs"] = jnp.concatenate([p["beta_c"] * p["v"], p["beta_c"] * p["k"] * e_gam], axis=-1)
        p["q_dec"] = p["q"] * e_gam
        p["k_dec"] = p["k"] * jnp.exp(gam_last - p["gam_c"])
        p["e_last"] = jnp.exp(gam_last)
    n = 2
    while n < L:
        for p in P:
            p["pw"] = _dot(p["pw"], p["pw"])
        for p in P:
            p["x"] = p["x"] + _dot(p["x"], p["pw"])
        n *= 2
    for p in P:
        sol = _dot(p["x"], p["rhs"])
        p["u"] = sol[:, :HEAD_DIM]
        p["w"] = sol[:, HEAD_DIM:]

    for c in range(n_sub):
        Pc = [p for p in P if p["c"] == c]
        for p in Pc:
            p["s_prev"] = s_ref[p["s"], p["h"]]
            p["ws"] = _dot(p["w"], p["s_prev"])
        for p in Pc:
            p["qs"] = _dot(p["q_dec"], p["s_prev"])
        for p in Pc:
            p["v_new"] = p["u"] - p["ws"]
        for p in Pc:
            p["o"] = p["qs"] + _dot(p["qk"], p["v_new"])
        for p in Pc:
            p["kv"] = _dot_tn(p["k_dec"], p["v_new"])
        for p in Pc:
            o = p["o"]
            on = o * lax.rsqrt(jnp.mean(o * o, axis=-1, keepdims=True) + EPS) * norm_ref[...]
            p["o_out"] = on * _silu(og_ref[p["r"], p["sl"]])
        for p in Pc:
            s_ref[p["s"], p["h"]] = p["e_last"] * p["s_prev"] + p["kv"]
            o_ref[p["r"], p["sl"]] = p["o_out"]


def _gdn(proj, gs, conv_w, a_row, dt_row, norm_row, init, *, n_batch, seq, s_valid, row_off,
         L, n_sub, G):
    Lt = n_sub * L
    nt = seq // Lt
    rows = G * Lt
    has_init = init is not None
    blk = _seq_tile_specs(G, Lt, nt, row_off)

    def col(c):
        return _proj_spec(rows, c, blk)

    def const(shape):
        return pl.BlockSpec(shape, lambda b, t: (0,) * len(shape))

    in_specs = [col(COL_GQ), col(COL_GK), col(COL_GV), col(COL_GO),
                pl.BlockSpec((rows, LANES), lambda b, t: (blk(b, t), 0)),
                const((CONV_W, 3 * BRANCH_W)), const((1, LANES)), const((1, LANES)),
                const((1, HEAD_DIM))]
    args = [proj, proj, proj, proj, gs, conv_w, a_row, dt_row, norm_row]
    s_spec = pl.BlockSpec((G, N_HEADS, HEAD_DIM, HEAD_DIM), lambda b, t: (b, 0, 0, 0))
    conv_spec = pl.BlockSpec((G, SUBLANES, 3 * BRANCH_W), lambda b, t: (b, 0, 0))
    if has_init:
        in_specs += [conv_spec, s_spec]
        args += list(init)
    return pl.pallas_call(
        functools.partial(_gdn_kernel, L=L, n_sub=n_sub, G=G, s_valid=s_valid, s_total=seq,
                          has_init=has_init),
        out_shape=(jax.ShapeDtypeStruct((n_batch * seq, BRANCH_W), F32),
                   jax.ShapeDtypeStruct((n_batch, N_HEADS, HEAD_DIM, HEAD_DIM), F32),
                   jax.ShapeDtypeStruct((n_batch, SUBLANES, 3 * BRANCH_W), F32)),
        grid=(n_batch // G, nt),
        in_specs=in_specs,
        out_specs=(pl.BlockSpec((rows, BRANCH_W), lambda b, t: (b * nt + t, 0)),
                   s_spec, conv_spec),
        scratch_shapes=[pltpu.VMEM((G, Lt + SUBLANES, 3 * BRANCH_W), F32),
                        pltpu.VMEM((rows, 3 * BRANCH_W), F32)],
        compiler_params=_params(("parallel", "arbitrary")),
        name="gdn",
    )(*args)


def _xattn_kernel(q_ref, mk_ref, mv_ref, o_ref, *, G, lt, mem_len, head_rows):
    scale = HEAD_DIM ** -0.5

    def head(ref, s, h):
        if head_rows:
            return ref[pl.ds(s * mem_len * N_HEADS + h, mem_len, stride=N_HEADS), :]
        return ref[s * mem_len:(s + 1) * mem_len, h * HEAD_DIM:(h + 1) * HEAD_DIM]

    P = []
    for s in range(G):
        for h in range(N_HEADS):
            P.append(dict(s=s, h=h, r=slice(s * lt, (s + 1) * lt),
                          sl=slice(h * HEAD_DIM, (h + 1) * HEAD_DIM)))
    for p in P:
        p["sc"] = _dot_nt(q_ref[p["r"], p["sl"]], head(mk_ref, p["s"], p["h"])) * scale
    for p in P:
        e = jnp.exp(p["sc"] - jnp.max(p["sc"], axis=-1, keepdims=True))
        p["p"] = e / jnp.sum(e, axis=-1, keepdims=True)
    for p in P:
        o_ref[p["r"], p["sl"]] = _dot(p["p"], head(mv_ref, p["s"], p["h"]))


def _xattn(proj, mk, mv, *, n_batch, seq, row_off, lt, G, mem_len, kv_specs, head_rows):
    nt = seq // lt
    blk = _seq_tile_specs(G, lt, nt, row_off)
    return pl.pallas_call(
        functools.partial(_xattn_kernel, G=G, lt=lt, mem_len=mem_len, head_rows=head_rows),
        out_shape=jax.ShapeDtypeStruct((n_batch * seq, BRANCH_W), F32),
        grid=(n_batch // G, nt),
        in_specs=[_proj_spec(G * lt, COL_XQ, blk)] + list(kv_specs),
        out_specs=pl.BlockSpec((G * lt, BRANCH_W), lambda b, t: (b * nt + t, 0)),
        compiler_params=_params(("parallel", "arbitrary")),
        name="xattn",
    )(proj, mk, mv)


def _lane_first_max(x, mask, lane_f):
    xm = jnp.where(mask, x, -jnp.inf)
    mx = jnp.max(xm, axis=-1, keepdims=True)
    idx = jnp.min(jnp.where(xm == mx, lane_f, float(LANES)), axis=-1, keepdims=True)
    return mx, idx


def _store_token_tiles(ref, x, tok_rows):
    n = x.shape[0]
    for j in range(x.shape[1] // LANES):
        ref[pl.ds(j, n, stride=tok_rows), :] = x[:, j * LANES:(j + 1) * LANES]


def _load_token_tiles(ref, n, tok_rows, n_chunks=SUBLANES):
    return jnp.concatenate([ref[pl.ds(j, n, stride=tok_rows), :] for j in range(n_chunks)], axis=1)


def _merge_kernel(hap_ref, has_ref, hbp_ref, hbs_ref, hcp_ref, hcs_ref, xp_ref, xs_ref,
                  ga0_ref, ga1_ref, gb0_ref, gb1_ref, gc0_ref, gc1_ref,
                  wa_ref, wb_ref, wc_ref, wo_ref, nf_ref, wr_ref, br_ref,
                  x1_ref, route_ref, *, n_first):
    i = pl.program_id(0)

    def gate(lo_ref, hi_ref):
        return jnp.concatenate([_sigmoid(lo_ref[...]), _sigmoid(hi_ref[...])], axis=1)

    def body(ha_ref, hb_ref, hc_ref, x_ref):
        mixed = (gate(ga0_ref, ga1_ref) * _dot(ha_ref[...], wa_ref[...])
                 + gate(gb0_ref, gb1_ref) * _dot(hb_ref[...], wb_ref[...])
                 + gate(gc0_ref, gc1_ref) * _dot(hc_ref[...], wc_ref[...]))
        x1 = x_ref[...] + _dot(mixed, wo_ref[...])
        h2 = _rms(x1, nf_ref[...])
        logits = _dot(h2, wr_ref[...]) + br_ref[...]
        lane = lax.broadcasted_iota(jnp.int32, logits.shape, 1)
        lane_f = lane.astype(F32)
        is_g = lane < N_GROUPS
        _, g_idx = _lane_first_max(logits, is_g, lane_f)
        e_lo = N_GROUPS + EXPERTS_PER_GROUP * g_idx
        in_grp = (lane_f >= e_lo) & (lane_f < e_lo + EXPERTS_PER_GROUP)
        _, i1 = _lane_first_max(logits, in_grp, lane_f)
        _, i2 = _lane_first_max(logits, in_grp & (lane_f != i1), lane_f)
        lo = jnp.minimum(i1, i2) - e_lo
        hi = jnp.maximum(i1, i2) - e_lo
        pair = lo * (7.0 - lo) * 0.5 + (hi - lo - 1.0)
        cls = g_idx * float(N_PAIRS) + pair
        _store_token_tiles(x1_ref, x1, TOKEN_ROWS)
        route_ref[...] = jnp.where(lane == ROUTE_CLS, cls, 0.0)

    @pl.when(i < n_first)
    def _():
        body(hap_ref, hbp_ref, hcp_ref, xp_ref)

    @pl.when(i >= n_first)
    def _():
        body(has_ref, hbs_ref, hcs_ref, xs_ref)


def _merge(ha, hb, hc, xs, proj, wa, wb, wc, wo, nf, wr, br, *, tm):
    d = xs[0].shape[1]
    n_first = xs[0].shape[0] // tm
    n_rows = xs[0].shape[0] + xs[1].shape[0]
    specs = []
    for width in (BRANCH_W, BRANCH_W, BRANCH_W, d):
        specs += list(_two_source_specs(n_first, tm, width))

    def const(a):
        return pl.BlockSpec(a.shape, lambda i: (0,) * a.ndim)

    n_gate = 3 * d // BRANCH_W
    specs += [_proj_spec(tm, COL_GATE + c, lambda i: i) for c in range(n_gate)]
    weights = [wa, wb, wc, wo, nf, wr, br]
    specs += [const(a) for a in weights]
    return pl.pallas_call(
        functools.partial(_merge_kernel, n_first=n_first),
        out_shape=(jax.ShapeDtypeStruct((n_rows * TOKEN_ROWS, LANES), F32),
                   jax.ShapeDtypeStruct((n_rows, LANES), F32)),
        grid=(n_rows // tm,),
        in_specs=specs,
        out_specs=(pl.BlockSpec((tm * TOKEN_ROWS, LANES), lambda i: (i, 0)),
                   pl.BlockSpec((tm, LANES), lambda i: (i, 0))),
        compiler_params=_params(("parallel",)),
        name="merge",
    )(*ha, *hb, *hc, *xs, *([proj] * n_gate), *weights)


GATHER_UNROLL = 8
MOE_SLOTS = 2
ROUTE_CLS = 0


def _record_gather_start(src_ref, dst_ref, sem, ids_ref, base, n, rec):
    def issue(r, carry):
        src = pl.multiple_of(ids_ref[base + r] * rec, rec)
        dst = pl.multiple_of(r * rec, rec)
        pltpu.make_async_copy(src_ref.at[pl.ds(src, rec)], dst_ref.at[pl.ds(dst, rec)], sem).start()
        return carry
    lax.fori_loop(0, n, issue, 0, unroll=GATHER_UNROLL)


def _record_gather_wait(src_ref, dst_ref, sem):
    pltpu.make_async_copy(src_ref.at[pl.ds(0, dst_ref.shape[0])], dst_ref, sem).wait()


def _moe_kernel(ids_ref, ea_ref, eb_ref, nused_ref, x_hbm, wga_ref, wua_ref, wda_ref,
                wgb_ref, wub_ref, wdb_ref, nf_ref, fin_ref, wr_ref, br_ref, y_ref, xbuf, sem,
                *, tm, d):
    i = pl.program_id(0)
    n_used = nused_ref[0]
    n_slots = xbuf.shape[0]
    ahead = n_slots - 1
    slot = i % n_slots

    def gather(tile):
        s = tile % n_slots
        _record_gather_start(x_hbm, xbuf.at[s], sem.at[s], ids_ref, tile * tm, tm, TOKEN_ROWS)

    @pl.when(i == 0)
    def _():
        for k in range(ahead):
            @pl.when(k < n_used)
            def _():
                gather(jnp.int32(k))

    @pl.when(i + ahead < n_used)
    def _():
        gather(i + ahead)

    @pl.when(i < n_used)
    def _():
        _record_gather_wait(x_hbm, xbuf.at[slot], sem.at[slot])
        x1 = _load_token_tiles(xbuf.at[slot], tm, TOKEN_ROWS)
        h2 = _rms(x1, nf_ref[...]).astype(BF16)
        logits = jnp.dot(h2, wr_ref[...], preferred_element_type=F32) + br_ref[...]
        lane = lax.broadcasted_iota(jnp.int32, logits.shape, 1)

        def pick(l):
            return jnp.sum(jnp.where(lane == l, logits, 0.0), axis=-1, keepdims=True)

        ea, eb = ea_ref[i], eb_ref[i]
        is_g = lane < N_GROUPS
        g_max = jnp.max(jnp.where(is_g, logits, -jnp.inf), axis=-1, keepdims=True)
        g_den = jnp.sum(jnp.where(is_g, jnp.exp(logits - g_max), 0.0), axis=-1, keepdims=True)
        p_grp = jnp.exp(pick(ea // EXPERTS_PER_GROUP) - g_max) / g_den
        v_a, v_b = pick(N_GROUPS + ea), pick(N_GROUPS + eb)
        combs = (p_grp / (1.0 + jnp.exp(v_b - v_a)), p_grp / (1.0 + jnp.exp(v_a - v_b)))
        acc = jnp.zeros((tm, d), F32)
        for comb, wg_ref, wu_ref, wd_ref in ((combs[0], wga_ref, wua_ref, wda_ref),
                                             (combs[1], wgb_ref, wub_ref, wdb_ref)):
            a = jnp.dot(h2, wg_ref[0], preferred_element_type=F32)
            u = jnp.dot(h2, wu_ref[0], preferred_element_type=F32)
            act = _silu(a) * u * comb
            acc = acc + _dot(act, wd_ref[0])
        _store_token_tiles(y_ref, _rms(x1 + acc, fin_ref[...]), SUBLANES)

    @pl.when(i >= n_used)
    def _():
        y_ref[...] = jnp.zeros_like(y_ref)


def _moe(ids, tile_ea, tile_eb, n_used, x1rec, wg, wu, wd, nf, fin, wr, br, *, tm, n_tiles):
    d = nf.shape[1]
    f = wg.shape[2]
    assert d == SUBLANES * LANES

    def expert(shape, which):
        return pl.BlockSpec((1,) + shape, lambda i, ids, ea, eb, nu: ((ea, eb)[which][i], 0, 0))

    def const(a):
        return pl.BlockSpec(a.shape, lambda i, *_: (0,) * a.ndim)

    grid_spec = pltpu.PrefetchScalarGridSpec(
        num_scalar_prefetch=4,
        grid=(n_tiles,),
        in_specs=[pl.BlockSpec(memory_space=pl.ANY),
                  expert((d, f), 0), expert((d, f), 0), expert((f, d), 0),
                  expert((d, f), 1), expert((d, f), 1), expert((f, d), 1),
                  const(nf), const(fin), const(wr), const(br)],
        out_specs=pl.BlockSpec((tm * SUBLANES, LANES), lambda i, *_: (i, 0)),
        scratch_shapes=[pltpu.VMEM((MOE_SLOTS, tm * TOKEN_ROWS, LANES), F32),
                        pltpu.SemaphoreType.DMA((MOE_SLOTS,))],
    )
    return pl.pallas_call(
        functools.partial(_moe_kernel, tm=tm, d=d),
        out_shape=jax.ShapeDtypeStruct((n_tiles * tm * SUBLANES, LANES), F32),
        grid_spec=grid_spec,
        compiler_params=_params(("arbitrary",)),
        name="moe",
    )(ids, tile_ea, tile_eb, n_used, x1rec, wg, wu, wd, wg, wu, wd, nf, fin, wr, br)


def _unsort_kernel(pos_ref, y_hbm, o_ref, buf, sem, *, tm, base):
    i = pl.program_id(0)
    slot = i % 2

    @pl.when(i == 0)
    def _():
        _record_gather_start(y_hbm, buf.at[0], sem.at[0], pos_ref, base, tm, SUBLANES)

    @pl.when(i + 1 < pl.num_programs(0))
    def _():
        _record_gather_start(y_hbm, buf.at[1 - slot], sem.at[1 - slot], pos_ref,
                             base + (i + 1) * tm, tm, SUBLANES)

    _record_gather_wait(y_hbm, buf.at[slot], sem.at[slot])
    o_ref[...] = _load_token_tiles(buf.at[slot], tm, SUBLANES)


def _unsort(pos, y_rec, *, base, n_rows, tm):
    d = SUBLANES * LANES
    grid_spec = pltpu.PrefetchScalarGridSpec(
        num_scalar_prefetch=1,
        grid=(n_rows // tm,),
        in_specs=[pl.BlockSpec(memory_space=pl.ANY)],
        out_specs=pl.BlockSpec((tm, d), lambda i, *_: (i, 0)),
        scratch_shapes=[pltpu.VMEM((2, tm * SUBLANES, LANES), F32),
                        pltpu.SemaphoreType.DMA((2,))],
    )
    return pl.pallas_call(
        functools.partial(_unsort_kernel, tm=tm, base=base),
        out_shape=jax.ShapeDtypeStruct((n_rows, d), F32),
        grid_spec=grid_spec,
        compiler_params=_params(("arbitrary",)),
        name="unsort",
    )(pos, y_rec)


def _sort_plan(cls, *, tm, n_tiles):
    n = cls.shape[0]
    onehot = (cls[:, None] == jnp.arange(N_CLASSES, dtype=jnp.int32)[None, :]).astype(jnp.int32)
    csum = jnp.cumsum(onehot, axis=0)
    rank = jnp.sum((csum - onehot) * onehot, axis=1)
    counts = csum[-1]
    tiles_per = (counts + tm - 1) // tm
    tile_end = jnp.cumsum(tiles_per)
    tile_start = tile_end - tiles_per
    pos = (jnp.sum(tile_start[None, :] * onehot, axis=1) * tm + rank).astype(jnp.int32)
    n_used = tile_end[-1].astype(jnp.int32)
    ids = jnp.zeros((n_tiles * tm,), jnp.int32).at[pos].set(jnp.arange(n, dtype=jnp.int32))
    tile = jnp.minimum(jnp.arange(n_tiles, dtype=jnp.int32), n_used - 1)
    tile_cls = jnp.sum((tile[:, None] >= tile_end[None, :]).astype(jnp.int32), axis=1)
    grp = tile_cls // N_PAIRS
    pair = tile_cls % N_PAIRS
    lo = jnp.asarray(PAIR_LO, jnp.int32)[pair]
    hi = jnp.asarray(PAIR_HI, jnp.int32)[pair]
    ea = (grp * EXPERTS_PER_GROUP + lo).astype(jnp.int32)
    eb = (grp * EXPERTS_PER_GROUP + hi).astype(jnp.int32)
    return pos, ids, ea, eb, n_used.reshape(1)


def _lane_row(values, start):
    row = jnp.zeros((LANES,), F32)
    return row.at[start:start + values.shape[0]].set(values.astype(F32)).reshape(1, LANES)


def _layer(x_prompt, x_sample, mem_prompt, cache_mem_k, cache_mem_v,
           state_mlstm_C, state_mlstm_n, state_mlstm_m, state_gdn_S, state_gdn_conv,
           norm_mix, w_in, b_igate, b_fgate, mlstm_norm, conv_w, a_log, dt_bias, gdn_norm,
           mem_norm, w_mem_kv, w_br_mlstm, w_br_gdn, w_br_xattn, w_out, norm_ffn,
           w_router_group, b_router_group, w_router_expert, b_router_expert,
           w_exp_gate, w_exp_up, w_exp_down, final_norm, *, cfg):
    bp, sp, d = x_prompt.shape
    bs, ss, _ = x_sample.shape
    mem_len = mem_prompt.shape[1]
    ss_pad = SUBLANES
    assert ss <= ss_pad and d % LANES == 0
    n_p = bp * sp
    n_s = bs * ss_pad
    W = BRANCH_W
    gs_ = cfg["sample_group"]

    sizes = (W, W, W, N_HEADS, N_HEADS, W, 3 * W, N_HEADS, N_HEADS, W, W, d, d, d)
    offs = [0]
    for s in sizes:
        offs.append(offs[-1] + s)
    (m_q, m_k, m_v, m_i, m_f, m_o, g_qkv, g_a, g_b, g_out, x_q, gate_a, gate_b, gate_c) = [
        w_in[:, offs[j]:offs[j + 1]] for j in range(len(sizes))]
    w_big = _col_blocks(
        jnp.concatenate([gate_a, gate_b, gate_c, m_q, m_k, m_v, m_o, g_qkv, g_out, x_q],
                        axis=1).astype(BF16), PROJ_GROUP * W)
    w_small = jnp.concatenate(
        [m_i, m_f, g_a, g_b, jnp.zeros((d, LANES - 4 * N_HEADS), w_in.dtype)], axis=1).astype(BF16)
    w_router = jnp.concatenate(
        [w_router_group, w_router_expert,
         jnp.zeros((d, LANES - N_GROUPS - N_EXPERTS), w_in.dtype)], axis=1).astype(BF16)
    b_router = _lane_row(jnp.concatenate([b_router_group, b_router_expert]), 0)
    gate_bias = _lane_row(jnp.concatenate([b_igate, b_fgate]), 0)
    a_row = _lane_row(-jnp.exp(a_log.astype(F32)), 2 * N_HEADS)
    dt_row = _lane_row(dt_bias, 2 * N_HEADS)

    xp2 = x_prompt.reshape(n_p, d)
    xs2 = jnp.pad(x_sample, ((0, 0), (0, ss_pad - ss), (0, 0))).reshape(n_s, d)

    proj, gs = _norm_proj(xp2, xs2, norm_mix.reshape(1, d), w_big, w_small, tm=cfg["tm_proj"])

    mem2 = mem_prompt.reshape(bp * mem_len, d)
    mem_kv, _ = _norm_proj(mem2, None, mem_norm.reshape(1, d),
                           _col_blocks(w_mem_kv.astype(BF16), W),
                           jnp.zeros((d, LANES), BF16), tm=mem_len)

    mnorm = mlstm_norm.reshape(1, W)
    gnorm = gdn_norm.reshape(1, HEAD_DIM)
    s_off = n_p // (gs_ * ss_pad)
    ha_p, c_p, nn_p, mm_p = _mlstm(proj, gs, gate_bias, mnorm, None, n_batch=bp, seq=sp,
                                   s_valid=sp, row_off=0, L=cfg["chunk_m"], n_sub=cfg["sub_m"], G=1)
    m0 = jnp.broadcast_to(state_mlstm_m[:, :, None], (bs, N_HEADS, LANES))
    ha_s, c_s, nn_s, mm_s = _mlstm(proj, gs, gate_bias, mnorm,
                                   (state_mlstm_C, state_mlstm_n, m0), n_batch=bs, seq=ss_pad,
                                   s_valid=ss, row_off=s_off, L=ss_pad, n_sub=1, G=gs_)
    hb_p, s_p, cv_p = _gdn(proj, gs, conv_w, a_row, dt_row, gnorm, None, n_batch=bp, seq=sp,
                           s_valid=sp, row_off=0, L=cfg["chunk_g"], n_sub=cfg["sub_g"], G=1)
    conv0 = jnp.pad(state_gdn_conv, ((0, 0), (SUBLANES - (CONV_W - 1), 0), (0, 0)))
    hb_s, s_s, cv_s = _gdn(proj, gs, conv_w, a_row, dt_row, gnorm, (conv0, state_gdn_S),
                           n_batch=bs, seq=ss_pad, s_valid=ss, row_off=s_off, L=ss_pad,
                           n_sub=1, G=gs_)
    lt = min(sp, cfg["lt_x"])
    hc_p = _xattn(proj, mem_kv, mem_kv, n_batch=bp, seq=sp, row_off=0, lt=lt, G=1,
                  mem_len=mem_len, head_rows=False,
                  kv_specs=[pl.BlockSpec((None, mem_len, W), lambda b, t, c=c: (c, b, 0))
                            for c in range(2)])
    kv_rows = gs_ * mem_len * N_HEADS
    hc_s = _xattn(proj, cache_mem_k.reshape(bs * mem_len * N_HEADS, HEAD_DIM),
                  cache_mem_v.reshape(bs * mem_len * N_HEADS, HEAD_DIM),
                  n_batch=bs, seq=ss_pad, row_off=s_off, lt=ss_pad, G=gs_, mem_len=mem_len,
                  head_rows=True,
                  kv_specs=[pl.BlockSpec((kv_rows, HEAD_DIM), lambda b, t: (b, 0))] * 2)

    tm = cfg["tm"]
    x1rec, route = _merge((ha_p, ha_s), (hb_p, hb_s), (hc_p, hc_s), (xp2, xs2), proj,
                          w_br_mlstm.astype(BF16), w_br_gdn.astype(BF16), w_br_xattn.astype(BF16),
                          w_out.astype(BF16), norm_ffn.reshape(1, d), w_router, b_router, tm=tm)

    tm_moe = cfg["tm_moe"]
    n_all = n_p + n_s
    n_tiles = (n_all + N_CLASSES * (tm_moe - 1)) // tm_moe + 1
    cls = route[:, ROUTE_CLS].astype(jnp.int32)
    pos, ids, ea, eb, n_used = _sort_plan(cls, tm=tm_moe, n_tiles=n_tiles)
    y_sorted = _moe(ids, ea, eb, n_used, x1rec, w_exp_gate.astype(BF16),
                    w_exp_up.astype(BF16), w_exp_down.astype(BF16), norm_ffn.reshape(1, d),
                    final_norm.reshape(1, d), w_router, b_router, tm=tm_moe, n_tiles=n_tiles)
    y_p = _unsort(pos, y_sorted, base=0, n_rows=n_p, tm=tm)
    y_s = _unsort(pos, y_sorted, base=n_p, n_rows=n_s, tm=tm)

    y_prompt = y_p.reshape(bp, sp, d)
    y_sample = y_s.reshape(bs, ss_pad, d)[:, :ss]
    mk_p = mem_kv[0].reshape(bp, mem_len, N_HEADS, HEAD_DIM)
    mv_p = mem_kv[1].reshape(bp, mem_len, N_HEADS, HEAD_DIM)
    tail = slice(SUBLANES - (CONV_W - 1), SUBLANES)
    return (y_prompt, y_sample, mk_p, mv_p,
            c_p, nn_p, mm_p[:, :, 0], s_p, cv_p[:, tail],
            c_s, nn_s, mm_s[:, :, 0], s_s, cv_s[:, tail])


CONFIG = dict(tm_proj=1024, tm=512, tm_moe=256, chunk_m=256, sub_m=1, chunk_g=64, sub_g=8,
              lt_x=512, sample_group=8)


def kernel(x_prompt, x_sample, mem_prompt, cache_mem_k, cache_mem_v, state_mlstm_C, state_mlstm_n, state_mlstm_m, state_gdn_S, state_gdn_conv, norm_mix, w_in, b_igate, b_fgate, mlstm_norm, conv_w, a_log, dt_bias, gdn_norm, mem_norm, w_mem_kv, w_br_mlstm, w_br_gdn, w_br_xattn, w_out, norm_ffn, w_router_group, b_router_group, w_router_expert, b_router_expert, w_exp_gate, w_exp_up, w_exp_down, final_norm):
    depth = w_in.shape[0]
    assert depth == 1, "one decoder layer"
    outs = _layer(
        x_prompt, x_sample, mem_prompt, cache_mem_k[0], cache_mem_v[0],
        state_mlstm_C[0], state_mlstm_n[0], state_mlstm_m[0], state_gdn_S[0], state_gdn_conv[0],
        norm_mix[0], w_in[0], b_igate[0], b_fgate[0], mlstm_norm[0], conv_w[0], a_log[0],
        dt_bias[0], gdn_norm[0], mem_norm[0], w_mem_kv[0], w_br_mlstm[0], w_br_gdn[0],
        w_br_xattn[0], w_out[0], norm_ffn[0], w_router_group[0], b_router_group[0],
        w_router_expert[0], b_router_expert[0], w_exp_gate[0], w_exp_up[0], w_exp_down[0],
        final_norm, cfg=CONFIG)
    y_prompt, y_sample = outs[0], outs[1]
    return (y_prompt, y_sample) + tuple(o[None] for o in outs[2:])
```

```python
import functools

import jax
import jax.numpy as jnp
from jax import lax
from jax.experimental import pallas as pl
from jax.experimental.pallas import tpu as pltpu

F32 = jnp.float32
BF16 = jnp.bfloat16
EPS = 1e-6

N_HEADS = 4
HEAD_DIM = 128
BRANCH_W = N_HEADS * HEAD_DIM
CONV_W = 4
N_GROUPS = 4
EXPERTS_PER_GROUP = 4
N_EXPERTS = N_GROUPS * EXPERTS_PER_GROUP
N_PAIRS = 6
N_CLASSES = N_GROUPS * N_PAIRS
PAIR_LO = (0, 0, 0, 1, 1, 2)
PAIR_HI = (1, 2, 3, 2, 3, 3)
NEG_BIG = -1e30
LANES = 128
SUBLANES = 8
TOKEN_ROWS = 8
PROJ_GROUP = 3
VMEM_LIMIT = 48 * 1024 * 1024

COL_GATE = 0
COL_MQ, COL_MK, COL_MV, COL_MO = 6, 7, 8, 9
COL_GQ, COL_GK, COL_GV, COL_GO, COL_XQ = 10, 11, 12, 13, 14
N_PROJ = 15 * BRANCH_W


def _dot(a, b):
    return jnp.dot(a.astype(BF16), b.astype(BF16), preferred_element_type=F32)


def _dot_nt(a, b):
    return lax.dot_general(a.astype(BF16), b.astype(BF16), (((1,), (1,)), ((), ())),
                           preferred_element_type=F32)


def _dot_tn(a, b):
    return lax.dot_general(a.astype(BF16), b.astype(BF16), (((0,), (0,)), ((), ())),
                           preferred_element_type=F32)


def _dot_exact(a, b):
    return jnp.dot(a, b, preferred_element_type=F32, precision=lax.Precision.HIGHEST)


def _lanes_to_rows(x, lane0):
    r = lax.broadcasted_iota(jnp.int32, (SUBLANES, LANES), 0)
    c = lax.broadcasted_iota(jnp.int32, (SUBLANES, LANES), 1)
    sel = (c == r + lane0).astype(F32)
    return lax.dot_general(sel, x, (((1,), (1,)), ((), ())), preferred_element_type=F32,
                           precision=lax.Precision.HIGHEST)


def _rms(x, g):
    return x * lax.rsqrt(jnp.mean(x * x, axis=-1, keepdims=True) + EPS) * g


def _softplus(x):
    return jnp.maximum(x, 0.0) + jnp.log1p(jnp.exp(-jnp.abs(x)))


def _sigmoid(x):
    return 1.0 / (1.0 + jnp.exp(-x))


def _silu(x):
    return x * _sigmoid(x)


def _tri(n, strict=False):
    r = lax.broadcasted_iota(jnp.int32, (n, n), 0)
    c = lax.broadcasted_iota(jnp.int32, (n, n), 1)
    return (c < r) if strict else (c <= r)


def _tri_blocks(n, block, upper=False):
    shift = block.bit_length() - 1
    assert block == 1 << shift
    r = lax.broadcasted_iota(jnp.int32, (n, n), 0)
    c = lax.broadcasted_iota(jnp.int32, (n, n), 1)
    same = jnp.right_shift(r, shift) == jnp.right_shift(c, shift)
    return same & ((r <= c) if upper else (c <= r))


def _pad_rows(shape, t, seq_rows, n_valid):
    assert seq_rows & (seq_rows - 1) == 0
    row = lax.broadcasted_iota(jnp.int32, shape, 0)
    return (row & (seq_rows - 1)) + t * seq_rows >= n_valid


def _params(sem):
    return pltpu.CompilerParams(dimension_semantics=sem, vmem_limit_bytes=VMEM_LIMIT)


def _two_source_specs(n_first, block, width):
    first = pl.BlockSpec((block, width), lambda i, *_: (jnp.minimum(i, n_first - 1), 0))
    second = pl.BlockSpec((block, width), lambda i, *_: (jnp.maximum(i - n_first, 0), 0))
    return first, second


def _norm_proj_kernel(xa_ref, xb_ref, g_ref, w_ref, ws_ref, o_ref, os_ref, hb_ref, *, n_first):
    i = pl.program_id(0)
    j = pl.program_id(1)

    @pl.when(j == 0)
    def _():
        @pl.when(i < n_first)
        def _():
            hb_ref[...] = _rms(xa_ref[...], g_ref[...]).astype(BF16)

        @pl.when(i >= n_first)
        def _():
            hb_ref[...] = _rms(xb_ref[...], g_ref[...]).astype(BF16)

        os_ref[...] = jnp.dot(hb_ref[...], ws_ref[...], preferred_element_type=F32)

    o_ref[...] = jnp.dot(hb_ref[...], w_ref[0], preferred_element_type=F32)


def _proj_spec(rows, c, row_block):
    return pl.BlockSpec((None, rows, BRANCH_W),
                        lambda *g: (c // PROJ_GROUP, row_block(*g), c % PROJ_GROUP))


def _col_blocks(w, tn):
    d, n = w.shape
    return w.reshape(d, n // tn, tn).transpose(1, 0, 2)


def _norm_proj(xa, xb, g, w, ws, *, tm):
    d = xa.shape[1]
    n_first = xa.shape[0] // tm
    n_rows = xa.shape[0] + (0 if xb is None else xb.shape[0])
    if xb is None:
        xb = xa
    tn = w.shape[2]
    n = w.shape[0] * tn
    sa, sb = _two_source_specs(n_first, tm, d)
    return pl.pallas_call(
        functools.partial(_norm_proj_kernel, n_first=n_first),
        out_shape=(jax.ShapeDtypeStruct((n // tn, n_rows, tn), F32),
                   jax.ShapeDtypeStruct((n_rows, ws.shape[1]), F32)),
        grid=(n_rows // tm, n // tn),
        in_specs=[sa, sb,
                  pl.BlockSpec((1, d), lambda i, j: (0, 0)),
                  pl.BlockSpec((1, d, tn), lambda i, j: (j, 0, 0)),
                  pl.BlockSpec((d, ws.shape[1]), lambda i, j: (0, 0))],
        out_specs=(pl.BlockSpec((None, tm, tn), lambda i, j: (j, i, 0)),
                   pl.BlockSpec((tm, ws.shape[1]), lambda i, j: (i, 0))),
        scratch_shapes=[pltpu.VMEM((tm, d), BF16)],
        compiler_params=_params(("parallel", "arbitrary")),
        name="norm_proj",
    )(xa, xb, g, w, ws)


def _seq_tile_specs(G, Lt, nt, row_off):
    assert G == 1 or nt == 1

    def rows(b, t):
        return row_off + b * nt + t

    return rows


def _mlstm_kernel(*refs, L, n_sub, G, s_valid, s_total, has_init):
    if has_init:
        (q_ref, k_ref, v_ref, og_ref, gs_ref, bias_ref, norm_ref, c0_ref, n0_ref, m0_ref,
         h_ref, c_ref, n_ref, m_ref) = refs
    else:
        (q_ref, k_ref, v_ref, og_ref, gs_ref, bias_ref, norm_ref,
         h_ref, c_ref, n_ref, m_ref) = refs
    t = pl.program_id(1)
    Lt = n_sub * L
    rows = G * Lt

    @pl.when(t == 0)
    def _():
        if has_init:
            c_ref[...] = c0_ref[...]
            n_ref[...] = n0_ref[...]
            m_ref[...] = m0_ref[...]
        else:
            c_ref[...] = jnp.zeros_like(c_ref)
            n_ref[...] = jnp.zeros_like(n_ref)
            m_ref[...] = jnp.zeros_like(m_ref)

    z = gs_ref[...] + bias_ref[...]
    lane = lax.broadcasted_iota(jnp.int32, z.shape, 1)
    log_f = jnp.minimum(z, 0.0) - jnp.log1p(jnp.exp(-jnp.abs(z)))
    g = jnp.where((lane >= N_HEADS) & (lane < 2 * N_HEADS), log_f, z)
    if s_valid < s_total:
        pad = _pad_rows(z.shape, t, Lt, s_valid)
        g = jnp.where(pad, jnp.where(lane < N_HEADS, NEG_BIG, 0.0), g)
    g_t = _lanes_to_rows(g, 0)
    bt_cols = _dot_exact(_tri_blocks(rows, L).astype(F32), g)
    bt_rows = _dot_exact(g_t, _tri_blocks(rows, L, upper=True).astype(F32))
    causal = _tri(L)
    scale = HEAD_DIM ** -0.5
    probs = [(s, h) for s in range(G) for h in range(N_HEADS)]

    for c in range(n_sub):
        P = []
        for s, h in probs:
            r = slice(s * Lt + c * L, s * Lt + (c + 1) * L)
            sl = slice(h * HEAD_DIM, (h + 1) * HEAD_DIM)
            p = dict(s=s, h=h, r=r, sl=sl, q=q_ref[r, sl], k=k_ref[r, sl] * scale, v=v_ref[r, sl],
                     bt_c=bt_cols[r, N_HEADS + h:N_HEADS + h + 1], it_c=g[r, h:h + 1])
            bt_r = bt_rows[N_HEADS + h:N_HEADS + h + 1, r]
            it_r = g_t[h:h + 1, r]
            p["log_d"] = jnp.where(causal, p["bt_c"] - bt_r + it_r, -jnp.inf)
            p["c_prev"] = c_ref[s, h]
            p["n_prev"] = n_ref[s, h:h + 1, :]
            p["m_prev"] = m_ref[s, h:h + 1, 0:1]
            P.append(p)
        for p in P:
            p["qk"] = _dot_nt(p["q"], p["k"])
        for p in P:
            p["qc"] = _dot(p["q"], p["c_prev"])
        for p in P:
            inter = p["bt_c"] + p["m_prev"]
            p["m_t"] = jnp.maximum(inter, jnp.max(p["log_d"], axis=-1, keepdims=True))
            p["inter_w"] = jnp.exp(inter - p["m_t"])
            p["sd"] = p["qk"] * jnp.exp(p["log_d"] - p["m_t"])
        for p in P:
            p["sv"] = _dot(p["sd"], p["v"])
        for p in P:
            bt_last = p["bt_c"][L - 1:L, :]
            p["m_new"] = p["m_t"][L - 1:L, :]
            w = jnp.exp(bt_last - p["bt_c"] + p["it_c"] - p["m_new"])
            p["decay"] = jnp.exp(bt_last + p["m_prev"] - p["m_new"])
            p["kw"] = p["k"] * w
        for p in P:
            p["kv"] = _dot_tn(p["kw"], p["v"])
        for p in P:
            num = p["sv"] + p["qc"] * p["inter_w"]
            den = (jnp.sum(p["sd"], axis=-1, keepdims=True)
                   + p["inter_w"] * jnp.sum(p["q"] * p["n_prev"], axis=-1, keepdims=True))
            hh = num / jnp.maximum(jnp.abs(den), jnp.exp(-p["m_t"]))
            hn = hh * lax.rsqrt(jnp.mean(hh * hh, axis=-1, keepdims=True) + EPS) * norm_ref[:, p["sl"]]
            p["h_out"] = hn * _sigmoid(og_ref[p["r"], p["sl"]])
            p["n_new"] = p["decay"] * p["n_prev"] + jnp.sum(p["kw"], axis=0, keepdims=True)
        for p in P:
            s, h = p["s"], p["h"]
            h_ref[p["r"], p["sl"]] = p["h_out"]
            n_ref[s, h:h + 1, :] = p["n_new"]
            m_ref[s, h:h + 1, :] = jnp.broadcast_to(p["m_new"], (1, LANES))
            c_ref[s, h] = p["decay"] * p["c_prev"] + p["kv"]


def _mlstm(proj, gs, bias_row, norm_row, init, *, n_batch, seq, s_valid, row_off, L, n_sub, G):
    Lt = n_sub * L
    nt = seq // Lt
    rows = G * Lt
    has_init = init is not None
    blk = _seq_tile_specs(G, Lt, nt, row_off)

    def col(c):
        return _proj_spec(rows, c, blk)

    in_specs = [col(COL_MQ), col(COL_MK), col(COL_MV), col(COL_MO),
                pl.BlockSpec((rows, LANES), lambda b, t: (blk(b, t), 0)),
                pl.BlockSpec((1, LANES), lambda b, t: (0, 0)),
                pl.BlockSpec((1, BRANCH_W), lambda b, t: (0, 0))]
    args = [proj, proj, proj, proj, gs, bias_row, norm_row]
    c_spec = pl.BlockSpec((G, N_HEADS, HEAD_DIM, HEAD_DIM), lambda b, t: (b, 0, 0, 0))
    n_spec = pl.BlockSpec((G, N_HEADS, HEAD_DIM), lambda b, t: (b, 0, 0))
    m_spec = pl.BlockSpec((G, N_HEADS, LANES), lambda b, t: (b, 0, 0))
    if has_init:
        in_specs += [c_spec, n_spec, m_spec]
        args += list(init)
    return pl.pallas_call(
        functools.partial(_mlstm_kernel, L=L, n_sub=n_sub, G=G, s_valid=s_valid, s_total=seq,
                          has_init=has_init),
        out_shape=(jax.ShapeDtypeStruct((n_batch * seq, BRANCH_W), F32),
                   jax.ShapeDtypeStruct((n_batch, N_HEADS, HEAD_DIM, HEAD_DIM), F32),
                   jax.ShapeDtypeStruct((n_batch, N_HEADS, HEAD_DIM), F32),
                   jax.ShapeDtypeStruct((n_batch, N_HEADS, LANES), F32)),
        grid=(n_batch // G, nt),
        in_specs=in_specs,
        out_specs=(pl.BlockSpec((rows, BRANCH_W), lambda b, t: (b * nt + t, 0)),
                   c_spec, n_spec, m_spec),
        compiler_params=_params(("parallel", "arbitrary")),
        name="mlstm",
    )(*args)


def _gdn_kernel(*refs, L, n_sub, G, s_valid, s_total, has_init):
    if has_init:
        (q_ref, k_ref, v_ref, og_ref, gs_ref, cw_ref, arow_ref, dtrow_ref, norm_ref,
         conv0_ref, s0_ref, o_ref, s_ref, conv1_ref, ext_ref, conv_ref) = refs
    else:
        (q_ref, k_ref, v_ref, og_ref, gs_ref, cw_ref, arow_ref, dtrow_ref, norm_ref,
         o_ref, s_ref, conv1_ref, ext_ref, conv_ref) = refs
    t = pl.program_id(1)
    Lt = n_sub * L
    rows = G * Lt
    nt = s_total // Lt
    W = BRANCH_W

    @pl.when(t == 0)
    def _():
        if has_init:
            s_ref[...] = s0_ref[...]
            ext_ref[:, 0:SUBLANES, :] = conv0_ref[...]
        else:
            s_ref[...] = jnp.zeros_like(s_ref)
            ext_ref[:, 0:SUBLANES, :] = jnp.zeros((G, SUBLANES, 3 * W), F32)

    @pl.when(t > 0)
    def _():
        ext_ref[:, 0:SUBLANES, :] = ext_ref[:, Lt:Lt + SUBLANES, :]

    base = SUBLANES - (CONV_W - 1)
    for s in range(G):
        rs = slice(s * Lt, (s + 1) * Lt)
        ext_ref[s, SUBLANES:SUBLANES + Lt, 0:W] = q_ref[rs, :]
        ext_ref[s, SUBLANES:SUBLANES + Lt, W:2 * W] = k_ref[rs, :]
        ext_ref[s, SUBLANES:SUBLANES + Lt, 2 * W:3 * W] = v_ref[rs, :]
        acc = ext_ref[s, base:base + Lt, :] * cw_ref[0:1, :]
        for j in range(1, CONV_W):
            acc = acc + ext_ref[s, base + j:base + j + Lt, :] * cw_ref[j:j + 1, :]
        conv_ref[rs, :] = _silu(acc)

    @pl.when(t == nt - 1)
    def _():
        v_last = s_valid - (nt - 1) * Lt
        conv1_ref[...] = ext_ref[:, v_last:v_last + SUBLANES, :]

    z = gs_ref[...]
    lane = lax.broadcasted_iota(jnp.int32, z.shape, 1)
    gdec = arow_ref[...] * _softplus(z + dtrow_ref[...])
    gb = jnp.where((lane >= 2 * N_HEADS) & (lane < 3 * N_HEADS), gdec,
                   jnp.where((lane >= 3 * N_HEADS) & (lane < 4 * N_HEADS), _sigmoid(z), 0.0))
    if s_valid < s_total:
        gb = jnp.where(_pad_rows(z.shape, t, Lt, s_valid), 0.0, gb)
    gb_t = _lanes_to_rows(gb, 2 * N_HEADS)
    gam_cols = _dot_exact(_tri_blocks(rows, L).astype(F32), gb)
    gam_rows = _dot_exact(gb_t, _tri_blocks(rows, L, upper=True).astype(F32))
    incl = _tri(L)
    strict = _tri(L, strict=True)
    eye = (lax.broadcasted_iota(jnp.int32, (L, L), 0)
           == lax.broadcasted_iota(jnp.int32, (L, L), 1)).astype(F32)

    P = []
    for s in range(G):
        for c in range(n_sub):
            for h in range(N_HEADS):
                r = slice(s * Lt + c * L, s * Lt + (c + 1) * L)
                sl = slice(h * HEAD_DIM, (h + 1) * HEAD_DIM)
                q = conv_ref[r, h * HEAD_DIM:(h + 1) * HEAD_DIM]
                k = conv_ref[r, W + h * HEAD_DIM:W + (h + 1) * HEAD_DIM]
                v = conv_ref[r, 2 * W + h * HEAD_DIM:2 * W + (h + 1) * HEAD_DIM]
                q = q * lax.rsqrt(jnp.sum(q * q, axis=-1, keepdims=True) + EPS) * (HEAD_DIM ** -0.5)
                k = k * lax.rsqrt(jnp.sum(k * k, axis=-1, keepdims=True) + EPS)
                gam_c = gam_cols[r, 2 * N_HEADS + h:2 * N_HEADS + h + 1]
                gam_r = gam_rows[h:h + 1, r]
                beta_c = gb[r, 3 * N_HEADS + h:3 * N_HEADS + h + 1]
                decay = jnp.where(incl, jnp.exp(jnp.where(incl, gam_c - gam_r, 0.0)), 0.0)
                P.append(dict(s=s, c=c, h=h, r=r, sl=sl, q=q, k=k, v=v, gam_c=gam_c,
                              beta_c=beta_c, decay=decay))
    for p in P:
        p["kk"] = _dot_nt(p["k"], p["k"])
    for p in P:
        p["qk"] = _dot_nt(p["q"], p["k"]) * p["decay"]
    for p in P:
        e_gam = jnp.exp(p["gam_c"])
        gam_last = p["gam_c"][L - 1:L, :]
        p["pw"] = -jnp.where(strict, p["beta_c"] * p["kk"] * p["decay"], 0.0)
        p["x"] = eye + p["pw"]
        p["rhs"] = jnp.concatenate([p["beta_c"] * p["v"], p["beta_c"] * p["k"] * e_gam], axis=-1)
        p["q_dec"] = p["q"] * e_gam
        p["k_dec"] = p["k"] * jnp.exp(gam_last - p["gam_c"])
        p["e_last"] = jnp.exp(gam_last)
    n = 2
    while n < L:
        for p in P:
            p["pw"] = _dot(p["pw"], p["pw"])
        for p in P:
            p["x"] = p["x"] + _dot(p["x"], p["pw"])
        n *= 2
    for p in P:
        sol = _dot(p["x"], p["rhs"])
        p["u"] = sol[:, :HEAD_DIM]
        p["w"] = sol[:, HEAD_DIM:]

    for c in range(n_sub):
        Pc = [p for p in P if p["c"] == c]
        for p in Pc:
            p["s_prev"] = s_ref[p["s"], p["h"]]
            p["ws"] = _dot(p["w"], p["s_prev"])
        for p in Pc:
            p["qs"] = _dot(p["q_dec"], p["s_prev"])
        for p in Pc:
            p["v_new"] = p["u"] - p["ws"]
        for p in Pc:
            p["o"] = p["qs"] + _dot(p["qk"], p["v_new"])
        for p in Pc:
            p["kv"] = _dot_tn(p["k_dec"], p["v_new"])
        for p in Pc:
            o = p["o"]
            on = o * lax.rsqrt(jnp.mean(o * o, axis=-1, keepdims=True) + EPS) * norm_ref[...]
            p["o_out"] = on * _silu(og_ref[p["r"], p["sl"]])
        for p in Pc:
            s_ref[p["s"], p["h"]] = p["e_last"] * p["s_prev"] + p["kv"]
            o_ref[p["r"], p["sl"]] = p["o_out"]


def _gdn(proj, gs, conv_w, a_row, dt_row, norm_row, init, *, n_batch, seq, s_valid, row_off,
         L, n_sub, G):
    Lt = n_sub * L
    nt = seq // Lt
    rows = G * Lt
    has_init = init is not None
    blk = _seq_tile_specs(G, Lt, nt, row_off)

    def col(c):
        return _proj_spec(rows, c, blk)

    def const(shape):
        return pl.BlockSpec(shape, lambda b, t: (0,) * len(shape))

    in_specs = [col(COL_GQ), col(COL_GK), col(COL_GV), col(COL_GO),
                pl.BlockSpec((rows, LANES), lambda b, t: (blk(b, t), 0)),
                const((CONV_W, 3 * BRANCH_W)), const((1, LANES)), const((1, LANES)),
                const((1, HEAD_DIM))]
    args = [proj, proj, proj, proj, gs, conv_w, a_row, dt_row, norm_row]
    s_spec = pl.BlockSpec((G, N_HEADS, HEAD_DIM, HEAD_DIM), lambda b, t: (b, 0, 0, 0))
    conv_spec = pl.BlockSpec((G, SUBLANES, 3 * BRANCH_W), lambda b, t: (b, 0, 0))
    if has_init:
        in_specs += [conv_spec, s_spec]
        args += list(init)
    return pl.pallas_call(
        functools.partial(_gdn_kernel, L=L, n_sub=n_sub, G=G, s_valid=s_valid, s_total=seq,
                          has_init=has_init),
        out_shape=(jax.ShapeDtypeStruct((n_batch * seq, BRANCH_W), F32),
                   jax.ShapeDtypeStruct((n_batch, N_HEADS, HEAD_DIM, HEAD_DIM), F32),
                   jax.ShapeDtypeStruct((n_batch, SUBLANES, 3 * BRANCH_W), F32)),
        grid=(n_batch // G, nt),
        in_specs=in_specs,
        out_specs=(pl.BlockSpec((rows, BRANCH_W), lambda b, t: (b * nt + t, 0)),
                   s_spec, conv_spec),
        scratch_shapes=[pltpu.VMEM((G, Lt + SUBLANES, 3 * BRANCH_W), F32),
                        pltpu.VMEM((rows, 3 * BRANCH_W), F32)],
        compiler_params=_params(("parallel", "arbitrary")),
        name="gdn",
    )(*args)


def _xattn_kernel(q_ref, mk_ref, mv_ref, o_ref, *, G, lt, mem_len, head_rows):
    scale = HEAD_DIM ** -0.5

    def head(ref, s, h):
        if head_rows:
            return ref[pl.ds(s * mem_len * N_HEADS + h, mem_len, stride=N_HEADS), :]
        return ref[s * mem_len:(s + 1) * mem_len, h * HEAD_DIM:(h + 1) * HEAD_DIM]

    P = []
    for s in range(G):
        for h in range(N_HEADS):
            P.append(dict(s=s, h=h, r=slice(s * lt, (s + 1) * lt),
                          sl=slice(h * HEAD_DIM, (h + 1) * HEAD_DIM)))
    for p in P:
        p["sc"] = _dot_nt(q_ref[p["r"], p["sl"]], head(mk_ref, p["s"], p["h"])) * scale
    for p in P:
        e = jnp.exp(p["sc"] - jnp.max(p["sc"], axis=-1, keepdims=True))
        p["p"] = e / jnp.sum(e, axis=-1, keepdims=True)
    for p in P:
        o_ref[p["r"], p["sl"]] = _dot(p["p"], head(mv_ref, p["s"], p["h"]))


def _xattn(proj, mk, mv, *, n_batch, seq, row_off, lt, G, mem_len, kv_specs, head_rows):
    nt = seq // lt
    blk = _seq_tile_specs(G, lt, nt, row_off)
    return pl.pallas_call(
        functools.partial(_xattn_kernel, G=G, lt=lt, mem_len=mem_len, head_rows=head_rows),
        out_shape=jax.ShapeDtypeStruct((n_batch * seq, BRANCH_W), F32),
        grid=(n_batch // G, nt),
        in_specs=[_proj_spec(G * lt, COL_XQ, blk)] + list(kv_specs),
        out_specs=pl.BlockSpec((G * lt, BRANCH_W), lambda b, t: (b * nt + t, 0)),
        compiler_params=_params(("parallel", "arbitrary")),
        name="xattn",
    )(proj, mk, mv)


def _lane_first_max(x, mask, lane_f):
    xm = jnp.where(mask, x, -jnp.inf)
    mx = jnp.max(xm, axis=-1, keepdims=True)
    idx = jnp.min(jnp.where(xm == mx, lane_f, float(LANES)), axis=-1, keepdims=True)
    return mx, idx


def _store_token_tiles(ref, x, tok_rows):
    n = x.shape[0]
    for j in range(x.shape[1] // LANES):
        ref[pl.ds(j, n, stride=tok_rows), :] = x[:, j * LANES:(j + 1) * LANES]


def _load_token_tiles(ref, n, tok_rows, n_chunks=SUBLANES):
    return jnp.concatenate([ref[pl.ds(j, n, stride=tok_rows), :] for j in range(n_chunks)], axis=1)


def _merge_kernel(hap_ref, has_ref, hbp_ref, hbs_ref, hcp_ref, hcs_ref, xp_ref, xs_ref,
                  ga0_ref, ga1_ref, gb0_ref, gb1_ref, gc0_ref, gc1_ref,
                  wa_ref, wb_ref, wc_ref, wo_ref, nf_ref, wr_ref, br_ref,
                  x1_ref, route_ref, *, n_first):
    i = pl.program_id(0)

    def gate(lo_ref, hi_ref):
        return jnp.concatenate([_sigmoid(lo_ref[...]), _sigmoid(hi_ref[...])], axis=1)

    def body(ha_ref, hb_ref, hc_ref, x_ref):
        mixed = (gate(ga0_ref, ga1_ref) * _dot(ha_ref[...], wa_ref[...])
                 + gate(gb0_ref, gb1_ref) * _dot(hb_ref[...], wb_ref[...])
                 + gate(gc0_ref, gc1_ref) * _dot(hc_ref[...], wc_ref[...]))
        x1 = x_ref[...] + _dot(mixed, wo_ref[...])
        h2 = _rms(x1, nf_ref[...])
        logits = _dot(h2, wr_ref[...]) + br_ref[...]
        lane = lax.broadcasted_iota(jnp.int32, logits.shape, 1)
        lane_f = lane.astype(F32)
        is_g = lane < N_GROUPS
        _, g_idx = _lane_first_max(logits, is_g, lane_f)
        e_lo = N_GROUPS + EXPERTS_PER_GROUP * g_idx
        in_grp = (lane_f >= e_lo) & (lane_f < e_lo + EXPERTS_PER_GROUP)
        _, i1 = _lane_first_max(logits, in_grp, lane_f)
        _, i2 = _lane_first_max(logits, in_grp & (lane_f != i1), lane_f)
        lo = jnp.minimum(i1, i2) - e_lo
        hi = jnp.maximum(i1, i2) - e_lo
        pair = lo * (7.0 - lo) * 0.5 + (hi - lo - 1.0)
        cls = g_idx * float(N_PAIRS) + pair
        _store_token_tiles(x1_ref, x1, TOKEN_ROWS)
        route_ref[...] = jnp.where(lane == ROUTE_CLS, cls, 0.0)

    @pl.when(i < n_first)
    def _():
        body(hap_ref, hbp_ref, hcp_ref, xp_ref)

    @pl.when(i >= n_first)
    def _():
        body(has_ref, hbs_ref, hcs_ref, xs_ref)


def _merge(ha, hb, hc, xs, proj, wa, wb, wc, wo, nf, wr, br, *, tm):
    d = xs[0].shape[1]
    n_first = xs[0].shape[0] // tm
    n_rows = xs[0].shape[0] + xs[1].shape[0]
    specs = []
    for width in (BRANCH_W, BRANCH_W, BRANCH_W, d):
        specs += list(_two_source_specs(n_first, tm, width))

    def const(a):
        return pl.BlockSpec(a.shape, lambda i: (0,) * a.ndim)

    n_gate = 3 * d // BRANCH_W
    specs += [_proj_spec(tm, COL_GATE + c, lambda i: i) for c in range(n_gate)]
    weights = [wa, wb, wc, wo, nf, wr, br]
    specs += [const(a) for a in weights]
    return pl.pallas_call(
        functools.partial(_merge_kernel, n_first=n_first),
        out_shape=(jax.ShapeDtypeStruct((n_rows * TOKEN_ROWS, LANES), F32),
                   jax.ShapeDtypeStruct((n_rows, LANES), F32)),
        grid=(n_rows // tm,),
        in_specs=specs,
        out_specs=(pl.BlockSpec((tm * TOKEN_ROWS, LANES), lambda i: (i, 0)),
                   pl.BlockSpec((tm, LANES), lambda i: (i, 0))),
        compiler_params=_params(("parallel",)),
        name="merge",
    )(*ha, *hb, *hc, *xs, *([proj] * n_gate), *weights)


GATHER_UNROLL = 8
MOE_SLOTS = 2
ROUTE_CLS = 0


def _record_gather_start(src_ref, dst_ref, sem, ids_ref, base, n, rec):
    def issue(r, carry):
        src = pl.multiple_of(ids_ref[base + r] * rec, rec)
        dst = pl.multiple_of(r * rec, rec)
        pltpu.make_async_copy(src_ref.at[pl.ds(src, rec)], dst_ref.at[pl.ds(dst, rec)], sem).start()
        return carry
    lax.fori_loop(0, n, issue, 0, unroll=GATHER_UNROLL)


def _record_gather_wait(src_ref, dst_ref, sem):
    pltpu.make_async_copy(src_ref.at[pl.ds(0, dst_ref.shape[0])], dst_ref, sem).wait()


def _moe_kernel(ea_ref, eb_ref, nused_ref, x_ref, wga_ref, wua_ref, wda_ref,
                wgb_ref, wub_ref, wdb_ref, nf_ref, fin_ref, wr_ref, br_ref, y_ref, *, tm, d):
    i = pl.program_id(0)
    n_used = nused_ref[0]

    @pl.when(i < n_used)
    def _():
        x1 = x_ref[...]
        h2 = _rms(x1, nf_ref[...]).astype(BF16)
        logits = jnp.dot(h2, wr_ref[...], preferred_element_type=F32) + br_ref[...]
        lane = lax.broadcasted_iota(jnp.int32, logits.shape, 1)

        def pick(l):
            return jnp.sum(jnp.where(lane == l, logits, 0.0), axis=-1, keepdims=True)

        ea, eb = ea_ref[i], eb_ref[i]
        is_g = lane < N_GROUPS
        g_max = jnp.max(jnp.where(is_g, logits, -jnp.inf), axis=-1, keepdims=True)
        g_den = jnp.sum(jnp.where(is_g, jnp.exp(logits - g_max), 0.0), axis=-1, keepdims=True)
        p_grp = jnp.exp(pick(ea // EXPERTS_PER_GROUP) - g_max) / g_den
        v_a, v_b = pick(N_GROUPS + ea), pick(N_GROUPS + eb)
        combs = (p_grp / (1.0 + jnp.exp(v_b - v_a)), p_grp / (1.0 + jnp.exp(v_a - v_b)))
        acc = jnp.zeros((tm, d), F32)
        for comb, wg_ref, wu_ref, wd_ref in ((combs[0], wga_ref, wua_ref, wda_ref),
                                             (combs[1], wgb_ref, wub_ref, wdb_ref)):
            a = jnp.dot(h2, wg_ref[0], preferred_element_type=F32)
            u = jnp.dot(h2, wu_ref[0], preferred_element_type=F32)
            act = _silu(a) * u * comb
            acc = acc + _dot(act, wd_ref[0])
        _store_token_tiles(y_ref, _rms(x1 + acc, fin_ref[...]), SUBLANES)

    @pl.when(i >= n_used)
    def _():
        y_ref[...] = jnp.zeros_like(y_ref)


def _moe(tile_ea, tile_eb, n_used, x_sorted, wg, wu, wd, nf, fin, wr, br, *, tm, n_tiles):
    d = nf.shape[1]
    f = wg.shape[2]
    assert d == SUBLANES * LANES

    def expert(shape, which):
        return pl.BlockSpec((1,) + shape, lambda i, ea, eb, nu: ((ea, eb)[which][i], 0, 0))

    def const(a):
        return pl.BlockSpec(a.shape, lambda i, *_: (0,) * a.ndim)

    grid_spec = pltpu.PrefetchScalarGridSpec(
        num_scalar_prefetch=3,
        grid=(n_tiles,),
        in_specs=[pl.BlockSpec((tm, d), lambda i, *_: (i, 0)),
                  expert((d, f), 0), expert((d, f), 0), expert((f, d), 0),
                  expert((d, f), 1), expert((d, f), 1), expert((f, d), 1),
                  const(nf), const(fin), const(wr), const(br)],
        out_specs=pl.BlockSpec((tm * SUBLANES, LANES), lambda i, *_: (i, 0)),
    )
    return pl.pallas_call(
        functools.partial(_moe_kernel, tm=tm, d=d),
        out_shape=jax.ShapeDtypeStruct((n_tiles * tm * SUBLANES, LANES), F32),
        grid_spec=grid_spec,
        compiler_params=_params(("arbitrary",)),
        name="moe",
    )(tile_ea, tile_eb, n_used, x_sorted, wg, wu, wd, wg, wu, wd, nf, fin, wr, br)


def _unsort_kernel(pos_ref, y_hbm, o_ref, buf, sem, *, tm, base):
    i = pl.program_id(0)
    slot = i % 2

    @pl.when(i == 0)
    def _():
        _record_gather_start(y_hbm, buf.at[0], sem.at[0], pos_ref, base, tm, SUBLANES)

    @pl.when(i + 1 < pl.num_programs(0))
    def _():
        _record_gather_start(y_hbm, buf.at[1 - slot], sem.at[1 - slot], pos_ref,
                             base + (i + 1) * tm, tm, SUBLANES)

    _record_gather_wait(y_hbm, buf.at[slot], sem.at[slot])
    o_ref[...] = _load_token_tiles(buf.at[slot], tm, SUBLANES)


def _unsort(pos, y_rec, *, base, n_rows, tm):
    d = SUBLANES * LANES
    grid_spec = pltpu.PrefetchScalarGridSpec(
        num_scalar_prefetch=1,
        grid=(n_rows // tm,),
        in_specs=[pl.BlockSpec(memory_space=pl.ANY)],
        out_specs=pl.BlockSpec((tm, d), lambda i, *_: (i, 0)),
        scratch_shapes=[pltpu.VMEM((2, tm * SUBLANES, LANES), F32),
                        pltpu.SemaphoreType.DMA((2,))],
    )
    return pl.pallas_call(
        functools.partial(_unsort_kernel, tm=tm, base=base),
        out_shape=jax.ShapeDtypeStruct((n_rows, d), F32),
        grid_spec=grid_spec,
        compiler_params=_params(("arbitrary",)),
        name="unsort",
    )(pos, y_rec)


def _sort_plan(cls, *, tm, n_tiles):
    n = cls.shape[0]
    onehot = (cls[:, None] == jnp.arange(N_CLASSES, dtype=jnp.int32)[None, :]).astype(jnp.int32)
    csum = jnp.cumsum(onehot, axis=0)
    rank = jnp.sum((csum - onehot) * onehot, axis=1)
    counts = csum[-1]
    tiles_per = (counts + tm - 1) // tm
    tile_end = jnp.cumsum(tiles_per)
    tile_start = tile_end - tiles_per
    pos = (jnp.sum(tile_start[None, :] * onehot, axis=1) * tm + rank).astype(jnp.int32)
    n_used = tile_end[-1].astype(jnp.int32)
    ids = jnp.zeros((n_tiles * tm,), jnp.int32).at[pos].set(jnp.arange(n, dtype=jnp.int32))
    tile = jnp.minimum(jnp.arange(n_tiles, dtype=jnp.int32), n_used - 1)
    tile_cls = jnp.sum((tile[:, None] >= tile_end[None, :]).astype(jnp.int32), axis=1)
    grp = tile_cls // N_PAIRS
    pair = tile_cls % N_PAIRS
    lo = jnp.asarray(PAIR_LO, jnp.int32)[pair]
    hi = jnp.asarray(PAIR_HI, jnp.int32)[pair]
    ea = (grp * EXPERTS_PER_GROUP + lo).astype(jnp.int32)
    eb = (grp * EXPERTS_PER_GROUP + hi).astype(jnp.int32)
    return pos, ids, ea, eb, n_used.reshape(1)


def _lane_row(values, start):
    row = jnp.zeros((LANES,), F32)
    return row.at[start:start + values.shape[0]].set(values.astype(F32)).reshape(1, LANES)


def _layer(x_prompt, x_sample, mem_prompt, cache_mem_k, cache_mem_v,
           state_mlstm_C, state_mlstm_n, state_mlstm_m, state_gdn_S, state_gdn_conv,
           norm_mix, w_in, b_igate, b_fgate, mlstm_norm, conv_w, a_log, dt_bias, gdn_norm,
           mem_norm, w_mem_kv, w_br_mlstm, w_br_gdn, w_br_xattn, w_out, norm_ffn,
           w_router_group, b_router_group, w_router_expert, b_router_expert,
           w_exp_gate, w_exp_up, w_exp_down, final_norm, *, cfg):
    bp, sp, d = x_prompt.shape
    bs, ss, _ = x_sample.shape
    mem_len = mem_prompt.shape[1]
    ss_pad = SUBLANES
    assert ss <= ss_pad and d % LANES == 0
    n_p = bp * sp
    n_s = bs * ss_pad
    W = BRANCH_W
    gs_ = cfg["sample_group"]

    sizes = (W, W, W, N_HEADS, N_HEADS, W, 3 * W, N_HEADS, N_HEADS, W, W, d, d, d)
    offs = [0]
    for s in sizes:
        offs.append(offs[-1] + s)
    (m_q, m_k, m_v, m_i, m_f, m_o, g_qkv, g_a, g_b, g_out, x_q, gate_a, gate_b, gate_c) = [
        w_in[:, offs[j]:offs[j + 1]] for j in range(len(sizes))]
    w_big = _col_blocks(
        jnp.concatenate([gate_a, gate_b, gate_c, m_q, m_k, m_v, m_o, g_qkv, g_out, x_q],
                        axis=1).astype(BF16), PROJ_GROUP * W)
    w_small = jnp.concatenate(
        [m_i, m_f, g_a, g_b, jnp.zeros((d, LANES - 4 * N_HEADS), w_in.dtype)], axis=1).astype(BF16)
    w_router = jnp.concatenate(
        [w_router_group, w_router_expert,
         jnp.zeros((d, LANES - N_GROUPS - N_EXPERTS), w_in.dtype)], axis=1).astype(BF16)
    b_router = _lane_row(jnp.concatenate([b_router_group, b_router_expert]), 0)
    gate_bias = _lane_row(jnp.concatenate([b_igate, b_fgate]), 0)
    a_row = _lane_row(-jnp.exp(a_log.astype(F32)), 2 * N_HEADS)
    dt_row = _lane_row(dt_bias, 2 * N_HEADS)

    xp2 = x_prompt.reshape(n_p, d)
    xs2 = jnp.pad(x_sample, ((0, 0), (0, ss_pad - ss), (0, 0))).reshape(n_s, d)

    proj, gs = _norm_proj(xp2, xs2, norm_mix.reshape(1, d), w_big, w_small, tm=cfg["tm_proj"])

    mem2 = mem_prompt.reshape(bp * mem_len, d)
    mem_kv, _ = _norm_proj(mem2, None, mem_norm.reshape(1, d),
                           _col_blocks(w_mem_kv.astype(BF16), W),
                           jnp.zeros((d, LANES), BF16), tm=mem_len)

    mnorm = mlstm_norm.reshape(1, W)
    gnorm = gdn_norm.reshape(1, HEAD_DIM)
    s_off = n_p // (gs_ * ss_pad)
    ha_p, c_p, nn_p, mm_p = _mlstm(proj, gs, gate_bias, mnorm, None, n_batch=bp, seq=sp,
                                   s_valid=sp, row_off=0, L=cfg["chunk_m"], n_sub=cfg["sub_m"], G=1)
    m0 = jnp.broadcast_to(state_mlstm_m[:, :, None], (bs, N_HEADS, LANES))
    ha_s, c_s, nn_s, mm_s = _mlstm(proj, gs, gate_bias, mnorm,
                                   (state_mlstm_C, state_mlstm_n, m0), n_batch=bs, seq=ss_pad,
                                   s_valid=ss, row_off=s_off, L=ss_pad, n_sub=1, G=gs_)
    hb_p, s_p, cv_p = _gdn(proj, gs, conv_w, a_row, dt_row, gnorm, None, n_batch=bp, seq=sp,
                           s_valid=sp, row_off=0, L=cfg["chunk_g"], n_sub=cfg["sub_g"], G=1)
    conv0 = jnp.pad(state_gdn_conv, ((0, 0), (SUBLANES - (CONV_W - 1), 0), (0, 0)))
    hb_s, s_s, cv_s = _gdn(proj, gs, conv_w, a_row, dt_row, gnorm, (conv0, state_gdn_S),
                           n_batch=bs, seq=ss_pad, s_valid=ss, row_off=s_off, L=ss_pad,
                           n_sub=1, G=gs_)
    lt = min(sp, cfg["lt_x"])
    hc_p = _xattn(proj, mem_kv, mem_kv, n_batch=bp, seq=sp, row_off=0, lt=lt, G=1,
                  mem_len=mem_len, head_rows=False,
                  kv_specs=[pl.BlockSpec((None, mem_len, W), lambda b, t, c=c: (c, b, 0))
                            for c in range(2)])
    kv_rows = gs_ * mem_len * N_HEADS
    hc_s = _xattn(proj, cache_mem_k.reshape(bs * mem_len * N_HEADS, HEAD_DIM),
                  cache_mem_v.reshape(bs * mem_len * N_HEADS, HEAD_DIM),
                  n_batch=bs, seq=ss_pad, row_off=s_off, lt=ss_pad, G=gs_, mem_len=mem_len,
                  head_rows=True,
                  kv_specs=[pl.BlockSpec((kv_rows, HEAD_DIM), lambda b, t: (b, 0))] * 2)

    tm = cfg["tm"]
    x1rec, route = _merge((ha_p, ha_s), (hb_p, hb_s), (hc_p, hc_s), (xp2, xs2), proj,
                          w_br_mlstm.astype(BF16), w_br_gdn.astype(BF16), w_br_xattn.astype(BF16),
                          w_out.astype(BF16), norm_ffn.reshape(1, d), w_router, b_router, tm=tm)

    tm_moe = cfg["tm_moe"]
    n_all = n_p + n_s
    n_tiles = (n_all + N_CLASSES * (tm_moe - 1)) // tm_moe + 1
    cls = route[:, ROUTE_CLS].astype(jnp.int32)
    pos, ids, ea, eb, n_used = _sort_plan(cls, tm=tm_moe, n_tiles=n_tiles)
    x_sorted = _unsort(ids, x1rec, base=0, n_rows=n_tiles * tm_moe, tm=tm)
    y_sorted = _moe(ea, eb, n_used, x_sorted, w_exp_gate.astype(BF16),
                    w_exp_up.astype(BF16), w_exp_down.astype(BF16), norm_ffn.reshape(1, d),
                    final_norm.reshape(1, d), w_router, b_router, tm=tm_moe, n_tiles=n_tiles)
    y_p = _unsort(pos, y_sorted, base=0, n_rows=n_p, tm=tm)
    y_s = _unsort(pos, y_sorted, base=n_p, n_rows=n_s, tm=tm)

    y_prompt = y_p.reshape(bp, sp, d)
    y_sample = y_s.reshape(bs, ss_pad, d)[:, :ss]
    mk_p = mem_kv[0].reshape(bp, mem_len, N_HEADS, HEAD_DIM)
    mv_p = mem_kv[1].reshape(bp, mem_len, N_HEADS, HEAD_DIM)
    tail = slice(SUBLANES - (CONV_W - 1), SUBLANES)
    return (y_prompt, y_sample, mk_p, mv_p,
            c_p, nn_p, mm_p[:, :, 0], s_p, cv_p[:, tail],
            c_s, nn_s, mm_s[:, :, 0], s_s, cv_s[:, tail])


CONFIG = dict(tm_proj=1024, tm=512, tm_moe=256, chunk_m=256, sub_m=1, chunk_g=64, sub_g=8,
              lt_x=512, sample_group=8)


def kernel(x_prompt, x_sample, mem_prompt, cache_mem_k, cache_mem_v, state_mlstm_C, state_mlstm_n, state_mlstm_m, state_gdn_S, state_gdn_conv, norm_mix, w_in, b_igate, b_fgate, mlstm_norm, conv_w, a_log, dt_bias, gdn_norm, mem_norm, w_mem_kv, w_br_mlstm, w_br_gdn, w_br_xattn, w_out, norm_ffn, w_router_group, b_router_group, w_router_expert, b_router_expert, w_exp_gate, w_exp_up, w_exp_down, final_norm):
    depth = w_in.shape[0]
    assert depth == 1, "one decoder layer"
    outs = _layer(
        x_prompt, x_sample, mem_prompt, cache_mem_k[0], cache_mem_v[0],
        state_mlstm_C[0], state_mlstm_n[0], state_mlstm_m[0], state_gdn_S[0], state_gdn_conv[0],
        norm_mix[0], w_in[0], b_igate[0], b_fgate[0], mlstm_norm[0], conv_w[0], a_log[0],
        dt_bias[0], gdn_norm[0], mem_norm[0], w_mem_kv[0], w_br_mlstm[0], w_br_gdn[0],
        w_br_xattn[0], w_out[0], norm_ffn[0], w_router_group[0], b_router_group[0],
        w_router_expert[0], b_router_expert[0], w_exp_gate[0], w_exp_up[0], w_exp_down[0],
        final_norm, cfg=CONFIG)
    y_prompt, y_sample = outs[0], outs[1]
    return (y_prompt, y_sample) + tuple(o[None] for o in outs[2:])
```

```python
import functools

import jax
import jax.numpy as jnp
from jax import lax
from jax.experimental import pallas as pl
from jax.experimental.pallas import tpu as pltpu

F32 = jnp.float32
BF16 = jnp.bfloat16
EPS = 1e-6

N_HEADS = 4
HEAD_DIM = 128
BRANCH_W = N_HEADS * HEAD_DIM
CONV_W = 4
N_GROUPS = 4
EXPERTS_PER_GROUP = 4
N_EXPERTS = N_GROUPS * EXPERTS_PER_GROUP
N_PAIRS = 6
N_CLASSES = N_GROUPS * N_PAIRS
PAIR_LO = (0, 0, 0, 1, 1, 2)
PAIR_HI = (1, 2, 3, 2, 3, 3)
NEG_BIG = -1e30
LANES = 128
SUBLANES = 8
TOKEN_ROWS = 8
PROJ_GROUP = 3
VMEM_LIMIT = 48 * 1024 * 1024

COL_GATE = 0
COL_MQ, COL_MK, COL_MV, COL_MO = 6, 7, 8, 9
COL_GQ, COL_GK, COL_GV, COL_GO, COL_XQ = 10, 11, 12, 13, 14
N_PROJ = 15 * BRANCH_W


def _dot(a, b):
    return jnp.dot(a.astype(BF16), b.astype(BF16), preferred_element_type=F32)


def _dot_nt(a, b):
    return lax.dot_general(a.astype(BF16), b.astype(BF16), (((1,), (1,)), ((), ())),
                           preferred_element_type=F32)


def _dot_tn(a, b):
    return lax.dot_general(a.astype(BF16), b.astype(BF16), (((0,), (0,)), ((), ())),
                           preferred_element_type=F32)


def _dot_exact(a, b):
    return jnp.dot(a, b, preferred_element_type=F32, precision=lax.Precision.HIGHEST)


def _lanes_to_rows(x, lane0):
    r = lax.broadcasted_iota(jnp.int32, (SUBLANES, LANES), 0)
    c = lax.broadcasted_iota(jnp.int32, (SUBLANES, LANES), 1)
    sel = (c == r + lane0).astype(F32)
    return lax.dot_general(sel, x, (((1,), (1,)), ((), ())), preferred_element_type=F32,
                           precision=lax.Precision.HIGHEST)


def _rms(x, g):
    return x * lax.rsqrt(jnp.mean(x * x, axis=-1, keepdims=True) + EPS) * g


def _softplus(x):
    return jnp.maximum(x, 0.0) + jnp.log1p(jnp.exp(-jnp.abs(x)))


def _sigmoid(x):
    return 1.0 / (1.0 + jnp.exp(-x))


def _silu(x):
    return x * _sigmoid(x)


def _tri(n, strict=False):
    r = lax.broadcasted_iota(jnp.int32, (n, n), 0)
    c = lax.broadcasted_iota(jnp.int32, (n, n), 1)
    return (c < r) if strict else (c <= r)


def _tri_blocks(n, block, upper=False):
    shift = block.bit_length() - 1
    assert block == 1 << shift
    r = lax.broadcasted_iota(jnp.int32, (n, n), 0)
    c = lax.broadcasted_iota(jnp.int32, (n, n), 1)
    same = jnp.right_shift(r, shift) == jnp.right_shift(c, shift)
    return same & ((r <= c) if upper else (c <= r))


def _pad_rows(shape, t, seq_rows, n_valid):
    assert seq_rows & (seq_rows - 1) == 0
    row = lax.broadcasted_iota(jnp.int32, shape, 0)
    return (row & (seq_rows - 1)) + t * seq_rows >= n_valid


def _params(sem):
    return pltpu.CompilerParams(dimension_semantics=sem, vmem_limit_bytes=VMEM_LIMIT)


def _two_source_specs(n_first, block, width):
    first = pl.BlockSpec((block, width), lambda i, *_: (jnp.minimum(i, n_first - 1), 0))
    second = pl.BlockSpec((block, width), lambda i, *_: (jnp.maximum(i - n_first, 0), 0))
    return first, second


def _norm_proj_kernel(xa_ref, xb_ref, g_ref, w_ref, ws_ref, o_ref, os_ref, hb_ref, *, n_first):
    i = pl.program_id(0)
    j = pl.program_id(1)

    @pl.when(j == 0)
    def _():
        @pl.when(i < n_first)
        def _():
            hb_ref[...] = _rms(xa_ref[...], g_ref[...]).astype(BF16)

        @pl.when(i >= n_first)
        def _():
            hb_ref[...] = _rms(xb_ref[...], g_ref[...]).astype(BF16)

        os_ref[...] = jnp.dot(hb_ref[...], ws_ref[...], preferred_element_type=F32)

    o_ref[...] = jnp.dot(hb_ref[...], w_ref[0], preferred_element_type=F32)


def _proj_spec(rows, c, row_block):
    return pl.BlockSpec((None, rows, BRANCH_W),
                        lambda *g: (c // PROJ_GROUP, row_block(*g), c % PROJ_GROUP))


def _col_blocks(w, tn):
    d, n = w.shape
    return w.reshape(d, n // tn, tn).transpose(1, 0, 2)


def _norm_proj(xa, xb, g, w, ws, *, tm):
    d = xa.shape[1]
    n_first = xa.shape[0] // tm
    n_rows = xa.shape[0] + (0 if xb is None else xb.shape[0])
    if xb is None:
        xb = xa
    tn = w.shape[2]
    n = w.shape[0] * tn
    sa, sb = _two_source_specs(n_first, tm, d)
    return pl.pallas_call(
        functools.partial(_norm_proj_kernel, n_first=n_first),
        out_shape=(jax.ShapeDtypeStruct((n // tn, n_rows, tn), F32),
                   jax.ShapeDtypeStruct((n_rows, ws.shape[1]), F32)),
        grid=(n_rows // tm, n // tn),
        in_specs=[sa, sb,
                  pl.BlockSpec((1, d), lambda i, j: (0, 0)),
                  pl.BlockSpec((1, d, tn), lambda i, j: (j, 0, 0)),
                  pl.BlockSpec((d, ws.shape[1]), lambda i, j: (0, 0))],
        out_specs=(pl.BlockSpec((None, tm, tn), lambda i, j: (j, i, 0)),
                   pl.BlockSpec((tm, ws.shape[1]), lambda i, j: (i, 0))),
        scratch_shapes=[pltpu.VMEM((tm, d), BF16)],
        compiler_params=_params(("parallel", "arbitrary")),
        name="norm_proj",
    )(xa, xb, g, w, ws)


def _seq_tile_specs(G, Lt, nt, row_off):
    assert G == 1 or nt == 1

    def rows(b, t):
        return row_off + b * nt + t

    return rows


def _mlstm_kernel(*refs, L, n_sub, G, s_valid, s_total, has_init):
    if has_init:
        (q_ref, k_ref, v_ref, og_ref, gs_ref, bias_ref, norm_ref, c0_ref, n0_ref, m0_ref,
         h_ref, c_ref, n_ref, m_ref) = refs
    else:
        (q_ref, k_ref, v_ref, og_ref, gs_ref, bias_ref, norm_ref,
         h_ref, c_ref, n_ref, m_ref) = refs
    t = pl.program_id(1)
    Lt = n_sub * L
    rows = G * Lt

    @pl.when(t == 0)
    def _():
        if has_init:
            c_ref[...] = c0_ref[...]
            n_ref[...] = n0_ref[...]
            m_ref[...] = m0_ref[...]
        else:
            c_ref[...] = jnp.zeros_like(c_ref)
            n_ref[...] = jnp.zeros_like(n_ref)
            m_ref[...] = jnp.zeros_like(m_ref)

    z = gs_ref[...] + bias_ref[...]
    lane = lax.broadcasted_iota(jnp.int32, z.shape, 1)
    log_f = jnp.minimum(z, 0.0) - jnp.log1p(jnp.exp(-jnp.abs(z)))
    g = jnp.where((lane >= N_HEADS) & (lane < 2 * N_HEADS), log_f, z)
    if s_valid < s_total:
        pad = _pad_rows(z.shape, t, Lt, s_valid)
        g = jnp.where(pad, jnp.where(lane < N_HEADS, NEG_BIG, 0.0), g)
    g_t = _lanes_to_rows(g, 0)
    bt_cols = _dot_exact(_tri_blocks(rows, L).astype(F32), g)
    bt_rows = _dot_exact(g_t, _tri_blocks(rows, L, upper=True).astype(F32))
    causal = _tri(L)
    scale = HEAD_DIM ** -0.5
    probs = [(s, h) for s in range(G) for h in range(N_HEADS)]

    for c in range(n_sub):
        P = []
        for s, h in probs:
            r = slice(s * Lt + c * L, s * Lt + (c + 1) * L)
            sl = slice(h * HEAD_DIM, (h + 1) * HEAD_DIM)
            p = dict(s=s, h=h, r=r, sl=sl, q=q_ref[r, sl], k=k_ref[r, sl] * scale, v=v_ref[r, sl],
                     bt_c=bt_cols[r, N_HEADS + h:N_HEADS + h + 1], it_c=g[r, h:h + 1])
            bt_r = bt_rows[N_HEADS + h:N_HEADS + h + 1, r]
            it_r = g_t[h:h + 1, r]
            p["log_d"] = jnp.where(causal, p["bt_c"] - bt_r + it_r, -jnp.inf)
            p["c_prev"] = c_ref[s, h]
            p["n_prev"] = n_ref[s, h:h + 1, :]
            p["m_prev"] = m_ref[s, h:h + 1, 0:1]
            P.append(p)
        for p in P:
            p["qk"] = _dot_nt(p["q"], p["k"])
        for p in P:
            p["qc"] = _dot(p["q"], p["c_prev"])
        for p in P:
            inter = p["bt_c"] + p["m_prev"]
            p["m_t"] = jnp.maximum(inter, jnp.max(p["log_d"], axis=-1, keepdims=True))
            p["inter_w"] = jnp.exp(inter - p["m_t"])
            p["sd"] = p["qk"] * jnp.exp(p["log_d"] - p["m_t"])
        for p in P:
            p["sv"] = _dot(p["sd"], p["v"])
        for p in P:
            bt_last = p["bt_c"][L - 1:L, :]
            p["m_new"] = p["m_t"][L - 1:L, :]
            w = jnp.exp(bt_last - p["bt_c"] + p["it_c"] - p["m_new"])
            p["decay"] = jnp.exp(bt_last + p["m_prev"] - p["m_new"])
            p["kw"] = p["k"] * w
        for p in P:
            p["kv"] = _dot_tn(p["kw"], p["v"])
        for p in P:
            num = p["sv"] + p["qc"] * p["inter_w"]
            den = (jnp.sum(p["sd"], axis=-1, keepdims=True)
                   + p["inter_w"] * jnp.sum(p["q"] * p["n_prev"], axis=-1, keepdims=True))
            hh = num / jnp.maximum(jnp.abs(den), jnp.exp(-p["m_t"]))
            hn = hh * lax.rsqrt(jnp.mean(hh * hh, axis=-1, keepdims=True) + EPS) * norm_ref[:, p["sl"]]
            p["h_out"] = hn * _sigmoid(og_ref[p["r"], p["sl"]])
            p["n_new"] = p["decay"] * p["n_prev"] + jnp.sum(p["kw"], axis=0, keepdims=True)
        for p in P:
            s, h = p["s"], p["h"]
            h_ref[p["r"], p["sl"]] = p["h_out"]
            n_ref[s, h:h + 1, :] = p["n_new"]
            m_ref[s, h:h + 1, :] = jnp.broadcast_to(p["m_new"], (1, LANES))
            c_ref[s, h] = p["decay"] * p["c_prev"] + p["kv"]


def _mlstm(proj, gs, bias_row, norm_row, init, *, n_batch, seq, s_valid, row_off, L, n_sub, G):
    Lt = n_sub * L
    nt = seq // Lt
    rows = G * Lt
    has_init = init is not None
    blk = _seq_tile_specs(G, Lt, nt, row_off)

    def col(c):
        return _proj_spec(rows, c, blk)

    in_specs = [col(COL_MQ), col(COL_MK), col(COL_MV), col(COL_MO),
                pl.BlockSpec((rows, LANES), lambda b, t: (blk(b, t), 0)),
                pl.BlockSpec((1, LANES), lambda b, t: (0, 0)),
                pl.BlockSpec((1, BRANCH_W), lambda b, t: (0, 0))]
    args = [proj, proj, proj, proj, gs, bias_row, norm_row]
    c_spec = pl.BlockSpec((G, N_HEADS, HEAD_DIM, HEAD_DIM), lambda b, t: (b, 0, 0, 0))
    n_spec = pl.BlockSpec((G, N_HEADS, HEAD_DIM), lambda b, t: (b, 0, 0))
    m_spec = pl.BlockSpec((G, N_HEADS, LANES), lambda b, t: (b, 0, 0))
    if has_init:
        in_specs += [c_spec, n_spec, m_spec]
        args += list(init)
    return pl.pallas_call(
        functools.partial(_mlstm_kernel, L=L, n_sub=n_sub, G=G, s_valid=s_valid, s_total=seq,
                          has_init=has_init),
        out_shape=(jax.ShapeDtypeStruct((n_batch * seq, BRANCH_W), F32),
                   jax.ShapeDtypeStruct((n_batch, N_HEADS, HEAD_DIM, HEAD_DIM), F32),
                   jax.ShapeDtypeStruct((n_batch, N_HEADS, HEAD_DIM), F32),
                   jax.ShapeDtypeStruct((n_batch, N_HEADS, LANES), F32)),
        grid=(n_batch // G, nt),
        in_specs=in_specs,
        out_specs=(pl.BlockSpec((rows, BRANCH_W), lambda b, t: (b * nt + t, 0)),
                   c_spec, n_spec, m_spec),
        compiler_params=_params(("parallel", "arbitrary")),
        name="mlstm",
    )(*args)


def _gdn_kernel(*refs, L, n_sub, G, s_valid, s_total, has_init):
    if has_init:
        (q_ref, k_ref, v_ref, og_ref, gs_ref, cw_ref, arow_ref, dtrow_ref, norm_ref,
         conv0_ref, s0_ref, o_ref, s_ref, conv1_ref, ext_ref, conv_ref) = refs
    else:
        (q_ref, k_ref, v_ref, og_ref, gs_ref, cw_ref, arow_ref, dtrow_ref, norm_ref,
         o_ref, s_ref, conv1_ref, ext_ref, conv_ref) = refs
    t = pl.program_id(1)
    Lt = n_sub * L
    rows = G * Lt
    nt = s_total // Lt
    W = BRANCH_W

    @pl.when(t == 0)
    def _():
        if has_init:
            s_ref[...] = s0_ref[...]
            ext_ref[:, 0:SUBLANES, :] = conv0_ref[...]
        else:
            s_ref[...] = jnp.zeros_like(s_ref)
            ext_ref[:, 0:SUBLANES, :] = jnp.zeros((G, SUBLANES, 3 * W), F32)

    @pl.when(t > 0)
    def _():
        ext_ref[:, 0:SUBLANES, :] = ext_ref[:, Lt:Lt + SUBLANES, :]

    base = SUBLANES - (CONV_W - 1)
    for s in range(G):
        rs = slice(s * Lt, (s + 1) * Lt)
        ext_ref[s, SUBLANES:SUBLANES + Lt, 0:W] = q_ref[rs, :]
        ext_ref[s, SUBLANES:SUBLANES + Lt, W:2 * W] = k_ref[rs, :]
        ext_ref[s, SUBLANES:SUBLANES + Lt, 2 * W:3 * W] = v_ref[rs, :]
        acc = ext_ref[s, base:base + Lt, :] * cw_ref[0:1, :]
        for j in range(1, CONV_W):
            acc = acc + ext_ref[s, base + j:base + j + Lt, :] * cw_ref[j:j + 1, :]
        conv_ref[rs, :] = _silu(acc)

    @pl.when(t == nt - 1)
    def _():
        v_last = s_valid - (nt - 1) * Lt
        conv1_ref[...] = ext_ref[:, v_last:v_last + SUBLANES, :]

    z = gs_ref[...]
    lane = lax.broadcasted_iota(jnp.int32, z.shape, 1)
    gdec = arow_ref[...] * _softplus(z + dtrow_ref[...])
    gb = jnp.where((lane >= 2 * N_HEADS) & (lane < 3 * N_HEADS), gdec,
                   jnp.where((lane >= 3 * N_HEADS) & (lane < 4 * N_HEADS), _sigmoid(z), 0.0))
    if s_valid < s_total:
        gb = jnp.where(_pad_rows(z.shape, t, Lt, s_valid), 0.0, gb)
    gb_t = _lanes_to_rows(gb, 2 * N_HEADS)
    gam_cols = _dot_exact(_tri_blocks(rows, L).astype(F32), gb)
    gam_rows = _dot_exact(gb_t, _tri_blocks(rows, L, upper=True).astype(F32))
    incl = _tri(L)
    strict = _tri(L, strict=True)
    eye = (lax.broadcasted_iota(jnp.int32, (L, L), 0)
           == lax.broadcasted_iota(jnp.int32, (L, L), 1)).astype(F32)

    P = []
    for s in range(G):
        for c in range(n_sub):
            for h in range(N_HEADS):
                r = slice(s * Lt + c * L, s * Lt + (c + 1) * L)
                sl = slice(h * HEAD_DIM, (h + 1) * HEAD_DIM)
                q = conv_ref[r, h * HEAD_DIM:(h + 1) * HEAD_DIM]
                k = conv_ref[r, W + h * HEAD_DIM:W + (h + 1) * HEAD_DIM]
                v = conv_ref[r, 2 * W + h * HEAD_DIM:2 * W + (h + 1) * HEAD_DIM]
                q = q * lax.rsqrt(jnp.sum(q * q, axis=-1, keepdims=True) + EPS) * (HEAD_DIM ** -0.5)
                k = k * lax.rsqrt(jnp.sum(k * k, axis=-1, keepdims=True) + EPS)
                gam_c = gam_cols[r, 2 * N_HEADS + h:2 * N_HEADS + h + 1]
                gam_r = gam_rows[h:h + 1, r]
                beta_c = gb[r, 3 * N_HEADS + h:3 * N_HEADS + h + 1]
                decay = jnp.where(incl, jnp.exp(jnp.where(incl, gam_c - gam_r, 0.0)), 0.0)
                P.append(dict(s=s, c=c, h=h, r=r, sl=sl, q=q, k=k, v=v, gam_c=gam_c,
                              beta_c=beta_c, decay=decay))
    for p in P:
        p["kk"] = _dot_nt(p["k"], p["k"])
    for p in P:
        p["qk"] = _dot_nt(p["q"], p["k"]) * p["decay"]
    for p in P:
        e_gam = jnp.exp(p["gam_c"])
        gam_last = p["gam_c"][L - 1:L, :]
        p["pw"] = -jnp.where(strict, p["beta_c"] * p["kk"] * p["decay"], 0.0)
        p["x"] = eye + p["pw"]
        p["rhs"] = jnp.concatenate([p["beta_c"] * p["v"], p["beta_c"] * p["k"] * e_gam], axis=-1)
        p["q_dec"] = p["q"] * e_gam
        p["k_dec"] = p["k"] * jnp.exp(gam_last - p["gam_c"])
        p["e_last"] = jnp.exp(gam_last)
    n = 2
    while n < L:
        for p in P:
            p["pw"] = _dot(p["pw"], p["pw"])
        for p in P:
            p["x"] = p["x"] + _dot(p["x"], p["pw"])
        n *= 2
    for p in P:
        sol = _dot(p["x"], p["rhs"])
        p["u"] = sol[:, :HEAD_DIM]
        p["w"] = sol[:, HEAD_DIM:]

    for c in range(n_sub):
        Pc = [p for p in P if p["c"] == c]
        for p in Pc:
            p["s_prev"] = s_ref[p["s"], p["h"]]
            p["ws"] = _dot(p["w"], p["s_prev"])
        for p in Pc:
            p["qs"] = _dot(p["q_dec"], p["s_prev"])
        for p in Pc:
            p["v_new"] = p["u"] - p["ws"]
        for p in Pc:
            p["o"] = p["qs"] + _dot(p["qk"], p["v_new"])
        for p in Pc:
            p["kv"] = _dot_tn(p["k_dec"], p["v_new"])
        for p in Pc:
            o = p["o"]
            on = o * lax.rsqrt(jnp.mean(o * o, axis=-1, keepdims=True) + EPS) * norm_ref[...]
            p["o_out"] = on * _silu(og_ref[p["r"], p["sl"]])
        for p in Pc:
            s_ref[p["s"], p["h"]] = p["e_last"] * p["s_prev"] + p["kv"]
            o_ref[p["r"], p["sl"]] = p["o_out"]


def _gdn(proj, gs, conv_w, a_row, dt_row, norm_row, init, *, n_batch, seq, s_valid, row_off,
         L, n_sub, G):
    Lt = n_sub * L
    nt = seq // Lt
    rows = G * Lt
    has_init = init is not None
    blk = _seq_tile_specs(G, Lt, nt, row_off)

    def col(c):
        return _proj_spec(rows, c, blk)

    def const(shape):
        return pl.BlockSpec(shape, lambda b, t: (0,) * len(shape))

    in_specs = [col(COL_GQ), col(COL_GK), col(COL_GV), col(COL_GO),
                pl.BlockSpec((rows, LANES), lambda b, t: (blk(b, t), 0)),
                const((CONV_W, 3 * BRANCH_W)), const((1, LANES)), const((1, LANES)),
                const((1, HEAD_DIM))]
    args = [proj, proj, proj, proj, gs, conv_w, a_row, dt_row, norm_row]
    s_spec = pl.BlockSpec((G, N_HEADS, HEAD_DIM, HEAD_DIM), lambda b, t: (b, 0, 0, 0))
    conv_spec = pl.BlockSpec((G, SUBLANES, 3 * BRANCH_W), lambda b, t: (b, 0, 0))
    if has_init:
        in_specs += [conv_spec, s_spec]
        args += list(init)
    return pl.pallas_call(
        functools.partial(_gdn_kernel, L=L, n_sub=n_sub, G=G, s_valid=s_valid, s_total=seq,
                          has_init=has_init),
        out_shape=(jax.ShapeDtypeStruct((n_batch * seq, BRANCH_W), F32),
                   jax.ShapeDtypeStruct((n_batch, N_HEADS, HEAD_DIM, HEAD_DIM), F32),
                   jax.ShapeDtypeStruct((n_batch, SUBLANES, 3 * BRANCH_W), F32)),
        grid=(n_batch // G, nt),
        in_specs=in_specs,
        out_specs=(pl.BlockSpec((rows, BRANCH_W), lambda b, t: (b * nt + t, 0)),
                   s_spec, conv_spec),
        scratch_shapes=[pltpu.VMEM((G, Lt + SUBLANES, 3 * BRANCH_W), F32),
                        pltpu.VMEM((rows, 3 * BRANCH_W), F32)],
        compiler_params=_params(("parallel", "arbitrary")),
        name="gdn",
    )(*args)


def _xattn_kernel(q_ref, mk_ref, mv_ref, o_ref, *, G, lt, mem_len, head_rows):
    scale = HEAD_DIM ** -0.5

    def head(ref, s, h):
        if head_rows:
            return ref[pl.ds(s * mem_len * N_HEADS + h, mem_len, stride=N_HEADS), :]
        return ref[s * mem_len:(s + 1) * mem_len, h * HEAD_DIM:(h + 1) * HEAD_DIM]

    P = []
    for s in range(G):
        for h in range(N_HEADS):
            P.append(dict(s=s, h=h, r=slice(s * lt, (s + 1) * lt),
                          sl=slice(h * HEAD_DIM, (h + 1) * HEAD_DIM)))
    for p in P:
        p["sc"] = _dot_nt(q_ref[p["r"], p["sl"]], head(mk_ref, p["s"], p["h"])) * scale
    for p in P:
        e = jnp.exp(p["sc"] - jnp.max(p["sc"], axis=-1, keepdims=True))
        p["p"] = e / jnp.sum(e, axis=-1, keepdims=True)
    for p in P:
        o_ref[p["r"], p["sl"]] = _dot(p["p"], head(mv_ref, p["s"], p["h"]))


def _xattn(proj, mk, mv, *, n_batch, seq, row_off, lt, G, mem_len, kv_specs, head_rows):
    nt = seq // lt
    blk = _seq_tile_specs(G, lt, nt, row_off)
    return pl.pallas_call(
        functools.partial(_xattn_kernel, G=G, lt=lt, mem_len=mem_len, head_rows=head_rows),
        out_shape=jax.ShapeDtypeStruct((n_batch * seq, BRANCH_W), F32),
        grid=(n_batch // G, nt),
        in_specs=[_proj_spec(G * lt, COL_XQ, blk)] + list(kv_specs),
        out_specs=pl.BlockSpec((G * lt, BRANCH_W), lambda b, t: (b * nt + t, 0)),
        compiler_params=_params(("parallel", "arbitrary")),
        name="xattn",
    )(proj, mk, mv)


def _lane_first_max(x, mask, lane_f):
    xm = jnp.where(mask, x, -jnp.inf)
    mx = jnp.max(xm, axis=-1, keepdims=True)
    idx = jnp.min(jnp.where(xm == mx, lane_f, float(LANES)), axis=-1, keepdims=True)
    return mx, idx


def _store_token_tiles(ref, x, tok_rows):
    n = x.shape[0]
    for j in range(x.shape[1] // LANES):
        ref[pl.ds(j, n, stride=tok_rows), :] = x[:, j * LANES:(j + 1) * LANES]


def _load_token_tiles(ref, n, tok_rows, n_chunks=SUBLANES):
    return jnp.concatenate([ref[pl.ds(j, n, stride=tok_rows), :] for j in range(n_chunks)], axis=1)


def _merge_kernel(hap_ref, has_ref, hbp_ref, hbs_ref, hcp_ref, hcs_ref, xp_ref, xs_ref,
                  ga0_ref, ga1_ref, gb0_ref, gb1_ref, gc0_ref, gc1_ref,
                  wa_ref, wb_ref, wc_ref, wo_ref, nf_ref, wr_ref, br_ref,
                  x1_ref, route_ref, *, n_first):
    i = pl.program_id(0)

    def gate(lo_ref, hi_ref):
        return jnp.concatenate([_sigmoid(lo_ref[...]), _sigmoid(hi_ref[...])], axis=1)

    def body(ha_ref, hb_ref, hc_ref, x_ref):
        mixed = (gate(ga0_ref, ga1_ref) * _dot(ha_ref[...], wa_ref[...])
                 + gate(gb0_ref, gb1_ref) * _dot(hb_ref[...], wb_ref[...])
                 + gate(gc0_ref, gc1_ref) * _dot(hc_ref[...], wc_ref[...]))
        x1 = x_ref[...] + _dot(mixed, wo_ref[...])
        h2 = _rms(x1, nf_ref[...])
        logits = _dot(h2, wr_ref[...]) + br_ref[...]
        lane = lax.broadcasted_iota(jnp.int32, logits.shape, 1)
        lane_f = lane.astype(F32)
        is_g = lane < N_GROUPS
        _, g_idx = _lane_first_max(logits, is_g, lane_f)
        e_lo = N_GROUPS + EXPERTS_PER_GROUP * g_idx
        in_grp = (lane_f >= e_lo) & (lane_f < e_lo + EXPERTS_PER_GROUP)
        _, i1 = _lane_first_max(logits, in_grp, lane_f)
        _, i2 = _lane_first_max(logits, in_grp & (lane_f != i1), lane_f)
        lo = jnp.minimum(i1, i2) - e_lo
        hi = jnp.maximum(i1, i2) - e_lo
        pair = lo * (7.0 - lo) * 0.5 + (hi - lo - 1.0)
        cls = g_idx * float(N_PAIRS) + pair
        _store_token_tiles(x1_ref, x1, TOKEN_ROWS)
        route_ref[...] = jnp.where(lane == ROUTE_CLS, cls, 0.0)

    @pl.when(i < n_first)
    def _():
        body(hap_ref, hbp_ref, hcp_ref, xp_ref)

    @pl.when(i >= n_first)
    def _():
        body(has_ref, hbs_ref, hcs_ref, xs_ref)


def _merge(ha, hb, hc, xs, proj, wa, wb, wc, wo, nf, wr, br, *, tm):
    d = xs[0].shape[1]
    n_first = xs[0].shape[0] // tm
    n_rows = xs[0].shape[0] + xs[1].shape[0]
    specs = []
    for width in (BRANCH_W, BRANCH_W, BRANCH_W, d):
        specs += list(_two_source_specs(n_first, tm, width))

    def const(a):
        return pl.BlockSpec(a.shape, lambda i: (0,) * a.ndim)

    n_gate = 3 * d // BRANCH_W
    specs += [_proj_spec(tm, COL_GATE + c, lambda i: i) for c in range(n_gate)]
    weights = [wa, wb, wc, wo, nf, wr, br]
    specs += [const(a) for a in weights]
    return pl.pallas_call(
        functools.partial(_merge_kernel, n_first=n_first),
        out_shape=(jax.ShapeDtypeStruct((n_rows * TOKEN_ROWS, LANES), F32),
                   jax.ShapeDtypeStruct((n_rows, LANES), F32)),
        grid=(n_rows // tm,),
        in_specs=specs,
        out_specs=(pl.BlockSpec((tm * TOKEN_ROWS, LANES), lambda i: (i, 0)),
                   pl.BlockSpec((tm, LANES), lambda i: (i, 0))),
        compiler_params=_params(("parallel",)),
        name="merge",
    )(*ha, *hb, *hc, *xs, *([proj] * n_gate), *weights)


GATHER_UNROLL = 8
ROUTE_CLS = 0


def _record_gather_start(src_ref, dst_ref, sem, ids_ref, base, n, rec):
    def issue(r, carry):
        src = pl.multiple_of(ids_ref[base + r] * rec, rec)
        dst = pl.multiple_of(r * rec, rec)
        pltpu.make_async_copy(src_ref.at[pl.ds(src, rec)], dst_ref.at[pl.ds(dst, rec)], sem).start()
        return carry
    lax.fori_loop(0, n, issue, 0, unroll=GATHER_UNROLL)


def _record_gather_wait(src_ref, dst_ref, sem):
    pltpu.make_async_copy(src_ref.at[pl.ds(0, dst_ref.shape[0])], dst_ref, sem).wait()


def _sort_scatter_kernel(pos_ref, x_ref, init_hbm, out_hbm, sem, *, tm):
    del init_hbm
    base = pl.program_id(0) * tm

    def issue(r, carry):
        src = pl.multiple_of(r * TOKEN_ROWS, TOKEN_ROWS)
        dst = pl.multiple_of(pos_ref[base + r] * TOKEN_ROWS, TOKEN_ROWS)
        pltpu.make_async_copy(x_ref.at[pl.ds(src, TOKEN_ROWS)], out_hbm.at[pl.ds(dst, TOKEN_ROWS)],
                              sem).start()
        return carry
    lax.fori_loop(0, tm, issue, 0, unroll=GATHER_UNROLL)
    pltpu.make_async_copy(x_ref, out_hbm.at[pl.ds(0, tm * TOKEN_ROWS)], sem).wait()


def _sort_scatter(pos, x1rec, n_sorted, *, tm):
    n_tok = x1rec.shape[0] // TOKEN_ROWS
    grid_spec = pltpu.PrefetchScalarGridSpec(
        num_scalar_prefetch=1,
        grid=(n_tok // tm,),
        in_specs=[pl.BlockSpec((tm * TOKEN_ROWS, LANES), lambda i, *_: (i, 0)),
                  pl.BlockSpec(memory_space=pl.ANY)],
        out_specs=pl.BlockSpec(memory_space=pl.ANY),
        scratch_shapes=[pltpu.SemaphoreType.DMA(())],
    )
    return pl.pallas_call(
        functools.partial(_sort_scatter_kernel, tm=tm),
        out_shape=jax.ShapeDtypeStruct((n_sorted * TOKEN_ROWS, LANES), F32),
        grid_spec=grid_spec,
        input_output_aliases={2: 0},
        compiler_params=_params(("arbitrary",)),
        name="sort_scatter",
    )(pos, x1rec, jnp.zeros((n_sorted * TOKEN_ROWS, LANES), F32))


def _moe_kernel(ea_ref, eb_ref, nused_ref, x_ref, wga_ref, wua_ref, wda_ref,
                wgb_ref, wub_ref, wdb_ref, nf_ref, fin_ref, wr_ref, br_ref, y_ref, *, tm, d):
    i = pl.program_id(0)
    n_used = nused_ref[0]

    @pl.when(i < n_used)
    def _():
        x1 = _load_token_tiles(x_ref, tm, TOKEN_ROWS)
        h2 = _rms(x1, nf_ref[...]).astype(BF16)
        logits = jnp.dot(h2, wr_ref[...], preferred_element_type=F32) + br_ref[...]
        lane = lax.broadcasted_iota(jnp.int32, logits.shape, 1)

        def pick(l):
            return jnp.sum(jnp.where(lane == l, logits, 0.0), axis=-1, keepdims=True)

        ea, eb = ea_ref[i], eb_ref[i]
        is_g = lane < N_GROUPS
        g_max = jnp.max(jnp.where(is_g, logits, -jnp.inf), axis=-1, keepdims=True)
        g_den = jnp.sum(jnp.where(is_g, jnp.exp(logits - g_max), 0.0), axis=-1, keepdims=True)
        p_grp = jnp.exp(pick(ea // EXPERTS_PER_GROUP) - g_max) / g_den
        v_a, v_b = pick(N_GROUPS + ea), pick(N_GROUPS + eb)
        combs = (p_grp / (1.0 + jnp.exp(v_b - v_a)), p_grp / (1.0 + jnp.exp(v_a - v_b)))
        acc = jnp.zeros((tm, d), F32)
        for comb, wg_ref, wu_ref, wd_ref in ((combs[0], wga_ref, wua_ref, wda_ref),
                                             (combs[1], wgb_ref, wub_ref, wdb_ref)):
            a = jnp.dot(h2, wg_ref[0], preferred_element_type=F32)
            u = jnp.dot(h2, wu_ref[0], preferred_element_type=F32)
            act = _silu(a) * u * comb
            acc = acc + _dot(act, wd_ref[0])
        _store_token_tiles(y_ref, _rms(x1 + acc, fin_ref[...]), SUBLANES)

    @pl.when(i >= n_used)
    def _():
        y_ref[...] = jnp.zeros_like(y_ref)


def _moe(tile_ea, tile_eb, n_used, x_sorted, wg, wu, wd, nf, fin, wr, br, *, tm, n_tiles):
    d = nf.shape[1]
    f = wg.shape[2]
    assert d == SUBLANES * LANES

    def expert(shape, which):
        return pl.BlockSpec((1,) + shape, lambda i, ea, eb, nu: ((ea, eb)[which][i], 0, 0))

    def const(a):
        return pl.BlockSpec(a.shape, lambda i, *_: (0,) * a.ndim)

    grid_spec = pltpu.PrefetchScalarGridSpec(
        num_scalar_prefetch=3,
        grid=(n_tiles,),
        in_specs=[pl.BlockSpec((tm * TOKEN_ROWS, LANES), lambda i, *_: (i, 0)),
                  expert((d, f), 0), expert((d, f), 0), expert((f, d), 0),
                  expert((d, f), 1), expert((d, f), 1), expert((f, d), 1),
                  const(nf), const(fin), const(wr), const(br)],
        out_specs=pl.BlockSpec((tm * SUBLANES, LANES), lambda i, *_: (i, 0)),
    )
    return pl.pallas_call(
        functools.partial(_moe_kernel, tm=tm, d=d),
        out_shape=jax.ShapeDtypeStruct((n_tiles * tm * SUBLANES, LANES), F32),
        grid_spec=grid_spec,
        compiler_params=_params(("arbitrary",)),
        name="moe",
    )(tile_ea, tile_eb, n_used, x_sorted, wg, wu, wd, wg, wu, wd, nf, fin, wr, br)


def _unsort_kernel(pos_ref, y_hbm, o_ref, buf, sem, *, tm, base):
    i = pl.program_id(0)
    slot = i % 2

    @pl.when(i == 0)
    def _():
        _record_gather_start(y_hbm, buf.at[0], sem.at[0], pos_ref, base, tm, SUBLANES)

    @pl.when(i + 1 < pl.num_programs(0))
    def _():
        _record_gather_start(y_hbm, buf.at[1 - slot], sem.at[1 - slot], pos_ref,
                             base + (i + 1) * tm, tm, SUBLANES)

    _record_gather_wait(y_hbm, buf.at[slot], sem.at[slot])
    o_ref[...] = _load_token_tiles(buf.at[slot], tm, SUBLANES)


def _unsort(pos, y_rec, *, base, n_rows, tm):
    d = SUBLANES * LANES
    grid_spec = pltpu.PrefetchScalarGridSpec(
        num_scalar_prefetch=1,
        grid=(n_rows // tm,),
        in_specs=[pl.BlockSpec(memory_space=pl.ANY)],
        out_specs=pl.BlockSpec((tm, d), lambda i, *_: (i, 0)),
        scratch_shapes=[pltpu.VMEM((2, tm * SUBLANES, LANES), F32),
                        pltpu.SemaphoreType.DMA((2,))],
    )
    return pl.pallas_call(
        functools.partial(_unsort_kernel, tm=tm, base=base),
        out_shape=jax.ShapeDtypeStruct((n_rows, d), F32),
        grid_spec=grid_spec,
        compiler_params=_params(("arbitrary",)),
        name="unsort",
    )(pos, y_rec)


def _sort_plan(cls, *, tm, n_tiles):
    onehot = (cls[:, None] == jnp.arange(N_CLASSES, dtype=jnp.int32)[None, :]).astype(jnp.int32)
    csum = jnp.cumsum(onehot, axis=0)
    rank = jnp.sum((csum - onehot) * onehot, axis=1)
    counts = csum[-1]
    tiles_per = (counts + tm - 1) // tm
    tile_end = jnp.cumsum(tiles_per)
    tile_start = tile_end - tiles_per
    pos = (jnp.sum(tile_start[None, :] * onehot, axis=1) * tm + rank).astype(jnp.int32)
    n_used = tile_end[-1].astype(jnp.int32)
    tile = jnp.minimum(jnp.arange(n_tiles, dtype=jnp.int32), n_used - 1)
    tile_cls = jnp.sum((tile[:, None] >= tile_end[None, :]).astype(jnp.int32), axis=1)
    grp = tile_cls // N_PAIRS
    pair = tile_cls % N_PAIRS
    lo = jnp.asarray(PAIR_LO, jnp.int32)[pair]
    hi = jnp.asarray(PAIR_HI, jnp.int32)[pair]
    ea = (grp * EXPERTS_PER_GROUP + lo).astype(jnp.int32)
    eb = (grp * EXPERTS_PER_GROUP + hi).astype(jnp.int32)
    return pos, ea, eb, n_used.reshape(1)


def _lane_row(values, start):
    row = jnp.zeros((LANES,), F32)
    return row.at[start:start + values.shape[0]].set(values.astype(F32)).reshape(1, LANES)


def _layer(x_prompt, x_sample, mem_prompt, cache_mem_k, cache_mem_v,
           state_mlstm_C, state_mlstm_n, state_mlstm_m, state_gdn_S, state_gdn_conv,
           norm_mix, w_in, b_igate, b_fgate, mlstm_norm, conv_w, a_log, dt_bias, gdn_norm,
           mem_norm, w_mem_kv, w_br_mlstm, w_br_gdn, w_br_xattn, w_out, norm_ffn,
           w_router_group, b_router_group, w_router_expert, b_router_expert,
           w_exp_gate, w_exp_up, w_exp_down, final_norm, *, cfg):
    bp, sp, d = x_prompt.shape
    bs, ss, _ = x_sample.shape
    mem_len = mem_prompt.shape[1]
    ss_pad = SUBLANES
    assert ss <= ss_pad and d % LANES == 0
    n_p = bp * sp
    n_s = bs * ss_pad
    W = BRANCH_W
    gs_ = cfg["sample_group"]

    sizes = (W, W, W, N_HEADS, N_HEADS, W, 3 * W, N_HEADS, N_HEADS, W, W, d, d, d)
    offs = [0]
    for s in sizes:
        offs.append(offs[-1] + s)
    (m_q, m_k, m_v, m_i, m_f, m_o, g_qkv, g_a, g_b, g_out, x_q, gate_a, gate_b, gate_c) = [
        w_in[:, offs[j]:offs[j + 1]] for j in range(len(sizes))]
    w_big = _col_blocks(
        jnp.concatenate([gate_a, gate_b, gate_c, m_q, m_k, m_v, m_o, g_qkv, g_out, x_q],
                        axis=1).astype(BF16), PROJ_GROUP * W)
    w_small = jnp.concatenate(
        [m_i, m_f, g_a, g_b, jnp.zeros((d, LANES - 4 * N_HEADS), w_in.dtype)], axis=1).astype(BF16)
    w_router = jnp.concatenate(
        [w_router_group, w_router_expert,
         jnp.zeros((d, LANES - N_GROUPS - N_EXPERTS), w_in.dtype)], axis=1).astype(BF16)
    b_router = _lane_row(jnp.concatenate([b_router_group, b_router_expert]), 0)
    gate_bias = _lane_row(jnp.concatenate([b_igate, b_fgate]), 0)
    a_row = _lane_row(-jnp.exp(a_log.astype(F32)), 2 * N_HEADS)
    dt_row = _lane_row(dt_bias, 2 * N_HEADS)

    xp2 = x_prompt.reshape(n_p, d)
    xs2 = jnp.pad(x_sample, ((0, 0), (0, ss_pad - ss), (0, 0))).reshape(n_s, d)

    proj, gs = _norm_proj(xp2, xs2, norm_mix.reshape(1, d), w_big, w_small, tm=cfg["tm_proj"])

    mem2 = mem_prompt.reshape(bp * mem_len, d)
    mem_kv, _ = _norm_proj(mem2, None, mem_norm.reshape(1, d),
                           _col_blocks(w_mem_kv.astype(BF16), W),
                           jnp.zeros((d, LANES), BF16), tm=mem_len)

    mnorm = mlstm_norm.reshape(1, W)
    gnorm = gdn_norm.reshape(1, HEAD_DIM)
    s_off = n_p // (gs_ * ss_pad)
    ha_p, c_p, nn_p, mm_p = _mlstm(proj, gs, gate_bias, mnorm, None, n_batch=bp, seq=sp,
                                   s_valid=sp, row_off=0, L=cfg["chunk_m"], n_sub=cfg["sub_m"], G=1)
    m0 = jnp.broadcast_to(state_mlstm_m[:, :, None], (bs, N_HEADS, LANES))
    ha_s, c_s, nn_s, mm_s = _mlstm(proj, gs, gate_bias, mnorm,
                                   (state_mlstm_C, state_mlstm_n, m0), n_batch=bs, seq=ss_pad,
                                   s_valid=ss, row_off=s_off, L=ss_pad, n_sub=1, G=gs_)
    hb_p, s_p, cv_p = _gdn(proj, gs, conv_w, a_row, dt_row, gnorm, None, n_batch=bp, seq=sp,
                           s_valid=sp, row_off=0, L=cfg["chunk_g"], n_sub=cfg["sub_g"], G=1)
    conv0 = jnp.pad(state_gdn_conv, ((0, 0), (SUBLANES - (CONV_W - 1), 0), (0, 0)))
    hb_s, s_s, cv_s = _gdn(proj, gs, conv_w, a_row, dt_row, gnorm, (conv0, state_gdn_S),
                           n_batch=bs, seq=ss_pad, s_valid=ss, row_off=s_off, L=ss_pad,
                           n_sub=1, G=gs_)
    lt = min(sp, cfg["lt_x"])
    hc_p = _xattn(proj, mem_kv, mem_kv, n_batch=bp, seq=sp, row_off=0, lt=lt, G=1,
                  mem_len=mem_len, head_rows=False,
                  kv_specs=[pl.BlockSpec((None, mem_len, W), lambda b, t, c=c: (c, b, 0))
                            for c in range(2)])
    kv_rows = gs_ * mem_len * N_HEADS
    hc_s = _xattn(proj, cache_mem_k.reshape(bs * mem_len * N_HEADS, HEAD_DIM),
                  cache_mem_v.reshape(bs * mem_len * N_HEADS, HEAD_DIM),
                  n_batch=bs, seq=ss_pad, row_off=s_off, lt=ss_pad, G=gs_, mem_len=mem_len,
                  head_rows=True,
                  kv_specs=[pl.BlockSpec((kv_rows, HEAD_DIM), lambda b, t: (b, 0))] * 2)

    tm = cfg["tm"]
    x1rec, route = _merge((ha_p, ha_s), (hb_p, hb_s), (hc_p, hc_s), (xp2, xs2), proj,
                          w_br_mlstm.astype(BF16), w_br_gdn.astype(BF16), w_br_xattn.astype(BF16),
                          w_out.astype(BF16), norm_ffn.reshape(1, d), w_router, b_router, tm=tm)

    tm_moe = cfg["tm_moe"]
    n_all = n_p + n_s
    n_tiles = (n_all + N_CLASSES * (tm_moe - 1)) // tm_moe + 1
    cls = route[:, ROUTE_CLS].astype(jnp.int32)
    pos, ea, eb, n_used = _sort_plan(cls, tm=tm_moe, n_tiles=n_tiles)
    x_sorted = _sort_scatter(pos, x1rec, n_tiles * tm_moe, tm=tm)
    y_sorted = _moe(ea, eb, n_used, x_sorted, w_exp_gate.astype(BF16),
                    w_exp_up.astype(BF16), w_exp_down.astype(BF16), norm_ffn.reshape(1, d),
                    final_norm.reshape(1, d), w_router, b_router, tm=tm_moe, n_tiles=n_tiles)
    y_p = _unsort(pos, y_sorted, base=0, n_rows=n_p, tm=tm)
    y_s = _unsort(pos, y_sorted, base=n_p, n_rows=n_s, tm=tm)

    y_prompt = y_p.reshape(bp, sp, d)
    y_sample = y_s.reshape(bs, ss_pad, d)[:, :ss]
    mk_p = mem_kv[0].reshape(bp, mem_len, N_HEADS, HEAD_DIM)
    mv_p = mem_kv[1].reshape(bp, mem_len, N_HEADS, HEAD_DIM)
    tail = slice(SUBLANES - (CONV_W - 1), SUBLANES)
    return (y_prompt, y_sample, mk_p, mv_p,
            c_p, nn_p, mm_p[:, :, 0], s_p, cv_p[:, tail],
            c_s, nn_s, mm_s[:, :, 0], s_s, cv_s[:, tail])


CONFIG = dict(tm_proj=1024, tm=512, tm_moe=256, chunk_m=256, sub_m=1, chunk_g=64, sub_g=8,
              lt_x=512, sample_group=8)


def kernel(x_prompt, x_sample, mem_prompt, cache_mem_k, cache_mem_v, state_mlstm_C, state_mlstm_n, state_mlstm_m, state_gdn_S, state_gdn_conv, norm_mix, w_in, b_igate, b_fgate, mlstm_norm, conv_w, a_log, dt_bias, gdn_norm, mem_norm, w_mem_kv, w_br_mlstm, w_br_gdn, w_br_xattn, w_out, norm_ffn, w_router_group, b_router_group, w_router_expert, b_router_expert, w_exp_gate, w_exp_up, w_exp_down, final_norm):
    depth = w_in.shape[0]
    assert depth == 1, "one decoder layer"
    outs = _layer(
        x_prompt, x_sample, mem_prompt, cache_mem_k[0], cache_mem_v[0],
        state_mlstm_C[0], state_mlstm_n[0], state_mlstm_m[0], state_gdn_S[0], state_gdn_conv[0],
        norm_mix[0], w_in[0], b_igate[0], b_fgate[0], mlstm_norm[0], conv_w[0], a_log[0],
        dt_bias[0], gdn_norm[0], mem_norm[0], w_mem_kv[0], w_br_mlstm[0], w_br_gdn[0],
        w_br_xattn[0], w_out[0], norm_ffn[0], w_router_group[0], b_router_group[0],
        w_router_expert[0], b_router_expert[0], w_exp_gate[0], w_exp_up[0], w_exp_down[0],
        final_norm, cfg=CONFIG)
    y_prompt, y_sample = outs[0], outs[1]
    return (y_prompt, y_sample) + tuple(o[None] for o in outs[2:])
```

```python
import functools

import jax
import jax.numpy as jnp
from jax import lax
from jax.experimental import pallas as pl
from jax.experimental.pallas import tpu as pltpu

F32 = jnp.float32
BF16 = jnp.bfloat16
EPS = 1e-6

N_HEADS = 4
HEAD_DIM = 128
BRANCH_W = N_HEADS * HEAD_DIM
CONV_W = 4
N_GROUPS = 4
EXPERTS_PER_GROUP = 4
N_EXPERTS = N_GROUPS * EXPERTS_PER_GROUP
N_PAIRS = 6
N_CLASSES = N_GROUPS * N_PAIRS
PAIR_LO = (0, 0, 0, 1, 1, 2)
PAIR_HI = (1, 2, 3, 2, 3, 3)
NEG_BIG = -1e30
LANES = 128
SUBLANES = 8
TOKEN_ROWS = 8
PROJ_GROUP = 3
VMEM_LIMIT = 48 * 1024 * 1024

COL_GATE = 0
COL_MQ, COL_MK, COL_MV, COL_MO = 6, 7, 8, 9
COL_GQ, COL_GK, COL_GV, COL_GO, COL_XQ = 10, 11, 12, 13, 14
N_PROJ = 15 * BRANCH_W


def _dot(a, b):
    return jnp.dot(a.astype(BF16), b.astype(BF16), preferred_element_type=F32)


def _dot_nt(a, b):
    return lax.dot_general(a.astype(BF16), b.astype(BF16), (((1,), (1,)), ((), ())),
                           preferred_element_type=F32)


def _dot_tn(a, b):
    return lax.dot_general(a.astype(BF16), b.astype(BF16), (((0,), (0,)), ((), ())),
                           preferred_element_type=F32)


def _dot_exact(a, b):
    return jnp.dot(a, b, preferred_element_type=F32, precision=lax.Precision.HIGHEST)


def _lanes_to_rows(x, lane0):
    r = lax.broadcasted_iota(jnp.int32, (SUBLANES, LANES), 0)
    c = lax.broadcasted_iota(jnp.int32, (SUBLANES, LANES), 1)
    sel = (c == r + lane0).astype(F32)
    return lax.dot_general(sel, x, (((1,), (1,)), ((), ())), preferred_element_type=F32,
                           precision=lax.Precision.HIGHEST)


def _rms(x, g):
    return x * lax.rsqrt(jnp.mean(x * x, axis=-1, keepdims=True) + EPS) * g


def _softplus(x):
    return jnp.maximum(x, 0.0) + jnp.log1p(jnp.exp(-jnp.abs(x)))


def _sigmoid(x):
    return 0.5 * jnp.tanh(0.5 * x) + 0.5


def _silu(x):
    return x * _sigmoid(x)


def _tri(n, strict=False):
    r = lax.broadcasted_iota(jnp.int32, (n, n), 0)
    c = lax.broadcasted_iota(jnp.int32, (n, n), 1)
    return (c < r) if strict else (c <= r)


def _tri_blocks(n, block, upper=False):
    shift = block.bit_length() - 1
    assert block == 1 << shift
    r = lax.broadcasted_iota(jnp.int32, (n, n), 0)
    c = lax.broadcasted_iota(jnp.int32, (n, n), 1)
    same = jnp.right_shift(r, shift) == jnp.right_shift(c, shift)
    return same & ((r <= c) if upper else (c <= r))


def _pad_rows(shape, t, seq_rows, n_valid):
    assert seq_rows & (seq_rows - 1) == 0
    row = lax.broadcasted_iota(jnp.int32, shape, 0)
    return (row & (seq_rows - 1)) + t * seq_rows >= n_valid


def _params(sem):
    return pltpu.CompilerParams(dimension_semantics=sem, vmem_limit_bytes=VMEM_LIMIT)


def _two_source_specs(n_first, block, width):
    first = pl.BlockSpec((block, width), lambda i, *_: (jnp.minimum(i, n_first - 1), 0))
    second = pl.BlockSpec((block, width), lambda i, *_: (jnp.maximum(i - n_first, 0), 0))
    return first, second


def _norm_proj_kernel(xa_ref, xb_ref, g_ref, w_ref, ws_ref, o_ref, os_ref, hb_ref, *, n_first):
    i = pl.program_id(0)
    j = pl.program_id(1)

    @pl.when(j == 0)
    def _():
        @pl.when(i < n_first)
        def _():
            hb_ref[...] = _rms(xa_ref[...], g_ref[...]).astype(BF16)

        @pl.when(i >= n_first)
        def _():
            hb_ref[...] = _rms(xb_ref[...], g_ref[...]).astype(BF16)

        os_ref[...] = jnp.dot(hb_ref[...], ws_ref[...], preferred_element_type=F32)

    o_ref[...] = jnp.dot(hb_ref[...], w_ref[0], preferred_element_type=F32)


def _proj_spec(rows, c, row_block):
    return pl.BlockSpec((None, rows, BRANCH_W),
                        lambda *g: (c // PROJ_GROUP, row_block(*g), c % PROJ_GROUP))


def _col_blocks(w, tn):
    d, n = w.shape
    return w.reshape(d, n // tn, tn).transpose(1, 0, 2)


def _norm_proj(xa, xb, g, w, ws, *, tm):
    d = xa.shape[1]
    n_first = xa.shape[0] // tm
    n_rows = xa.shape[0] + (0 if xb is None else xb.shape[0])
    if xb is None:
        xb = xa
    tn = w.shape[2]
    n = w.shape[0] * tn
    sa, sb = _two_source_specs(n_first, tm, d)
    return pl.pallas_call(
        functools.partial(_norm_proj_kernel, n_first=n_first),
        out_shape=(jax.ShapeDtypeStruct((n // tn, n_rows, tn), F32),
                   jax.ShapeDtypeStruct((n_rows, ws.shape[1]), F32)),
        grid=(n_rows // tm, n // tn),
        in_specs=[sa, sb,
                  pl.BlockSpec((1, d), lambda i, j: (0, 0)),
                  pl.BlockSpec((1, d, tn), lambda i, j: (j, 0, 0)),
                  pl.BlockSpec((d, ws.shape[1]), lambda i, j: (0, 0))],
        out_specs=(pl.BlockSpec((None, tm, tn), lambda i, j: (j, i, 0)),
                   pl.BlockSpec((tm, ws.shape[1]), lambda i, j: (i, 0))),
        scratch_shapes=[pltpu.VMEM((tm, d), BF16)],
        compiler_params=_params(("parallel", "arbitrary")),
        name="norm_proj",
    )(xa, xb, g, w, ws)


def _seq_tile_specs(G, Lt, nt, row_off):
    assert G == 1 or nt == 1

    def rows(b, t):
        return row_off + b * nt + t

    return rows


def _mlstm_kernel(*refs, L, n_sub, G, s_valid, s_total, has_init):
    if has_init:
        (q_ref, k_ref, v_ref, og_ref, gs_ref, bias_ref, norm_ref, c0_ref, n0_ref, m0_ref,
         h_ref, c_ref, n_ref, m_ref) = refs
    else:
        (q_ref, k_ref, v_ref, og_ref, gs_ref, bias_ref, norm_ref,
         h_ref, c_ref, n_ref, m_ref) = refs
    t = pl.program_id(1)
    Lt = n_sub * L
    rows = G * Lt

    @pl.when(t == 0)
    def _():
        if has_init:
            c_ref[...] = c0_ref[...]
            n_ref[...] = n0_ref[...]
            m_ref[...] = m0_ref[...]
        else:
            c_ref[...] = jnp.zeros_like(c_ref)
            n_ref[...] = jnp.zeros_like(n_ref)
            m_ref[...] = jnp.zeros_like(m_ref)

    z = gs_ref[...] + bias_ref[...]
    lane = lax.broadcasted_iota(jnp.int32, z.shape, 1)
    log_f = jnp.minimum(z, 0.0) - jnp.log1p(jnp.exp(-jnp.abs(z)))
    g = jnp.where((lane >= N_HEADS) & (lane < 2 * N_HEADS), log_f, z)
    if s_valid < s_total:
        pad = _pad_rows(z.shape, t, Lt, s_valid)
        g = jnp.where(pad, jnp.where(lane < N_HEADS, NEG_BIG, 0.0), g)
    g_t = _lanes_to_rows(g, 0)
    bt_cols = _dot_exact(_tri_blocks(rows, L).astype(F32), g)
    bt_rows = _dot_exact(g_t, _tri_blocks(rows, L, upper=True).astype(F32))
    causal = _tri(L)
    scale = HEAD_DIM ** -0.5
    probs = [(s, h) for s in range(G) for h in range(N_HEADS)]

    for c in range(n_sub):
        P = []
        for s, h in probs:
            r = slice(s * Lt + c * L, s * Lt + (c + 1) * L)
            sl = slice(h * HEAD_DIM, (h + 1) * HEAD_DIM)
            p = dict(s=s, h=h, r=r, sl=sl, q=q_ref[r, sl], k=k_ref[r, sl] * scale, v=v_ref[r, sl],
                     bt_c=bt_cols[r, N_HEADS + h:N_HEADS + h + 1], it_c=g[r, h:h + 1])
            bt_r = bt_rows[N_HEADS + h:N_HEADS + h + 1, r]
            it_r = g_t[h:h + 1, r]
            p["log_d"] = jnp.where(causal, p["bt_c"] - bt_r + it_r, -jnp.inf)
            p["c_prev"] = c_ref[s, h]
            p["n_prev"] = n_ref[s, h:h + 1, :]
            p["m_prev"] = m_ref[s, h:h + 1, 0:1]
            P.append(p)
        for p in P:
            p["qk"] = _dot_nt(p["q"], p["k"])
        for p in P:
            p["qc"] = _dot(p["q"], p["c_prev"])
        for p in P:
            inter = p["bt_c"] + p["m_prev"]
            p["m_t"] = jnp.maximum(inter, jnp.max(p["log_d"], axis=-1, keepdims=True))
            p["inter_w"] = jnp.exp(inter - p["m_t"])
            p["sd"] = p["qk"] * jnp.exp(p["log_d"] - p["m_t"])
        for p in P:
            p["sv"] = _dot(p["sd"], p["v"])
        for p in P:
            bt_last = p["bt_c"][L - 1:L, :]
            p["m_new"] = p["m_t"][L - 1:L, :]
            w = jnp.exp(bt_last - p["bt_c"] + p["it_c"] - p["m_new"])
            p["decay"] = jnp.exp(bt_last + p["m_prev"] - p["m_new"])
            p["kw"] = p["k"] * w
        for p in P:
            p["kv"] = _dot_tn(p["kw"], p["v"])
        for p in P:
            num = p["sv"] + p["qc"] * p["inter_w"]
            den = (jnp.sum(p["sd"], axis=-1, keepdims=True)
                   + p["inter_w"] * jnp.sum(p["q"] * p["n_prev"], axis=-1, keepdims=True))
            hh = num / jnp.maximum(jnp.abs(den), jnp.exp(-p["m_t"]))
            hn = hh * lax.rsqrt(jnp.mean(hh * hh, axis=-1, keepdims=True) + EPS) * norm_ref[:, p["sl"]]
            p["h_out"] = hn * _sigmoid(og_ref[p["r"], p["sl"]])
            p["n_new"] = p["decay"] * p["n_prev"] + jnp.sum(p["kw"], axis=0, keepdims=True)
        for p in P:
            s, h = p["s"], p["h"]
            h_ref[p["r"], p["sl"]] = p["h_out"]
            n_ref[s, h:h + 1, :] = p["n_new"]
            m_ref[s, h:h + 1, :] = jnp.broadcast_to(p["m_new"], (1, LANES))
            c_ref[s, h] = p["decay"] * p["c_prev"] + p["kv"]


def _mlstm(proj, gs, bias_row, norm_row, init, *, n_batch, seq, s_valid, row_off, L, n_sub, G):
    Lt = n_sub * L
    nt = seq // Lt
    rows = G * Lt
    has_init = init is not None
    blk = _seq_tile_specs(G, Lt, nt, row_off)

    def col(c):
        return _proj_spec(rows, c, blk)

    in_specs = [col(COL_MQ), col(COL_MK), col(COL_MV), col(COL_MO),
                pl.BlockSpec((rows, LANES), lambda b, t: (blk(b, t), 0)),
                pl.BlockSpec((1, LANES), lambda b, t: (0, 0)),
                pl.BlockSpec((1, BRANCH_W), lambda b, t: (0, 0))]
    args = [proj, proj, proj, proj, gs, bias_row, norm_row]
    c_spec = pl.BlockSpec((G, N_HEADS, HEAD_DIM, HEAD_DIM), lambda b, t: (b, 0, 0, 0))
    n_spec = pl.BlockSpec((G, N_HEADS, HEAD_DIM), lambda b, t: (b, 0, 0))
    m_spec = pl.BlockSpec((G, N_HEADS, LANES), lambda b, t: (b, 0, 0))
    if has_init:
        in_specs += [c_spec, n_spec, m_spec]
        args += list(init)
    return pl.pallas_call(
        functools.partial(_mlstm_kernel, L=L, n_sub=n_sub, G=G, s_valid=s_valid, s_total=seq,
                          has_init=has_init),
        out_shape=(jax.ShapeDtypeStruct((n_batch * seq, BRANCH_W), F32),
                   jax.ShapeDtypeStruct((n_batch, N_HEADS, HEAD_DIM, HEAD_DIM), F32),
                   jax.ShapeDtypeStruct((n_batch, N_HEADS, HEAD_DIM), F32),
                   jax.ShapeDtypeStruct((n_batch, N_HEADS, LANES), F32)),
        grid=(n_batch // G, nt),
        in_specs=in_specs,
        out_specs=(pl.BlockSpec((rows, BRANCH_W), lambda b, t: (b * nt + t, 0)),
                   c_spec, n_spec, m_spec),
        compiler_params=_params(("parallel", "arbitrary")),
        name="mlstm",
    )(*args)


def _gdn_kernel(*refs, L, n_sub, G, s_valid, s_total, has_init):
    if has_init:
        (q_ref, k_ref, v_ref, og_ref, gs_ref, cw_ref, arow_ref, dtrow_ref, norm_ref,
         conv0_ref, s0_ref, o_ref, s_ref, conv1_ref, ext_ref, conv_ref) = refs
    else:
        (q_ref, k_ref, v_ref, og_ref, gs_ref, cw_ref, arow_ref, dtrow_ref, norm_ref,
         o_ref, s_ref, conv1_ref, ext_ref, conv_ref) = refs
    t = pl.program_id(1)
    Lt = n_sub * L
    rows = G * Lt
    nt = s_total // Lt
    W = BRANCH_W

    @pl.when(t == 0)
    def _():
        if has_init:
            s_ref[...] = s0_ref[...]
            ext_ref[:, 0:SUBLANES, :] = conv0_ref[...]
        else:
            s_ref[...] = jnp.zeros_like(s_ref)
            ext_ref[:, 0:SUBLANES, :] = jnp.zeros((G, SUBLANES, 3 * W), F32)

    @pl.when(t > 0)
    def _():
        ext_ref[:, 0:SUBLANES, :] = ext_ref[:, Lt:Lt + SUBLANES, :]

    base = SUBLANES - (CONV_W - 1)
    for s in range(G):
        rs = slice(s * Lt, (s + 1) * Lt)
        ext_ref[s, SUBLANES:SUBLANES + Lt, 0:W] = q_ref[rs, :]
        ext_ref[s, SUBLANES:SUBLANES + Lt, W:2 * W] = k_ref[rs, :]
        ext_ref[s, SUBLANES:SUBLANES + Lt, 2 * W:3 * W] = v_ref[rs, :]
        acc = ext_ref[s, base:base + Lt, :] * cw_ref[0:1, :]
        for j in range(1, CONV_W):
            acc = acc + ext_ref[s, base + j:base + j + Lt, :] * cw_ref[j:j + 1, :]
        conv_ref[rs, :] = _silu(acc)

    @pl.when(t == nt - 1)
    def _():
        v_last = s_valid - (nt - 1) * Lt
        conv1_ref[...] = ext_ref[:, v_last:v_last + SUBLANES, :]

    z = gs_ref[...]
    lane = lax.broadcasted_iota(jnp.int32, z.shape, 1)
    gdec = arow_ref[...] * _softplus(z + dtrow_ref[...])
    gb = jnp.where((lane >= 2 * N_HEADS) & (lane < 3 * N_HEADS), gdec,
                   jnp.where((lane >= 3 * N_HEADS) & (lane < 4 * N_HEADS), _sigmoid(z), 0.0))
    if s_valid < s_total:
        gb = jnp.where(_pad_rows(z.shape, t, Lt, s_valid), 0.0, gb)
    gb_t = _lanes_to_rows(gb, 2 * N_HEADS)
    gam_cols = _dot_exact(_tri_blocks(rows, L).astype(F32), gb)
    gam_rows = _dot_exact(gb_t, _tri_blocks(rows, L, upper=True).astype(F32))
    incl = _tri(L)
    strict = _tri(L, strict=True)
    eye = (lax.broadcasted_iota(jnp.int32, (L, L), 0)
           == lax.broadcasted_iota(jnp.int32, (L, L), 1)).astype(F32)

    P = []
    for s in range(G):
        for c in range(n_sub):
            for h in range(N_HEADS):
                r = slice(s * Lt + c * L, s * Lt + (c + 1) * L)
                sl = slice(h * HEAD_DIM, (h + 1) * HEAD_DIM)
                q = conv_ref[r, h * HEAD_DIM:(h + 1) * HEAD_DIM]
                k = conv_ref[r, W + h * HEAD_DIM:W + (h + 1) * HEAD_DIM]
                v = conv_ref[r, 2 * W + h * HEAD_DIM:2 * W + (h + 1) * HEAD_DIM]
                q = q * lax.rsqrt(jnp.sum(q * q, axis=-1, keepdims=True) + EPS) * (HEAD_DIM ** -0.5)
                k = k * lax.rsqrt(jnp.sum(k * k, axis=-1, keepdims=True) + EPS)
                gam_c = gam_cols[r, 2 * N_HEADS + h:2 * N_HEADS + h + 1]
                gam_r = gam_rows[h:h + 1, r]
                beta_c = gb[r, 3 * N_HEADS + h:3 * N_HEADS + h + 1]
                decay = jnp.exp(jnp.where(incl, gam_c - gam_r, -jnp.inf))
                P.append(dict(s=s, c=c, h=h, r=r, sl=sl, q=q, k=k, v=v, gam_c=gam_c,
                              beta_c=beta_c, decay=decay))
    for p in P:
        p["kk"] = _dot_nt(p["k"], p["k"])
    for p in P:
        p["qk"] = _dot_nt(p["q"], p["k"]) * p["decay"]
    for p in P:
        e_gam = jnp.exp(p["gam_c"])
        gam_last = p["gam_c"][L - 1:L, :]
        p["pw"] = -jnp.where(strict, p["beta_c"] * p["kk"] * p["decay"], 0.0)
        p["x"] = eye + p["pw"]
        p["rhs"] = jnp.concatenate([p["beta_c"] * p["v"], p["beta_c"] * p["k"] * e_gam], axis=-1)
        p["q_dec"] = p["q"] * e_gam
        p["k_dec"] = p["k"] * jnp.exp(gam_last - p["gam_c"])
        p["e_last"] = jnp.exp(gam_last)
    n = 2
    while n < L:
        for p in P:
            p["pw"] = _dot(p["pw"], p["pw"])
        for p in P:
            p["x"] = p["x"] + _dot(p["x"], p["pw"])
        n *= 2
    for p in P:
        sol = _dot(p["x"], p["rhs"])
        p["u"] = sol[:, :HEAD_DIM]
        p["w"] = sol[:, HEAD_DIM:]

    for c in range(n_sub):
        Pc = [p for p in P if p["c"] == c]
        for p in Pc:
            p["s_prev"] = s_ref[p["s"], p["h"]]
            p["ws"] = _dot(p["w"], p["s_prev"])
        for p in Pc:
            p["qs"] = _dot(p["q_dec"], p["s_prev"])
        for p in Pc:
            p["v_new"] = p["u"] - p["ws"]
        for p in Pc:
            p["o"] = p["qs"] + _dot(p["qk"], p["v_new"])
        for p in Pc:
            p["kv"] = _dot_tn(p["k_dec"], p["v_new"])
        for p in Pc:
            o = p["o"]
            on = o * lax.rsqrt(jnp.mean(o * o, axis=-1, keepdims=True) + EPS) * norm_ref[...]
            p["o_out"] = on * _silu(og_ref[p["r"], p["sl"]])
        for p in Pc:
            s_ref[p["s"], p["h"]] = p["e_last"] * p["s_prev"] + p["kv"]
            o_ref[p["r"], p["sl"]] = p["o_out"]


def _gdn(proj, gs, conv_w, a_row, dt_row, norm_row, init, *, n_batch, seq, s_valid, row_off,
         L, n_sub, G):
    Lt = n_sub * L
    nt = seq // Lt
    rows = G * Lt
    has_init = init is not None
    blk = _seq_tile_specs(G, Lt, nt, row_off)

    def col(c):
        return _proj_spec(rows, c, blk)

    def const(shape):
        return pl.BlockSpec(shape, lambda b, t: (0,) * len(shape))

    in_specs = [col(COL_GQ), col(COL_GK), col(COL_GV), col(COL_GO),
                pl.BlockSpec((rows, LANES), lambda b, t: (blk(b, t), 0)),
                const((CONV_W, 3 * BRANCH_W)), const((1, LANES)), const((1, LANES)),
                const((1, HEAD_DIM))]
    args = [proj, proj, proj, proj, gs, conv_w, a_row, dt_row, norm_row]
    s_spec = pl.BlockSpec((G, N_HEADS, HEAD_DIM, HEAD_DIM), lambda b, t: (b, 0, 0, 0))
    conv_spec = pl.BlockSpec((G, SUBLANES, 3 * BRANCH_W), lambda b, t: (b, 0, 0))
    if has_init:
        in_specs += [conv_spec, s_spec]
        args += list(init)
    return pl.pallas_call(
        functools.partial(_gdn_kernel, L=L, n_sub=n_sub, G=G, s_valid=s_valid, s_total=seq,
                          has_init=has_init),
        out_shape=(jax.ShapeDtypeStruct((n_batch * seq, BRANCH_W), F32),
                   jax.ShapeDtypeStruct((n_batch, N_HEADS, HEAD_DIM, HEAD_DIM), F32),
                   jax.ShapeDtypeStruct((n_batch, SUBLANES, 3 * BRANCH_W), F32)),
        grid=(n_batch // G, nt),
        in_specs=in_specs,
        out_specs=(pl.BlockSpec((rows, BRANCH_W), lambda b, t: (b * nt + t, 0)),
                   s_spec, conv_spec),
        scratch_shapes=[pltpu.VMEM((G, Lt + SUBLANES, 3 * BRANCH_W), F32),
                        pltpu.VMEM((rows, 3 * BRANCH_W), F32)],
        compiler_params=_params(("parallel", "arbitrary")),
        name="gdn",
    )(*args)


def _xattn_kernel(q_ref, mk_ref, mv_ref, o_ref, *, G, lt, mem_len, head_rows):
    scale = HEAD_DIM ** -0.5

    def head(ref, s, h):
        if head_rows:
            return ref[pl.ds(s * mem_len * N_HEADS + h, mem_len, stride=N_HEADS), :]
        return ref[s * mem_len:(s + 1) * mem_len, h * HEAD_DIM:(h + 1) * HEAD_DIM]

    P = []
    for s in range(G):
        for h in range(N_HEADS):
            P.append(dict(s=s, h=h, r=slice(s * lt, (s + 1) * lt),
                          sl=slice(h * HEAD_DIM, (h + 1) * HEAD_DIM)))
    for p in P:
        p["sc"] = _dot_nt(q_ref[p["r"], p["sl"]], head(mk_ref, p["s"], p["h"])) * scale
    for p in P:
        e = jnp.exp(p["sc"] - jnp.max(p["sc"], axis=-1, keepdims=True))
        p["p"] = e / jnp.sum(e, axis=-1, keepdims=True)
    for p in P:
        o_ref[p["r"], p["sl"]] = _dot(p["p"], head(mv_ref, p["s"], p["h"]))


def _xattn(proj, mk, mv, *, n_batch, seq, row_off, lt, G, mem_len, kv_specs, head_rows):
    nt = seq // lt
    blk = _seq_tile_specs(G, lt, nt, row_off)
    return pl.pallas_call(
        functools.partial(_xattn_kernel, G=G, lt=lt, mem_len=mem_len, head_rows=head_rows),
        out_shape=jax.ShapeDtypeStruct((n_batch * seq, BRANCH_W), F32),
        grid=(n_batch // G, nt),
        in_specs=[_proj_spec(G * lt, COL_XQ, blk)] + list(kv_specs),
        out_specs=pl.BlockSpec((G * lt, BRANCH_W), lambda b, t: (b * nt + t, 0)),
        compiler_params=_params(("parallel", "arbitrary")),
        name="xattn",
    )(proj, mk, mv)


def _lane_first_max(x, mask, lane_f):
    xm = jnp.where(mask, x, -jnp.inf)
    mx = jnp.max(xm, axis=-1, keepdims=True)
    idx = jnp.min(jnp.where(xm == mx, lane_f, float(LANES)), axis=-1, keepdims=True)
    return mx, idx


def _store_token_tiles(ref, x, tok_rows):
    n = x.shape[0]
    for j in range(x.shape[1] // LANES):
        ref[pl.ds(j, n, stride=tok_rows), :] = x[:, j * LANES:(j + 1) * LANES]


def _load_token_tiles(ref, n, tok_rows, n_chunks=SUBLANES):
    return jnp.concatenate([ref[pl.ds(j, n, stride=tok_rows), :] for j in range(n_chunks)], axis=1)


def _merge_kernel(hap_ref, has_ref, hbp_ref, hbs_ref, hcp_ref, hcs_ref, xp_ref, xs_ref,
                  ga0_ref, ga1_ref, gb0_ref, gb1_ref, gc0_ref, gc1_ref,
                  wa_ref, wb_ref, wc_ref, wo_ref, nf_ref, wr_ref, br_ref,
                  x1_ref, route_ref, *, n_first):
    i = pl.program_id(0)

    def gate(lo_ref, hi_ref):
        return jnp.concatenate([_sigmoid(lo_ref[...]), _sigmoid(hi_ref[...])], axis=1)

    def body(ha_ref, hb_ref, hc_ref, x_ref):
        mixed = (gate(ga0_ref, ga1_ref) * _dot(ha_ref[...], wa_ref[...])
                 + gate(gb0_ref, gb1_ref) * _dot(hb_ref[...], wb_ref[...])
                 + gate(gc0_ref, gc1_ref) * _dot(hc_ref[...], wc_ref[...]))
        x1 = x_ref[...] + _dot(mixed, wo_ref[...])
        h2 = _rms(x1, nf_ref[...])
        logits = _dot(h2, wr_ref[...]) + br_ref[...]
        lane = lax.broadcasted_iota(jnp.int32, logits.shape, 1)
        lane_f = lane.astype(F32)
        is_g = lane < N_GROUPS
        _, g_idx = _lane_first_max(logits, is_g, lane_f)
        e_lo = N_GROUPS + EXPERTS_PER_GROUP * g_idx
        in_grp = (lane_f >= e_lo) & (lane_f < e_lo + EXPERTS_PER_GROUP)
        _, i1 = _lane_first_max(logits, in_grp, lane_f)
        _, i2 = _lane_first_max(logits, in_grp & (lane_f != i1), lane_f)
        lo = jnp.minimum(i1, i2) - e_lo
        hi = jnp.maximum(i1, i2) - e_lo
        pair = lo * (7.0 - lo) * 0.5 + (hi - lo - 1.0)
        cls = g_idx * float(N_PAIRS) + pair
        _store_token_tiles(x1_ref, x1, TOKEN_ROWS)
        route_ref[...] = jnp.where(lane == ROUTE_CLS, cls, 0.0)

    @pl.when(i < n_first)
    def _():
        body(hap_ref, hbp_ref, hcp_ref, xp_ref)

    @pl.when(i >= n_first)
    def _():
        body(has_ref, hbs_ref, hcs_ref, xs_ref)


def _merge(ha, hb, hc, xs, proj, wa, wb, wc, wo, nf, wr, br, *, tm):
    d = xs[0].shape[1]
    n_first = xs[0].shape[0] // tm
    n_rows = xs[0].shape[0] + xs[1].shape[0]
    specs = []
    for width in (BRANCH_W, BRANCH_W, BRANCH_W, d):
        specs += list(_two_source_specs(n_first, tm, width))

    def const(a):
        return pl.BlockSpec(a.shape, lambda i: (0,) * a.ndim)

    n_gate = 3 * d // BRANCH_W
    specs += [_proj_spec(tm, COL_GATE + c, lambda i: i) for c in range(n_gate)]
    weights = [wa, wb, wc, wo, nf, wr, br]
    specs += [const(a) for a in weights]
    return pl.pallas_call(
        functools.partial(_merge_kernel, n_first=n_first),
        out_shape=(jax.ShapeDtypeStruct((n_rows * TOKEN_ROWS, LANES), F32),
                   jax.ShapeDtypeStruct((n_rows, LANES), F32)),
        grid=(n_rows // tm,),
        in_specs=specs,
        out_specs=(pl.BlockSpec((tm * TOKEN_ROWS, LANES), lambda i: (i, 0)),
                   pl.BlockSpec((tm, LANES), lambda i: (i, 0))),
        compiler_params=_params(("parallel",)),
        name="merge",
    )(*ha, *hb, *hc, *xs, *([proj] * n_gate), *weights)


GATHER_UNROLL = 8
ROUTE_CLS = 0


def _record_gather_start(src_ref, dst_ref, sem, ids_ref, base, n, rec):
    def issue(r, carry):
        src = pl.multiple_of(ids_ref[base + r] * rec, rec)
        dst = pl.multiple_of(r * rec, rec)
        pltpu.make_async_copy(src_ref.at[pl.ds(src, rec)], dst_ref.at[pl.ds(dst, rec)], sem).start()
        return carry
    lax.fori_loop(0, n, issue, 0, unroll=GATHER_UNROLL)


def _record_gather_wait(src_ref, dst_ref, sem):
    pltpu.make_async_copy(src_ref.at[pl.ds(0, dst_ref.shape[0])], dst_ref, sem).wait()


def _sort_scatter_kernel(pos_ref, x_ref, init_hbm, out_hbm, sem, *, tm):
    del init_hbm
    base = pl.program_id(0) * tm

    def issue(r, carry):
        src = pl.multiple_of(r * TOKEN_ROWS, TOKEN_ROWS)
        dst = pl.multiple_of(pos_ref[base + r] * TOKEN_ROWS, TOKEN_ROWS)
        pltpu.make_async_copy(x_ref.at[pl.ds(src, TOKEN_ROWS)], out_hbm.at[pl.ds(dst, TOKEN_ROWS)],
                              sem).start()
        return carry
    lax.fori_loop(0, tm, issue, 0, unroll=GATHER_UNROLL)
    pltpu.make_async_copy(x_ref, out_hbm.at[pl.ds(0, tm * TOKEN_ROWS)], sem).wait()


def _sort_scatter(pos, x1rec, n_sorted, *, tm):
    n_tok = x1rec.shape[0] // TOKEN_ROWS
    grid_spec = pltpu.PrefetchScalarGridSpec(
        num_scalar_prefetch=1,
        grid=(n_tok // tm,),
        in_specs=[pl.BlockSpec((tm * TOKEN_ROWS, LANES), lambda i, *_: (i, 0)),
                  pl.BlockSpec(memory_space=pl.ANY)],
        out_specs=pl.BlockSpec(memory_space=pl.ANY),
        scratch_shapes=[pltpu.SemaphoreType.DMA(())],
    )
    return pl.pallas_call(
        functools.partial(_sort_scatter_kernel, tm=tm),
        out_shape=jax.ShapeDtypeStruct((n_sorted * TOKEN_ROWS, LANES), F32),
        grid_spec=grid_spec,
        input_output_aliases={2: 0},
        compiler_params=_params(("arbitrary",)),
        name="sort_scatter",
    )(pos, x1rec, jnp.zeros((n_sorted * TOKEN_ROWS, LANES), F32))


def _moe_kernel(ea_ref, eb_ref, nused_ref, x_ref, wga_ref, wua_ref, wda_ref,
                wgb_ref, wub_ref, wdb_ref, nf_ref, fin_ref, wr_ref, br_ref, y_ref, *, tm, d):
    i = pl.program_id(0)
    n_used = nused_ref[0]

    @pl.when(i < n_used)
    def _():
        x1 = _load_token_tiles(x_ref, tm, TOKEN_ROWS)
        h2 = _rms(x1, nf_ref[...]).astype(BF16)
        logits = jnp.dot(h2, wr_ref[...], preferred_element_type=F32) + br_ref[...]
        lane = lax.broadcasted_iota(jnp.int32, logits.shape, 1)

        def pick(l):
            return jnp.sum(jnp.where(lane == l, logits, 0.0), axis=-1, keepdims=True)

        ea, eb = ea_ref[i], eb_ref[i]
        is_g = lane < N_GROUPS
        g_max = jnp.max(jnp.where(is_g, logits, -jnp.inf), axis=-1, keepdims=True)
        g_den = jnp.sum(jnp.where(is_g, jnp.exp(logits - g_max), 0.0), axis=-1, keepdims=True)
        p_grp = jnp.exp(pick(ea // EXPERTS_PER_GROUP) - g_max) / g_den
        v_a, v_b = pick(N_GROUPS + ea), pick(N_GROUPS + eb)
        combs = (p_grp / (1.0 + jnp.exp(v_b - v_a)), p_grp / (1.0 + jnp.exp(v_a - v_b)))
        acc = jnp.zeros((tm, d), F32)
        for comb, wg_ref, wu_ref, wd_ref in ((combs[0], wga_ref, wua_ref, wda_ref),
                                             (combs[1], wgb_ref, wub_ref, wdb_ref)):
            a = jnp.dot(h2, wg_ref[0], preferred_element_type=F32)
            u = jnp.dot(h2, wu_ref[0], preferred_element_type=F32)
            act = _silu(a) * u * comb
            acc = acc + _dot(act, wd_ref[0])
        _store_token_tiles(y_ref, _rms(x1 + acc, fin_ref[...]), SUBLANES)

    @pl.when(i >= n_used)
    def _():
        y_ref[...] = jnp.zeros_like(y_ref)


def _moe(tile_ea, tile_eb, n_used, x_sorted, wg, wu, wd, nf, fin, wr, br, *, tm, n_tiles):
    d = nf.shape[1]
    f = wg.shape[2]
    assert d == SUBLANES * LANES

    def expert(shape, which):
        return pl.BlockSpec((1,) + shape, lambda i, ea, eb, nu: ((ea, eb)[which][i], 0, 0))

    def const(a):
        return pl.BlockSpec(a.shape, lambda i, *_: (0,) * a.ndim)

    grid_spec = pltpu.PrefetchScalarGridSpec(
        num_scalar_prefetch=3,
        grid=(n_tiles,),
        in_specs=[pl.BlockSpec((tm * TOKEN_ROWS, LANES), lambda i, *_: (i, 0)),
                  expert((d, f), 0), expert((d, f), 0), expert((f, d), 0),
                  expert((d, f), 1), expert((d, f), 1), expert((f, d), 1),
                  const(nf), const(fin), const(wr), const(br)],
        out_specs=pl.BlockSpec((tm * SUBLANES, LANES), lambda i, *_: (i, 0)),
    )
    return pl.pallas_call(
        functools.partial(_moe_kernel, tm=tm, d=d),
        out_shape=jax.ShapeDtypeStruct((n_tiles * tm * SUBLANES, LANES), F32),
        grid_spec=grid_spec,
        compiler_params=_params(("arbitrary",)),
        name="moe",
    )(tile_ea, tile_eb, n_used, x_sorted, wg, wu, wd, wg, wu, wd, nf, fin, wr, br)


def _unsort_kernel(pos_ref, y_hbm, o_ref, buf, sem, *, tm, base):
    i = pl.program_id(0)
    slot = i % 2

    @pl.when(i == 0)
    def _():
        _record_gather_start(y_hbm, buf.at[0], sem.at[0], pos_ref, base, tm, SUBLANES)

    @pl.when(i + 1 < pl.num_programs(0))
    def _():
        _record_gather_start(y_hbm, buf.at[1 - slot], sem.at[1 - slot], pos_ref,
                             base + (i + 1) * tm, tm, SUBLANES)

    _record_gather_wait(y_hbm, buf.at[slot], sem.at[slot])
    o_ref[...] = _load_token_tiles(buf.at[slot], tm, SUBLANES)


def _unsort(pos, y_rec, *, base, n_rows, tm):
    d = SUBLANES * LANES
    grid_spec = pltpu.PrefetchScalarGridSpec(
        num_scalar_prefetch=1,
        grid=(n_rows // tm,),
        in_specs=[pl.BlockSpec(memory_space=pl.ANY)],
        out_specs=pl.BlockSpec((tm, d), lambda i, *_: (i, 0)),
        scratch_shapes=[pltpu.VMEM((2, tm * SUBLANES, LANES), F32),
                        pltpu.SemaphoreType.DMA((2,))],
    )
    return pl.pallas_call(
        functools.partial(_unsort_kernel, tm=tm, base=base),
        out_shape=jax.ShapeDtypeStruct((n_rows, d), F32),
        grid_spec=grid_spec,
        compiler_params=_params(("arbitrary",)),
        name="unsort",
    )(pos, y_rec)


def _sort_plan(cls, *, tm, n_tiles):
    onehot = (cls[:, None] == jnp.arange(N_CLASSES, dtype=jnp.int32)[None, :]).astype(jnp.int32)
    csum = jnp.cumsum(onehot, axis=0)
    rank = jnp.sum((csum - onehot) * onehot, axis=1)
    counts = csum[-1]
    tiles_per = (counts + tm - 1) // tm
    tile_end = jnp.cumsum(tiles_per)
    tile_start = tile_end - tiles_per
    pos = (jnp.sum(tile_start[None, :] * onehot, axis=1) * tm + rank).astype(jnp.int32)
    n_used = tile_end[-1].astype(jnp.int32)
    tile = jnp.minimum(jnp.arange(n_tiles, dtype=jnp.int32), n_used - 1)
    tile_cls = jnp.sum((tile[:, None] >= tile_end[None, :]).astype(jnp.int32), axis=1)
    grp = tile_cls // N_PAIRS
    pair = tile_cls % N_PAIRS
    lo = jnp.asarray(PAIR_LO, jnp.int32)[pair]
    hi = jnp.asarray(PAIR_HI, jnp.int32)[pair]
    ea = (grp * EXPERTS_PER_GROUP + lo).astype(jnp.int32)
    eb = (grp * EXPERTS_PER_GROUP + hi).astype(jnp.int32)
    return pos, ea, eb, n_used.reshape(1)


def _lane_row(values, start):
    row = jnp.zeros((LANES,), F32)
    return row.at[start:start + values.shape[0]].set(values.astype(F32)).reshape(1, LANES)


def _layer(x_prompt, x_sample, mem_prompt, cache_mem_k, cache_mem_v,
           state_mlstm_C, state_mlstm_n, state_mlstm_m, state_gdn_S, state_gdn_conv,
           norm_mix, w_in, b_igate, b_fgate, mlstm_norm, conv_w, a_log, dt_bias, gdn_norm,
           mem_norm, w_mem_kv, w_br_mlstm, w_br_gdn, w_br_xattn, w_out, norm_ffn,
           w_router_group, b_router_group, w_router_expert, b_router_expert,
           w_exp_gate, w_exp_up, w_exp_down, final_norm, *, cfg):
    bp, sp, d = x_prompt.shape
    bs, ss, _ = x_sample.shape
    mem_len = mem_prompt.shape[1]
    ss_pad = SUBLANES
    assert ss <= ss_pad and d % LANES == 0
    n_p = bp * sp
    n_s = bs * ss_pad
    W = BRANCH_W
    gs_ = cfg["sample_group"]

    sizes = (W, W, W, N_HEADS, N_HEADS, W, 3 * W, N_HEADS, N_HEADS, W, W, d, d, d)
    offs = [0]
    for s in sizes:
        offs.append(offs[-1] + s)
    (m_q, m_k, m_v, m_i, m_f, m_o, g_qkv, g_a, g_b, g_out, x_q, gate_a, gate_b, gate_c) = [
        w_in[:, offs[j]:offs[j + 1]] for j in range(len(sizes))]
    w_big = _col_blocks(
        jnp.concatenate([gate_a, gate_b, gate_c, m_q, m_k, m_v, m_o, g_qkv, g_out, x_q],
                        axis=1).astype(BF16), PROJ_GROUP * W)
    w_small = jnp.concatenate(
        [m_i, m_f, g_a, g_b, jnp.zeros((d, LANES - 4 * N_HEADS), w_in.dtype)], axis=1).astype(BF16)
    w_router = jnp.concatenate(
        [w_router_group, w_router_expert,
         jnp.zeros((d, LANES - N_GROUPS - N_EXPERTS), w_in.dtype)], axis=1).astype(BF16)
    b_router = _lane_row(jnp.concatenate([b_router_group, b_router_expert]), 0)
    gate_bias = _lane_row(jnp.concatenate([b_igate, b_fgate]), 0)
    a_row = _lane_row(-jnp.exp(a_log.astype(F32)), 2 * N_HEADS)
    dt_row = _lane_row(dt_bias, 2 * N_HEADS)

    xp2 = x_prompt.reshape(n_p, d)
    xs2 = jnp.pad(x_sample, ((0, 0), (0, ss_pad - ss), (0, 0))).reshape(n_s, d)

    proj, gs = _norm_proj(xp2, xs2, norm_mix.reshape(1, d), w_big, w_small, tm=cfg["tm_proj"])

    mem2 = mem_prompt.reshape(bp * mem_len, d)
    mem_kv, _ = _norm_proj(mem2, None, mem_norm.reshape(1, d),
                           _col_blocks(w_mem_kv.astype(BF16), W),
                           jnp.zeros((d, LANES), BF16), tm=mem_len)

    mnorm = mlstm_norm.reshape(1, W)
    gnorm = gdn_norm.reshape(1, HEAD_DIM)
    s_off = n_p // (gs_ * ss_pad)
    ha_p, c_p, nn_p, mm_p = _mlstm(proj, gs, gate_bias, mnorm, None, n_batch=bp, seq=sp,
                                   s_valid=sp, row_off=0, L=cfg["chunk_m"], n_sub=cfg["sub_m"], G=1)
    m0 = jnp.broadcast_to(state_mlstm_m[:, :, None], (bs, N_HEADS, LANES))
    ha_s, c_s, nn_s, mm_s = _mlstm(proj, gs, gate_bias, mnorm,
                                   (state_mlstm_C, state_mlstm_n, m0), n_batch=bs, seq=ss_pad,
                                   s_valid=ss, row_off=s_off, L=ss_pad, n_sub=1, G=gs_)
    hb_p, s_p, cv_p = _gdn(proj, gs, conv_w, a_row, dt_row, gnorm, None, n_batch=bp, seq=sp,
                           s_valid=sp, row_off=0, L=cfg["chunk_g"], n_sub=cfg["sub_g"], G=1)
    conv0 = jnp.pad(state_gdn_conv, ((0, 0), (SUBLANES - (CONV_W - 1), 0), (0, 0)))
    hb_s, s_s, cv_s = _gdn(proj, gs, conv_w, a_row, dt_row, gnorm, (conv0, state_gdn_S),
                           n_batch=bs, seq=ss_pad, s_valid=ss, row_off=s_off, L=ss_pad,
                           n_sub=1, G=gs_)
    lt = min(sp, cfg["lt_x"])
    hc_p = _xattn(proj, mem_kv, mem_kv, n_batch=bp, seq=sp, row_off=0, lt=lt, G=1,
                  mem_len=mem_len, head_rows=False,
                  kv_specs=[pl.BlockSpec((None, mem_len, W), lambda b, t, c=c: (c, b, 0))
                            for c in range(2)])
    kv_rows = gs_ * mem_len * N_HEADS
    hc_s = _xattn(proj, cache_mem_k.reshape(bs * mem_len * N_HEADS, HEAD_DIM),
                  cache_mem_v.reshape(bs * mem_len * N_HEADS, HEAD_DIM),
                  n_batch=bs, seq=ss_pad, row_off=s_off, lt=ss_pad, G=gs_, mem_len=mem_len,
                  head_rows=True,
                  kv_specs=[pl.BlockSpec((kv_rows, HEAD_DIM), lambda b, t: (b, 0))] * 2)

    tm = cfg["tm"]
    x1rec, route = _merge((ha_p, ha_s), (hb_p, hb_s), (hc_p, hc_s), (xp2, xs2), proj,
                          w_br_mlstm.astype(BF16), w_br_gdn.astype(BF16), w_br_xattn.astype(BF16),
                          w_out.astype(BF16), norm_ffn.reshape(1, d), w_router, b_router, tm=tm)

    tm_moe = cfg["tm_moe"]
    n_all = n_p + n_s
    n_tiles = (n_all + N_CLASSES * (tm_moe - 1)) // tm_moe + 1
    cls = route[:, ROUTE_CLS].astype(jnp.int32)
    pos, ea, eb, n_used = _sort_plan(cls, tm=tm_moe, n_tiles=n_tiles)
    x_sorted = _sort_scatter(pos, x1rec, n_tiles * tm_moe, tm=tm)
    y_sorted = _moe(ea, eb, n_used, x_sorted, w_exp_gate.astype(BF16),
                    w_exp_up.astype(BF16), w_exp_down.astype(BF16), norm_ffn.reshape(1, d),
                    final_norm.reshape(1, d), w_router, b_router, tm=tm_moe, n_tiles=n_tiles)
    y_p = _unsort(pos, y_sorted, base=0, n_rows=n_p, tm=tm)
    y_s = _unsort(pos, y_sorted, base=n_p, n_rows=n_s, tm=tm)

    y_prompt = y_p.reshape(bp, sp, d)
    y_sample = y_s.reshape(bs, ss_pad, d)[:, :ss]
    mk_p = mem_kv[0].reshape(bp, mem_len, N_HEADS, HEAD_DIM)
    mv_p = mem_kv[1].reshape(bp, mem_len, N_HEADS, HEAD_DIM)
    tail = slice(SUBLANES - (CONV_W - 1), SUBLANES)
    return (y_prompt, y_sample, mk_p, mv_p,
            c_p, nn_p, mm_p[:, :, 0], s_p, cv_p[:, tail],
            c_s, nn_s, mm_s[:, :, 0], s_s, cv_s[:, tail])


CONFIG = dict(tm_proj=1024, tm=512, tm_moe=256, chunk_m=256, sub_m=1, chunk_g=64, sub_g=8,
              lt_x=1024, sample_group=8)


def kernel(x_prompt, x_sample, mem_prompt, cache_mem_k, cache_mem_v, state_mlstm_C, state_mlstm_n, state_mlstm_m, state_gdn_S, state_gdn_conv, norm_mix, w_in, b_igate, b_fgate, mlstm_norm, conv_w, a_log, dt_bias, gdn_norm, mem_norm, w_mem_kv, w_br_mlstm, w_br_gdn, w_br_xattn, w_out, norm_ffn, w_router_group, b_router_group, w_router_expert, b_router_expert, w_exp_gate, w_exp_up, w_exp_down, final_norm):
    depth = w_in.shape[0]
    assert depth == 1, "one decoder layer"
    outs = _layer(
        x_prompt, x_sample, mem_prompt, cache_mem_k[0], cache_mem_v[0],
        state_mlstm_C[0], state_mlstm_n[0], state_mlstm_m[0], state_gdn_S[0], state_gdn_conv[0],
        norm_mix[0], w_in[0], b_igate[0], b_fgate[0], mlstm_norm[0], conv_w[0], a_log[0],
        dt_bias[0], gdn_norm[0], mem_norm[0], w_mem_kv[0], w_br_mlstm[0], w_br_gdn[0],
        w_br_xattn[0], w_out[0], norm_ffn[0], w_router_group[0], b_router_group[0],
        w_router_expert[0], b_router_expert[0], w_exp_gate[0], w_exp_up[0], w_exp_down[0],
        final_norm, cfg=CONFIG)
    y_prompt, y_sample = outs[0], outs[1]
    return (y_prompt, y_sample) + tuple(o[None] for o in outs[2:])
```

```python
import functools

import jax
import jax.numpy as jnp
from jax import lax
from jax.experimental import pallas as pl
from jax.experimental.pallas import tpu as pltpu

F32 = jnp.float32
BF16 = jnp.bfloat16
EPS = 1e-6

N_HEADS = 4
HEAD_DIM = 128
BRANCH_W = N_HEADS * HEAD_DIM
CONV_W = 4
N_GROUPS = 4
EXPERTS_PER_GROUP = 4
N_EXPERTS = N_GROUPS * EXPERTS_PER_GROUP
N_PAIRS = 6
N_CLASSES = N_GROUPS * N_PAIRS
PAIR_LO = (0, 0, 0, 1, 1, 2)
PAIR_HI = (1, 2, 3, 2, 3, 3)
NEG_BIG = -1e30
LANES = 128
SUBLANES = 8
TOKEN_ROWS = 8
PROJ_GROUP = 3
VMEM_LIMIT = 48 * 1024 * 1024

COL_GATE = 0
COL_MQ, COL_MK, COL_MV, COL_MO = 6, 7, 8, 9
COL_GQ, COL_GK, COL_GV, COL_GO, COL_XQ = 10, 11, 12, 13, 14
N_PROJ = 15 * BRANCH_W


def _dot(a, b):
    return jnp.dot(a.astype(BF16), b.astype(BF16), preferred_element_type=F32)


def _dot_nt(a, b):
    return lax.dot_general(a.astype(BF16), b.astype(BF16), (((1,), (1,)), ((), ())),
                           preferred_element_type=F32)


def _dot_tn(a, b):
    return lax.dot_general(a.astype(BF16), b.astype(BF16), (((0,), (0,)), ((), ())),
                           preferred_element_type=F32)


def _dot_exact(a, b):
    return jnp.dot(a, b, preferred_element_type=F32, precision=lax.Precision.HIGHEST)


def _lanes_to_rows(x, lane0):
    r = lax.broadcasted_iota(jnp.int32, (SUBLANES, LANES), 0)
    c = lax.broadcasted_iota(jnp.int32, (SUBLANES, LANES), 1)
    sel = (c == r + lane0).astype(F32)
    return lax.dot_general(sel, x, (((1,), (1,)), ((), ())), preferred_element_type=F32,
                           precision=lax.Precision.HIGHEST)


def _rms(x, g):
    return x * lax.rsqrt(jnp.mean(x * x, axis=-1, keepdims=True) + EPS) * g


def _softplus(x):
    return jnp.maximum(x, 0.0) + jnp.log1p(jnp.exp(-jnp.abs(x)))


def _sigmoid(x):
    return 0.5 * jnp.tanh(0.5 * x) + 0.5


def _silu(x):
    return x * _sigmoid(x)


def _tri(n, strict=False):
    r = lax.broadcasted_iota(jnp.int32, (n, n), 0)
    c = lax.broadcasted_iota(jnp.int32, (n, n), 1)
    return (c < r) if strict else (c <= r)


def _tri_blocks(n, block, upper=False):
    shift = block.bit_length() - 1
    assert block == 1 << shift
    r = lax.broadcasted_iota(jnp.int32, (n, n), 0)
    c = lax.broadcasted_iota(jnp.int32, (n, n), 1)
    same = jnp.right_shift(r, shift) == jnp.right_shift(c, shift)
    return same & ((r <= c) if upper else (c <= r))


def _pad_rows(shape, t, seq_rows, n_valid):
    assert seq_rows & (seq_rows - 1) == 0
    row = lax.broadcasted_iota(jnp.int32, shape, 0)
    return (row & (seq_rows - 1)) + t * seq_rows >= n_valid


def _params(sem):
    return pltpu.CompilerParams(dimension_semantics=sem, vmem_limit_bytes=VMEM_LIMIT)


def _two_source_specs(n_first, block, width):
    first = pl.BlockSpec((block, width), lambda i, *_: (jnp.minimum(i, n_first - 1), 0))
    second = pl.BlockSpec((block, width), lambda i, *_: (jnp.maximum(i - n_first, 0), 0))
    return first, second


def _norm_proj_kernel(xa_ref, xb_ref, g_ref, w_ref, ws_ref, o_ref, os_ref, hb_ref, *, n_first):
    i = pl.program_id(0)
    j = pl.program_id(1)

    @pl.when(j == 0)
    def _():
        @pl.when(i < n_first)
        def _():
            hb_ref[...] = _rms(xa_ref[...], g_ref[...]).astype(BF16)

        @pl.when(i >= n_first)
        def _():
            hb_ref[...] = _rms(xb_ref[...], g_ref[...]).astype(BF16)

        os_ref[...] = jnp.dot(hb_ref[...], ws_ref[...], preferred_element_type=F32)

    o_ref[...] = jnp.dot(hb_ref[...], w_ref[0], preferred_element_type=F32)


def _proj_spec(rows, c, row_block):
    return pl.BlockSpec((None, rows, BRANCH_W),
                        lambda *g: (c // PROJ_GROUP, row_block(*g), c % PROJ_GROUP))


def _col_blocks(w, tn):
    d, n = w.shape
    return w.reshape(d, n // tn, tn).transpose(1, 0, 2)


def _norm_proj(xa, xb, g, w, ws, *, tm):
    d = xa.shape[1]
    n_first = xa.shape[0] // tm
    n_rows = xa.shape[0] + (0 if xb is None else xb.shape[0])
    if xb is None:
        xb = xa
    tn = w.shape[2]
    n = w.shape[0] * tn
    sa, sb = _two_source_specs(n_first, tm, d)
    return pl.pallas_call(
        functools.partial(_norm_proj_kernel, n_first=n_first),
        out_shape=(jax.ShapeDtypeStruct((n // tn, n_rows, tn), F32),
                   jax.ShapeDtypeStruct((n_rows, ws.shape[1]), F32)),
        grid=(n_rows // tm, n // tn),
        in_specs=[sa, sb,
                  pl.BlockSpec((1, d), lambda i, j: (0, 0)),
                  pl.BlockSpec((1, d, tn), lambda i, j: (j, 0, 0)),
                  pl.BlockSpec((d, ws.shape[1]), lambda i, j: (0, 0))],
        out_specs=(pl.BlockSpec((None, tm, tn), lambda i, j: (j, i, 0)),
                   pl.BlockSpec((tm, ws.shape[1]), lambda i, j: (i, 0))),
        scratch_shapes=[pltpu.VMEM((tm, d), BF16)],
        compiler_params=_params(("parallel", "arbitrary")),
        name="norm_proj",
    )(xa, xb, g, w, ws)


def _seq_tile_specs(G, Lt, nt, row_off):
    assert G == 1 or nt == 1

    def rows(b, t):
        return row_off + b * nt + t

    return rows


def _mlstm_kernel(*refs, L, n_sub, G, s_valid, s_total, has_init):
    if has_init:
        (q_ref, k_ref, v_ref, og_ref, gs_ref, bias_ref, norm_ref, c0_ref, n0_ref, m0_ref,
         h_ref, c_ref, n_ref, m_ref) = refs
    else:
        (q_ref, k_ref, v_ref, og_ref, gs_ref, bias_ref, norm_ref,
         h_ref, c_ref, n_ref, m_ref) = refs
    t = pl.program_id(1)
    Lt = n_sub * L
    rows = G * Lt

    @pl.when(t == 0)
    def _():
        if has_init:
            c_ref[...] = c0_ref[...]
            n_ref[...] = n0_ref[...]
            m_ref[...] = m0_ref[...]
        else:
            c_ref[...] = jnp.zeros_like(c_ref)
            n_ref[...] = jnp.zeros_like(n_ref)
            m_ref[...] = jnp.zeros_like(m_ref)

    z = gs_ref[...] + bias_ref[...]
    lane = lax.broadcasted_iota(jnp.int32, z.shape, 1)
    log_f = jnp.minimum(z, 0.0) - jnp.log1p(jnp.exp(-jnp.abs(z)))
    g = jnp.where((lane >= N_HEADS) & (lane < 2 * N_HEADS), log_f, z)
    if s_valid < s_total:
        pad = _pad_rows(z.shape, t, Lt, s_valid)
        g = jnp.where(pad, jnp.where(lane < N_HEADS, NEG_BIG, 0.0), g)
    g_t = _lanes_to_rows(g, 0)
    bt_cols = _dot_exact(_tri_blocks(rows, L).astype(F32), g)
    bt_rows = _dot_exact(g_t, _tri_blocks(rows, L, upper=True).astype(F32))
    causal = _tri(L)
    scale = HEAD_DIM ** -0.5
    probs = [(s, h) for s in range(G) for h in range(N_HEADS)]

    for c in range(n_sub):
        P = []
        for s, h in probs:
            r = slice(s * Lt + c * L, s * Lt + (c + 1) * L)
            sl = slice(h * HEAD_DIM, (h + 1) * HEAD_DIM)
            p = dict(s=s, h=h, r=r, sl=sl, q=q_ref[r, sl], k=k_ref[r, sl] * scale, v=v_ref[r, sl],
                     bt_c=bt_cols[r, N_HEADS + h:N_HEADS + h + 1], it_c=g[r, h:h + 1])
            bt_r = bt_rows[N_HEADS + h:N_HEADS + h + 1, r]
            it_r = g_t[h:h + 1, r]
            p["log_d"] = jnp.where(causal, p["bt_c"] - bt_r + it_r, -jnp.inf)
            p["c_prev"] = c_ref[s, h]
            p["n_prev"] = n_ref[s, h:h + 1, :]
            p["m_prev"] = m_ref[s, h:h + 1, 0:1]
            P.append(p)
        for p in P:
            p["qk"] = _dot_nt(p["q"], p["k"])
        for p in P:
            p["qc"] = _dot(p["q"], p["c_prev"])
        for p in P:
            inter = p["bt_c"] + p["m_prev"]
            p["m_t"] = jnp.maximum(inter, jnp.max(p["log_d"], axis=-1, keepdims=True))
            p["inter_w"] = jnp.exp(inter - p["m_t"])
            p["sd"] = p["qk"] * jnp.exp(p["log_d"] - p["m_t"])
        for p in P:
            p["sv"] = _dot(p["sd"], p["v"])
        for p in P:
            bt_last = p["bt_c"][L - 1:L, :]
            p["m_new"] = p["m_t"][L - 1:L, :]
            w = jnp.exp(bt_last - p["bt_c"] + p["it_c"] - p["m_new"])
            p["decay"] = jnp.exp(bt_last + p["m_prev"] - p["m_new"])
            p["kw"] = p["k"] * w
        for p in P:
            p["kv"] = _dot_tn(p["kw"], p["v"])
        for p in P:
            num = p["sv"] + p["qc"] * p["inter_w"]
            den = (jnp.sum(p["sd"], axis=-1, keepdims=True)
                   + p["inter_w"] * jnp.sum(p["q"] * p["n_prev"], axis=-1, keepdims=True))
            hh = num / jnp.maximum(jnp.abs(den), jnp.exp(-p["m_t"]))
            hn = hh * lax.rsqrt(jnp.mean(hh * hh, axis=-1, keepdims=True) + EPS) * norm_ref[:, p["sl"]]
            p["h_out"] = hn * _sigmoid(og_ref[p["r"], p["sl"]])
            p["n_new"] = p["decay"] * p["n_prev"] + jnp.sum(p["kw"], axis=0, keepdims=True)
        for p in P:
            s, h = p["s"], p["h"]
            h_ref[p["r"], p["sl"]] = p["h_out"]
            n_ref[s, h:h + 1, :] = p["n_new"]
            m_ref[s, h:h + 1, :] = jnp.broadcast_to(p["m_new"], (1, LANES))
            c_ref[s, h] = p["decay"] * p["c_prev"] + p["kv"]


def _mlstm(proj, gs, bias_row, norm_row, init, *, n_batch, seq, s_valid, row_off, L, n_sub, G):
    Lt = n_sub * L
    nt = seq // Lt
    rows = G * Lt
    has_init = init is not None
    blk = _seq_tile_specs(G, Lt, nt, row_off)

    def col(c):
        return _proj_spec(rows, c, blk)

    in_specs = [col(COL_MQ), col(COL_MK), col(COL_MV), col(COL_MO),
                pl.BlockSpec((rows, LANES), lambda b, t: (blk(b, t), 0)),
                pl.BlockSpec((1, LANES), lambda b, t: (0, 0)),
                pl.BlockSpec((1, BRANCH_W), lambda b, t: (0, 0))]
    args = [proj, proj, proj, proj, gs, bias_row, norm_row]
    c_spec = pl.BlockSpec((G, N_HEADS, HEAD_DIM, HEAD_DIM), lambda b, t: (b, 0, 0, 0))
    n_spec = pl.BlockSpec((G, N_HEADS, HEAD_DIM), lambda b, t: (b, 0, 0))
    m_spec = pl.BlockSpec((G, N_HEADS, LANES), lambda b, t: (b, 0, 0))
    if has_init:
        in_specs += [c_spec, n_spec, m_spec]
        args += list(init)
    return pl.pallas_call(
        functools.partial(_mlstm_kernel, L=L, n_sub=n_sub, G=G, s_valid=s_valid, s_total=seq,
                          has_init=has_init),
        out_shape=(jax.ShapeDtypeStruct((n_batch * seq, BRANCH_W), F32),
                   jax.ShapeDtypeStruct((n_batch, N_HEADS, HEAD_DIM, HEAD_DIM), F32),
                   jax.ShapeDtypeStruct((n_batch, N_HEADS, HEAD_DIM), F32),
                   jax.ShapeDtypeStruct((n_batch, N_HEADS, LANES), F32)),
        grid=(n_batch // G, nt),
        in_specs=in_specs,
        out_specs=(pl.BlockSpec((rows, BRANCH_W), lambda b, t: (b * nt + t, 0)),
                   c_spec, n_spec, m_spec),
        compiler_params=_params(("parallel", "arbitrary")),
        name="mlstm",
    )(*args)


def _gdn_kernel(*refs, L, n_sub, G, s_valid, s_total, has_init):
    if has_init:
        (q_ref, k_ref, v_ref, og_ref, gs_ref, cw_ref, arow_ref, dtrow_ref, norm_ref,
         conv0_ref, s0_ref, o_ref, s_ref, conv1_ref, ext_ref, conv_ref) = refs
    else:
        (q_ref, k_ref, v_ref, og_ref, gs_ref, cw_ref, arow_ref, dtrow_ref, norm_ref,
         o_ref, s_ref, conv1_ref, ext_ref, conv_ref) = refs
    t = pl.program_id(1)
    Lt = n_sub * L
    rows = G * Lt
    nt = s_total // Lt
    W = BRANCH_W

    @pl.when(t == 0)
    def _():
        if has_init:
            s_ref[...] = s0_ref[...]
            ext_ref[:, 0:SUBLANES, :] = conv0_ref[...]
        else:
            s_ref[...] = jnp.zeros_like(s_ref)
            ext_ref[:, 0:SUBLANES, :] = jnp.zeros((G, SUBLANES, 3 * W), F32)

    @pl.when(t > 0)
    def _():
        ext_ref[:, 0:SUBLANES, :] = ext_ref[:, Lt:Lt + SUBLANES, :]

    base = SUBLANES - (CONV_W - 1)
    for s in range(G):
        rs = slice(s * Lt, (s + 1) * Lt)
        ext_ref[s, SUBLANES:SUBLANES + Lt, 0:W] = q_ref[rs, :]
        ext_ref[s, SUBLANES:SUBLANES + Lt, W:2 * W] = k_ref[rs, :]
        ext_ref[s, SUBLANES:SUBLANES + Lt, 2 * W:3 * W] = v_ref[rs, :]
        acc = ext_ref[s, base:base + Lt, :] * cw_ref[0:1, :]
        for j in range(1, CONV_W):
            acc = acc + ext_ref[s, base + j:base + j + Lt, :] * cw_ref[j:j + 1, :]
        conv_ref[rs, :] = _silu(acc)

    @pl.when(t == nt - 1)
    def _():
        v_last = s_valid - (nt - 1) * Lt
        conv1_ref[...] = ext_ref[:, v_last:v_last + SUBLANES, :]

    z = gs_ref[...]
    lane = lax.broadcasted_iota(jnp.int32, z.shape, 1)
    gdec = arow_ref[...] * _softplus(z + dtrow_ref[...])
    gb = jnp.where((lane >= 2 * N_HEADS) & (lane < 3 * N_HEADS), gdec,
                   jnp.where((lane >= 3 * N_HEADS) & (lane < 4 * N_HEADS), _sigmoid(z), 0.0))
    if s_valid < s_total:
        gb = jnp.where(_pad_rows(z.shape, t, Lt, s_valid), 0.0, gb)
    gb_t = _lanes_to_rows(gb, 2 * N_HEADS)
    gam_cols = _dot_exact(_tri_blocks(rows, L).astype(F32), gb)
    gam_rows = _dot_exact(gb_t, _tri_blocks(rows, L, upper=True).astype(F32))
    incl = _tri(L)
    strict = _tri(L, strict=True)
    eye = (lax.broadcasted_iota(jnp.int32, (L, L), 0)
           == lax.broadcasted_iota(jnp.int32, (L, L), 1)).astype(F32)

    P = []
    for s in range(G):
        for c in range(n_sub):
            for h in range(N_HEADS):
                r = slice(s * Lt + c * L, s * Lt + (c + 1) * L)
                sl = slice(h * HEAD_DIM, (h + 1) * HEAD_DIM)
                q = conv_ref[r, h * HEAD_DIM:(h + 1) * HEAD_DIM]
                k = conv_ref[r, W + h * HEAD_DIM:W + (h + 1) * HEAD_DIM]
                v = conv_ref[r, 2 * W + h * HEAD_DIM:2 * W + (h + 1) * HEAD_DIM]
                q = q * lax.rsqrt(jnp.sum(q * q, axis=-1, keepdims=True) + EPS) * (HEAD_DIM ** -0.5)
                k = k * lax.rsqrt(jnp.sum(k * k, axis=-1, keepdims=True) + EPS)
                gam_c = gam_cols[r, 2 * N_HEADS + h:2 * N_HEADS + h + 1]
                gam_r = gam_rows[h:h + 1, r]
                beta_c = gb[r, 3 * N_HEADS + h:3 * N_HEADS + h + 1]
                decay = jnp.exp(jnp.where(incl, gam_c - gam_r, -jnp.inf))
                P.append(dict(s=s, c=c, h=h, r=r, sl=sl, q=q, k=k, v=v, gam_c=gam_c,
                              beta_c=beta_c, decay=decay))
    for p in P:
        p["kk"] = _dot_nt(p["k"], p["k"])
    for p in P:
        p["qk"] = _dot_nt(p["q"], p["k"]) * p["decay"]
    for p in P:
        e_gam = jnp.exp(p["gam_c"])
        gam_last = p["gam_c"][L - 1:L, :]
        p["pw"] = -jnp.where(strict, p["beta_c"] * p["kk"] * p["decay"], 0.0)
        p["x"] = eye + p["pw"]
        p["rhs"] = jnp.concatenate([p["beta_c"] * p["v"], p["beta_c"] * p["k"] * e_gam], axis=-1)
        p["q_dec"] = p["q"] * e_gam
        p["k_dec"] = p["k"] * jnp.exp(gam_last - p["gam_c"])
        p["e_last"] = jnp.exp(gam_last)
    n = 2
    while n < L:
        for p in P:
            p["pw"] = _dot(p["pw"], p["pw"])
        for p in P:
            p["x"] = p["x"] + _dot(p["x"], p["pw"])
        n *= 2
    for p in P:
        sol = _dot(p["x"], p["rhs"])
        p["u"] = sol[:, :HEAD_DIM]
        p["w"] = sol[:, HEAD_DIM:]

    for c in range(n_sub):
        Pc = [p for p in P if p["c"] == c]
        for p in Pc:
            p["s_prev"] = s_ref[p["s"], p["h"]]
            p["ws"] = _dot(p["w"], p["s_prev"])
        for p in Pc:
            p["qs"] = _dot(p["q_dec"], p["s_prev"])
        for p in Pc:
            p["v_new"] = p["u"] - p["ws"]
        for p in Pc:
            p["o"] = p["qs"] + _dot(p["qk"], p["v_new"])
        for p in Pc:
            p["kv"] = _dot_tn(p["k_dec"], p["v_new"])
        for p in Pc:
            o = p["o"]
            on = o * lax.rsqrt(jnp.mean(o * o, axis=-1, keepdims=True) + EPS) * norm_ref[...]
            p["o_out"] = on * _silu(og_ref[p["r"], p["sl"]])
        for p in Pc:
            s_ref[p["s"], p["h"]] = p["e_last"] * p["s_prev"] + p["kv"]
            o_ref[p["r"], p["sl"]] = p["o_out"]


def _gdn(proj, gs, conv_w, a_row, dt_row, norm_row, init, *, n_batch, seq, s_valid, row_off,
         L, n_sub, G):
    Lt = n_sub * L
    nt = seq // Lt
    rows = G * Lt
    has_init = init is not None
    blk = _seq_tile_specs(G, Lt, nt, row_off)

    def col(c):
        return _proj_spec(rows, c, blk)

    def const(shape):
        return pl.BlockSpec(shape, lambda b, t: (0,) * len(shape))

    in_specs = [col(COL_GQ), col(COL_GK), col(COL_GV), col(COL_GO),
                pl.BlockSpec((rows, LANES), lambda b, t: (blk(b, t), 0)),
                const((CONV_W, 3 * BRANCH_W)), const((1, LANES)), const((1, LANES)),
                const((1, HEAD_DIM))]
    args = [proj, proj, proj, proj, gs, conv_w, a_row, dt_row, norm_row]
    s_spec = pl.BlockSpec((G, N_HEADS, HEAD_DIM, HEAD_DIM), lambda b, t: (b, 0, 0, 0))
    conv_spec = pl.BlockSpec((G, SUBLANES, 3 * BRANCH_W), lambda b, t: (b, 0, 0))
    if has_init:
        in_specs += [conv_spec, s_spec]
        args += list(init)
    return pl.pallas_call(
        functools.partial(_gdn_kernel, L=L, n_sub=n_sub, G=G, s_valid=s_valid, s_total=seq,
                          has_init=has_init),
        out_shape=(jax.ShapeDtypeStruct((n_batch * seq, BRANCH_W), F32),
                   jax.ShapeDtypeStruct((n_batch, N_HEADS, HEAD_DIM, HEAD_DIM), F32),
                   jax.ShapeDtypeStruct((n_batch, SUBLANES, 3 * BRANCH_W), F32)),
        grid=(n_batch // G, nt),
        in_specs=in_specs,
        out_specs=(pl.BlockSpec((rows, BRANCH_W), lambda b, t: (b * nt + t, 0)),
                   s_spec, conv_spec),
        scratch_shapes=[pltpu.VMEM((G, Lt + SUBLANES, 3 * BRANCH_W), F32),
                        pltpu.VMEM((rows, 3 * BRANCH_W), F32)],
        compiler_params=_params(("parallel", "arbitrary")),
        name="gdn",
    )(*args)


def _xattn_kernel(q_ref, mk_ref, mv_ref, o_ref, *, G, lt, mem_len, head_rows):
    scale = HEAD_DIM ** -0.5

    def head(ref, s, h):
        if head_rows:
            return ref[pl.ds(s * mem_len * N_HEADS + h, mem_len, stride=N_HEADS), :]
        return ref[s * mem_len:(s + 1) * mem_len, h * HEAD_DIM:(h + 1) * HEAD_DIM]

    P = []
    for s in range(G):
        for h in range(N_HEADS):
            P.append(dict(s=s, h=h, r=slice(s * lt, (s + 1) * lt),
                          sl=slice(h * HEAD_DIM, (h + 1) * HEAD_DIM)))
    for p in P:
        p["sc"] = _dot_nt(q_ref[p["r"], p["sl"]], head(mk_ref, p["s"], p["h"])) * scale
    for p in P:
        e = jnp.exp(p["sc"] - jnp.max(p["sc"], axis=-1, keepdims=True))
        p["p"] = e / jnp.sum(e, axis=-1, keepdims=True)
    for p in P:
        o_ref[p["r"], p["sl"]] = _dot(p["p"], head(mv_ref, p["s"], p["h"]))


def _xattn(proj, mk, mv, *, n_batch, seq, row_off, lt, G, mem_len, kv_specs, head_rows):
    nt = seq // lt
    blk = _seq_tile_specs(G, lt, nt, row_off)
    return pl.pallas_call(
        functools.partial(_xattn_kernel, G=G, lt=lt, mem_len=mem_len, head_rows=head_rows),
        out_shape=jax.ShapeDtypeStruct((n_batch * seq, BRANCH_W), F32),
        grid=(n_batch // G, nt),
        in_specs=[_proj_spec(G * lt, COL_XQ, blk)] + list(kv_specs),
        out_specs=pl.BlockSpec((G * lt, BRANCH_W), lambda b, t: (b * nt + t, 0)),
        compiler_params=_params(("parallel", "arbitrary")),
        name="xattn",
    )(proj, mk, mv)


def _lane_first_max(x, mask, lane_f):
    xm = jnp.where(mask, x, -jnp.inf)
    mx = jnp.max(xm, axis=-1, keepdims=True)
    idx = jnp.min(jnp.where(xm == mx, lane_f, float(LANES)), axis=-1, keepdims=True)
    return mx, idx


def _store_token_tiles(ref, x, tok_rows):
    n = x.shape[0]
    for j in range(x.shape[1] // LANES):
        ref[pl.ds(j, n, stride=tok_rows), :] = x[:, j * LANES:(j + 1) * LANES]


def _load_token_tiles(ref, n, tok_rows, n_chunks=SUBLANES):
    return jnp.concatenate([ref[pl.ds(j, n, stride=tok_rows), :] for j in range(n_chunks)], axis=1)


def _merge_kernel(hap_ref, has_ref, hbp_ref, hbs_ref, hcp_ref, hcs_ref, xp_ref, xs_ref,
                  ga0_ref, ga1_ref, gb0_ref, gb1_ref, gc0_ref, gc1_ref,
                  wa_ref, wb_ref, wc_ref, wo_ref, nf_ref, wr_ref, br_ref,
                  x1_ref, route_ref, *, n_first):
    i = pl.program_id(0)

    def gate(lo_ref, hi_ref):
        return jnp.concatenate([_sigmoid(lo_ref[...]), _sigmoid(hi_ref[...])], axis=1)

    def body(ha_ref, hb_ref, hc_ref, x_ref):
        mixed = (gate(ga0_ref, ga1_ref) * _dot(ha_ref[...], wa_ref[...])
                 + gate(gb0_ref, gb1_ref) * _dot(hb_ref[...], wb_ref[...])
                 + gate(gc0_ref, gc1_ref) * _dot(hc_ref[...], wc_ref[...]))
        x1 = x_ref[...] + _dot(mixed, wo_ref[...])
        h2 = _rms(x1, nf_ref[...])
        logits = _dot(h2, wr_ref[...]) + br_ref[...]
        lane = lax.broadcasted_iota(jnp.int32, logits.shape, 1)
        lane_f = lane.astype(F32)
        is_g = lane < N_GROUPS
        _, g_idx = _lane_first_max(logits, is_g, lane_f)
        e_lo = N_GROUPS + EXPERTS_PER_GROUP * g_idx
        in_grp = (lane_f >= e_lo) & (lane_f < e_lo + EXPERTS_PER_GROUP)
        _, i1 = _lane_first_max(logits, in_grp, lane_f)
        _, i2 = _lane_first_max(logits, in_grp & (lane_f != i1), lane_f)
        lo = jnp.minimum(i1, i2) - e_lo
        hi = jnp.maximum(i1, i2) - e_lo
        pair = lo * (7.0 - lo) * 0.5 + (hi - lo - 1.0)
        cls = g_idx * float(N_PAIRS) + pair
        _store_token_tiles(x1_ref, x1, TOKEN_ROWS)
        route_ref[...] = jnp.where(lane == ROUTE_CLS, cls, 0.0)

    @pl.when(i < n_first)
    def _():
        body(hap_ref, hbp_ref, hcp_ref, xp_ref)

    @pl.when(i >= n_first)
    def _():
        body(has_ref, hbs_ref, hcs_ref, xs_ref)


def _merge(ha, hb, hc, xs, proj, wa, wb, wc, wo, nf, wr, br, *, tm):
    d = xs[0].shape[1]
    n_first = xs[0].shape[0] // tm
    n_rows = xs[0].shape[0] + xs[1].shape[0]
    specs = []
    for width in (BRANCH_W, BRANCH_W, BRANCH_W, d):
        specs += list(_two_source_specs(n_first, tm, width))

    def const(a):
        return pl.BlockSpec(a.shape, lambda i: (0,) * a.ndim)

    n_gate = 3 * d // BRANCH_W
    specs += [_proj_spec(tm, COL_GATE + c, lambda i: i) for c in range(n_gate)]
    weights = [wa, wb, wc, wo, nf, wr, br]
    specs += [const(a) for a in weights]
    return pl.pallas_call(
        functools.partial(_merge_kernel, n_first=n_first),
        out_shape=(jax.ShapeDtypeStruct((n_rows * TOKEN_ROWS, LANES), F32),
                   jax.ShapeDtypeStruct((n_rows, LANES), F32)),
        grid=(n_rows // tm,),
        in_specs=specs,
        out_specs=(pl.BlockSpec((tm * TOKEN_ROWS, LANES), lambda i: (i, 0)),
                   pl.BlockSpec((tm, LANES), lambda i: (i, 0))),
        compiler_params=_params(("parallel",)),
        name="merge",
    )(*ha, *hb, *hc, *xs, *([proj] * n_gate), *weights)


GATHER_UNROLL = 8
ROUTE_CLS = 0


def _record_gather_start(src_ref, dst_ref, sem, ids_ref, base, n, rec):
    def issue(r, carry):
        src = pl.multiple_of(ids_ref[base + r] * rec, rec)
        dst = pl.multiple_of(r * rec, rec)
        pltpu.make_async_copy(src_ref.at[pl.ds(src, rec)], dst_ref.at[pl.ds(dst, rec)], sem).start()
        return carry
    lax.fori_loop(0, n, issue, 0, unroll=GATHER_UNROLL)


def _record_gather_wait(src_ref, dst_ref, sem):
    pltpu.make_async_copy(src_ref.at[pl.ds(0, dst_ref.shape[0])], dst_ref, sem).wait()


def _sort_scatter_kernel(pos_ref, x_ref, init_hbm, out_hbm, sem, *, tm):
    del init_hbm
    base = pl.program_id(0) * tm

    def issue(r, carry):
        src = pl.multiple_of(r * TOKEN_ROWS, TOKEN_ROWS)
        dst = pl.multiple_of(pos_ref[base + r] * TOKEN_ROWS, TOKEN_ROWS)
        pltpu.make_async_copy(x_ref.at[pl.ds(src, TOKEN_ROWS)], out_hbm.at[pl.ds(dst, TOKEN_ROWS)],
                              sem).start()
        return carry
    lax.fori_loop(0, tm, issue, 0, unroll=GATHER_UNROLL)
    pltpu.make_async_copy(x_ref, out_hbm.at[pl.ds(0, tm * TOKEN_ROWS)], sem).wait()


def _sort_scatter(pos, x1rec, n_sorted, *, tm):
    n_tok = x1rec.shape[0] // TOKEN_ROWS
    grid_spec = pltpu.PrefetchScalarGridSpec(
        num_scalar_prefetch=1,
        grid=(n_tok // tm,),
        in_specs=[pl.BlockSpec((tm * TOKEN_ROWS, LANES), lambda i, *_: (i, 0)),
                  pl.BlockSpec(memory_space=pl.ANY)],
        out_specs=pl.BlockSpec(memory_space=pl.ANY),
        scratch_shapes=[pltpu.SemaphoreType.DMA(())],
    )
    return pl.pallas_call(
        functools.partial(_sort_scatter_kernel, tm=tm),
        out_shape=jax.ShapeDtypeStruct((n_sorted * TOKEN_ROWS, LANES), F32),
        grid_spec=grid_spec,
        input_output_aliases={2: 0},
        compiler_params=_params(("arbitrary",)),
        name="sort_scatter",
    )(pos, x1rec, jnp.zeros((n_sorted * TOKEN_ROWS, LANES), F32))


def _moe_kernel(ea_ref, eb_ref, nused_ref, x_ref, wga_ref, wua_ref, wda_ref,
                wgb_ref, wub_ref, wdb_ref, nf_ref, fin_ref, wr_ref, br_ref, y_ref,
                wg_bf, wu_bf, wd_bf, *, tm, d):
    i = pl.program_id(0)
    n_used = nused_ref[0]

    prev = jnp.maximum(i - 1, 0)
    for slot, e_ref, srcs in ((0, ea_ref, (wga_ref, wua_ref, wda_ref)),
                              (1, eb_ref, (wgb_ref, wub_ref, wdb_ref))):
        @pl.when((i < n_used) & ((i == 0) | (e_ref[i] != e_ref[prev])))
        def _():
            for dst, src in zip((wg_bf, wu_bf, wd_bf), srcs):
                dst[slot] = src[0].astype(BF16)

    @pl.when(i < n_used)
    def _():
        x1 = _load_token_tiles(x_ref, tm, TOKEN_ROWS)
        h2 = _rms(x1, nf_ref[...]).astype(BF16)
        logits = jnp.dot(h2, wr_ref[...], preferred_element_type=F32) + br_ref[...]
        lane = lax.broadcasted_iota(jnp.int32, logits.shape, 1)

        def pick(l):
            return jnp.sum(jnp.where(lane == l, logits, 0.0), axis=-1, keepdims=True)

        ea, eb = ea_ref[i], eb_ref[i]
        is_g = lane < N_GROUPS
        g_max = jnp.max(jnp.where(is_g, logits, -jnp.inf), axis=-1, keepdims=True)
        g_den = jnp.sum(jnp.where(is_g, jnp.exp(logits - g_max), 0.0), axis=-1, keepdims=True)
        p_grp = jnp.exp(pick(ea // EXPERTS_PER_GROUP) - g_max) / g_den
        v_a, v_b = pick(N_GROUPS + ea), pick(N_GROUPS + eb)
        combs = (p_grp / (1.0 + jnp.exp(v_b - v_a)), p_grp / (1.0 + jnp.exp(v_a - v_b)))
        acc = jnp.zeros((tm, d), F32)
        for slot, comb in enumerate(combs):
            a = jnp.dot(h2, wg_bf[slot], preferred_element_type=F32)
            u = jnp.dot(h2, wu_bf[slot], preferred_element_type=F32)
            act = _silu(a) * u * comb
            acc = acc + _dot(act, wd_bf[slot])
        _store_token_tiles(y_ref, _rms(x1 + acc, fin_ref[...]), SUBLANES)

    @pl.when(i >= n_used)
    def _():
        y_ref[...] = jnp.zeros_like(y_ref)


def _moe(tile_ea, tile_eb, n_used, x_sorted, wg, wu, wd, nf, fin, wr, br, *, tm, n_tiles):
    d = nf.shape[1]
    f = wg.shape[2]
    assert d == SUBLANES * LANES

    def expert(shape, which):
        return pl.BlockSpec((1,) + shape, lambda i, ea, eb, nu: ((ea, eb)[which][i], 0, 0))

    def const(a):
        return pl.BlockSpec(a.shape, lambda i, *_: (0,) * a.ndim)

    grid_spec = pltpu.PrefetchScalarGridSpec(
        num_scalar_prefetch=3,
        grid=(n_tiles,),
        in_specs=[pl.BlockSpec((tm * TOKEN_ROWS, LANES), lambda i, *_: (i, 0)),
                  expert((d, f), 0), expert((d, f), 0), expert((f, d), 0),
                  expert((d, f), 1), expert((d, f), 1), expert((f, d), 1),
                  const(nf), const(fin), const(wr), const(br)],
        out_specs=pl.BlockSpec((tm * SUBLANES, LANES), lambda i, *_: (i, 0)),
        scratch_shapes=[pltpu.VMEM((2, d, f), BF16), pltpu.VMEM((2, d, f), BF16),
                        pltpu.VMEM((2, f, d), BF16)],
    )
    return pl.pallas_call(
        functools.partial(_moe_kernel, tm=tm, d=d),
        out_shape=jax.ShapeDtypeStruct((n_tiles * tm * SUBLANES, LANES), F32),
        grid_spec=grid_spec,
        compiler_params=_params(("arbitrary",)),
        name="moe",
    )(tile_ea, tile_eb, n_used, x_sorted, wg, wu, wd, wg, wu, wd, nf, fin, wr, br)


def _unsort_kernel(pos_ref, y_hbm, o_ref, buf, sem, *, tm, base):
    i = pl.program_id(0)
    slot = i % 2

    @pl.when(i == 0)
    def _():
        _record_gather_start(y_hbm, buf.at[0], sem.at[0], pos_ref, base, tm, SUBLANES)

    @pl.when(i + 1 < pl.num_programs(0))
    def _():
        _record_gather_start(y_hbm, buf.at[1 - slot], sem.at[1 - slot], pos_ref,
                             base + (i + 1) * tm, tm, SUBLANES)

    _record_gather_wait(y_hbm, buf.at[slot], sem.at[slot])
    o_ref[...] = _load_token_tiles(buf.at[slot], tm, SUBLANES)


def _unsort(pos, y_rec, *, base, n_rows, tm):
    d = SUBLANES * LANES
    grid_spec = pltpu.PrefetchScalarGridSpec(
        num_scalar_prefetch=1,
        grid=(n_rows // tm,),
        in_specs=[pl.BlockSpec(memory_space=pl.ANY)],
        out_specs=pl.BlockSpec((tm, d), lambda i, *_: (i, 0)),
        scratch_shapes=[pltpu.VMEM((2, tm * SUBLANES, LANES), F32),
                        pltpu.SemaphoreType.DMA((2,))],
    )
    return pl.pallas_call(
        functools.partial(_unsort_kernel, tm=tm, base=base),
        out_shape=jax.ShapeDtypeStruct((n_rows, d), F32),
        grid_spec=grid_spec,
        compiler_params=_params(("arbitrary",)),
        name="unsort",
    )(pos, y_rec)


def _sort_plan(cls, *, tm, n_tiles):
    onehot = (cls[:, None] == jnp.arange(N_CLASSES, dtype=jnp.int32)[None, :]).astype(jnp.int32)
    csum = jnp.cumsum(onehot, axis=0)
    rank = jnp.sum((csum - onehot) * onehot, axis=1)
    counts = csum[-1]
    tiles_per = (counts + tm - 1) // tm
    tile_end = jnp.cumsum(tiles_per)
    tile_start = tile_end - tiles_per
    pos = (jnp.sum(tile_start[None, :] * onehot, axis=1) * tm + rank).astype(jnp.int32)
    n_used = tile_end[-1].astype(jnp.int32)
    tile = jnp.minimum(jnp.arange(n_tiles, dtype=jnp.int32), n_used - 1)
    tile_cls = jnp.sum((tile[:, None] >= tile_end[None, :]).astype(jnp.int32), axis=1)
    grp = tile_cls // N_PAIRS
    pair = tile_cls % N_PAIRS
    lo = jnp.asarray(PAIR_LO, jnp.int32)[pair]
    hi = jnp.asarray(PAIR_HI, jnp.int32)[pair]
    ea = (grp * EXPERTS_PER_GROUP + lo).astype(jnp.int32)
    eb = (grp * EXPERTS_PER_GROUP + hi).astype(jnp.int32)
    return pos, ea, eb, n_used.reshape(1)


def _lane_row(values, start):
    row = jnp.zeros((LANES,), F32)
    return row.at[start:start + values.shape[0]].set(values.astype(F32)).reshape(1, LANES)


def _layer(x_prompt, x_sample, mem_prompt, cache_mem_k, cache_mem_v,
           state_mlstm_C, state_mlstm_n, state_mlstm_m, state_gdn_S, state_gdn_conv,
           norm_mix, w_in, b_igate, b_fgate, mlstm_norm, conv_w, a_log, dt_bias, gdn_norm,
           mem_norm, w_mem_kv, w_br_mlstm, w_br_gdn, w_br_xattn, w_out, norm_ffn,
           w_router_group, b_router_group, w_router_expert, b_router_expert,
           w_exp_gate, w_exp_up, w_exp_down, final_norm, *, cfg):
    bp, sp, d = x_prompt.shape
    bs, ss, _ = x_sample.shape
    mem_len = mem_prompt.shape[1]
    ss_pad = SUBLANES
    assert ss <= ss_pad and d % LANES == 0
    n_p = bp * sp
    n_s = bs * ss_pad
    W = BRANCH_W
    gs_ = cfg["sample_group"]

    sizes = (W, W, W, N_HEADS, N_HEADS, W, 3 * W, N_HEADS, N_HEADS, W, W, d, d, d)
    offs = [0]
    for s in sizes:
        offs.append(offs[-1] + s)
    (m_q, m_k, m_v, m_i, m_f, m_o, g_qkv, g_a, g_b, g_out, x_q, gate_a, gate_b, gate_c) = [
        w_in[:, offs[j]:offs[j + 1]] for j in range(len(sizes))]
    w_big = _col_blocks(
        jnp.concatenate([gate_a, gate_b, gate_c, m_q, m_k, m_v, m_o, g_qkv, g_out, x_q],
                        axis=1).astype(BF16), PROJ_GROUP * W)
    w_small = jnp.concatenate(
        [m_i, m_f, g_a, g_b, jnp.zeros((d, LANES - 4 * N_HEADS), w_in.dtype)], axis=1).astype(BF16)
    w_router = jnp.concatenate(
        [w_router_group, w_router_expert,
         jnp.zeros((d, LANES - N_GROUPS - N_EXPERTS), w_in.dtype)], axis=1).astype(BF16)
    b_router = _lane_row(jnp.concatenate([b_router_group, b_router_expert]), 0)
    gate_bias = _lane_row(jnp.concatenate([b_igate, b_fgate]), 0)
    a_row = _lane_row(-jnp.exp(a_log.astype(F32)), 2 * N_HEADS)
    dt_row = _lane_row(dt_bias, 2 * N_HEADS)

    xp2 = x_prompt.reshape(n_p, d)
    xs2 = jnp.pad(x_sample, ((0, 0), (0, ss_pad - ss), (0, 0))).reshape(n_s, d)

    proj, gs = _norm_proj(xp2, xs2, norm_mix.reshape(1, d), w_big, w_small, tm=cfg["tm_proj"])

    mem2 = mem_prompt.reshape(bp * mem_len, d)
    mem_kv, _ = _norm_proj(mem2, None, mem_norm.reshape(1, d),
                           _col_blocks(w_mem_kv.astype(BF16), W),
                           jnp.zeros((d, LANES), BF16), tm=min(cfg["tm_proj"], bp * mem_len))

    mnorm = mlstm_norm.reshape(1, W)
    gnorm = gdn_norm.reshape(1, HEAD_DIM)
    s_off = n_p // (gs_ * ss_pad)
    ha_p, c_p, nn_p, mm_p = _mlstm(proj, gs, gate_bias, mnorm, None, n_batch=bp, seq=sp,
                                   s_valid=sp, row_off=0, L=cfg["chunk_m"], n_sub=cfg["sub_m"], G=1)
    m0 = jnp.broadcast_to(state_mlstm_m[:, :, None], (bs, N_HEADS, LANES))
    ha_s, c_s, nn_s, mm_s = _mlstm(proj, gs, gate_bias, mnorm,
                                   (state_mlstm_C, state_mlstm_n, m0), n_batch=bs, seq=ss_pad,
                                   s_valid=ss, row_off=s_off, L=ss_pad, n_sub=1, G=gs_)
    hb_p, s_p, cv_p = _gdn(proj, gs, conv_w, a_row, dt_row, gnorm, None, n_batch=bp, seq=sp,
                           s_valid=sp, row_off=0, L=cfg["chunk_g"], n_sub=cfg["sub_g"], G=1)
    conv0 = jnp.pad(state_gdn_conv, ((0, 0), (SUBLANES - (CONV_W - 1), 0), (0, 0)))
    hb_s, s_s, cv_s = _gdn(proj, gs, conv_w, a_row, dt_row, gnorm, (conv0, state_gdn_S),
                           n_batch=bs, seq=ss_pad, s_valid=ss, row_off=s_off, L=ss_pad,
                           n_sub=1, G=gs_)
    lt = min(sp, cfg["lt_x"])
    hc_p = _xattn(proj, mem_kv, mem_kv, n_batch=bp, seq=sp, row_off=0, lt=lt, G=1,
                  mem_len=mem_len, head_rows=False,
                  kv_specs=[pl.BlockSpec((None, mem_len, W), lambda b, t, c=c: (c, b, 0))
                            for c in range(2)])
    kv_rows = gs_ * mem_len * N_HEADS
    hc_s = _xattn(proj, cache_mem_k.reshape(bs * mem_len * N_HEADS, HEAD_DIM),
                  cache_mem_v.reshape(bs * mem_len * N_HEADS, HEAD_DIM),
                  n_batch=bs, seq=ss_pad, row_off=s_off, lt=ss_pad, G=gs_, mem_len=mem_len,
                  head_rows=True,
                  kv_specs=[pl.BlockSpec((kv_rows, HEAD_DIM), lambda b, t: (b, 0))] * 2)

    tm = cfg["tm"]
    x1rec, route = _merge((ha_p, ha_s), (hb_p, hb_s), (hc_p, hc_s), (xp2, xs2), proj,
                          w_br_mlstm.astype(BF16), w_br_gdn.astype(BF16), w_br_xattn.astype(BF16),
                          w_out.astype(BF16), norm_ffn.reshape(1, d), w_router, b_router, tm=tm)

    tm_moe = cfg["tm_moe"]
    n_all = n_p + n_s
    n_tiles = (n_all + N_CLASSES * (tm_moe - 1)) // tm_moe + 1
    cls = route[:, ROUTE_CLS].astype(jnp.int32)
    pos, ea, eb, n_used = _sort_plan(cls, tm=tm_moe, n_tiles=n_tiles)
    x_sorted = _sort_scatter(pos, x1rec, n_tiles * tm_moe, tm=tm)
    y_sorted = _moe(ea, eb, n_used, x_sorted, w_exp_gate, w_exp_up, w_exp_down,
                    norm_ffn.reshape(1, d),
                    final_norm.reshape(1, d), w_router, b_router, tm=tm_moe, n_tiles=n_tiles)
    y_p = _unsort(pos, y_sorted, base=0, n_rows=n_p, tm=tm)
    y_s = _unsort(pos, y_sorted, base=n_p, n_rows=n_s, tm=tm)

    y_prompt = y_p.reshape(bp, sp, d)
    y_sample = y_s.reshape(bs, ss_pad, d)[:, :ss]
    mk_p = mem_kv[0].reshape(bp, mem_len, N_HEADS, HEAD_DIM)
    mv_p = mem_kv[1].reshape(bp, mem_len, N_HEADS, HEAD_DIM)
    tail = slice(SUBLANES - (CONV_W - 1), SUBLANES)
    return (y_prompt, y_sample, mk_p, mv_p,
            c_p, nn_p, mm_p[:, :, 0], s_p, cv_p[:, tail],
            c_s, nn_s, mm_s[:, :, 0], s_s, cv_s[:, tail])


CONFIG = dict(tm_proj=1024, tm=512, tm_moe=256, chunk_m=256, sub_m=1, chunk_g=64, sub_g=8,
              lt_x=1024, sample_group=8)


def kernel(x_prompt, x_sample, mem_prompt, cache_mem_k, cache_mem_v, state_mlstm_C, state_mlstm_n, state_mlstm_m, state_gdn_S, state_gdn_conv, norm_mix, w_in, b_igate, b_fgate, mlstm_norm, conv_w, a_log, dt_bias, gdn_norm, mem_norm, w_mem_kv, w_br_mlstm, w_br_gdn, w_br_xattn, w_out, norm_ffn, w_router_group, b_router_group, w_router_expert, b_router_expert, w_exp_gate, w_exp_up, w_exp_down, final_norm):
    depth = w_in.shape[0]
    assert depth == 1, "one decoder layer"
    outs = _layer(
        x_prompt, x_sample, mem_prompt, cache_mem_k[0], cache_mem_v[0],
        state_mlstm_C[0], state_mlstm_n[0], state_mlstm_m[0], state_gdn_S[0], state_gdn_conv[0],
        norm_mix[0], w_in[0], b_igate[0], b_fgate[0], mlstm_norm[0], conv_w[0], a_log[0],
        dt_bias[0], gdn_norm[0], mem_norm[0], w_mem_kv[0], w_br_mlstm[0], w_br_gdn[0],
        w_br_xattn[0], w_out[0], norm_ffn[0], w_router_group[0], b_router_group[0],
        w_router_expert[0], b_router_expert[0], w_exp_gate[0], w_exp_up[0], w_exp_down[0],
        final_norm, cfg=CONFIG)
    y_prompt, y_sample = outs[0], outs[1]
    return (y_prompt, y_sample) + tuple(o[None] for o in outs[2:])
```

```python
import functools

import jax
import jax.numpy as jnp
from jax import lax
from jax.experimental import pallas as pl
from jax.experimental.pallas import tpu as pltpu

F32 = jnp.float32
BF16 = jnp.bfloat16
EPS = 1e-6

N_HEADS = 4
HEAD_DIM = 128
BRANCH_W = N_HEADS * HEAD_DIM
CONV_W = 4
N_GROUPS = 4
EXPERTS_PER_GROUP = 4
N_EXPERTS = N_GROUPS * EXPERTS_PER_GROUP
N_PAIRS = 6
N_CLASSES = N_GROUPS * N_PAIRS
PAIR_LO = (0, 0, 0, 1, 1, 2)
PAIR_HI = (1, 2, 3, 2, 3, 3)
NEG_BIG = -1e30
LANES = 128
SUBLANES = 8
TOKEN_ROWS = 8
PROJ_GROUP = 3
VMEM_LIMIT = 48 * 1024 * 1024

COL_GATE = 0
COL_MQ, COL_MK, COL_MV, COL_MO = 6, 7, 8, 9
COL_GQ, COL_GK, COL_GV, COL_GO, COL_XQ = 10, 11, 12, 13, 14
N_PROJ = 15 * BRANCH_W


def _dot(a, b):
    return jnp.dot(a.astype(BF16), b.astype(BF16), preferred_element_type=F32)


def _dot_nt(a, b):
    return lax.dot_general(a.astype(BF16), b.astype(BF16), (((1,), (1,)), ((), ())),
                           preferred_element_type=F32)


def _dot_tn(a, b):
    return lax.dot_general(a.astype(BF16), b.astype(BF16), (((0,), (0,)), ((), ())),
                           preferred_element_type=F32)


def _dot_exact(a, b):
    return jnp.dot(a, b, preferred_element_type=F32, precision=lax.Precision.HIGHEST)


def _lanes_to_rows(x, lane0):
    r = lax.broadcasted_iota(jnp.int32, (SUBLANES, LANES), 0)
    c = lax.broadcasted_iota(jnp.int32, (SUBLANES, LANES), 1)
    sel = (c == r + lane0).astype(F32)
    return lax.dot_general(sel, x, (((1,), (1,)), ((), ())), preferred_element_type=F32,
                           precision=lax.Precision.HIGHEST)


def _rms(x, g):
    return x * lax.rsqrt(jnp.mean(x * x, axis=-1, keepdims=True) + EPS) * g


def _softplus(x):
    return jnp.maximum(x, 0.0) + jnp.log1p(jnp.exp(-jnp.abs(x)))


def _sigmoid(x):
    return 0.5 * jnp.tanh(0.5 * x) + 0.5


def _silu(x):
    return x * _sigmoid(x)


def _tri(n, strict=False):
    r = lax.broadcasted_iota(jnp.int32, (n, n), 0)
    c = lax.broadcasted_iota(jnp.int32, (n, n), 1)
    return (c < r) if strict else (c <= r)


def _tri_blocks(n, block, upper=False):
    shift = block.bit_length() - 1
    assert block == 1 << shift
    r = lax.broadcasted_iota(jnp.int32, (n, n), 0)
    c = lax.broadcasted_iota(jnp.int32, (n, n), 1)
    same = jnp.right_shift(r, shift) == jnp.right_shift(c, shift)
    return same & ((r <= c) if upper else (c <= r))


def _pad_rows(shape, t, seq_rows, n_valid):
    assert seq_rows & (seq_rows - 1) == 0
    row = lax.broadcasted_iota(jnp.int32, shape, 0)
    return (row & (seq_rows - 1)) + t * seq_rows >= n_valid


def _params(sem):
    return pltpu.CompilerParams(dimension_semantics=sem, vmem_limit_bytes=VMEM_LIMIT)


def _two_source_specs(n_first, block, width):
    first = pl.BlockSpec((block, width), lambda i, *_: (jnp.minimum(i, n_first - 1), 0))
    second = pl.BlockSpec((block, width), lambda i, *_: (jnp.maximum(i - n_first, 0), 0))
    return first, second


def _norm_proj_kernel(xa_ref, xb_ref, g_ref, w_ref, ws_ref, o_ref, os_ref, hb_ref, *, n_first):
    i = pl.program_id(0)
    j = pl.program_id(1)

    @pl.when(j == 0)
    def _():
        @pl.when(i < n_first)
        def _():
            hb_ref[...] = _rms(xa_ref[...], g_ref[...]).astype(BF16)

        @pl.when(i >= n_first)
        def _():
            hb_ref[...] = _rms(xb_ref[...], g_ref[...]).astype(BF16)

        os_ref[...] = jnp.dot(hb_ref[...], ws_ref[...], preferred_element_type=F32)

    o_ref[...] = jnp.dot(hb_ref[...], w_ref[0], preferred_element_type=F32)


def _proj_spec(rows, c, row_block):
    return pl.BlockSpec((None, rows, BRANCH_W),
                        lambda *g: (c // PROJ_GROUP, row_block(*g), c % PROJ_GROUP))


def _col_blocks(w, tn):
    d, n = w.shape
    return w.reshape(d, n // tn, tn).transpose(1, 0, 2)


def _norm_proj(xa, xb, g, w, ws, *, tm):
    d = xa.shape[1]
    n_first = xa.shape[0] // tm
    n_rows = xa.shape[0] + (0 if xb is None else xb.shape[0])
    if xb is None:
        xb = xa
    tn = w.shape[2]
    n = w.shape[0] * tn
    sa, sb = _two_source_specs(n_first, tm, d)
    return pl.pallas_call(
        functools.partial(_norm_proj_kernel, n_first=n_first),
        out_shape=(jax.ShapeDtypeStruct((n // tn, n_rows, tn), F32),
                   jax.ShapeDtypeStruct((n_rows, ws.shape[1]), F32)),
        grid=(n_rows // tm, n // tn),
        in_specs=[sa, sb,
                  pl.BlockSpec((1, d), lambda i, j: (0, 0)),
                  pl.BlockSpec((1, d, tn), lambda i, j: (j, 0, 0)),
                  pl.BlockSpec((d, ws.shape[1]), lambda i, j: (0, 0))],
        out_specs=(pl.BlockSpec((None, tm, tn), lambda i, j: (j, i, 0)),
                   pl.BlockSpec((tm, ws.shape[1]), lambda i, j: (i, 0))),
        scratch_shapes=[pltpu.VMEM((tm, d), BF16)],
        compiler_params=_params(("parallel", "arbitrary")),
        name="norm_proj",
    )(xa, xb, g, w, ws)


def _seq_tile_specs(G, Lt, nt, row_off):
    assert G == 1 or nt == 1

    def rows(b, t):
        return row_off + b * nt + t

    return rows


def _mlstm_kernel(*refs, L, n_sub, G, s_valid, s_total, has_init):
    if has_init:
        (q_ref, k_ref, v_ref, og_ref, gs_ref, bias_ref, norm_ref, c0_ref, n0_ref, m0_ref,
         h_ref, c_ref, n_ref, m_ref) = refs
    else:
        (q_ref, k_ref, v_ref, og_ref, gs_ref, bias_ref, norm_ref,
         h_ref, c_ref, n_ref, m_ref) = refs
    t = pl.program_id(1)
    Lt = n_sub * L
    rows = G * Lt

    @pl.when(t == 0)
    def _():
        if has_init:
            c_ref[...] = c0_ref[...]
            n_ref[...] = n0_ref[...]
            m_ref[...] = m0_ref[...]
        else:
            c_ref[...] = jnp.zeros_like(c_ref)
            n_ref[...] = jnp.zeros_like(n_ref)
            m_ref[...] = jnp.zeros_like(m_ref)

    z = gs_ref[...] + bias_ref[...]
    lane = lax.broadcasted_iota(jnp.int32, z.shape, 1)
    log_f = jnp.minimum(z, 0.0) - jnp.log1p(jnp.exp(-jnp.abs(z)))
    g = jnp.where((lane >= N_HEADS) & (lane < 2 * N_HEADS), log_f, z)
    if s_valid < s_total:
        pad = _pad_rows(z.shape, t, Lt, s_valid)
        g = jnp.where(pad, jnp.where(lane < N_HEADS, NEG_BIG, 0.0), g)
    g_t = _lanes_to_rows(g, 0)
    bt_cols = _dot_exact(_tri_blocks(rows, L).astype(F32), g)
    bt_rows = _dot_exact(g_t, _tri_blocks(rows, L, upper=True).astype(F32))
    scale = HEAD_DIM ** -0.5
    assert G == 1 or n_sub == 1
    n_r = G * L
    mask = _tri_blocks(n_r, L)

    def per_seq(x):
        return [x[j * L:(j + 1) * L] for j in range(G)]

    def expand(pieces):
        if G == 1:
            return pieces[0]
        return jnp.concatenate([jnp.broadcast_to(x, (L, x.shape[1])) for x in pieces], axis=0)

    for c in range(n_sub):
        r = slice(c * L, c * L + n_r)
        P = []
        for h in range(N_HEADS):
            sl = slice(h * HEAD_DIM, (h + 1) * HEAD_DIM)
            p = dict(h=h, sl=sl, q=q_ref[r, sl], k=k_ref[r, sl] * scale, v=v_ref[r, sl],
                     bt_c=bt_cols[r, N_HEADS + h:N_HEADS + h + 1], it_c=g[r, h:h + 1])
            bt_r = bt_rows[N_HEADS + h:N_HEADS + h + 1, r]
            it_r = g_t[h:h + 1, r]
            p["log_d"] = jnp.where(mask, p["bt_c"] - bt_r + it_r, -jnp.inf)
            p["c_prev"] = [c_ref[s, h] for s in range(G)]
            p["n_prev"] = [n_ref[s, h:h + 1, :] for s in range(G)]
            p["m_prev"] = [m_ref[s, h:h + 1, 0:1] for s in range(G)]
            P.append(p)
        for p in P:
            p["qk"] = _dot_nt(p["q"], p["k"])
        for p in P:
            qs = per_seq(p["q"])
            qc = [_dot(qs[s], p["c_prev"][s]) for s in range(G)]
            p["qc"] = qc[0] if G == 1 else jnp.concatenate(qc, axis=0)
        for p in P:
            inter = p["bt_c"] + expand(p["m_prev"])
            p["m_t"] = jnp.maximum(inter, jnp.max(p["log_d"], axis=-1, keepdims=True))
            p["inter_w"] = jnp.exp(inter - p["m_t"])
            p["sd"] = p["qk"] * jnp.exp(p["log_d"] - p["m_t"])
        for p in P:
            p["sv"] = _dot(p["sd"], p["v"])
        for p in P:
            bt_last = [x[L - 1:L, :] for x in per_seq(p["bt_c"])]
            p["m_new"] = [x[L - 1:L, :] for x in per_seq(p["m_t"])]
            w = jnp.exp(expand(bt_last) - p["bt_c"] + p["it_c"] - expand(p["m_new"]))
            p["decay"] = [jnp.exp(bt_last[s] + p["m_prev"][s] - p["m_new"][s]) for s in range(G)]
            p["kw"] = p["k"] * w
        for p in P:
            kws, vs = per_seq(p["kw"]), per_seq(p["v"])
            p["kv"] = [_dot_tn(kws[s], vs[s]) for s in range(G)]
            p["n_new"] = [p["decay"][s] * p["n_prev"][s] + jnp.sum(kws[s], axis=0, keepdims=True)
                          for s in range(G)]
        for p in P:
            num = p["sv"] + p["qc"] * p["inter_w"]
            den = (jnp.sum(p["sd"], axis=-1, keepdims=True)
                   + p["inter_w"] * jnp.sum(p["q"] * expand(p["n_prev"]), axis=-1, keepdims=True))
            hh = num / jnp.maximum(jnp.abs(den), jnp.exp(-p["m_t"]))
            hn = hh * lax.rsqrt(jnp.mean(hh * hh, axis=-1, keepdims=True) + EPS) * norm_ref[:, p["sl"]]
            p["h_out"] = hn * _sigmoid(og_ref[r, p["sl"]])
        for p in P:
            h = p["h"]
            h_ref[r, p["sl"]] = p["h_out"]
            for s in range(G):
                n_ref[s, h:h + 1, :] = p["n_new"][s]
                m_ref[s, h:h + 1, :] = jnp.broadcast_to(p["m_new"][s], (1, LANES))
                c_ref[s, h] = p["decay"][s] * p["c_prev"][s] + p["kv"][s]


def _mlstm(proj, gs, bias_row, norm_row, init, *, n_batch, seq, s_valid, row_off, L, n_sub, G):
    Lt = n_sub * L
    nt = seq // Lt
    rows = G * Lt
    has_init = init is not None
    blk = _seq_tile_specs(G, Lt, nt, row_off)

    def col(c):
        return _proj_spec(rows, c, blk)

    in_specs = [col(COL_MQ), col(COL_MK), col(COL_MV), col(COL_MO),
                pl.BlockSpec((rows, LANES), lambda b, t: (blk(b, t), 0)),
                pl.BlockSpec((1, LANES), lambda b, t: (0, 0)),
                pl.BlockSpec((1, BRANCH_W), lambda b, t: (0, 0))]
    args = [proj, proj, proj, proj, gs, bias_row, norm_row]
    c_spec = pl.BlockSpec((G, N_HEADS, HEAD_DIM, HEAD_DIM), lambda b, t: (b, 0, 0, 0))
    n_spec = pl.BlockSpec((G, N_HEADS, HEAD_DIM), lambda b, t: (b, 0, 0))
    m_spec = pl.BlockSpec((G, N_HEADS, LANES), lambda b, t: (b, 0, 0))
    if has_init:
        in_specs += [c_spec, n_spec, m_spec]
        args += list(init)
    return pl.pallas_call(
        functools.partial(_mlstm_kernel, L=L, n_sub=n_sub, G=G, s_valid=s_valid, s_total=seq,
                          has_init=has_init),
        out_shape=(jax.ShapeDtypeStruct((n_batch * seq, BRANCH_W), F32),
                   jax.ShapeDtypeStruct((n_batch, N_HEADS, HEAD_DIM, HEAD_DIM), F32),
                   jax.ShapeDtypeStruct((n_batch, N_HEADS, HEAD_DIM), F32),
                   jax.ShapeDtypeStruct((n_batch, N_HEADS, LANES), F32)),
        grid=(n_batch // G, nt),
        in_specs=in_specs,
        out_specs=(pl.BlockSpec((rows, BRANCH_W), lambda b, t: (b * nt + t, 0)),
                   c_spec, n_spec, m_spec),
        compiler_params=_params(("parallel", "arbitrary")),
        name="mlstm",
    )(*args)


def _gdn_kernel(*refs, L, n_sub, G, s_valid, s_total, has_init):
    if has_init:
        (q_ref, k_ref, v_ref, og_ref, gs_ref, cw_ref, arow_ref, dtrow_ref, norm_ref,
         conv0_ref, s0_ref, o_ref, s_ref, conv1_ref, ext_ref, conv_ref) = refs
    else:
        (q_ref, k_ref, v_ref, og_ref, gs_ref, cw_ref, arow_ref, dtrow_ref, norm_ref,
         o_ref, s_ref, conv1_ref, ext_ref, conv_ref) = refs
    t = pl.program_id(1)
    Lt = n_sub * L
    rows = G * Lt
    nt = s_total // Lt
    W = BRANCH_W

    @pl.when(t == 0)
    def _():
        if has_init:
            s_ref[...] = s0_ref[...]
            ext_ref[:, 0:SUBLANES, :] = conv0_ref[...]
        else:
            s_ref[...] = jnp.zeros_like(s_ref)
            ext_ref[:, 0:SUBLANES, :] = jnp.zeros((G, SUBLANES, 3 * W), F32)

    @pl.when(t > 0)
    def _():
        ext_ref[:, 0:SUBLANES, :] = ext_ref[:, Lt:Lt + SUBLANES, :]

    base = SUBLANES - (CONV_W - 1)
    for s in range(G):
        rs = slice(s * Lt, (s + 1) * Lt)
        ext_ref[s, SUBLANES:SUBLANES + Lt, 0:W] = q_ref[rs, :]
        ext_ref[s, SUBLANES:SUBLANES + Lt, W:2 * W] = k_ref[rs, :]
        ext_ref[s, SUBLANES:SUBLANES + Lt, 2 * W:3 * W] = v_ref[rs, :]
        acc = ext_ref[s, base:base + Lt, :] * cw_ref[0:1, :]
        for j in range(1, CONV_W):
            acc = acc + ext_ref[s, base + j:base + j + Lt, :] * cw_ref[j:j + 1, :]
        conv_ref[rs, :] = _silu(acc)

    @pl.when(t == nt - 1)
    def _():
        v_last = s_valid - (nt - 1) * Lt
        conv1_ref[...] = ext_ref[:, v_last:v_last + SUBLANES, :]

    z = gs_ref[...]
    lane = lax.broadcasted_iota(jnp.int32, z.shape, 1)
    gdec = arow_ref[...] * _softplus(z + dtrow_ref[...])
    gb = jnp.where((lane >= 2 * N_HEADS) & (lane < 3 * N_HEADS), gdec,
                   jnp.where((lane >= 3 * N_HEADS) & (lane < 4 * N_HEADS), _sigmoid(z), 0.0))
    if s_valid < s_total:
        gb = jnp.where(_pad_rows(z.shape, t, Lt, s_valid), 0.0, gb)
    gb_t = _lanes_to_rows(gb, 2 * N_HEADS)
    gam_cols = _dot_exact(_tri_blocks(rows, L).astype(F32), gb)
    gam_rows = _dot_exact(gb_t, _tri_blocks(rows, L, upper=True).astype(F32))
    incl = _tri(L)
    strict = _tri(L, strict=True)
    eye = (lax.broadcasted_iota(jnp.int32, (L, L), 0)
           == lax.broadcasted_iota(jnp.int32, (L, L), 1)).astype(F32)

    P = []
    for s in range(G):
        for c in range(n_sub):
            for h in range(N_HEADS):
                r = slice(s * Lt + c * L, s * Lt + (c + 1) * L)
                sl = slice(h * HEAD_DIM, (h + 1) * HEAD_DIM)
                q = conv_ref[r, h * HEAD_DIM:(h + 1) * HEAD_DIM]
                k = conv_ref[r, W + h * HEAD_DIM:W + (h + 1) * HEAD_DIM]
                v = conv_ref[r, 2 * W + h * HEAD_DIM:2 * W + (h + 1) * HEAD_DIM]
                q = q * lax.rsqrt(jnp.sum(q * q, axis=-1, keepdims=True) + EPS) * (HEAD_DIM ** -0.5)
                k = k * lax.rsqrt(jnp.sum(k * k, axis=-1, keepdims=True) + EPS)
                gam_c = gam_cols[r, 2 * N_HEADS + h:2 * N_HEADS + h + 1]
                gam_r = gam_rows[h:h + 1, r]
                beta_c = gb[r, 3 * N_HEADS + h:3 * N_HEADS + h + 1]
                decay = jnp.exp(jnp.where(incl, gam_c - gam_r, -jnp.inf))
                P.append(dict(s=s, c=c, h=h, r=r, sl=sl, q=q, k=k, v=v, gam_c=gam_c,
                              beta_c=beta_c, decay=decay))
    for p in P:
        p["kk"] = _dot_nt(p["k"], p["k"])
    for p in P:
        p["qk"] = _dot_nt(p["q"], p["k"]) * p["decay"]
    for p in P:
        e_gam = jnp.exp(p["gam_c"])
        gam_last = p["gam_c"][L - 1:L, :]
        p["pw"] = -jnp.where(strict, p["beta_c"] * p["kk"] * p["decay"], 0.0)
        p["x"] = eye + p["pw"]
        p["rhs"] = jnp.concatenate([p["beta_c"] * p["v"], p["beta_c"] * p["k"] * e_gam], axis=-1)
        p["q_dec"] = p["q"] * e_gam
        p["k_dec"] = p["k"] * jnp.exp(gam_last - p["gam_c"])
        p["e_last"] = jnp.exp(gam_last)
    n = 2
    while n < L:
        for p in P:
            p["pw"] = _dot(p["pw"], p["pw"])
        for p in P:
            p["x"] = p["x"] + _dot(p["x"], p["pw"])
        n *= 2
    for p in P:
        sol = _dot(p["x"], p["rhs"])
        p["u"] = sol[:, :HEAD_DIM]
        p["w"] = sol[:, HEAD_DIM:]

    for c in range(n_sub):
        Pc = [p for p in P if p["c"] == c]
        for p in Pc:
            p["s_prev"] = s_ref[p["s"], p["h"]]
            p["ws"] = _dot(p["w"], p["s_prev"])
        for p in Pc:
            p["qs"] = _dot(p["q_dec"], p["s_prev"])
        for p in Pc:
            p["v_new"] = p["u"] - p["ws"]
        for p in Pc:
            p["o"] = p["qs"] + _dot(p["qk"], p["v_new"])
        for p in Pc:
            p["kv"] = _dot_tn(p["k_dec"], p["v_new"])
        for p in Pc:
            o = p["o"]
            on = o * lax.rsqrt(jnp.mean(o * o, axis=-1, keepdims=True) + EPS) * norm_ref[...]
            p["o_out"] = on * _silu(og_ref[p["r"], p["sl"]])
        for p in Pc:
            s_ref[p["s"], p["h"]] = p["e_last"] * p["s_prev"] + p["kv"]
            o_ref[p["r"], p["sl"]] = p["o_out"]


def _gdn(proj, gs, conv_w, a_row, dt_row, norm_row, init, *, n_batch, seq, s_valid, row_off,
         L, n_sub, G):
    Lt = n_sub * L
    nt = seq // Lt
    rows = G * Lt
    has_init = init is not None
    blk = _seq_tile_specs(G, Lt, nt, row_off)

    def col(c):
        return _proj_spec(rows, c, blk)

    def const(shape):
        return pl.BlockSpec(shape, lambda b, t: (0,) * len(shape))

    in_specs = [col(COL_GQ), col(COL_GK), col(COL_GV), col(COL_GO),
                pl.BlockSpec((rows, LANES), lambda b, t: (blk(b, t), 0)),
                const((CONV_W, 3 * BRANCH_W)), const((1, LANES)), const((1, LANES)),
                const((1, HEAD_DIM))]
    args = [proj, proj, proj, proj, gs, conv_w, a_row, dt_row, norm_row]
    s_spec = pl.BlockSpec((G, N_HEADS, HEAD_DIM, HEAD_DIM), lambda b, t: (b, 0, 0, 0))
    conv_spec = pl.BlockSpec((G, SUBLANES, 3 * BRANCH_W), lambda b, t: (b, 0, 0))
    if has_init:
        in_specs += [conv_spec, s_spec]
        args += list(init)
    return pl.pallas_call(
        functools.partial(_gdn_kernel, L=L, n_sub=n_sub, G=G, s_valid=s_valid, s_total=seq,
                          has_init=has_init),
        out_shape=(jax.ShapeDtypeStruct((n_batch * seq, BRANCH_W), F32),
                   jax.ShapeDtypeStruct((n_batch, N_HEADS, HEAD_DIM, HEAD_DIM), F32),
                   jax.ShapeDtypeStruct((n_batch, SUBLANES, 3 * BRANCH_W), F32)),
        grid=(n_batch // G, nt),
        in_specs=in_specs,
        out_specs=(pl.BlockSpec((rows, BRANCH_W), lambda b, t: (b * nt + t, 0)),
                   s_spec, conv_spec),
        scratch_shapes=[pltpu.VMEM((G, Lt + SUBLANES, 3 * BRANCH_W), F32),
                        pltpu.VMEM((rows, 3 * BRANCH_W), F32)],
        compiler_params=_params(("parallel", "arbitrary")),
        name="gdn",
    )(*args)


def _xattn_kernel(q_ref, mk_ref, mv_ref, o_ref, *, G, lt, mem_len, head_rows):
    scale = HEAD_DIM ** -0.5

    def head(ref, s, h):
        if head_rows:
            return ref[pl.ds(s * mem_len * N_HEADS + h, mem_len, stride=N_HEADS), :]
        return ref[s * mem_len:(s + 1) * mem_len, h * HEAD_DIM:(h + 1) * HEAD_DIM]

    P = []
    for s in range(G):
        for h in range(N_HEADS):
            P.append(dict(s=s, h=h, r=slice(s * lt, (s + 1) * lt),
                          sl=slice(h * HEAD_DIM, (h + 1) * HEAD_DIM)))
    for p in P:
        p["sc"] = _dot_nt(q_ref[p["r"], p["sl"]], head(mk_ref, p["s"], p["h"])) * scale
    for p in P:
        e = jnp.exp(p["sc"] - jnp.max(p["sc"], axis=-1, keepdims=True))
        p["p"] = e / jnp.sum(e, axis=-1, keepdims=True)
    for p in P:
        o_ref[p["r"], p["sl"]] = _dot(p["p"], head(mv_ref, p["s"], p["h"]))


def _xattn(proj, mk, mv, *, n_batch, seq, row_off, lt, G, mem_len, kv_specs, head_rows):
    nt = seq // lt
    blk = _seq_tile_specs(G, lt, nt, row_off)
    return pl.pallas_call(
        functools.partial(_xattn_kernel, G=G, lt=lt, mem_len=mem_len, head_rows=head_rows),
        out_shape=jax.ShapeDtypeStruct((n_batch * seq, BRANCH_W), F32),
        grid=(n_batch // G, nt),
        in_specs=[_proj_spec(G * lt, COL_XQ, blk)] + list(kv_specs),
        out_specs=pl.BlockSpec((G * lt, BRANCH_W), lambda b, t: (b * nt + t, 0)),
        compiler_params=_params(("parallel", "arbitrary")),
        name="xattn",
    )(proj, mk, mv)


def _lane_first_max(x, mask, lane_f):
    xm = jnp.where(mask, x, -jnp.inf)
    mx = jnp.max(xm, axis=-1, keepdims=True)
    idx = jnp.min(jnp.where(xm == mx, lane_f, float(LANES)), axis=-1, keepdims=True)
    return mx, idx


def _store_token_tiles(ref, x, tok_rows):
    n = x.shape[0]
    for j in range(x.shape[1] // LANES):
        ref[pl.ds(j, n, stride=tok_rows), :] = x[:, j * LANES:(j + 1) * LANES]


def _load_token_tiles(ref, n, tok_rows, n_chunks=SUBLANES):
    return jnp.concatenate([ref[pl.ds(j, n, stride=tok_rows), :] for j in range(n_chunks)], axis=1)


def _merge_kernel(hap_ref, has_ref, hbp_ref, hbs_ref, hcp_ref, hcs_ref, xp_ref, xs_ref,
                  ga0_ref, ga1_ref, gb0_ref, gb1_ref, gc0_ref, gc1_ref,
                  wa_ref, wb_ref, wc_ref, wo_ref, nf_ref, wr_ref, br_ref,
                  x1_ref, route_ref, *, n_first):
    i = pl.program_id(0)

    def gate(lo_ref, hi_ref):
        return jnp.concatenate([_sigmoid(lo_ref[...]), _sigmoid(hi_ref[...])], axis=1)

    def body(ha_ref, hb_ref, hc_ref, x_ref):
        mixed = (gate(ga0_ref, ga1_ref) * _dot(ha_ref[...], wa_ref[...])
                 + gate(gb0_ref, gb1_ref) * _dot(hb_ref[...], wb_ref[...])
                 + gate(gc0_ref, gc1_ref) * _dot(hc_ref[...], wc_ref[...]))
        x1 = x_ref[...] + _dot(mixed, wo_ref[...])
        h2 = _rms(x1, nf_ref[...])
        logits = _dot(h2, wr_ref[...]) + br_ref[...]
        lane = lax.broadcasted_iota(jnp.int32, logits.shape, 1)
        lane_f = lane.astype(F32)
        is_g = lane < N_GROUPS
        _, g_idx = _lane_first_max(logits, is_g, lane_f)
        e_lo = N_GROUPS + EXPERTS_PER_GROUP * g_idx
        in_grp = (lane_f >= e_lo) & (lane_f < e_lo + EXPERTS_PER_GROUP)
        _, i1 = _lane_first_max(logits, in_grp, lane_f)
        _, i2 = _lane_first_max(logits, in_grp & (lane_f != i1), lane_f)
        lo = jnp.minimum(i1, i2) - e_lo
        hi = jnp.maximum(i1, i2) - e_lo
        pair = lo * (7.0 - lo) * 0.5 + (hi - lo - 1.0)
        cls = g_idx * float(N_PAIRS) + pair
        _store_token_tiles(x1_ref, x1, TOKEN_ROWS)
        route_ref[...] = jnp.where(lane == ROUTE_CLS, cls, 0.0)

    @pl.when(i < n_first)
    def _():
        body(hap_ref, hbp_ref, hcp_ref, xp_ref)

    @pl.when(i >= n_first)
    def _():
        body(has_ref, hbs_ref, hcs_ref, xs_ref)


def _merge(ha, hb, hc, xs, proj, wa, wb, wc, wo, nf, wr, br, *, tm):
    d = xs[0].shape[1]
    n_first = xs[0].shape[0] // tm
    n_rows = xs[0].shape[0] + xs[1].shape[0]
    specs = []
    for width in (BRANCH_W, BRANCH_W, BRANCH_W, d):
        specs += list(_two_source_specs(n_first, tm, width))

    def const(a):
        return pl.BlockSpec(a.shape, lambda i: (0,) * a.ndim)

    n_gate = 3 * d // BRANCH_W
    specs += [_proj_spec(tm, COL_GATE + c, lambda i: i) for c in range(n_gate)]
    weights = [wa, wb, wc, wo, nf, wr, br]
    specs += [const(a) for a in weights]
    return pl.pallas_call(
        functools.partial(_merge_kernel, n_first=n_first),
        out_shape=(jax.ShapeDtypeStruct((n_rows * TOKEN_ROWS, LANES), F32),
                   jax.ShapeDtypeStruct((n_rows, LANES), F32)),
        grid=(n_rows // tm,),
        in_specs=specs,
        out_specs=(pl.BlockSpec((tm * TOKEN_ROWS, LANES), lambda i: (i, 0)),
                   pl.BlockSpec((tm, LANES), lambda i: (i, 0))),
        compiler_params=_params(("parallel",)),
        name="merge",
    )(*ha, *hb, *hc, *xs, *([proj] * n_gate), *weights)


GATHER_UNROLL = 8
ROUTE_CLS = 0


def _record_gather_start(src_ref, dst_ref, sem, ids_ref, base, n, rec):
    def issue(r, carry):
        src = pl.multiple_of(ids_ref[base + r] * rec, rec)
        dst = pl.multiple_of(r * rec, rec)
        pltpu.make_async_copy(src_ref.at[pl.ds(src, rec)], dst_ref.at[pl.ds(dst, rec)], sem).start()
        return carry
    lax.fori_loop(0, n, issue, 0, unroll=GATHER_UNROLL)


def _record_gather_wait(src_ref, dst_ref, sem):
    pltpu.make_async_copy(src_ref.at[pl.ds(0, dst_ref.shape[0])], dst_ref, sem).wait()


def _sort_scatter_kernel(pos_ref, x_ref, init_hbm, out_hbm, sem, *, tm):
    del init_hbm
    base = pl.program_id(0) * tm

    def issue(r, carry):
        src = pl.multiple_of(r * TOKEN_ROWS, TOKEN_ROWS)
        dst = pl.multiple_of(pos_ref[base + r] * TOKEN_ROWS, TOKEN_ROWS)
        pltpu.make_async_copy(x_ref.at[pl.ds(src, TOKEN_ROWS)], out_hbm.at[pl.ds(dst, TOKEN_ROWS)],
                              sem).start()
        return carry
    lax.fori_loop(0, tm, issue, 0, unroll=GATHER_UNROLL)
    pltpu.make_async_copy(x_ref, out_hbm.at[pl.ds(0, tm * TOKEN_ROWS)], sem).wait()


def _sort_scatter(pos, x1rec, n_sorted, *, tm):
    n_tok = x1rec.shape[0] // TOKEN_ROWS
    grid_spec = pltpu.PrefetchScalarGridSpec(
        num_scalar_prefetch=1,
        grid=(n_tok // tm,),
        in_specs=[pl.BlockSpec((tm * TOKEN_ROWS, LANES), lambda i, *_: (i, 0)),
                  pl.BlockSpec(memory_space=pl.ANY)],
        out_specs=pl.BlockSpec(memory_space=pl.ANY),
        scratch_shapes=[pltpu.SemaphoreType.DMA(())],
    )
    return pl.pallas_call(
        functools.partial(_sort_scatter_kernel, tm=tm),
        out_shape=jax.ShapeDtypeStruct((n_sorted * TOKEN_ROWS, LANES), F32),
        grid_spec=grid_spec,
        input_output_aliases={2: 0},
        compiler_params=_params(("arbitrary",)),
        name="sort_scatter",
    )(pos, x1rec, jnp.zeros((n_sorted * TOKEN_ROWS, LANES), F32))


def _moe_kernel(ea_ref, eb_ref, nused_ref, x_ref, wga_ref, wua_ref, wda_ref,
                wgb_ref, wub_ref, wdb_ref, nf_ref, fin_ref, wr_ref, br_ref, y_ref,
                wg_bf, wu_bf, wd_bf, *, tm, d):
    i = pl.program_id(0)
    n_used = nused_ref[0]

    prev = jnp.maximum(i - 1, 0)
    for slot, e_ref, srcs in ((0, ea_ref, (wga_ref, wua_ref, wda_ref)),
                              (1, eb_ref, (wgb_ref, wub_ref, wdb_ref))):
        @pl.when((i < n_used) & ((i == 0) | (e_ref[i] != e_ref[prev])))
        def _():
            for dst, src in zip((wg_bf, wu_bf, wd_bf), srcs):
                dst[slot] = src[0].astype(BF16)

    @pl.when(i < n_used)
    def _():
        x1 = _load_token_tiles(x_ref, tm, TOKEN_ROWS)
        h2 = _rms(x1, nf_ref[...]).astype(BF16)
        logits = jnp.dot(h2, wr_ref[...], preferred_element_type=F32) + br_ref[...]
        lane = lax.broadcasted_iota(jnp.int32, logits.shape, 1)

        def pick(l):
            return jnp.sum(jnp.where(lane == l, logits, 0.0), axis=-1, keepdims=True)

        ea, eb = ea_ref[i], eb_ref[i]
        is_g = lane < N_GROUPS
        g_max = jnp.max(jnp.where(is_g, logits, -jnp.inf), axis=-1, keepdims=True)
        g_den = jnp.sum(jnp.where(is_g, jnp.exp(logits - g_max), 0.0), axis=-1, keepdims=True)
        p_grp = jnp.exp(pick(ea // EXPERTS_PER_GROUP) - g_max) / g_den
        v_a, v_b = pick(N_GROUPS + ea), pick(N_GROUPS + eb)
        combs = (p_grp / (1.0 + jnp.exp(v_b - v_a)), p_grp / (1.0 + jnp.exp(v_a - v_b)))
        acc = jnp.zeros((tm, d), F32)
        for slot, comb in enumerate(combs):
            a = jnp.dot(h2, wg_bf[slot], preferred_element_type=F32)
            u = jnp.dot(h2, wu_bf[slot], preferred_element_type=F32)
            act = _silu(a) * u * comb
            acc = acc + _dot(act, wd_bf[slot])
        _store_token_tiles(y_ref, _rms(x1 + acc, fin_ref[...]), SUBLANES)

    @pl.when(i >= n_used)
    def _():
        y_ref[...] = jnp.zeros_like(y_ref)


def _moe(tile_ea, tile_eb, n_used, x_sorted, wg, wu, wd, nf, fin, wr, br, *, tm, n_tiles):
    d = nf.shape[1]
    f = wg.shape[2]
    assert d == SUBLANES * LANES

    def expert(shape, which):
        return pl.BlockSpec((1,) + shape, lambda i, ea, eb, nu: ((ea, eb)[which][i], 0, 0))

    def const(a):
        return pl.BlockSpec(a.shape, lambda i, *_: (0,) * a.ndim)

    grid_spec = pltpu.PrefetchScalarGridSpec(
        num_scalar_prefetch=3,
        grid=(n_tiles,),
        in_specs=[pl.BlockSpec((tm * TOKEN_ROWS, LANES), lambda i, *_: (i, 0)),
                  expert((d, f), 0), expert((d, f), 0), expert((f, d), 0),
                  expert((d, f), 1), expert((d, f), 1), expert((f, d), 1),
                  const(nf), const(fin), const(wr), const(br)],
        out_specs=pl.BlockSpec((tm * SUBLANES, LANES), lambda i, *_: (i, 0)),
        scratch_shapes=[pltpu.VMEM((2, d, f), BF16), pltpu.VMEM((2, d, f), BF16),
                        pltpu.VMEM((2, f, d), BF16)],
    )
    return pl.pallas_call(
        functools.partial(_moe_kernel, tm=tm, d=d),
        out_shape=jax.ShapeDtypeStruct((n_tiles * tm * SUBLANES, LANES), F32),
        grid_spec=grid_spec,
        compiler_params=_params(("arbitrary",)),
        name="moe",
    )(tile_ea, tile_eb, n_used, x_sorted, wg, wu, wd, wg, wu, wd, nf, fin, wr, br)


def _unsort_kernel(pos_ref, y_hbm, o_ref, buf, sem, *, tm, base):
    i = pl.program_id(0)
    slot = i % 2

    @pl.when(i == 0)
    def _():
        _record_gather_start(y_hbm, buf.at[0], sem.at[0], pos_ref, base, tm, SUBLANES)

    @pl.when(i + 1 < pl.num_programs(0))
    def _():
        _record_gather_start(y_hbm, buf.at[1 - slot], sem.at[1 - slot], pos_ref,
                             base + (i + 1) * tm, tm, SUBLANES)

    _record_gather_wait(y_hbm, buf.at[slot], sem.at[slot])
    o_ref[...] = _load_token_tiles(buf.at[slot], tm, SUBLANES)


def _unsort(pos, y_rec, *, base, n_rows, tm):
    d = SUBLANES * LANES
    grid_spec = pltpu.PrefetchScalarGridSpec(
        num_scalar_prefetch=1,
        grid=(n_rows // tm,),
        in_specs=[pl.BlockSpec(memory_space=pl.ANY)],
        out_specs=pl.BlockSpec((tm, d), lambda i, *_: (i, 0)),
        scratch_shapes=[pltpu.VMEM((2, tm * SUBLANES, LANES), F32),
                        pltpu.SemaphoreType.DMA((2,))],
    )
    return pl.pallas_call(
        functools.partial(_unsort_kernel, tm=tm, base=base),
        out_shape=jax.ShapeDtypeStruct((n_rows, d), F32),
        grid_spec=grid_spec,
        compiler_params=_params(("arbitrary",)),
        name="unsort",
    )(pos, y_rec)


def _sort_plan(cls, *, tm, n_tiles):
    onehot = (cls[:, None] == jnp.arange(N_CLASSES, dtype=jnp.int32)[None, :]).astype(jnp.int32)
    csum = jnp.cumsum(onehot, axis=0)
    rank = jnp.sum((csum - onehot) * onehot, axis=1)
    counts = csum[-1]
    tiles_per = (counts + tm - 1) // tm
    tile_end = jnp.cumsum(tiles_per)
    tile_start = tile_end - tiles_per
    pos = (jnp.sum(tile_start[None, :] * onehot, axis=1) * tm + rank).astype(jnp.int32)
    n_used = tile_end[-1].astype(jnp.int32)
    tile = jnp.minimum(jnp.arange(n_tiles, dtype=jnp.int32), n_used - 1)
    tile_cls = jnp.sum((tile[:, None] >= tile_end[None, :]).astype(jnp.int32), axis=1)
    grp = tile_cls // N_PAIRS
    pair = tile_cls % N_PAIRS
    lo = jnp.asarray(PAIR_LO, jnp.int32)[pair]
    hi = jnp.asarray(PAIR_HI, jnp.int32)[pair]
    ea = (grp * EXPERTS_PER_GROUP + lo).astype(jnp.int32)
    eb = (grp * EXPERTS_PER_GROUP + hi).astype(jnp.int32)
    return pos, ea, eb, n_used.reshape(1)


def _lane_row(values, start):
    row = jnp.zeros((LANES,), F32)
    return row.at[start:start + values.shape[0]].set(values.astype(F32)).reshape(1, LANES)


def _layer(x_prompt, x_sample, mem_prompt, cache_mem_k, cache_mem_v,
           state_mlstm_C, state_mlstm_n, state_mlstm_m, state_gdn_S, state_gdn_conv,
           norm_mix, w_in, b_igate, b_fgate, mlstm_norm, conv_w, a_log, dt_bias, gdn_norm,
           mem_norm, w_mem_kv, w_br_mlstm, w_br_gdn, w_br_xattn, w_out, norm_ffn,
           w_router_group, b_router_group, w_router_expert, b_router_expert,
           w_exp_gate, w_exp_up, w_exp_down, final_norm, *, cfg):
    bp, sp, d = x_prompt.shape
    bs, ss, _ = x_sample.shape
    mem_len = mem_prompt.shape[1]
    ss_pad = SUBLANES
    assert ss <= ss_pad and d % LANES == 0
    n_p = bp * sp
    n_s = bs * ss_pad
    W = BRANCH_W
    gs_ = cfg["sample_group"]

    sizes = (W, W, W, N_HEADS, N_HEADS, W, 3 * W, N_HEADS, N_HEADS, W, W, d, d, d)
    offs = [0]
    for s in sizes:
        offs.append(offs[-1] + s)
    (m_q, m_k, m_v, m_i, m_f, m_o, g_qkv, g_a, g_b, g_out, x_q, gate_a, gate_b, gate_c) = [
        w_in[:, offs[j]:offs[j + 1]] for j in range(len(sizes))]
    w_big = _col_blocks(
        jnp.concatenate([gate_a, gate_b, gate_c, m_q, m_k, m_v, m_o, g_qkv, g_out, x_q],
                        axis=1).astype(BF16), PROJ_GROUP * W)
    w_small = jnp.concatenate(
        [m_i, m_f, g_a, g_b, jnp.zeros((d, LANES - 4 * N_HEADS), w_in.dtype)], axis=1).astype(BF16)
    w_router = jnp.concatenate(
        [w_router_group, w_router_expert,
         jnp.zeros((d, LANES - N_GROUPS - N_EXPERTS), w_in.dtype)], axis=1).astype(BF16)
    b_router = _lane_row(jnp.concatenate([b_router_group, b_router_expert]), 0)
    gate_bias = _lane_row(jnp.concatenate([b_igate, b_fgate]), 0)
    a_row = _lane_row(-jnp.exp(a_log.astype(F32)), 2 * N_HEADS)
    dt_row = _lane_row(dt_bias, 2 * N_HEADS)

    xp2 = x_prompt.reshape(n_p, d)
    xs2 = jnp.pad(x_sample, ((0, 0), (0, ss_pad - ss), (0, 0))).reshape(n_s, d)

    proj, gs = _norm_proj(xp2, xs2, norm_mix.reshape(1, d), w_big, w_small, tm=cfg["tm_proj"])

    mem2 = mem_prompt.reshape(bp * mem_len, d)
    mem_kv, _ = _norm_proj(mem2, None, mem_norm.reshape(1, d),
                           _col_blocks(w_mem_kv.astype(BF16), W),
                           jnp.zeros((d, LANES), BF16), tm=min(cfg["tm_proj"], bp * mem_len))

    mnorm = mlstm_norm.reshape(1, W)
    gnorm = gdn_norm.reshape(1, HEAD_DIM)
    s_off = n_p // (gs_ * ss_pad)
    ha_p, c_p, nn_p, mm_p = _mlstm(proj, gs, gate_bias, mnorm, None, n_batch=bp, seq=sp,
                                   s_valid=sp, row_off=0, L=cfg["chunk_m"], n_sub=cfg["sub_m"], G=1)
    m0 = jnp.broadcast_to(state_mlstm_m[:, :, None], (bs, N_HEADS, LANES))
    ha_s, c_s, nn_s, mm_s = _mlstm(proj, gs, gate_bias, mnorm,
                                   (state_mlstm_C, state_mlstm_n, m0), n_batch=bs, seq=ss_pad,
                                   s_valid=ss, row_off=s_off, L=ss_pad, n_sub=1, G=gs_)
    hb_p, s_p, cv_p = _gdn(proj, gs, conv_w, a_row, dt_row, gnorm, None, n_batch=bp, seq=sp,
                           s_valid=sp, row_off=0, L=cfg["chunk_g"], n_sub=cfg["sub_g"], G=1)
    conv0 = jnp.pad(state_gdn_conv, ((0, 0), (SUBLANES - (CONV_W - 1), 0), (0, 0)))
    hb_s, s_s, cv_s = _gdn(proj, gs, conv_w, a_row, dt_row, gnorm, (conv0, state_gdn_S),
                           n_batch=bs, seq=ss_pad, s_valid=ss, row_off=s_off, L=ss_pad,
                           n_sub=1, G=gs_)
    lt = min(sp, cfg["lt_x"])
    hc_p = _xattn(proj, mem_kv, mem_kv, n_batch=bp, seq=sp, row_off=0, lt=lt, G=1,
                  mem_len=mem_len, head_rows=False,
                  kv_specs=[pl.BlockSpec((None, mem_len, W), lambda b, t, c=c: (c, b, 0))
                            for c in range(2)])
    kv_rows = gs_ * mem_len * N_HEADS
    hc_s = _xattn(proj, cache_mem_k.reshape(bs * mem_len * N_HEADS, HEAD_DIM),
                  cache_mem_v.reshape(bs * mem_len * N_HEADS, HEAD_DIM),
                  n_batch=bs, seq=ss_pad, row_off=s_off, lt=ss_pad, G=gs_, mem_len=mem_len,
                  head_rows=True,
                  kv_specs=[pl.BlockSpec((kv_rows, HEAD_DIM), lambda b, t: (b, 0))] * 2)

    tm = cfg["tm"]
    x1rec, route = _merge((ha_p, ha_s), (hb_p, hb_s), (hc_p, hc_s), (xp2, xs2), proj,
                          w_br_mlstm.astype(BF16), w_br_gdn.astype(BF16), w_br_xattn.astype(BF16),
                          w_out.astype(BF16), norm_ffn.reshape(1, d), w_router, b_router, tm=tm)

    tm_moe = cfg["tm_moe"]
    n_all = n_p + n_s
    n_tiles = (n_all + N_CLASSES * (tm_moe - 1)) // tm_moe + 1
    cls = route[:, ROUTE_CLS].astype(jnp.int32)
    pos, ea, eb, n_used = _sort_plan(cls, tm=tm_moe, n_tiles=n_tiles)
    x_sorted = _sort_scatter(pos, x1rec, n_tiles * tm_moe, tm=tm)
    y_sorted = _moe(ea, eb, n_used, x_sorted, w_exp_gate, w_exp_up, w_exp_down,
                    norm_ffn.reshape(1, d),
                    final_norm.reshape(1, d), w_router, b_router, tm=tm_moe, n_tiles=n_tiles)
    y_p = _unsort(pos, y_sorted, base=0, n_rows=n_p, tm=tm)
    y_s = _unsort(pos, y_sorted, base=n_p, n_rows=n_s, tm=tm)

    y_prompt = y_p.reshape(bp, sp, d)
    y_sample = y_s.reshape(bs, ss_pad, d)[:, :ss]
    mk_p = mem_kv[0].reshape(bp, mem_len, N_HEADS, HEAD_DIM)
    mv_p = mem_kv[1].reshape(bp, mem_len, N_HEADS, HEAD_DIM)
    tail = slice(SUBLANES - (CONV_W - 1), SUBLANES)
    return (y_prompt, y_sample, mk_p, mv_p,
            c_p, nn_p, mm_p[:, :, 0], s_p, cv_p[:, tail],
            c_s, nn_s, mm_s[:, :, 0], s_s, cv_s[:, tail])


CONFIG = dict(tm_proj=1024, tm=512, tm_moe=256, chunk_m=256, sub_m=1, chunk_g=64, sub_g=8,
              lt_x=1024, sample_group=8)


def kernel(x_prompt, x_sample, mem_prompt, cache_mem_k, cache_mem_v, state_mlstm_C, state_mlstm_n, state_mlstm_m, state_gdn_S, state_gdn_conv, norm_mix, w_in, b_igate, b_fgate, mlstm_norm, conv_w, a_log, dt_bias, gdn_norm, mem_norm, w_mem_kv, w_br_mlstm, w_br_gdn, w_br_xattn, w_out, norm_ffn, w_router_group, b_router_group, w_router_expert, b_router_expert, w_exp_gate, w_exp_up, w_exp_down, final_norm):
    depth = w_in.shape[0]
    assert depth == 1, "one decoder layer"
    outs = _layer(
        x_prompt, x_sample, mem_prompt, cache_mem_k[0], cache_mem_v[0],
        state_mlstm_C[0], state_mlstm_n[0], state_mlstm_m[0], state_gdn_S[0], state_gdn_conv[0],
        norm_mix[0], w_in[0], b_igate[0], b_fgate[0], mlstm_norm[0], conv_w[0], a_log[0],
        dt_bias[0], gdn_norm[0], mem_norm[0], w_mem_kv[0], w_br_mlstm[0], w_br_gdn[0],
        w_br_xattn[0], w_out[0], norm_ffn[0], w_router_group[0], b_router_group[0],
        w_router_expert[0], b_router_expert[0], w_exp_gate[0], w_exp_up[0], w_exp_down[0],
        final_norm, cfg=CONFIG)
    y_prompt, y_sample = outs[0], outs[1]
    return (y_prompt, y_sample) + tuple(o[None] for o in outs[2:])
```

```python
import functools

import jax
import jax.numpy as jnp
from jax import lax
from jax.experimental import pallas as pl
from jax.experimental.pallas import tpu as pltpu

F32 = jnp.float32
BF16 = jnp.bfloat16
EPS = 1e-6

N_HEADS = 4
HEAD_DIM = 128
BRANCH_W = N_HEADS * HEAD_DIM
CONV_W = 4
N_GROUPS = 4
EXPERTS_PER_GROUP = 4
N_EXPERTS = N_GROUPS * EXPERTS_PER_GROUP
N_PAIRS = 6
N_CLASSES = N_GROUPS * N_PAIRS
PAIR_LO = (0, 0, 0, 1, 1, 2)
PAIR_HI = (1, 2, 3, 2, 3, 3)
NEG_BIG = -1e30
LANES = 128
SUBLANES = 8
TOKEN_ROWS = 8
PROJ_GROUP = 5
VMEM_PHYSICAL = 64 * 1024 * 1024
VMEM_LIMIT = 48 * 1024 * 1024
VMEM_SLACK = 4 * 1024 * 1024

COL_GATE = 0
COL_MQ, COL_MK, COL_MV, COL_MO = 6, 7, 8, 9
COL_GQ, COL_GK, COL_GV, COL_GO, COL_XQ = 10, 11, 12, 13, 14
N_PROJ = 15 * BRANCH_W


def _dot(a, b):
    return jnp.dot(a.astype(BF16), b.astype(BF16), preferred_element_type=F32)


def _dot_nt(a, b):
    return lax.dot_general(a.astype(BF16), b.astype(BF16), (((1,), (1,)), ((), ())),
                           preferred_element_type=F32)


def _dot_tn(a, b):
    return lax.dot_general(a.astype(BF16), b.astype(BF16), (((0,), (0,)), ((), ())),
                           preferred_element_type=F32)


def _dot_exact(a, b):
    return jnp.dot(a, b, preferred_element_type=F32, precision=lax.Precision.HIGHEST)


def _lanes_to_rows(x, lane0):
    r = lax.broadcasted_iota(jnp.int32, (SUBLANES, LANES), 0)
    c = lax.broadcasted_iota(jnp.int32, (SUBLANES, LANES), 1)
    sel = (c == r + lane0).astype(F32)
    return lax.dot_general(sel, x, (((1,), (1,)), ((), ())), preferred_element_type=F32,
                           precision=lax.Precision.HIGHEST)


def _rms(x, g):
    return x * lax.rsqrt(jnp.mean(x * x, axis=-1, keepdims=True) + EPS) * g


def _softplus(x):
    return jnp.maximum(x, 0.0) + jnp.log1p(jnp.exp(-jnp.abs(x)))


def _sigmoid(x):
    return 0.5 * jnp.tanh(0.5 * x) + 0.5


def _silu(x):
    return x * _sigmoid(x)


def _tri(n, strict=False):
    r = lax.broadcasted_iota(jnp.int32, (n, n), 0)
    c = lax.broadcasted_iota(jnp.int32, (n, n), 1)
    return (c < r) if strict else (c <= r)


def _tri_blocks(n, block, upper=False):
    shift = block.bit_length() - 1
    assert block == 1 << shift
    r = lax.broadcasted_iota(jnp.int32, (n, n), 0)
    c = lax.broadcasted_iota(jnp.int32, (n, n), 1)
    same = jnp.right_shift(r, shift) == jnp.right_shift(c, shift)
    return same & ((r <= c) if upper else (c <= r))


def _pad_rows(shape, t, seq_rows, n_valid):
    assert seq_rows & (seq_rows - 1) == 0
    row = lax.broadcasted_iota(jnp.int32, shape, 0)
    return (row & (seq_rows - 1)) + t * seq_rows >= n_valid


def _params(sem, vmem_bytes=VMEM_LIMIT):
    return pltpu.CompilerParams(dimension_semantics=sem, vmem_limit_bytes=vmem_bytes)


def _two_source_specs(n_first, block, width):
    first = pl.BlockSpec((block, width), lambda i, *_: (jnp.minimum(i, n_first - 1), 0))
    second = pl.BlockSpec((block, width), lambda i, *_: (jnp.maximum(i - n_first, 0), 0))
    return first, second


def _norm_proj_kernel(xa_ref, xb_ref, g_ref, w_ref, ws_ref, o_ref, os_ref, hb_ref, *, n_first):
    i = pl.program_id(0)
    j = pl.program_id(1)

    @pl.when(j == 0)
    def _():
        @pl.when(i < n_first)
        def _():
            hb_ref[...] = _rms(xa_ref[...], g_ref[...]).astype(BF16)

        @pl.when(i >= n_first)
        def _():
            hb_ref[...] = _rms(xb_ref[...], g_ref[...]).astype(BF16)

        os_ref[...] = jnp.dot(hb_ref[...], ws_ref[...], preferred_element_type=F32)

    o_ref[...] = jnp.dot(hb_ref[...], w_ref[0], preferred_element_type=F32)


def _proj_spec(rows, c, row_block):
    return pl.BlockSpec((None, rows, BRANCH_W),
                        lambda *g: (c // PROJ_GROUP, row_block(*g), c % PROJ_GROUP))


def _col_blocks(w, tn):
    d, n = w.shape
    return w.reshape(d, n // tn, tn).transpose(1, 0, 2)


def _norm_proj(xa, xb, g, w, ws, *, tm):
    d = xa.shape[1]
    n_first = xa.shape[0] // tm
    n_rows = xa.shape[0] + (0 if xb is None else xb.shape[0])
    if xb is None:
        xb = xa
    tn = w.shape[2]
    n = w.shape[0] * tn
    sa, sb = _two_source_specs(n_first, tm, d)
    vmem = (4 * tm * d * 4 + 2 * d * tn * 2 + 2 * tm * tn * 4 + tm * d * 2
            + 2 * (tm + d) * ws.shape[1] * 4)
    vmem = min(max(vmem + VMEM_SLACK, VMEM_LIMIT), VMEM_PHYSICAL - VMEM_SLACK)
    return pl.pallas_call(
        functools.partial(_norm_proj_kernel, n_first=n_first),
        out_shape=(jax.ShapeDtypeStruct((n // tn, n_rows, tn), F32),
                   jax.ShapeDtypeStruct((n_rows, ws.shape[1]), F32)),
        grid=(n_rows // tm, n // tn),
        in_specs=[sa, sb,
                  pl.BlockSpec((1, d), lambda i, j: (0, 0)),
                  pl.BlockSpec((1, d, tn), lambda i, j: (j, 0, 0)),
                  pl.BlockSpec((d, ws.shape[1]), lambda i, j: (0, 0))],
        out_specs=(pl.BlockSpec((None, tm, tn), lambda i, j: (j, i, 0)),
                   pl.BlockSpec((tm, ws.shape[1]), lambda i, j: (i, 0))),
        scratch_shapes=[pltpu.VMEM((tm, d), BF16)],
        compiler_params=_params(("parallel", "arbitrary"), vmem),
        name="norm_proj",
    )(xa, xb, g, w, ws)


def _seq_tile_specs(G, Lt, nt, row_off):
    assert G == 1 or nt == 1

    def rows(b, t):
        return row_off + b * nt + t

    return rows


def _mlstm_kernel(*refs, L, n_sub, G, s_valid, s_total, has_init):
    if has_init:
        (q_ref, k_ref, v_ref, og_ref, gs_ref, bias_ref, norm_ref, c0_ref, n0_ref, m0_ref,
         h_ref, c_ref, n_ref, m_ref) = refs
    else:
        (q_ref, k_ref, v_ref, og_ref, gs_ref, bias_ref, norm_ref,
         h_ref, c_ref, n_ref, m_ref) = refs
    t = pl.program_id(1)
    Lt = n_sub * L
    rows = G * Lt

    @pl.when(t == 0)
    def _():
        if has_init:
            c_ref[...] = c0_ref[...]
            n_ref[...] = n0_ref[...]
            m_ref[...] = m0_ref[...]
        else:
            c_ref[...] = jnp.zeros_like(c_ref)
            n_ref[...] = jnp.zeros_like(n_ref)
            m_ref[...] = jnp.zeros_like(m_ref)

    z = gs_ref[...] + bias_ref[...]
    lane = lax.broadcasted_iota(jnp.int32, z.shape, 1)
    log_f = jnp.minimum(z, 0.0) - jnp.log1p(jnp.exp(-jnp.abs(z)))
    g = jnp.where((lane >= N_HEADS) & (lane < 2 * N_HEADS), log_f, z)
    if s_valid < s_total:
        pad = _pad_rows(z.shape, t, Lt, s_valid)
        g = jnp.where(pad, jnp.where(lane < N_HEADS, NEG_BIG, 0.0), g)
    g_t = _lanes_to_rows(g, 0)
    bt_cols = _dot_exact(_tri_blocks(rows, L).astype(F32), g)
    bt_rows = _dot_exact(g_t, _tri_blocks(rows, L, upper=True).astype(F32))
    scale = HEAD_DIM ** -0.5
    assert G == 1 or n_sub == 1
    n_r = G * L
    mask = _tri_blocks(n_r, L)

    def per_seq(x):
        return [x[j * L:(j + 1) * L] for j in range(G)]

    def expand(pieces):
        if G == 1:
            return pieces[0]
        return jnp.concatenate([jnp.broadcast_to(x, (L, x.shape[1])) for x in pieces], axis=0)

    for c in range(n_sub):
        r = slice(c * L, c * L + n_r)
        P = []
        for h in range(N_HEADS):
            sl = slice(h * HEAD_DIM, (h + 1) * HEAD_DIM)
            p = dict(h=h, sl=sl, q=q_ref[r, sl], k=k_ref[r, sl] * scale, v=v_ref[r, sl],
                     bt_c=bt_cols[r, N_HEADS + h:N_HEADS + h + 1], it_c=g[r, h:h + 1])
            bt_r = bt_rows[N_HEADS + h:N_HEADS + h + 1, r]
            it_r = g_t[h:h + 1, r]
            p["log_d"] = jnp.where(mask, p["bt_c"] - bt_r + it_r, -jnp.inf)
            p["c_prev"] = [c_ref[s, h] for s in range(G)]
            p["n_prev"] = [n_ref[s, h:h + 1, :] for s in range(G)]
            p["m_prev"] = [m_ref[s, h:h + 1, 0:1] for s in range(G)]
            P.append(p)
        for p in P:
            p["qk"] = _dot_nt(p["q"], p["k"])
        for p in P:
            qs = per_seq(p["q"])
            qc = [_dot(qs[s], p["c_prev"][s]) for s in range(G)]
            p["qc"] = qc[0] if G == 1 else jnp.concatenate(qc, axis=0)
        for p in P:
            inter = p["bt_c"] + expand(p["m_prev"])
            p["m_t"] = jnp.maximum(inter, jnp.max(p["log_d"], axis=-1, keepdims=True))
            p["inter_w"] = jnp.exp(inter - p["m_t"])
            p["sd"] = p["qk"] * jnp.exp(p["log_d"] - p["m_t"])
        for p in P:
            p["sv"] = _dot(p["sd"], p["v"])
        for p in P:
            bt_last = [x[L - 1:L, :] for x in per_seq(p["bt_c"])]
            p["m_new"] = [x[L - 1:L, :] for x in per_seq(p["m_t"])]
            w = jnp.exp(expand(bt_last) - p["bt_c"] + p["it_c"] - expand(p["m_new"]))
            p["decay"] = [jnp.exp(bt_last[s] + p["m_prev"][s] - p["m_new"][s]) for s in range(G)]
            p["kw"] = p["k"] * w
        for p in P:
            kws, vs = per_seq(p["kw"]), per_seq(p["v"])
            p["kv"] = [_dot_tn(kws[s], vs[s]) for s in range(G)]
            p["n_new"] = [p["decay"][s] * p["n_prev"][s] + jnp.sum(kws[s], axis=0, keepdims=True)
                          for s in range(G)]
        for p in P:
            num = p["sv"] + p["qc"] * p["inter_w"]
            den = (jnp.sum(p["sd"], axis=-1, keepdims=True)
                   + p["inter_w"] * jnp.sum(p["q"] * expand(p["n_prev"]), axis=-1, keepdims=True))
            hh = num / jnp.maximum(jnp.abs(den), jnp.exp(-p["m_t"]))
            hn = hh * lax.rsqrt(jnp.mean(hh * hh, axis=-1, keepdims=True) + EPS) * norm_ref[:, p["sl"]]
            p["h_out"] = hn * _sigmoid(og_ref[r, p["sl"]])
        for p in P:
            h = p["h"]
            h_ref[r, p["sl"]] = p["h_out"]
            for s in range(G):
                n_ref[s, h:h + 1, :] = p["n_new"][s]
                m_ref[s, h:h + 1, :] = jnp.broadcast_to(p["m_new"][s], (1, LANES))
                c_ref[s, h] = p["decay"][s] * p["c_prev"][s] + p["kv"][s]


def _mlstm(proj, gs, bias_row, norm_row, init, *, n_batch, seq, s_valid, row_off, L, n_sub, G):
    Lt = n_sub * L
    nt = seq // Lt
    rows = G * Lt
    has_init = init is not None
    blk = _seq_tile_specs(G, Lt, nt, row_off)

    def col(c):
        return _proj_spec(rows, c, blk)

    in_specs = [col(COL_MQ), col(COL_MK), col(COL_MV), col(COL_MO),
                pl.BlockSpec((rows, LANES), lambda b, t: (blk(b, t), 0)),
                pl.BlockSpec((1, LANES), lambda b, t: (0, 0)),
                pl.BlockSpec((1, BRANCH_W), lambda b, t: (0, 0))]
    args = [proj, proj, proj, proj, gs, bias_row, norm_row]
    c_spec = pl.BlockSpec((G, N_HEADS, HEAD_DIM, HEAD_DIM), lambda b, t: (b, 0, 0, 0))
    n_spec = pl.BlockSpec((G, N_HEADS, HEAD_DIM), lambda b, t: (b, 0, 0))
    m_spec = pl.BlockSpec((G, N_HEADS, LANES), lambda b, t: (b, 0, 0))
    if has_init:
        in_specs += [c_spec, n_spec, m_spec]
        args += list(init)
    return pl.pallas_call(
        functools.partial(_mlstm_kernel, L=L, n_sub=n_sub, G=G, s_valid=s_valid, s_total=seq,
                          has_init=has_init),
        out_shape=(jax.ShapeDtypeStruct((n_batch * seq, BRANCH_W), F32),
                   jax.ShapeDtypeStruct((n_batch, N_HEADS, HEAD_DIM, HEAD_DIM), F32),
                   jax.ShapeDtypeStruct((n_batch, N_HEADS, HEAD_DIM), F32),
                   jax.ShapeDtypeStruct((n_batch, N_HEADS, LANES), F32)),
        grid=(n_batch // G, nt),
        in_specs=in_specs,
        out_specs=(pl.BlockSpec((rows, BRANCH_W), lambda b, t: (b * nt + t, 0)),
                   c_spec, n_spec, m_spec),
        compiler_params=_params(("parallel", "arbitrary")),
        name="mlstm",
    )(*args)


def _gdn_kernel(*refs, L, n_sub, G, s_valid, s_total, has_init):
    if has_init:
        (q_ref, k_ref, v_ref, og_ref, gs_ref, cw_ref, arow_ref, dtrow_ref, norm_ref,
         conv0_ref, s0_ref, o_ref, s_ref, conv1_ref, ext_ref, conv_ref) = refs
    else:
        (q_ref, k_ref, v_ref, og_ref, gs_ref, cw_ref, arow_ref, dtrow_ref, norm_ref,
         o_ref, s_ref, conv1_ref, ext_ref, conv_ref) = refs
    t = pl.program_id(1)
    Lt = n_sub * L
    rows = G * Lt
    nt = s_total // Lt
    W = BRANCH_W

    @pl.when(t == 0)
    def _():
        if has_init:
            s_ref[...] = s0_ref[...]
            ext_ref[:, 0:SUBLANES, :] = conv0_ref[...]
        else:
            s_ref[...] = jnp.zeros_like(s_ref)
            ext_ref[:, 0:SUBLANES, :] = jnp.zeros((G, SUBLANES, 3 * W), F32)

    @pl.when(t > 0)
    def _():
        ext_ref[:, 0:SUBLANES, :] = ext_ref[:, Lt:Lt + SUBLANES, :]

    base = SUBLANES - (CONV_W - 1)
    for s in range(G):
        rs = slice(s * Lt, (s + 1) * Lt)
        ext_ref[s, SUBLANES:SUBLANES + Lt, 0:W] = q_ref[rs, :]
        ext_ref[s, SUBLANES:SUBLANES + Lt, W:2 * W] = k_ref[rs, :]
        ext_ref[s, SUBLANES:SUBLANES + Lt, 2 * W:3 * W] = v_ref[rs, :]
        acc = ext_ref[s, base:base + Lt, :] * cw_ref[0:1, :]
        for j in range(1, CONV_W):
            acc = acc + ext_ref[s, base + j:base + j + Lt, :] * cw_ref[j:j + 1, :]
        conv_ref[rs, :] = _silu(acc)

    @pl.when(t == nt - 1)
    def _():
        v_last = s_valid - (nt - 1) * Lt
        conv1_ref[...] = ext_ref[:, v_last:v_last + SUBLANES, :]

    z = gs_ref[...]
    lane = lax.broadcasted_iota(jnp.int32, z.shape, 1)
    gdec = arow_ref[...] * _softplus(z + dtrow_ref[...])
    gb = jnp.where((lane >= 2 * N_HEADS) & (lane < 3 * N_HEADS), gdec,
                   jnp.where((lane >= 3 * N_HEADS) & (lane < 4 * N_HEADS), _sigmoid(z), 0.0))
    if s_valid < s_total:
        gb = jnp.where(_pad_rows(z.shape, t, Lt, s_valid), 0.0, gb)
    gb_t = _lanes_to_rows(gb, 2 * N_HEADS)
    gam_cols = _dot_exact(_tri_blocks(rows, L).astype(F32), gb)
    gam_rows = _dot_exact(gb_t, _tri_blocks(rows, L, upper=True).astype(F32))
    incl = _tri(L)
    strict = _tri(L, strict=True)
    eye = (lax.broadcasted_iota(jnp.int32, (L, L), 0)
           == lax.broadcasted_iota(jnp.int32, (L, L), 1)).astype(F32)

    P = []
    for s in range(G):
        for c in range(n_sub):
            for h in range(N_HEADS):
                r = slice(s * Lt + c * L, s * Lt + (c + 1) * L)
                sl = slice(h * HEAD_DIM, (h + 1) * HEAD_DIM)
                q = conv_ref[r, h * HEAD_DIM:(h + 1) * HEAD_DIM]
                k = conv_ref[r, W + h * HEAD_DIM:W + (h + 1) * HEAD_DIM]
                v = conv_ref[r, 2 * W + h * HEAD_DIM:2 * W + (h + 1) * HEAD_DIM]
                q = q * lax.rsqrt(jnp.sum(q * q, axis=-1, keepdims=True) + EPS) * (HEAD_DIM ** -0.5)
                k = k * lax.rsqrt(jnp.sum(k * k, axis=-1, keepdims=True) + EPS)
                gam_c = gam_cols[r, 2 * N_HEADS + h:2 * N_HEADS + h + 1]
                gam_r = gam_rows[h:h + 1, r]
                beta_c = gb[r, 3 * N_HEADS + h:3 * N_HEADS + h + 1]
                decay = jnp.exp(jnp.where(incl, gam_c - gam_r, -jnp.inf))
                P.append(dict(s=s, c=c, h=h, r=r, sl=sl, q=q, k=k, v=v, gam_c=gam_c,
                              beta_c=beta_c, decay=decay))
    for p in P:
        p["kk"] = _dot_nt(p["k"], p["k"])
    for p in P:
        p["qk"] = _dot_nt(p["q"], p["k"]) * p["decay"]
    for p in P:
        e_gam = jnp.exp(p["gam_c"])
        gam_last = p["gam_c"][L - 1:L, :]
        p["pw"] = -jnp.where(strict, p["beta_c"] * p["kk"] * p["decay"], 0.0)
        p["x"] = eye + p["pw"]
        p["rhs"] = jnp.concatenate([p["beta_c"] * p["v"], p["beta_c"] * p["k"] * e_gam], axis=-1)
        p["q_dec"] = p["q"] * e_gam
        p["k_dec"] = p["k"] * jnp.exp(gam_last - p["gam_c"])
        p["e_last"] = jnp.exp(gam_last)
    n = 2
    while n < L:
        for p in P:
            p["pw"] = _dot(p["pw"], p["pw"])
        for p in P:
            p["x"] = p["x"] + _dot(p["x"], p["pw"])
        n *= 2
    for p in P:
        sol = _dot(p["x"], p["rhs"])
        p["u"] = sol[:, :HEAD_DIM]
        p["w"] = sol[:, HEAD_DIM:]

    for c in range(n_sub):
        Pc = [p for p in P if p["c"] == c]
        for p in Pc:
            p["s_prev"] = s_ref[p["s"], p["h"]]
            p["ws"] = _dot(p["w"], p["s_prev"])
        for p in Pc:
            p["qs"] = _dot(p["q_dec"], p["s_prev"])
        for p in Pc:
            p["v_new"] = p["u"] - p["ws"]
        for p in Pc:
            p["o"] = p["qs"] + _dot(p["qk"], p["v_new"])
        for p in Pc:
            p["kv"] = _dot_tn(p["k_dec"], p["v_new"])
        for p in Pc:
            o = p["o"]
            on = o * lax.rsqrt(jnp.mean(o * o, axis=-1, keepdims=True) + EPS) * norm_ref[...]
            p["o_out"] = on * _silu(og_ref[p["r"], p["sl"]])
        for p in Pc:
            s_ref[p["s"], p["h"]] = p["e_last"] * p["s_prev"] + p["kv"]
            o_ref[p["r"], p["sl"]] = p["o_out"]


def _gdn(proj, gs, conv_w, a_row, dt_row, norm_row, init, *, n_batch, seq, s_valid, row_off,
         L, n_sub, G):
    Lt = n_sub * L
    nt = seq // Lt
    rows = G * Lt
    has_init = init is not None
    blk = _seq_tile_specs(G, Lt, nt, row_off)

    def col(c):
        return _proj_spec(rows, c, blk)

    def const(shape):
        return pl.BlockSpec(shape, lambda b, t: (0,) * len(shape))

    in_specs = [col(COL_GQ), col(COL_GK), col(COL_GV), col(COL_GO),
                pl.BlockSpec((rows, LANES), lambda b, t: (blk(b, t), 0)),
                const((CONV_W, 3 * BRANCH_W)), const((1, LANES)), const((1, LANES)),
                const((1, HEAD_DIM))]
    args = [proj, proj, proj, proj, gs, conv_w, a_row, dt_row, norm_row]
    s_spec = pl.BlockSpec((G, N_HEADS, HEAD_DIM, HEAD_DIM), lambda b, t: (b, 0, 0, 0))
    conv_spec = pl.BlockSpec((G, SUBLANES, 3 * BRANCH_W), lambda b, t: (b, 0, 0))
    if has_init:
        in_specs += [conv_spec, s_spec]
        args += list(init)
    return pl.pallas_call(
        functools.partial(_gdn_kernel, L=L, n_sub=n_sub, G=G, s_valid=s_valid, s_total=seq,
                          has_init=has_init),
        out_shape=(jax.ShapeDtypeStruct((n_batch * seq, BRANCH_W), F32),
                   jax.ShapeDtypeStruct((n_batch, N_HEADS, HEAD_DIM, HEAD_DIM), F32),
                   jax.ShapeDtypeStruct((n_batch, SUBLANES, 3 * BRANCH_W), F32)),
        grid=(n_batch // G, nt),
        in_specs=in_specs,
        out_specs=(pl.BlockSpec((rows, BRANCH_W), lambda b, t: (b * nt + t, 0)),
                   s_spec, conv_spec),
        scratch_shapes=[pltpu.VMEM((G, Lt + SUBLANES, 3 * BRANCH_W), F32),
                        pltpu.VMEM((rows, 3 * BRANCH_W), F32)],
        compiler_params=_params(("parallel", "arbitrary")),
        name="gdn",
    )(*args)


def _xattn_kernel(q_ref, mk_ref, mv_ref, o_ref, *, G, lt, mem_len, head_rows):
    scale = HEAD_DIM ** -0.5

    def head(ref, s, h):
        if head_rows:
            return ref[pl.ds(s * mem_len * N_HEADS + h, mem_len, stride=N_HEADS), :]
        return ref[s * mem_len:(s + 1) * mem_len, h * HEAD_DIM:(h + 1) * HEAD_DIM]

    P = []
    for s in range(G):
        for h in range(N_HEADS):
            P.append(dict(s=s, h=h, r=slice(s * lt, (s + 1) * lt),
                          sl=slice(h * HEAD_DIM, (h + 1) * HEAD_DIM)))
    for p in P:
        p["sc"] = _dot_nt(q_ref[p["r"], p["sl"]], head(mk_ref, p["s"], p["h"])) * scale
    for p in P:
        e = jnp.exp(p["sc"] - jnp.max(p["sc"], axis=-1, keepdims=True))
        p["p"] = e / jnp.sum(e, axis=-1, keepdims=True)
    for p in P:
        o_ref[p["r"], p["sl"]] = _dot(p["p"], head(mv_ref, p["s"], p["h"]))


def _xattn(proj, mk, mv, *, n_batch, seq, row_off, lt, G, mem_len, kv_specs, head_rows):
    nt = seq // lt
    blk = _seq_tile_specs(G, lt, nt, row_off)
    return pl.pallas_call(
        functools.partial(_xattn_kernel, G=G, lt=lt, mem_len=mem_len, head_rows=head_rows),
        out_shape=jax.ShapeDtypeStruct((n_batch * seq, BRANCH_W), F32),
        grid=(n_batch // G, nt),
        in_specs=[_proj_spec(G * lt, COL_XQ, blk)] + list(kv_specs),
        out_specs=pl.BlockSpec((G * lt, BRANCH_W), lambda b, t: (b * nt + t, 0)),
        compiler_params=_params(("parallel", "arbitrary")),
        name="xattn",
    )(proj, mk, mv)


def _lane_first_max(x, mask, lane_f):
    xm = jnp.where(mask, x, -jnp.inf)
    mx = jnp.max(xm, axis=-1, keepdims=True)
    idx = jnp.min(jnp.where(xm == mx, lane_f, float(LANES)), axis=-1, keepdims=True)
    return mx, idx


def _store_token_tiles(ref, x, tok_rows):
    n = x.shape[0]
    for j in range(x.shape[1] // LANES):
        ref[pl.ds(j, n, stride=tok_rows), :] = x[:, j * LANES:(j + 1) * LANES]


def _load_token_tiles(ref, n, tok_rows, n_chunks=SUBLANES):
    return jnp.concatenate([ref[pl.ds(j, n, stride=tok_rows), :] for j in range(n_chunks)], axis=1)


def _merge_kernel(hap_ref, has_ref, hbp_ref, hbs_ref, hcp_ref, hcs_ref, xp_ref, xs_ref,
                  ga0_ref, ga1_ref, gb0_ref, gb1_ref, gc0_ref, gc1_ref,
                  wa_ref, wb_ref, wc_ref, wo_ref, nf_ref, wr_ref, br_ref,
                  x1_ref, route_ref, *, n_first):
    i = pl.program_id(0)

    def gate(lo_ref, hi_ref):
        return jnp.concatenate([_sigmoid(lo_ref[...]), _sigmoid(hi_ref[...])], axis=1)

    def body(ha_ref, hb_ref, hc_ref, x_ref):
        mixed = (gate(ga0_ref, ga1_ref) * _dot(ha_ref[...], wa_ref[...])
                 + gate(gb0_ref, gb1_ref) * _dot(hb_ref[...], wb_ref[...])
                 + gate(gc0_ref, gc1_ref) * _dot(hc_ref[...], wc_ref[...]))
        x1 = x_ref[...] + _dot(mixed, wo_ref[...])
        h2 = _rms(x1, nf_ref[...])
        logits = _dot(h2, wr_ref[...]) + br_ref[...]
        lane = lax.broadcasted_iota(jnp.int32, logits.shape, 1)
        lane_f = lane.astype(F32)
        is_g = lane < N_GROUPS
        _, g_idx = _lane_first_max(logits, is_g, lane_f)
        e_lo = N_GROUPS + EXPERTS_PER_GROUP * g_idx
        in_grp = (lane_f >= e_lo) & (lane_f < e_lo + EXPERTS_PER_GROUP)
        _, i1 = _lane_first_max(logits, in_grp, lane_f)
        _, i2 = _lane_first_max(logits, in_grp & (lane_f != i1), lane_f)
        lo = jnp.minimum(i1, i2) - e_lo
        hi = jnp.maximum(i1, i2) - e_lo
        pair = lo * (7.0 - lo) * 0.5 + (hi - lo - 1.0)
        cls = g_idx * float(N_PAIRS) + pair
        _store_token_tiles(x1_ref, x1, TOKEN_ROWS)
        route_ref[...] = jnp.where(lane == ROUTE_CLS, cls, 0.0)

    @pl.when(i < n_first)
    def _():
        body(hap_ref, hbp_ref, hcp_ref, xp_ref)

    @pl.when(i >= n_first)
    def _():
        body(has_ref, hbs_ref, hcs_ref, xs_ref)


def _merge(ha, hb, hc, xs, proj, wa, wb, wc, wo, nf, wr, br, *, tm):
    d = xs[0].shape[1]
    n_first = xs[0].shape[0] // tm
    n_rows = xs[0].shape[0] + xs[1].shape[0]
    specs = []
    for width in (BRANCH_W, BRANCH_W, BRANCH_W, d):
        specs += list(_two_source_specs(n_first, tm, width))

    def const(a):
        return pl.BlockSpec(a.shape, lambda i: (0,) * a.ndim)

    n_gate = 3 * d // BRANCH_W
    specs += [_proj_spec(tm, COL_GATE + c, lambda i: i) for c in range(n_gate)]
    weights = [wa, wb, wc, wo, nf, wr, br]
    specs += [const(a) for a in weights]
    return pl.pallas_call(
        functools.partial(_merge_kernel, n_first=n_first),
        out_shape=(jax.ShapeDtypeStruct((n_rows * TOKEN_ROWS, LANES), F32),
                   jax.ShapeDtypeStruct((n_rows, LANES), F32)),
        grid=(n_rows // tm,),
        in_specs=specs,
        out_specs=(pl.BlockSpec((tm * TOKEN_ROWS, LANES), lambda i: (i, 0)),
                   pl.BlockSpec((tm, LANES), lambda i: (i, 0))),
        compiler_params=_params(("parallel",)),
        name="merge",
    )(*ha, *hb, *hc, *xs, *([proj] * n_gate), *weights)


GATHER_UNROLL = 8
N_DMA_PRIORITIES = 2
ROUTE_CLS = 0


def _record_gather_start(src_ref, dst_ref, sem, ids_ref, base, n, rec):
    def issue(r2, carry):
        for prio in range(N_DMA_PRIORITIES):
            r = r2 * N_DMA_PRIORITIES + prio
            src = pl.multiple_of(ids_ref[base + r] * rec, rec)
            dst = pl.multiple_of(r * rec, rec)
            pltpu.make_async_copy(src_ref.at[pl.ds(src, rec)], dst_ref.at[pl.ds(dst, rec)],
                                  sem).start(priority=prio)
        return carry
    lax.fori_loop(0, n // N_DMA_PRIORITIES, issue, 0, unroll=GATHER_UNROLL // N_DMA_PRIORITIES)


def _record_gather_wait(src_ref, dst_ref, sem):
    pltpu.make_async_copy(src_ref.at[pl.ds(0, dst_ref.shape[0])], dst_ref, sem).wait()


def _sort_scatter_kernel(pos_ref, x_ref, init_hbm, out_hbm, sem, *, tm):
    del init_hbm
    base = pl.program_id(0) * tm

    def issue(r2, carry):
        for prio in range(N_DMA_PRIORITIES):
            r = r2 * N_DMA_PRIORITIES + prio
            src = pl.multiple_of(r * TOKEN_ROWS, TOKEN_ROWS)
            dst = pl.multiple_of(pos_ref[base + r] * TOKEN_ROWS, TOKEN_ROWS)
            pltpu.make_async_copy(x_ref.at[pl.ds(src, TOKEN_ROWS)],
                                  out_hbm.at[pl.ds(dst, TOKEN_ROWS)], sem).start(priority=prio)
        return carry
    lax.fori_loop(0, tm // N_DMA_PRIORITIES, issue, 0, unroll=GATHER_UNROLL // N_DMA_PRIORITIES)
    pltpu.make_async_copy(x_ref, out_hbm.at[pl.ds(0, tm * TOKEN_ROWS)], sem).wait()


def _sort_scatter(pos, x1rec, n_sorted, *, tm):
    n_tok = x1rec.shape[0] // TOKEN_ROWS
    grid_spec = pltpu.PrefetchScalarGridSpec(
        num_scalar_prefetch=1,
        grid=(n_tok // tm,),
        in_specs=[pl.BlockSpec((tm * TOKEN_ROWS, LANES), lambda i, *_: (i, 0)),
                  pl.BlockSpec(memory_space=pl.ANY)],
        out_specs=pl.BlockSpec(memory_space=pl.ANY),
        scratch_shapes=[pltpu.SemaphoreType.DMA(())],
    )
    return pl.pallas_call(
        functools.partial(_sort_scatter_kernel, tm=tm),
        out_shape=jax.ShapeDtypeStruct((n_sorted * TOKEN_ROWS, LANES), F32),
        grid_spec=grid_spec,
        input_output_aliases={2: 0},
        compiler_params=_params(("arbitrary",)),
        name="sort_scatter",
    )(pos, x1rec, jnp.zeros((n_sorted * TOKEN_ROWS, LANES), F32))


def _moe_kernel(ea_ref, eb_ref, nused_ref, x_ref, wga_ref, wua_ref, wda_ref,
                wgb_ref, wub_ref, wdb_ref, nf_ref, fin_ref, wr_ref, br_ref, y_ref,
                wg_bf, wu_bf, wd_bf, *, tm, d):
    i = pl.program_id(0)
    n_used = nused_ref[0]

    prev = jnp.maximum(i - 1, 0)
    for slot, e_ref, srcs in ((0, ea_ref, (wga_ref, wua_ref, wda_ref)),
                              (1, eb_ref, (wgb_ref, wub_ref, wdb_ref))):
        @pl.when((i < n_used) & ((i == 0) | (e_ref[i] != e_ref[prev])))
        def _():
            for dst, src in zip((wg_bf, wu_bf, wd_bf), srcs):
                dst[slot] = src[0].astype(BF16)

    @pl.when(i < n_used)
    def _():
        x1 = _load_token_tiles(x_ref, tm, TOKEN_ROWS)
        h2 = _rms(x1, nf_ref[...]).astype(BF16)
        logits = jnp.dot(h2, wr_ref[...], preferred_element_type=F32) + br_ref[...]
        lane = lax.broadcasted_iota(jnp.int32, logits.shape, 1)

        def pick(l):
            return jnp.sum(jnp.where(lane == l, logits, 0.0), axis=-1, keepdims=True)

        ea, eb = ea_ref[i], eb_ref[i]
        is_g = lane < N_GROUPS
        g_max = jnp.max(jnp.where(is_g, logits, -jnp.inf), axis=-1, keepdims=True)
        g_den = jnp.sum(jnp.where(is_g, jnp.exp(logits - g_max), 0.0), axis=-1, keepdims=True)
        p_grp = jnp.exp(pick(ea // EXPERTS_PER_GROUP) - g_max) / g_den
        v_a, v_b = pick(N_GROUPS + ea), pick(N_GROUPS + eb)
        combs = (p_grp / (1.0 + jnp.exp(v_b - v_a)), p_grp / (1.0 + jnp.exp(v_a - v_b)))
        acc = jnp.zeros((tm, d), F32)
        for slot, comb in enumerate(combs):
            a = jnp.dot(h2, wg_bf[slot], preferred_element_type=F32)
            u = jnp.dot(h2, wu_bf[slot], preferred_element_type=F32)
            act = _silu(a) * u * comb
            acc = acc + _dot(act, wd_bf[slot])
        _store_token_tiles(y_ref, _rms(x1 + acc, fin_ref[...]), SUBLANES)

    @pl.when(i >= n_used)
    def _():
        y_ref[...] = jnp.zeros_like(y_ref)


def _moe(tile_ea, tile_eb, n_used, x_sorted, wg, wu, wd, nf, fin, wr, br, *, tm, n_tiles):
    d = nf.shape[1]
    f = wg.shape[2]
    assert d == SUBLANES * LANES

    def expert(shape, which):
        return pl.BlockSpec((1,) + shape, lambda i, ea, eb, nu: ((ea, eb)[which][i], 0, 0))

    def const(a):
        return pl.BlockSpec(a.shape, lambda i, *_: (0,) * a.ndim)

    grid_spec = pltpu.PrefetchScalarGridSpec(
        num_scalar_prefetch=3,
        grid=(n_tiles,),
        in_specs=[pl.BlockSpec((tm * TOKEN_ROWS, LANES), lambda i, *_: (i, 0)),
                  expert((d, f), 0), expert((d, f), 0), expert((f, d), 0),
                  expert((d, f), 1), expert((d, f), 1), expert((f, d), 1),
                  const(nf), const(fin), const(wr), const(br)],
        out_specs=pl.BlockSpec((tm * SUBLANES, LANES), lambda i, *_: (i, 0)),
        scratch_shapes=[pltpu.VMEM((2, d, f), BF16), pltpu.VMEM((2, d, f), BF16),
                        pltpu.VMEM((2, f, d), BF16)],
    )
    return pl.pallas_call(
        functools.partial(_moe_kernel, tm=tm, d=d),
        out_shape=jax.ShapeDtypeStruct((n_tiles * tm * SUBLANES, LANES), F32),
        grid_spec=grid_spec,
        compiler_params=_params(("arbitrary",)),
        name="moe",
    )(tile_ea, tile_eb, n_used, x_sorted, wg, wu, wd, wg, wu, wd, nf, fin, wr, br)


def _unsort_kernel(pos_ref, y_hbm, o_ref, buf, sem, *, tm, base):
    i = pl.program_id(0)
    slot = i % 2

    @pl.when(i == 0)
    def _():
        _record_gather_start(y_hbm, buf.at[0], sem.at[0], pos_ref, base, tm, SUBLANES)

    @pl.when(i + 1 < pl.num_programs(0))
    def _():
        _record_gather_start(y_hbm, buf.at[1 - slot], sem.at[1 - slot], pos_ref,
                             base + (i + 1) * tm, tm, SUBLANES)

    _record_gather_wait(y_hbm, buf.at[slot], sem.at[slot])
    o_ref[...] = _load_token_tiles(buf.at[slot], tm, SUBLANES)


def _unsort(pos, y_rec, *, base, n_rows, tm):
    d = SUBLANES * LANES
    grid_spec = pltpu.PrefetchScalarGridSpec(
        num_scalar_prefetch=1,
        grid=(n_rows // tm,),
        in_specs=[pl.BlockSpec(memory_space=pl.ANY)],
        out_specs=pl.BlockSpec((tm, d), lambda i, *_: (i, 0)),
        scratch_shapes=[pltpu.VMEM((2, tm * SUBLANES, LANES), F32),
                        pltpu.SemaphoreType.DMA((2,))],
    )
    return pl.pallas_call(
        functools.partial(_unsort_kernel, tm=tm, base=base),
        out_shape=jax.ShapeDtypeStruct((n_rows, d), F32),
        grid_spec=grid_spec,
        compiler_params=_params(("arbitrary",)),
        name="unsort",
    )(pos, y_rec)


def _sort_plan(cls, *, tm, n_tiles):
    onehot = (cls[:, None] == jnp.arange(N_CLASSES, dtype=jnp.int32)[None, :]).astype(jnp.int32)
    csum = jnp.cumsum(onehot, axis=0)
    rank = jnp.sum((csum - onehot) * onehot, axis=1)
    counts = csum[-1]
    tiles_per = (counts + tm - 1) // tm
    tile_end = jnp.cumsum(tiles_per)
    tile_start = tile_end - tiles_per
    pos = (jnp.sum(tile_start[None, :] * onehot, axis=1) * tm + rank).astype(jnp.int32)
    n_used = tile_end[-1].astype(jnp.int32)
    tile = jnp.minimum(jnp.arange(n_tiles, dtype=jnp.int32), n_used - 1)
    tile_cls = jnp.sum((tile[:, None] >= tile_end[None, :]).astype(jnp.int32), axis=1)
    grp = tile_cls // N_PAIRS
    pair = tile_cls % N_PAIRS
    lo = jnp.asarray(PAIR_LO, jnp.int32)[pair]
    hi = jnp.asarray(PAIR_HI, jnp.int32)[pair]
    ea = (grp * EXPERTS_PER_GROUP + lo).astype(jnp.int32)
    eb = (grp * EXPERTS_PER_GROUP + hi).astype(jnp.int32)
    return pos, ea, eb, n_used.reshape(1)


def _lane_row(values, start):
    row = jnp.zeros((LANES,), F32)
    return row.at[start:start + values.shape[0]].set(values.astype(F32)).reshape(1, LANES)


def _layer(x_prompt, x_sample, mem_prompt, cache_mem_k, cache_mem_v,
           state_mlstm_C, state_mlstm_n, state_mlstm_m, state_gdn_S, state_gdn_conv,
           norm_mix, w_in, b_igate, b_fgate, mlstm_norm, conv_w, a_log, dt_bias, gdn_norm,
           mem_norm, w_mem_kv, w_br_mlstm, w_br_gdn, w_br_xattn, w_out, norm_ffn,
           w_router_group, b_router_group, w_router_expert, b_router_expert,
           w_exp_gate, w_exp_up, w_exp_down, final_norm, *, cfg):
    bp, sp, d = x_prompt.shape
    bs, ss, _ = x_sample.shape
    mem_len = mem_prompt.shape[1]
    ss_pad = SUBLANES
    assert ss <= ss_pad and d % LANES == 0
    n_p = bp * sp
    n_s = bs * ss_pad
    W = BRANCH_W
    gs_ = cfg["sample_group"]

    sizes = (W, W, W, N_HEADS, N_HEADS, W, 3 * W, N_HEADS, N_HEADS, W, W, d, d, d)
    offs = [0]
    for s in sizes:
        offs.append(offs[-1] + s)
    (m_q, m_k, m_v, m_i, m_f, m_o, g_qkv, g_a, g_b, g_out, x_q, gate_a, gate_b, gate_c) = [
        w_in[:, offs[j]:offs[j + 1]] for j in range(len(sizes))]
    w_big = _col_blocks(
        jnp.concatenate([gate_a, gate_b, gate_c, m_q, m_k, m_v, m_o, g_qkv, g_out, x_q],
                        axis=1).astype(BF16), PROJ_GROUP * W)
    w_small = jnp.concatenate(
        [m_i, m_f, g_a, g_b, jnp.zeros((d, LANES - 4 * N_HEADS), w_in.dtype)], axis=1).astype(BF16)
    w_router = jnp.concatenate(
        [w_router_group, w_router_expert,
         jnp.zeros((d, LANES - N_GROUPS - N_EXPERTS), w_in.dtype)], axis=1).astype(BF16)
    b_router = _lane_row(jnp.concatenate([b_router_group, b_router_expert]), 0)
    gate_bias = _lane_row(jnp.concatenate([b_igate, b_fgate]), 0)
    a_row = _lane_row(-jnp.exp(a_log.astype(F32)), 2 * N_HEADS)
    dt_row = _lane_row(dt_bias, 2 * N_HEADS)

    xp2 = x_prompt.reshape(n_p, d)
    xs2 = jnp.pad(x_sample, ((0, 0), (0, ss_pad - ss), (0, 0))).reshape(n_s, d)

    proj, gs = _norm_proj(xp2, xs2, norm_mix.reshape(1, d), w_big, w_small, tm=cfg["tm_proj"])

    mem2 = mem_prompt.reshape(bp * mem_len, d)
    mem_kv, _ = _norm_proj(mem2, None, mem_norm.reshape(1, d),
                           _col_blocks(w_mem_kv.astype(BF16), W),
                           jnp.zeros((d, LANES), BF16), tm=min(cfg["tm_proj"], bp * mem_len))

    mnorm = mlstm_norm.reshape(1, W)
    gnorm = gdn_norm.reshape(1, HEAD_DIM)
    s_off = n_p // (gs_ * ss_pad)
    ha_p, c_p, nn_p, mm_p = _mlstm(proj, gs, gate_bias, mnorm, None, n_batch=bp, seq=sp,
                                   s_valid=sp, row_off=0, L=cfg["chunk_m"], n_sub=cfg["sub_m"], G=1)
    m0 = jnp.broadcast_to(state_mlstm_m[:, :, None], (bs, N_HEADS, LANES))
    ha_s, c_s, nn_s, mm_s = _mlstm(proj, gs, gate_bias, mnorm,
                                   (state_mlstm_C, state_mlstm_n, m0), n_batch=bs, seq=ss_pad,
                                   s_valid=ss, row_off=s_off, L=ss_pad, n_sub=1, G=gs_)
    hb_p, s_p, cv_p = _gdn(proj, gs, conv_w, a_row, dt_row, gnorm, None, n_batch=bp, seq=sp,
                           s_valid=sp, row_off=0, L=cfg["chunk_g"], n_sub=cfg["sub_g"], G=1)
    conv0 = jnp.pad(state_gdn_conv, ((0, 0), (SUBLANES - (CONV_W - 1), 0), (0, 0)))
    hb_s, s_s, cv_s = _gdn(proj, gs, conv_w, a_row, dt_row, gnorm, (conv0, state_gdn_S),
                           n_batch=bs, seq=ss_pad, s_valid=ss, row_off=s_off, L=ss_pad,
                           n_sub=1, G=gs_)
    lt = min(sp, cfg["lt_x"])
    hc_p = _xattn(proj, mem_kv, mem_kv, n_batch=bp, seq=sp, row_off=0, lt=lt, G=1,
                  mem_len=mem_len, head_rows=False,
                  kv_specs=[pl.BlockSpec((None, mem_len, W), lambda b, t, c=c: (c, b, 0))
                            for c in range(2)])
    kv_rows = gs_ * mem_len * N_HEADS
    hc_s = _xattn(proj, cache_mem_k.reshape(bs * mem_len * N_HEADS, HEAD_DIM),
                  cache_mem_v.reshape(bs * mem_len * N_HEADS, HEAD_DIM),
                  n_batch=bs, seq=ss_pad, row_off=s_off, lt=ss_pad, G=gs_, mem_len=mem_len,
                  head_rows=True,
                  kv_specs=[pl.BlockSpec((kv_rows, HEAD_DIM), lambda b, t: (b, 0))] * 2)

    tm = cfg["tm"]
    x1rec, route = _merge((ha_p, ha_s), (hb_p, hb_s), (hc_p, hc_s), (xp2, xs2), proj,
                          w_br_mlstm.astype(BF16), w_br_gdn.astype(BF16), w_br_xattn.astype(BF16),
                          w_out.astype(BF16), norm_ffn.reshape(1, d), w_router, b_router, tm=tm)

    tm_moe = cfg["tm_moe"]
    n_all = n_p + n_s
    n_tiles = (n_all + N_CLASSES * (tm_moe - 1)) // tm_moe + 1
    cls = route[:, ROUTE_CLS].astype(jnp.int32)
    pos, ea, eb, n_used = _sort_plan(cls, tm=tm_moe, n_tiles=n_tiles)
    x_sorted = _sort_scatter(pos, x1rec, n_tiles * tm_moe, tm=tm)
    y_sorted = _moe(ea, eb, n_used, x_sorted, w_exp_gate, w_exp_up, w_exp_down,
                    norm_ffn.reshape(1, d),
                    final_norm.reshape(1, d), w_router, b_router, tm=tm_moe, n_tiles=n_tiles)
    y_p = _unsort(pos, y_sorted, base=0, n_rows=n_p, tm=tm)
    y_s = _unsort(pos, y_sorted, base=n_p, n_rows=n_s, tm=tm)

    y_prompt = y_p.reshape(bp, sp, d)
    y_sample = y_s.reshape(bs, ss_pad, d)[:, :ss]
    mk_p = mem_kv[0].reshape(bp, mem_len, N_HEADS, HEAD_DIM)
    mv_p = mem_kv[1].reshape(bp, mem_len, N_HEADS, HEAD_DIM)
    tail = slice(SUBLANES - (CONV_W - 1), SUBLANES)
    return (y_prompt, y_sample, mk_p, mv_p,
            c_p, nn_p, mm_p[:, :, 0], s_p, cv_p[:, tail],
            c_s, nn_s, mm_s[:, :, 0], s_s, cv_s[:, tail])


CONFIG = dict(tm_proj=1024, tm=512, tm_moe=256, chunk_m=256, sub_m=1, chunk_g=64, sub_g=8,
              lt_x=1024, sample_group=8)


def kernel(x_prompt, x_sample, mem_prompt, cache_mem_k, cache_mem_v, state_mlstm_C, state_mlstm_n, state_mlstm_m, state_gdn_S, state_gdn_conv, norm_mix, w_in, b_igate, b_fgate, mlstm_norm, conv_w, a_log, dt_bias, gdn_norm, mem_norm, w_mem_kv, w_br_mlstm, w_br_gdn, w_br_xattn, w_out, norm_ffn, w_router_group, b_router_group, w_router_expert, b_router_expert, w_exp_gate, w_exp_up, w_exp_down, final_norm):
    depth = w_in.shape[0]
    assert depth == 1, "one decoder layer"
    outs = _layer(
        x_prompt, x_sample, mem_prompt, cache_mem_k[0], cache_mem_v[0],
        state_mlstm_C[0], state_mlstm_n[0], state_mlstm_m[0], state_gdn_S[0], state_gdn_conv[0],
        norm_mix[0], w_in[0], b_igate[0], b_fgate[0], mlstm_norm[0], conv_w[0], a_log[0],
        dt_bias[0], gdn_norm[0], mem_norm[0], w_mem_kv[0], w_br_mlstm[0], w_br_gdn[0],
        w_br_xattn[0], w_out[0], norm_ffn[0], w_router_group[0], b_router_group[0],
        w_router_expert[0], b_router_expert[0], w_exp_gate[0], w_exp_up[0], w_exp_down[0],
        final_norm, cfg=CONFIG)
    y_prompt, y_sample = outs[0], outs[1]
    return (y_prompt, y_sample) + tuple(o[None] for o in outs[2:])
```

```python
import functools

import jax
import jax.numpy as jnp
from jax import lax
from jax.experimental import pallas as pl
from jax.experimental.pallas import tpu as pltpu

F32 = jnp.float32
BF16 = jnp.bfloat16
EPS = 1e-6

N_HEADS = 4
HEAD_DIM = 128
BRANCH_W = N_HEADS * HEAD_DIM
CONV_W = 4
N_GROUPS = 4
EXPERTS_PER_GROUP = 4
N_EXPERTS = N_GROUPS * EXPERTS_PER_GROUP
N_PAIRS = 6
N_CLASSES = N_GROUPS * N_PAIRS
PAIR_LO = (0, 0, 0, 1, 1, 2)
PAIR_HI = (1, 2, 3, 2, 3, 3)
NEG_BIG = -1e30
LANES = 128
SUBLANES = 8
TOKEN_ROWS = 8
PROJ_GROUP = 5
VMEM_PHYSICAL = 64 * 1024 * 1024
VMEM_LIMIT = 48 * 1024 * 1024
VMEM_SLACK = 4 * 1024 * 1024

COL_GATE = 0
COL_MQ, COL_MK, COL_MV, COL_MO = 6, 7, 8, 9
COL_GQ, COL_GK, COL_GV, COL_GO, COL_XQ = 10, 11, 12, 13, 14
N_PROJ = 15 * BRANCH_W


def _dot(a, b):
    return jnp.dot(a.astype(BF16), b.astype(BF16), preferred_element_type=F32)


def _dot_nt(a, b):
    return lax.dot_general(a.astype(BF16), b.astype(BF16), (((1,), (1,)), ((), ())),
                           preferred_element_type=F32)


def _dot_tn(a, b):
    return lax.dot_general(a.astype(BF16), b.astype(BF16), (((0,), (0,)), ((), ())),
                           preferred_element_type=F32)


def _dot_exact(a, b):
    return jnp.dot(a, b, preferred_element_type=F32, precision=lax.Precision.HIGHEST)


def _lanes_to_rows(x, lane0):
    r = lax.broadcasted_iota(jnp.int32, (SUBLANES, LANES), 0)
    c = lax.broadcasted_iota(jnp.int32, (SUBLANES, LANES), 1)
    sel = (c == r + lane0).astype(F32)
    return lax.dot_general(sel, x, (((1,), (1,)), ((), ())), preferred_element_type=F32,
                           precision=lax.Precision.HIGHEST)


def _rms(x, g):
    return x * lax.rsqrt(jnp.mean(x * x, axis=-1, keepdims=True) + EPS) * g


def _softplus(x):
    return jnp.maximum(x, 0.0) + jnp.log1p(jnp.exp(-jnp.abs(x)))


def _sigmoid(x):
    return 0.5 * jnp.tanh(0.5 * x) + 0.5


def _silu(x):
    return x * _sigmoid(x)


def _tri(n, strict=False):
    r = lax.broadcasted_iota(jnp.int32, (n, n), 0)
    c = lax.broadcasted_iota(jnp.int32, (n, n), 1)
    return (c < r) if strict else (c <= r)


def _tri_blocks(n, block, upper=False):
    shift = block.bit_length() - 1
    assert block == 1 << shift
    r = lax.broadcasted_iota(jnp.int32, (n, n), 0)
    c = lax.broadcasted_iota(jnp.int32, (n, n), 1)
    same = jnp.right_shift(r, shift) == jnp.right_shift(c, shift)
    return same & ((r <= c) if upper else (c <= r))


def _chunk_cumsum(x, x_t, L):
    rows = x.shape[0]
    blk = max(L, LANES)
    if rows % blk:
        blk = rows
    lower = _tri_blocks(blk, L).astype(F32)
    upper = _tri_blocks(blk, L, upper=True).astype(F32)
    cols = [_dot_exact(lower, x[c * blk:(c + 1) * blk]) for c in range(rows // blk)]
    rws = [_dot_exact(x_t[:, c * blk:(c + 1) * blk], upper) for c in range(rows // blk)]
    if len(cols) == 1:
        return cols[0], rws[0]
    return jnp.concatenate(cols, axis=0), jnp.concatenate(rws, axis=1)


def _pad_rows(shape, t, seq_rows, n_valid):
    assert seq_rows & (seq_rows - 1) == 0
    row = lax.broadcasted_iota(jnp.int32, shape, 0)
    return (row & (seq_rows - 1)) + t * seq_rows >= n_valid


def _params(sem, vmem_bytes=VMEM_LIMIT):
    return pltpu.CompilerParams(dimension_semantics=sem, vmem_limit_bytes=vmem_bytes)


def _two_source_specs(n_first, block, width):
    first = pl.BlockSpec((block, width), lambda i, *_: (jnp.minimum(i, n_first - 1), 0))
    second = pl.BlockSpec((block, width), lambda i, *_: (jnp.maximum(i - n_first, 0), 0))
    return first, second


def _norm_proj_kernel(xa_ref, xb_ref, g_ref, w_ref, ws_ref, o_ref, os_ref, hb_ref, *, n_first):
    i = pl.program_id(0)
    j = pl.program_id(1)

    @pl.when(j == 0)
    def _():
        @pl.when(i < n_first)
        def _():
            hb_ref[...] = _rms(xa_ref[...], g_ref[...]).astype(BF16)

        @pl.when(i >= n_first)
        def _():
            hb_ref[...] = _rms(xb_ref[...], g_ref[...]).astype(BF16)

        os_ref[...] = jnp.dot(hb_ref[...], ws_ref[...], preferred_element_type=F32)

    o_ref[...] = jnp.dot(hb_ref[...], w_ref[0], preferred_element_type=F32)


def _proj_spec(rows, c, row_block):
    return pl.BlockSpec((None, rows, BRANCH_W),
                        lambda *g: (c // PROJ_GROUP, row_block(*g), c % PROJ_GROUP))


def _col_blocks(w, tn):
    d, n = w.shape
    return w.reshape(d, n // tn, tn).transpose(1, 0, 2)


def _norm_proj(xa, xb, g, w, ws, *, tm):
    d = xa.shape[1]
    n_first = xa.shape[0] // tm
    n_rows = xa.shape[0] + (0 if xb is None else xb.shape[0])
    if xb is None:
        xb = xa
    tn = w.shape[2]
    n = w.shape[0] * tn
    sa, sb = _two_source_specs(n_first, tm, d)
    vmem = (4 * tm * d * 4 + 2 * d * tn * 2 + 2 * tm * tn * 4 + tm * d * 2
            + 2 * (tm + d) * ws.shape[1] * 4)
    vmem = min(max(vmem + VMEM_SLACK, VMEM_LIMIT), VMEM_PHYSICAL - VMEM_SLACK)
    return pl.pallas_call(
        functools.partial(_norm_proj_kernel, n_first=n_first),
        out_shape=(jax.ShapeDtypeStruct((n // tn, n_rows, tn), F32),
                   jax.ShapeDtypeStruct((n_rows, ws.shape[1]), F32)),
        grid=(n_rows // tm, n // tn),
        in_specs=[sa, sb,
                  pl.BlockSpec((1, d), lambda i, j: (0, 0)),
                  pl.BlockSpec((1, d, tn), lambda i, j: (j, 0, 0)),
                  pl.BlockSpec((d, ws.shape[1]), lambda i, j: (0, 0))],
        out_specs=(pl.BlockSpec((None, tm, tn), lambda i, j: (j, i, 0)),
                   pl.BlockSpec((tm, ws.shape[1]), lambda i, j: (i, 0))),
        scratch_shapes=[pltpu.VMEM((tm, d), BF16)],
        compiler_params=_params(("parallel", "arbitrary"), vmem),
        name="norm_proj",
    )(xa, xb, g, w, ws)


def _seq_tile_specs(G, Lt, nt, row_off):
    assert G == 1 or nt == 1

    def rows(b, t):
        return row_off + b * nt + t

    return rows


def _mlstm_kernel(*refs, L, n_sub, G, s_valid, s_total, has_init):
    if has_init:
        (q_ref, k_ref, v_ref, og_ref, gs_ref, bias_ref, norm_ref, c0_ref, n0_ref, m0_ref,
         h_ref, c_ref, n_ref, m_ref) = refs
    else:
        (q_ref, k_ref, v_ref, og_ref, gs_ref, bias_ref, norm_ref,
         h_ref, c_ref, n_ref, m_ref) = refs
    t = pl.program_id(1)
    Lt = n_sub * L
    rows = G * Lt

    @pl.when(t == 0)
    def _():
        if has_init:
            c_ref[...] = c0_ref[...]
            n_ref[...] = n0_ref[...]
            m_ref[...] = m0_ref[...]
        else:
            c_ref[...] = jnp.zeros_like(c_ref)
            n_ref[...] = jnp.zeros_like(n_ref)
            m_ref[...] = jnp.zeros_like(m_ref)

    z = gs_ref[...] + bias_ref[...]
    lane = lax.broadcasted_iota(jnp.int32, z.shape, 1)
    log_f = jnp.minimum(z, 0.0) - jnp.log1p(jnp.exp(-jnp.abs(z)))
    g = jnp.where((lane >= N_HEADS) & (lane < 2 * N_HEADS), log_f, z)
    if s_valid < s_total:
        pad = _pad_rows(z.shape, t, Lt, s_valid)
        g = jnp.where(pad, jnp.where(lane < N_HEADS, NEG_BIG, 0.0), g)
    g_t = _lanes_to_rows(g, 0)
    bt_cols, bt_rows = _chunk_cumsum(g, g_t, L)
    scale = HEAD_DIM ** -0.5
    assert G == 1 or n_sub == 1
    n_r = G * L
    mask = _tri_blocks(n_r, L)

    def per_seq(x):
        return [x[j * L:(j + 1) * L] for j in range(G)]

    def expand(pieces):
        if G == 1:
            return pieces[0]
        return jnp.concatenate([jnp.broadcast_to(x, (L, x.shape[1])) for x in pieces], axis=0)

    for c in range(n_sub):
        r = slice(c * L, c * L + n_r)
        P = []
        for h in range(N_HEADS):
            sl = slice(h * HEAD_DIM, (h + 1) * HEAD_DIM)
            p = dict(h=h, sl=sl, q=q_ref[r, sl], k=k_ref[r, sl] * scale, v=v_ref[r, sl],
                     bt_c=bt_cols[r, N_HEADS + h:N_HEADS + h + 1], it_c=g[r, h:h + 1])
            bt_r = bt_rows[N_HEADS + h:N_HEADS + h + 1, r]
            it_r = g_t[h:h + 1, r]
            p["log_d"] = jnp.where(mask, p["bt_c"] - bt_r + it_r, -jnp.inf)
            p["c_prev"] = [c_ref[s, h] for s in range(G)]
            p["n_prev"] = [n_ref[s, h:h + 1, :] for s in range(G)]
            p["m_prev"] = [m_ref[s, h:h + 1, 0:1] for s in range(G)]
            P.append(p)
        for p in P:
            p["qk"] = _dot_nt(p["q"], p["k"])
        for p in P:
            qs = per_seq(p["q"])
            qc = [_dot(qs[s], p["c_prev"][s]) for s in range(G)]
            p["qc"] = qc[0] if G == 1 else jnp.concatenate(qc, axis=0)
        for p in P:
            inter = p["bt_c"] + expand(p["m_prev"])
            p["m_t"] = jnp.maximum(inter, jnp.max(p["log_d"], axis=-1, keepdims=True))
            p["inter_w"] = jnp.exp(inter - p["m_t"])
            p["sd"] = p["qk"] * jnp.exp(p["log_d"] - p["m_t"])
        for p in P:
            p["sv"] = _dot(p["sd"], p["v"])
        for p in P:
            bt_last = [x[L - 1:L, :] for x in per_seq(p["bt_c"])]
            p["m_new"] = [x[L - 1:L, :] for x in per_seq(p["m_t"])]
            w = jnp.exp(expand(bt_last) - p["bt_c"] + p["it_c"] - expand(p["m_new"]))
            p["decay"] = [jnp.exp(bt_last[s] + p["m_prev"][s] - p["m_new"][s]) for s in range(G)]
            p["kw"] = p["k"] * w
        for p in P:
            kws, vs = per_seq(p["kw"]), per_seq(p["v"])
            p["kv"] = [_dot_tn(kws[s], vs[s]) for s in range(G)]
            p["n_new"] = [p["decay"][s] * p["n_prev"][s] + jnp.sum(kws[s], axis=0, keepdims=True)
                          for s in range(G)]
        for p in P:
            num = p["sv"] + p["qc"] * p["inter_w"]
            den = (jnp.sum(p["sd"], axis=-1, keepdims=True)
                   + p["inter_w"] * jnp.sum(p["q"] * expand(p["n_prev"]), axis=-1, keepdims=True))
            hh = num / jnp.maximum(jnp.abs(den), jnp.exp(-p["m_t"]))
            hn = hh * lax.rsqrt(jnp.mean(hh * hh, axis=-1, keepdims=True) + EPS) * norm_ref[:, p["sl"]]
            p["h_out"] = hn * _sigmoid(og_ref[r, p["sl"]])
        for p in P:
            h = p["h"]
            h_ref[r, p["sl"]] = p["h_out"]
            for s in range(G):
                n_ref[s, h:h + 1, :] = p["n_new"][s]
                m_ref[s, h:h + 1, :] = jnp.broadcast_to(p["m_new"][s], (1, LANES))
                c_ref[s, h] = p["decay"][s] * p["c_prev"][s] + p["kv"][s]


def _mlstm(proj, gs, bias_row, norm_row, init, *, n_batch, seq, s_valid, row_off, L, n_sub, G):
    Lt = n_sub * L
    nt = seq // Lt
    rows = G * Lt
    has_init = init is not None
    blk = _seq_tile_specs(G, Lt, nt, row_off)

    def col(c):
        return _proj_spec(rows, c, blk)

    in_specs = [col(COL_MQ), col(COL_MK), col(COL_MV), col(COL_MO),
                pl.BlockSpec((rows, LANES), lambda b, t: (blk(b, t), 0)),
                pl.BlockSpec((1, LANES), lambda b, t: (0, 0)),
                pl.BlockSpec((1, BRANCH_W), lambda b, t: (0, 0))]
    args = [proj, proj, proj, proj, gs, bias_row, norm_row]
    c_spec = pl.BlockSpec((G, N_HEADS, HEAD_DIM, HEAD_DIM), lambda b, t: (b, 0, 0, 0))
    n_spec = pl.BlockSpec((G, N_HEADS, HEAD_DIM), lambda b, t: (b, 0, 0))
    m_spec = pl.BlockSpec((G, N_HEADS, LANES), lambda b, t: (b, 0, 0))
    if has_init:
        in_specs += [c_spec, n_spec, m_spec]
        args += list(init)
    return pl.pallas_call(
        functools.partial(_mlstm_kernel, L=L, n_sub=n_sub, G=G, s_valid=s_valid, s_total=seq,
                          has_init=has_init),
        out_shape=(jax.ShapeDtypeStruct((n_batch * seq, BRANCH_W), F32),
                   jax.ShapeDtypeStruct((n_batch, N_HEADS, HEAD_DIM, HEAD_DIM), F32),
                   jax.ShapeDtypeStruct((n_batch, N_HEADS, HEAD_DIM), F32),
                   jax.ShapeDtypeStruct((n_batch, N_HEADS, LANES), F32)),
        grid=(n_batch // G, nt),
        in_specs=in_specs,
        out_specs=(pl.BlockSpec((rows, BRANCH_W), lambda b, t: (b * nt + t, 0)),
                   c_spec, n_spec, m_spec),
        compiler_params=_params(("parallel", "arbitrary")),
        name="mlstm",
    )(*args)


def _gdn_kernel(*refs, L, n_sub, G, s_valid, s_total, has_init):
    if has_init:
        (q_ref, k_ref, v_ref, og_ref, gs_ref, cw_ref, arow_ref, dtrow_ref, norm_ref,
         conv0_ref, s0_ref, o_ref, s_ref, conv1_ref, ext_ref, conv_ref) = refs
    else:
        (q_ref, k_ref, v_ref, og_ref, gs_ref, cw_ref, arow_ref, dtrow_ref, norm_ref,
         o_ref, s_ref, conv1_ref, ext_ref, conv_ref) = refs
    t = pl.program_id(1)
    Lt = n_sub * L
    rows = G * Lt
    nt = s_total // Lt
    W = BRANCH_W

    @pl.when(t == 0)
    def _():
        if has_init:
            s_ref[...] = s0_ref[...]
            ext_ref[:, 0:SUBLANES, :] = conv0_ref[...]
        else:
            s_ref[...] = jnp.zeros_like(s_ref)
            ext_ref[:, 0:SUBLANES, :] = jnp.zeros((G, SUBLANES, 3 * W), F32)

    @pl.when(t > 0)
    def _():
        ext_ref[:, 0:SUBLANES, :] = ext_ref[:, Lt:Lt + SUBLANES, :]

    base = SUBLANES - (CONV_W - 1)
    for s in range(G):
        rs = slice(s * Lt, (s + 1) * Lt)
        ext_ref[s, SUBLANES:SUBLANES + Lt, 0:W] = q_ref[rs, :]
        ext_ref[s, SUBLANES:SUBLANES + Lt, W:2 * W] = k_ref[rs, :]
        ext_ref[s, SUBLANES:SUBLANES + Lt, 2 * W:3 * W] = v_ref[rs, :]
        acc = ext_ref[s, base:base + Lt, :] * cw_ref[0:1, :]
        for j in range(1, CONV_W):
            acc = acc + ext_ref[s, base + j:base + j + Lt, :] * cw_ref[j:j + 1, :]
        conv_ref[rs, :] = _silu(acc)

    @pl.when(t == nt - 1)
    def _():
        v_last = s_valid - (nt - 1) * Lt
        conv1_ref[...] = ext_ref[:, v_last:v_last + SUBLANES, :]

    z = gs_ref[...]
    lane = lax.broadcasted_iota(jnp.int32, z.shape, 1)
    gdec = arow_ref[...] * _softplus(z + dtrow_ref[...])
    gb = jnp.where((lane >= 2 * N_HEADS) & (lane < 3 * N_HEADS), gdec,
                   jnp.where((lane >= 3 * N_HEADS) & (lane < 4 * N_HEADS), _sigmoid(z), 0.0))
    if s_valid < s_total:
        gb = jnp.where(_pad_rows(z.shape, t, Lt, s_valid), 0.0, gb)
    gb_t = _lanes_to_rows(gb, 2 * N_HEADS)
    gam_cols, gam_rows = _chunk_cumsum(gb, gb_t, L)
    incl = _tri(L)
    strict = _tri(L, strict=True)
    eye = (lax.broadcasted_iota(jnp.int32, (L, L), 0)
           == lax.broadcasted_iota(jnp.int32, (L, L), 1)).astype(F32)

    P = []
    for s in range(G):
        for c in range(n_sub):
            for h in range(N_HEADS):
                r = slice(s * Lt + c * L, s * Lt + (c + 1) * L)
                sl = slice(h * HEAD_DIM, (h + 1) * HEAD_DIM)
                q = conv_ref[r, h * HEAD_DIM:(h + 1) * HEAD_DIM]
                k = conv_ref[r, W + h * HEAD_DIM:W + (h + 1) * HEAD_DIM]
                v = conv_ref[r, 2 * W + h * HEAD_DIM:2 * W + (h + 1) * HEAD_DIM]
                q = q * lax.rsqrt(jnp.sum(q * q, axis=-1, keepdims=True) + EPS) * (HEAD_DIM ** -0.5)
                k = k * lax.rsqrt(jnp.sum(k * k, axis=-1, keepdims=True) + EPS)
                gam_c = gam_cols[r, 2 * N_HEADS + h:2 * N_HEADS + h + 1]
                gam_r = gam_rows[h:h + 1, r]
                beta_c = gb[r, 3 * N_HEADS + h:3 * N_HEADS + h + 1]
                decay = jnp.exp(jnp.where(incl, gam_c - gam_r, -jnp.inf))
                P.append(dict(s=s, c=c, h=h, r=r, sl=sl, q=q, k=k, v=v, gam_c=gam_c,
                              beta_c=beta_c, decay=decay))
    for p in P:
        p["kk"] = _dot_nt(p["k"], p["k"])
    for p in P:
        p["qk"] = _dot_nt(p["q"], p["k"]) * p["decay"]
    for p in P:
        e_gam = jnp.exp(p["gam_c"])
        gam_last = p["gam_c"][L - 1:L, :]
        p["pw"] = -jnp.where(strict, p["beta_c"] * p["kk"] * p["decay"], 0.0)
        p["x"] = eye + p["pw"]
        p["rhs"] = jnp.concatenate([p["beta_c"] * p["v"], p["beta_c"] * p["k"] * e_gam], axis=-1)
        p["q_dec"] = p["q"] * e_gam
        p["k_dec"] = p["k"] * jnp.exp(gam_last - p["gam_c"])
        p["e_last"] = jnp.exp(gam_last)
    n = 2
    while n < L:
        for p in P:
            p["pw"] = _dot(p["pw"], p["pw"])
        for p in P:
            p["x"] = p["x"] + _dot(p["x"], p["pw"])
        n *= 2
    for p in P:
        sol = _dot(p["x"], p["rhs"])
        p["u"] = sol[:, :HEAD_DIM]
        p["w"] = sol[:, HEAD_DIM:]

    for c in range(n_sub):
        Pc = [p for p in P if p["c"] == c]
        for p in Pc:
            p["s_prev"] = s_ref[p["s"], p["h"]]
            p["ws"] = _dot(p["w"], p["s_prev"])
        for p in Pc:
            p["qs"] = _dot(p["q_dec"], p["s_prev"])
        for p in Pc:
            p["v_new"] = p["u"] - p["ws"]
        for p in Pc:
            p["o"] = p["qs"] + _dot(p["qk"], p["v_new"])
        for p in Pc:
            p["kv"] = _dot_tn(p["k_dec"], p["v_new"])
        for p in Pc:
            o = p["o"]
            on = o * lax.rsqrt(jnp.mean(o * o, axis=-1, keepdims=True) + EPS) * norm_ref[...]
            p["o_out"] = on * _silu(og_ref[p["r"], p["sl"]])
        for p in Pc:
            s_ref[p["s"], p["h"]] = p["e_last"] * p["s_prev"] + p["kv"]
            o_ref[p["r"], p["sl"]] = p["o_out"]


def _gdn(proj, gs, conv_w, a_row, dt_row, norm_row, init, *, n_batch, seq, s_valid, row_off,
         L, n_sub, G):
    Lt = n_sub * L
    nt = seq // Lt
    rows = G * Lt
    has_init = init is not None
    blk = _seq_tile_specs(G, Lt, nt, row_off)

    def col(c):
        return _proj_spec(rows, c, blk)

    def const(shape):
        return pl.BlockSpec(shape, lambda b, t: (0,) * len(shape))

    in_specs = [col(COL_GQ), col(COL_GK), col(COL_GV), col(COL_GO),
                pl.BlockSpec((rows, LANES), lambda b, t: (blk(b, t), 0)),
                const((CONV_W, 3 * BRANCH_W)), const((1, LANES)), const((1, LANES)),
                const((1, HEAD_DIM))]
    args = [proj, proj, proj, proj, gs, conv_w, a_row, dt_row, norm_row]
    s_spec = pl.BlockSpec((G, N_HEADS, HEAD_DIM, HEAD_DIM), lambda b, t: (b, 0, 0, 0))
    conv_spec = pl.BlockSpec((G, SUBLANES, 3 * BRANCH_W), lambda b, t: (b, 0, 0))
    if has_init:
        in_specs += [conv_spec, s_spec]
        args += list(init)
    return pl.pallas_call(
        functools.partial(_gdn_kernel, L=L, n_sub=n_sub, G=G, s_valid=s_valid, s_total=seq,
                          has_init=has_init),
        out_shape=(jax.ShapeDtypeStruct((n_batch * seq, BRANCH_W), F32),
                   jax.ShapeDtypeStruct((n_batch, N_HEADS, HEAD_DIM, HEAD_DIM), F32),
                   jax.ShapeDtypeStruct((n_batch, SUBLANES, 3 * BRANCH_W), F32)),
        grid=(n_batch // G, nt),
        in_specs=in_specs,
        out_specs=(pl.BlockSpec((rows, BRANCH_W), lambda b, t: (b * nt + t, 0)),
                   s_spec, conv_spec),
        scratch_shapes=[pltpu.VMEM((G, Lt + SUBLANES, 3 * BRANCH_W), F32),
                        pltpu.VMEM((rows, 3 * BRANCH_W), F32)],
        compiler_params=_params(("parallel", "arbitrary")),
        name="gdn",
    )(*args)


def _xattn_kernel(q_ref, mk_ref, mv_ref, o_ref, *, G, lt, mem_len, head_rows):
    scale = HEAD_DIM ** -0.5

    def head(ref, s, h):
        if head_rows:
            return ref[pl.ds(s * mem_len * N_HEADS + h, mem_len, stride=N_HEADS), :]
        return ref[s * mem_len:(s + 1) * mem_len, h * HEAD_DIM:(h + 1) * HEAD_DIM]

    P = []
    for s in range(G):
        for h in range(N_HEADS):
            P.append(dict(s=s, h=h, r=slice(s * lt, (s + 1) * lt),
                          sl=slice(h * HEAD_DIM, (h + 1) * HEAD_DIM)))
    for p in P:
        p["sc"] = _dot_nt(q_ref[p["r"], p["sl"]], head(mk_ref, p["s"], p["h"])) * scale
    for p in P:
        e = jnp.exp(p["sc"] - jnp.max(p["sc"], axis=-1, keepdims=True))
        p["p"] = e / jnp.sum(e, axis=-1, keepdims=True)
    for p in P:
        o_ref[p["r"], p["sl"]] = _dot(p["p"], head(mv_ref, p["s"], p["h"]))


def _xattn(proj, mk, mv, *, n_batch, seq, row_off, lt, G, mem_len, kv_specs, head_rows):
    nt = seq // lt
    blk = _seq_tile_specs(G, lt, nt, row_off)
    return pl.pallas_call(
        functools.partial(_xattn_kernel, G=G, lt=lt, mem_len=mem_len, head_rows=head_rows),
        out_shape=jax.ShapeDtypeStruct((n_batch * seq, BRANCH_W), F32),
        grid=(n_batch // G, nt),
        in_specs=[_proj_spec(G * lt, COL_XQ, blk)] + list(kv_specs),
        out_specs=pl.BlockSpec((G * lt, BRANCH_W), lambda b, t: (b * nt + t, 0)),
        compiler_params=_params(("parallel", "arbitrary")),
        name="xattn",
    )(proj, mk, mv)


def _lane_first_max(x, mask, lane_f):
    xm = jnp.where(mask, x, -jnp.inf)
    mx = jnp.max(xm, axis=-1, keepdims=True)
    idx = jnp.min(jnp.where(xm == mx, lane_f, float(LANES)), axis=-1, keepdims=True)
    return mx, idx


def _store_token_tiles(ref, x, tok_rows):
    n = x.shape[0]
    for j in range(x.shape[1] // LANES):
        ref[pl.ds(j, n, stride=tok_rows), :] = x[:, j * LANES:(j + 1) * LANES]


def _load_token_tiles(ref, n, tok_rows, n_chunks=SUBLANES):
    return jnp.concatenate([ref[pl.ds(j, n, stride=tok_rows), :] for j in range(n_chunks)], axis=1)


def _merge_kernel(hap_ref, has_ref, hbp_ref, hbs_ref, hcp_ref, hcs_ref, xp_ref, xs_ref,
                  ga0_ref, ga1_ref, gb0_ref, gb1_ref, gc0_ref, gc1_ref,
                  wa_ref, wb_ref, wc_ref, wo_ref, nf_ref, wr_ref, br_ref,
                  x1_ref, route_ref, count_ref, *, n_first):
    i = pl.program_id(0)

    @pl.when(i == 0)
    def _():
        count_ref[...] = jnp.zeros_like(count_ref)

    def gate(lo_ref, hi_ref):
        return jnp.concatenate([_sigmoid(lo_ref[...]), _sigmoid(hi_ref[...])], axis=1)

    def body(ha_ref, hb_ref, hc_ref, x_ref):
        mixed = (gate(ga0_ref, ga1_ref) * _dot(ha_ref[...], wa_ref[...])
                 + gate(gb0_ref, gb1_ref) * _dot(hb_ref[...], wb_ref[...])
                 + gate(gc0_ref, gc1_ref) * _dot(hc_ref[...], wc_ref[...]))
        x1 = x_ref[...] + _dot(mixed, wo_ref[...])
        h2 = _rms(x1, nf_ref[...])
        logits = _dot(h2, wr_ref[...]) + br_ref[...]
        lane = lax.broadcasted_iota(jnp.int32, logits.shape, 1)
        lane_f = lane.astype(F32)
        is_g = lane < N_GROUPS
        _, g_idx = _lane_first_max(logits, is_g, lane_f)
        e_lo = N_GROUPS + EXPERTS_PER_GROUP * g_idx
        in_grp = (lane_f >= e_lo) & (lane_f < e_lo + EXPERTS_PER_GROUP)
        _, i1 = _lane_first_max(logits, in_grp, lane_f)
        _, i2 = _lane_first_max(logits, in_grp & (lane_f != i1), lane_f)
        lo = jnp.minimum(i1, i2) - e_lo
        hi = jnp.maximum(i1, i2) - e_lo
        pair = lo * (7.0 - lo) * 0.5 + (hi - lo - 1.0)
        cls = g_idx * float(N_PAIRS) + pair
        _store_token_tiles(x1_ref, x1, TOKEN_ROWS)
        tm = logits.shape[0]
        onehot = (lane_f == cls).astype(F32)
        earlier = jnp.dot(_tri(tm, strict=True).astype(BF16), onehot.astype(BF16),
                          preferred_element_type=F32)
        rank = jnp.sum((earlier + count_ref[0:1, :]) * onehot, axis=-1, keepdims=True)
        count_ref[0:1, :] = count_ref[0:1, :] + jnp.sum(onehot, axis=0, keepdims=True)
        route_ref[...] = jnp.where(lane == ROUTE_CLS, cls, jnp.where(lane == ROUTE_RANK, rank, 0.0))

    @pl.when(i < n_first)
    def _():
        body(hap_ref, hbp_ref, hcp_ref, xp_ref)

    @pl.when(i >= n_first)
    def _():
        body(has_ref, hbs_ref, hcs_ref, xs_ref)


def _merge(ha, hb, hc, xs, proj, wa, wb, wc, wo, nf, wr, br, *, tm):
    d = xs[0].shape[1]
    n_first = xs[0].shape[0] // tm
    n_rows = xs[0].shape[0] + xs[1].shape[0]
    specs = []
    for width in (BRANCH_W, BRANCH_W, BRANCH_W, d):
        specs += list(_two_source_specs(n_first, tm, width))

    def const(a):
        return pl.BlockSpec(a.shape, lambda i: (0,) * a.ndim)

    n_gate = 3 * d // BRANCH_W
    specs += [_proj_spec(tm, COL_GATE + c, lambda i: i) for c in range(n_gate)]
    weights = [wa, wb, wc, wo, nf, wr, br]
    specs += [const(a) for a in weights]
    return pl.pallas_call(
        functools.partial(_merge_kernel, n_first=n_first),
        out_shape=(jax.ShapeDtypeStruct((n_rows * TOKEN_ROWS, LANES), F32),
                   jax.ShapeDtypeStruct((n_rows, LANES), F32),
                   jax.ShapeDtypeStruct((SUBLANES, LANES), F32)),
        grid=(n_rows // tm,),
        in_specs=specs,
        out_specs=(pl.BlockSpec((tm * TOKEN_ROWS, LANES), lambda i: (i, 0)),
                   pl.BlockSpec((tm, LANES), lambda i: (i, 0)),
                   pl.BlockSpec((SUBLANES, LANES), lambda i: (0, 0))),
        compiler_params=_params(("arbitrary",)),
        name="merge",
    )(*ha, *hb, *hc, *xs, *([proj] * n_gate), *weights)


GATHER_UNROLL = 8
N_DMA_PRIORITIES = 2
ROUTE_CLS, ROUTE_RANK = 0, 1


def _record_gather_start(src_ref, dst_ref, sem, ids_ref, base, n, rec):
    def issue(r2, carry):
        for prio in range(N_DMA_PRIORITIES):
            r = r2 * N_DMA_PRIORITIES + prio
            src = pl.multiple_of(ids_ref[base + r] * rec, rec)
            dst = pl.multiple_of(r * rec, rec)
            pltpu.make_async_copy(src_ref.at[pl.ds(src, rec)], dst_ref.at[pl.ds(dst, rec)],
                                  sem).start(priority=prio)
        return carry
    lax.fori_loop(0, n // N_DMA_PRIORITIES, issue, 0, unroll=GATHER_UNROLL // N_DMA_PRIORITIES)


def _record_gather_wait(src_ref, dst_ref, sem):
    pltpu.make_async_copy(src_ref.at[pl.ds(0, dst_ref.shape[0])], dst_ref, sem).wait()


def _sort_scatter_kernel(pos_ref, x_ref, init_hbm, out_hbm, sem, *, tm):
    del init_hbm
    base = pl.program_id(0) * tm

    def issue(r2, carry):
        for prio in range(N_DMA_PRIORITIES):
            r = r2 * N_DMA_PRIORITIES + prio
            src = pl.multiple_of(r * TOKEN_ROWS, TOKEN_ROWS)
            dst = pl.multiple_of(pos_ref[base + r] * TOKEN_ROWS, TOKEN_ROWS)
            pltpu.make_async_copy(x_ref.at[pl.ds(src, TOKEN_ROWS)],
                                  out_hbm.at[pl.ds(dst, TOKEN_ROWS)], sem).start(priority=prio)
        return carry
    lax.fori_loop(0, tm // N_DMA_PRIORITIES, issue, 0, unroll=GATHER_UNROLL // N_DMA_PRIORITIES)
    pltpu.make_async_copy(x_ref, out_hbm.at[pl.ds(0, tm * TOKEN_ROWS)], sem).wait()


def _sort_scatter(pos, x1rec, n_sorted, *, tm):
    n_tok = x1rec.shape[0] // TOKEN_ROWS
    grid_spec = pltpu.PrefetchScalarGridSpec(
        num_scalar_prefetch=1,
        grid=(n_tok // tm,),
        in_specs=[pl.BlockSpec((tm * TOKEN_ROWS, LANES), lambda i, *_: (i, 0)),
                  pl.BlockSpec(memory_space=pl.ANY)],
        out_specs=pl.BlockSpec(memory_space=pl.ANY),
        scratch_shapes=[pltpu.SemaphoreType.DMA(())],
    )
    return pl.pallas_call(
        functools.partial(_sort_scatter_kernel, tm=tm),
        out_shape=jax.ShapeDtypeStruct((n_sorted * TOKEN_ROWS, LANES), F32),
        grid_spec=grid_spec,
        input_output_aliases={2: 0},
        compiler_params=_params(("arbitrary",)),
        name="sort_scatter",
    )(pos, x1rec, jnp.zeros((n_sorted * TOKEN_ROWS, LANES), F32))


def _moe_kernel(ea_ref, eb_ref, nused_ref, x_ref, wga_ref, wua_ref, wda_ref,
                wgb_ref, wub_ref, wdb_ref, nf_ref, fin_ref, wr_ref, br_ref, y_ref,
                wg_bf, wu_bf, wd_bf, *, tm, d):
    i = pl.program_id(0)
    n_used = nused_ref[0]

    prev = jnp.maximum(i - 1, 0)
    for slot, e_ref, srcs in ((0, ea_ref, (wga_ref, wua_ref, wda_ref)),
                              (1, eb_ref, (wgb_ref, wub_ref, wdb_ref))):
        @pl.when((i < n_used) & ((i == 0) | (e_ref[i] != e_ref[prev])))
        def _():
            for dst, src in zip((wg_bf, wu_bf, wd_bf), srcs):
                dst[slot] = src[0].astype(BF16)

    @pl.when(i < n_used)
    def _():
        x1 = _load_token_tiles(x_ref, tm, TOKEN_ROWS)
        h2 = _rms(x1, nf_ref[...]).astype(BF16)
        logits = jnp.dot(h2, wr_ref[...], preferred_element_type=F32) + br_ref[...]
        lane = lax.broadcasted_iota(jnp.int32, logits.shape, 1)

        def pick(l):
            return jnp.sum(jnp.where(lane == l, logits, 0.0), axis=-1, keepdims=True)

        ea, eb = ea_ref[i], eb_ref[i]
        is_g = lane < N_GROUPS
        g_max = jnp.max(jnp.where(is_g, logits, -jnp.inf), axis=-1, keepdims=True)
        g_den = jnp.sum(jnp.where(is_g, jnp.exp(logits - g_max), 0.0), axis=-1, keepdims=True)
        p_grp = jnp.exp(pick(ea // EXPERTS_PER_GROUP) - g_max) / g_den
        v_a, v_b = pick(N_GROUPS + ea), pick(N_GROUPS + eb)
        combs = (p_grp / (1.0 + jnp.exp(v_b - v_a)), p_grp / (1.0 + jnp.exp(v_a - v_b)))
        acc = jnp.zeros((tm, d), F32)
        for slot, comb in enumerate(combs):
            a = jnp.dot(h2, wg_bf[slot], preferred_element_type=F32)
            u = jnp.dot(h2, wu_bf[slot], preferred_element_type=F32)
            act = _silu(a) * u * comb
            acc = acc + _dot(act, wd_bf[slot])
        _store_token_tiles(y_ref, _rms(x1 + acc, fin_ref[...]), SUBLANES)

    @pl.when(i >= n_used)
    def _():
        y_ref[...] = jnp.zeros_like(y_ref)


def _moe(tile_ea, tile_eb, n_used, x_sorted, wg, wu, wd, nf, fin, wr, br, *, tm, n_tiles):
    d = nf.shape[1]
    f = wg.shape[2]
    assert d == SUBLANES * LANES

    def expert(shape, which):
        return pl.BlockSpec((1,) + shape, lambda i, ea, eb, nu: ((ea, eb)[which][i], 0, 0))

    def const(a):
        return pl.BlockSpec(a.shape, lambda i, *_: (0,) * a.ndim)

    grid_spec = pltpu.PrefetchScalarGridSpec(
        num_scalar_prefetch=3,
        grid=(n_tiles,),
        in_specs=[pl.BlockSpec((tm * TOKEN_ROWS, LANES), lambda i, *_: (i, 0)),
                  expert((d, f), 0), expert((d, f), 0), expert((f, d), 0),
                  expert((d, f), 1), expert((d, f), 1), expert((f, d), 1),
                  const(nf), const(fin), const(wr), const(br)],
        out_specs=pl.BlockSpec((tm * SUBLANES, LANES), lambda i, *_: (i, 0)),
        scratch_shapes=[pltpu.VMEM((2, d, f), BF16), pltpu.VMEM((2, d, f), BF16),
                        pltpu.VMEM((2, f, d), BF16)],
    )
    return pl.pallas_call(
        functools.partial(_moe_kernel, tm=tm, d=d),
        out_shape=jax.ShapeDtypeStruct((n_tiles * tm * SUBLANES, LANES), F32),
        grid_spec=grid_spec,
        compiler_params=_params(("arbitrary",)),
        name="moe",
    )(tile_ea, tile_eb, n_used, x_sorted, wg, wu, wd, wg, wu, wd, nf, fin, wr, br)


def _unsort_kernel(pos_ref, y_hbm, o_ref, buf, sem, *, tm, base):
    i = pl.program_id(0)
    slot = i % 2

    @pl.when(i == 0)
    def _():
        _record_gather_start(y_hbm, buf.at[0], sem.at[0], pos_ref, base, tm, SUBLANES)

    @pl.when(i + 1 < pl.num_programs(0))
    def _():
        _record_gather_start(y_hbm, buf.at[1 - slot], sem.at[1 - slot], pos_ref,
                             base + (i + 1) * tm, tm, SUBLANES)

    _record_gather_wait(y_hbm, buf.at[slot], sem.at[slot])
    o_ref[...] = _load_token_tiles(buf.at[slot], tm, SUBLANES)


def _unsort(pos, y_rec, *, base, n_rows, tm):
    d = SUBLANES * LANES
    grid_spec = pltpu.PrefetchScalarGridSpec(
        num_scalar_prefetch=1,
        grid=(n_rows // tm,),
        in_specs=[pl.BlockSpec(memory_space=pl.ANY)],
        out_specs=pl.BlockSpec((tm, d), lambda i, *_: (i, 0)),
        scratch_shapes=[pltpu.VMEM((2, tm * SUBLANES, LANES), F32),
                        pltpu.SemaphoreType.DMA((2,))],
    )
    return pl.pallas_call(
        functools.partial(_unsort_kernel, tm=tm, base=base),
        out_shape=jax.ShapeDtypeStruct((n_rows, d), F32),
        grid_spec=grid_spec,
        compiler_params=_params(("arbitrary",)),
        name="unsort",
    )(pos, y_rec)


def _sort_plan(cls, rank, counts, *, tm, n_tiles):
    onehot = (cls[:, None] == jnp.arange(N_CLASSES, dtype=jnp.int32)[None, :]).astype(jnp.int32)
    tiles_per = (counts + tm - 1) // tm
    tile_end = jnp.cumsum(tiles_per)
    tile_start = tile_end - tiles_per
    pos = (jnp.sum(tile_start[None, :] * onehot, axis=1) * tm + rank).astype(jnp.int32)
    n_used = tile_end[-1].astype(jnp.int32)
    tile = jnp.minimum(jnp.arange(n_tiles, dtype=jnp.int32), n_used - 1)
    tile_cls = jnp.sum((tile[:, None] >= tile_end[None, :]).astype(jnp.int32), axis=1)
    grp = tile_cls // N_PAIRS
    pair = tile_cls % N_PAIRS
    lo = jnp.asarray(PAIR_LO, jnp.int32)[pair]
    hi = jnp.asarray(PAIR_HI, jnp.int32)[pair]
    ea = (grp * EXPERTS_PER_GROUP + lo).astype(jnp.int32)
    eb = (grp * EXPERTS_PER_GROUP + hi).astype(jnp.int32)
    return pos, ea, eb, n_used.reshape(1)


def _lane_row(values, start):
    row = jnp.zeros((LANES,), F32)
    return row.at[start:start + values.shape[0]].set(values.astype(F32)).reshape(1, LANES)


def _layer(x_prompt, x_sample, mem_prompt, cache_mem_k, cache_mem_v,
           state_mlstm_C, state_mlstm_n, state_mlstm_m, state_gdn_S, state_gdn_conv,
           norm_mix, w_in, b_igate, b_fgate, mlstm_norm, conv_w, a_log, dt_bias, gdn_norm,
           mem_norm, w_mem_kv, w_br_mlstm, w_br_gdn, w_br_xattn, w_out, norm_ffn,
           w_router_group, b_router_group, w_router_expert, b_router_expert,
           w_exp_gate, w_exp_up, w_exp_down, final_norm, *, cfg):
    bp, sp, d = x_prompt.shape
    bs, ss, _ = x_sample.shape
    mem_len = mem_prompt.shape[1]
    ss_pad = SUBLANES
    assert ss <= ss_pad and d % LANES == 0
    n_p = bp * sp
    n_s = bs * ss_pad
    W = BRANCH_W
    gs_ = cfg["sample_group"]

    sizes = (W, W, W, N_HEADS, N_HEADS, W, 3 * W, N_HEADS, N_HEADS, W, W, d, d, d)
    offs = [0]
    for s in sizes:
        offs.append(offs[-1] + s)
    (m_q, m_k, m_v, m_i, m_f, m_o, g_qkv, g_a, g_b, g_out, x_q, gate_a, gate_b, gate_c) = [
        w_in[:, offs[j]:offs[j + 1]] for j in range(len(sizes))]
    w_big = _col_blocks(
        jnp.concatenate([gate_a, gate_b, gate_c, m_q, m_k, m_v, m_o, g_qkv, g_out, x_q],
                        axis=1).astype(BF16), PROJ_GROUP * W)
    w_small = jnp.concatenate(
        [m_i, m_f, g_a, g_b, jnp.zeros((d, LANES - 4 * N_HEADS), w_in.dtype)], axis=1).astype(BF16)
    w_router = jnp.concatenate(
        [w_router_group, w_router_expert,
         jnp.zeros((d, LANES - N_GROUPS - N_EXPERTS), w_in.dtype)], axis=1).astype(BF16)
    b_router = _lane_row(jnp.concatenate([b_router_group, b_router_expert]), 0)
    gate_bias = _lane_row(jnp.concatenate([b_igate, b_fgate]), 0)
    a_row = _lane_row(-jnp.exp(a_log.astype(F32)), 2 * N_HEADS)
    dt_row = _lane_row(dt_bias, 2 * N_HEADS)

    xp2 = x_prompt.reshape(n_p, d)
    xs2 = jnp.pad(x_sample, ((0, 0), (0, ss_pad - ss), (0, 0))).reshape(n_s, d)

    proj, gs = _norm_proj(xp2, xs2, norm_mix.reshape(1, d), w_big, w_small, tm=cfg["tm_proj"])

    mem2 = mem_prompt.reshape(bp * mem_len, d)
    mem_kv, _ = _norm_proj(mem2, None, mem_norm.reshape(1, d),
                           _col_blocks(w_mem_kv.astype(BF16), W),
                           jnp.zeros((d, LANES), BF16), tm=min(cfg["tm_proj"], bp * mem_len))

    mnorm = mlstm_norm.reshape(1, W)
    gnorm = gdn_norm.reshape(1, HEAD_DIM)
    s_off = n_p // (gs_ * ss_pad)
    ha_p, c_p, nn_p, mm_p = _mlstm(proj, gs, gate_bias, mnorm, None, n_batch=bp, seq=sp,
                                   s_valid=sp, row_off=0, L=cfg["chunk_m"], n_sub=cfg["sub_m"], G=1)
    m0 = jnp.broadcast_to(state_mlstm_m[:, :, None], (bs, N_HEADS, LANES))
    ha_s, c_s, nn_s, mm_s = _mlstm(proj, gs, gate_bias, mnorm,
                                   (state_mlstm_C, state_mlstm_n, m0), n_batch=bs, seq=ss_pad,
                                   s_valid=ss, row_off=s_off, L=ss_pad, n_sub=1, G=gs_)
    hb_p, s_p, cv_p = _gdn(proj, gs, conv_w, a_row, dt_row, gnorm, None, n_batch=bp, seq=sp,
                           s_valid=sp, row_off=0, L=cfg["chunk_g"], n_sub=cfg["sub_g"], G=1)
    conv0 = jnp.pad(state_gdn_conv, ((0, 0), (SUBLANES - (CONV_W - 1), 0), (0, 0)))
    hb_s, s_s, cv_s = _gdn(proj, gs, conv_w, a_row, dt_row, gnorm, (conv0, state_gdn_S),
                           n_batch=bs, seq=ss_pad, s_valid=ss, row_off=s_off, L=ss_pad,
                           n_sub=1, G=gs_)
    lt = min(sp, cfg["lt_x"])
    hc_p = _xattn(proj, mem_kv, mem_kv, n_batch=bp, seq=sp, row_off=0, lt=lt, G=1,
                  mem_len=mem_len, head_rows=False,
                  kv_specs=[pl.BlockSpec((None, mem_len, W), lambda b, t, c=c: (c, b, 0))
                            for c in range(2)])
    kv_rows = gs_ * mem_len * N_HEADS
    hc_s = _xattn(proj, cache_mem_k.reshape(bs * mem_len * N_HEADS, HEAD_DIM),
                  cache_mem_v.reshape(bs * mem_len * N_HEADS, HEAD_DIM),
                  n_batch=bs, seq=ss_pad, row_off=s_off, lt=ss_pad, G=gs_, mem_len=mem_len,
                  head_rows=True,
                  kv_specs=[pl.BlockSpec((kv_rows, HEAD_DIM), lambda b, t: (b, 0))] * 2)

    tm = cfg["tm"]
    x1rec, route, counts = _merge((ha_p, ha_s), (hb_p, hb_s), (hc_p, hc_s), (xp2, xs2), proj,
                          w_br_mlstm.astype(BF16), w_br_gdn.astype(BF16), w_br_xattn.astype(BF16),
                          w_out.astype(BF16), norm_ffn.reshape(1, d), w_router, b_router, tm=tm)

    tm_moe = cfg["tm_moe"]
    n_all = n_p + n_s
    n_tiles = (n_all + N_CLASSES * (tm_moe - 1)) // tm_moe + 1
    pos, ea, eb, n_used = _sort_plan(route[:, ROUTE_CLS].astype(jnp.int32),
                                     route[:, ROUTE_RANK].astype(jnp.int32),
                                     counts[0, :N_CLASSES].astype(jnp.int32),
                                     tm=tm_moe, n_tiles=n_tiles)
    x_sorted = _sort_scatter(pos, x1rec, n_tiles * tm_moe, tm=tm)
    y_sorted = _moe(ea, eb, n_used, x_sorted, w_exp_gate, w_exp_up, w_exp_down,
                    norm_ffn.reshape(1, d),
                    final_norm.reshape(1, d), w_router, b_router, tm=tm_moe, n_tiles=n_tiles)
    y_p = _unsort(pos, y_sorted, base=0, n_rows=n_p, tm=tm)
    y_s = _unsort(pos, y_sorted, base=n_p, n_rows=n_s, tm=tm)

    y_prompt = y_p.reshape(bp, sp, d)
    y_sample = y_s.reshape(bs, ss_pad, d)[:, :ss]
    mk_p = mem_kv[0].reshape(bp, mem_len, N_HEADS, HEAD_DIM)
    mv_p = mem_kv[1].reshape(bp, mem_len, N_HEADS, HEAD_DIM)
    tail = slice(SUBLANES - (CONV_W - 1), SUBLANES)
    return (y_prompt, y_sample, mk_p, mv_p,
            c_p, nn_p, mm_p[:, :, 0], s_p, cv_p[:, tail],
            c_s, nn_s, mm_s[:, :, 0], s_s, cv_s[:, tail])


CONFIG = dict(tm_proj=1024, tm=512, tm_moe=256, chunk_m=256, sub_m=1, chunk_g=64, sub_g=8,
              lt_x=1024, sample_group=16)


def kernel(x_prompt, x_sample, mem_prompt, cache_mem_k, cache_mem_v, state_mlstm_C, state_mlstm_n, state_mlstm_m, state_gdn_S, state_gdn_conv, norm_mix, w_in, b_igate, b_fgate, mlstm_norm, conv_w, a_log, dt_bias, gdn_norm, mem_norm, w_mem_kv, w_br_mlstm, w_br_gdn, w_br_xattn, w_out, norm_ffn, w_router_group, b_router_group, w_router_expert, b_router_expert, w_exp_gate, w_exp_up, w_exp_down, final_norm):
    depth = w_in.shape[0]
    assert depth == 1, "one decoder layer"
    outs = _layer(
        x_prompt, x_sample, mem_prompt, cache_mem_k[0], cache_mem_v[0],
        state_mlstm_C[0], state_mlstm_n[0], state_mlstm_m[0], state_gdn_S[0], state_gdn_conv[0],
        norm_mix[0], w_in[0], b_igate[0], b_fgate[0], mlstm_norm[0], conv_w[0], a_log[0],
        dt_bias[0], gdn_norm[0], mem_norm[0], w_mem_kv[0], w_br_mlstm[0], w_br_gdn[0],
        w_br_xattn[0], w_out[0], norm_ffn[0], w_router_group[0], b_router_group[0],
        w_router_expert[0], b_router_expert[0], w_exp_gate[0], w_exp_up[0], w_exp_down[0],
        final_norm, cfg=CONFIG)
    y_prompt, y_sample = outs[0], outs[1]
    return (y_prompt, y_sample) + tuple(o[None] for o in outs[2:])
```

```python
import functools

import jax
import jax.numpy as jnp
from jax import lax
from jax.experimental import pallas as pl
from jax.experimental.pallas import tpu as pltpu

F32 = jnp.float32
BF16 = jnp.bfloat16
EPS = 1e-6

N_HEADS = 4
HEAD_DIM = 128
BRANCH_W = N_HEADS * HEAD_DIM
CONV_W = 4
N_GROUPS = 4
EXPERTS_PER_GROUP = 4
N_EXPERTS = N_GROUPS * EXPERTS_PER_GROUP
N_PAIRS = 6
N_CLASSES = N_GROUPS * N_PAIRS
PAIR_LO = (0, 0, 0, 1, 1, 2)
PAIR_HI = (1, 2, 3, 2, 3, 3)
NEG_BIG = -1e30
LANES = 128
SUBLANES = 8
TOKEN_ROWS = 8
PROJ_GROUP = 5
VMEM_PHYSICAL = 64 * 1024 * 1024
VMEM_LIMIT = 48 * 1024 * 1024
VMEM_SLACK = 4 * 1024 * 1024

COL_GATE = 0
COL_MQ, COL_MK, COL_MV, COL_MO = 6, 7, 8, 9
COL_GQ, COL_GK, COL_GV, COL_GO, COL_XQ = 10, 11, 12, 13, 14
N_PROJ = 15 * BRANCH_W


def _dot(a, b):
    return jnp.dot(a.astype(BF16), b.astype(BF16), preferred_element_type=F32)


def _dot_nt(a, b):
    return lax.dot_general(a.astype(BF16), b.astype(BF16), (((1,), (1,)), ((), ())),
                           preferred_element_type=F32)


def _dot_tn(a, b):
    return lax.dot_general(a.astype(BF16), b.astype(BF16), (((0,), (0,)), ((), ())),
                           preferred_element_type=F32)


def _dot_exact(a, b):
    return jnp.dot(a, b, preferred_element_type=F32, precision=lax.Precision.HIGHEST)


def _lanes_to_rows(x, lane0):
    r = lax.broadcasted_iota(jnp.int32, (SUBLANES, LANES), 0)
    c = lax.broadcasted_iota(jnp.int32, (SUBLANES, LANES), 1)
    sel = (c == r + lane0).astype(F32)
    return lax.dot_general(sel, x, (((1,), (1,)), ((), ())), preferred_element_type=F32,
                           precision=lax.Precision.HIGHEST)


def _rms(x, g):
    return x * lax.rsqrt(jnp.mean(x * x, axis=-1, keepdims=True) + EPS) * g


def _softplus(x):
    return jnp.maximum(x, 0.0) + jnp.log1p(jnp.exp(-jnp.abs(x)))


def _sigmoid(x):
    return 0.5 * jnp.tanh(0.5 * x) + 0.5


def _silu(x):
    return x * _sigmoid(x)


def _tri(n, strict=False):
    r = lax.broadcasted_iota(jnp.int32, (n, n), 0)
    c = lax.broadcasted_iota(jnp.int32, (n, n), 1)
    return (c < r) if strict else (c <= r)


def _tri_blocks(n, block, upper=False):
    shift = block.bit_length() - 1
    assert block == 1 << shift
    r = lax.broadcasted_iota(jnp.int32, (n, n), 0)
    c = lax.broadcasted_iota(jnp.int32, (n, n), 1)
    same = jnp.right_shift(r, shift) == jnp.right_shift(c, shift)
    return same & ((r <= c) if upper else (c <= r))


def _chunk_cumsum(x, x_t, L):
    rows = x.shape[0]
    blk = max(L, LANES)
    if rows % blk:
        blk = rows
    lower = _tri_blocks(blk, L).astype(F32)
    upper = _tri_blocks(blk, L, upper=True).astype(F32)
    cols = [_dot_exact(lower, x[c * blk:(c + 1) * blk]) for c in range(rows // blk)]
    rws = [_dot_exact(x_t[:, c * blk:(c + 1) * blk], upper) for c in range(rows // blk)]
    if len(cols) == 1:
        return cols[0], rws[0]
    return jnp.concatenate(cols, axis=0), jnp.concatenate(rws, axis=1)


def _pad_rows(shape, t, seq_rows, n_valid):
    assert seq_rows & (seq_rows - 1) == 0
    row = lax.broadcasted_iota(jnp.int32, shape, 0)
    return (row & (seq_rows - 1)) + t * seq_rows >= n_valid


def _params(sem, vmem_bytes=VMEM_LIMIT):
    return pltpu.CompilerParams(dimension_semantics=sem, vmem_limit_bytes=vmem_bytes)


def _two_source_specs(n_first, block, width):
    first = pl.BlockSpec((block, width), lambda i, *_: (jnp.minimum(i, n_first - 1), 0))
    second = pl.BlockSpec((block, width), lambda i, *_: (jnp.maximum(i - n_first, 0), 0))
    return first, second


def _norm_proj_kernel(xa_ref, xb_ref, g_ref, w_ref, ws_ref, o_ref, os_ref, hb_ref, *, n_first):
    i = pl.program_id(0)
    j = pl.program_id(1)

    @pl.when(j == 0)
    def _():
        @pl.when(i < n_first)
        def _():
            hb_ref[...] = _rms(xa_ref[...], g_ref[...]).astype(BF16)

        @pl.when(i >= n_first)
        def _():
            hb_ref[...] = _rms(xb_ref[...], g_ref[...]).astype(BF16)

        os_ref[...] = jnp.dot(hb_ref[...], ws_ref[...], preferred_element_type=F32)

    o_ref[...] = jnp.dot(hb_ref[...], w_ref[0], preferred_element_type=F32)


def _proj_spec(rows, c, row_block):
    return pl.BlockSpec((None, rows, BRANCH_W),
                        lambda *g: (c // PROJ_GROUP, row_block(*g), c % PROJ_GROUP))


def _col_blocks(w, tn):
    d, n = w.shape
    return w.reshape(d, n // tn, tn).transpose(1, 0, 2)


def _norm_proj(xa, xb, g, w, ws, *, tm):
    d = xa.shape[1]
    n_first = xa.shape[0] // tm
    n_rows = xa.shape[0] + (0 if xb is None else xb.shape[0])
    if xb is None:
        xb = xa
    tn = w.shape[2]
    n = w.shape[0] * tn
    sa, sb = _two_source_specs(n_first, tm, d)
    vmem = (4 * tm * d * 4 + 2 * d * tn * 2 + 2 * tm * tn * 4 + tm * d * 2
            + 2 * (tm + d) * ws.shape[1] * 4)
    vmem = min(max(vmem + VMEM_SLACK, VMEM_LIMIT), VMEM_PHYSICAL - VMEM_SLACK)
    return pl.pallas_call(
        functools.partial(_norm_proj_kernel, n_first=n_first),
        out_shape=(jax.ShapeDtypeStruct((n // tn, n_rows, tn), F32),
                   jax.ShapeDtypeStruct((n_rows, ws.shape[1]), F32)),
        grid=(n_rows // tm, n // tn),
        in_specs=[sa, sb,
                  pl.BlockSpec((1, d), lambda i, j: (0, 0)),
                  pl.BlockSpec((1, d, tn), lambda i, j: (j, 0, 0)),
                  pl.BlockSpec((d, ws.shape[1]), lambda i, j: (0, 0))],
        out_specs=(pl.BlockSpec((None, tm, tn), lambda i, j: (j, i, 0)),
                   pl.BlockSpec((tm, ws.shape[1]), lambda i, j: (i, 0))),
        scratch_shapes=[pltpu.VMEM((tm, d), BF16)],
        compiler_params=_params(("parallel", "arbitrary"), vmem),
        name="norm_proj",
    )(xa, xb, g, w, ws)


def _seq_tile_specs(G, Lt, nt, row_off):
    assert G == 1 or nt == 1

    def rows(b, t):
        return row_off + b * nt + t

    return rows


def _mlstm_kernel(*refs, L, n_sub, G, s_valid, s_total, has_init):
    if has_init:
        (q_ref, k_ref, v_ref, og_ref, gs_ref, bias_ref, norm_ref, c0_ref, n0_ref, m0_ref,
         h_ref, c_ref, n_ref, m_ref) = refs
    else:
        (q_ref, k_ref, v_ref, og_ref, gs_ref, bias_ref, norm_ref,
         h_ref, c_ref, n_ref, m_ref) = refs
    t = pl.program_id(1)
    Lt = n_sub * L
    rows = G * Lt

    @pl.when(t == 0)
    def _():
        if has_init:
            c_ref[...] = c0_ref[...]
            n_ref[...] = n0_ref[...]
            m_ref[...] = m0_ref[...]
        else:
            c_ref[...] = jnp.zeros_like(c_ref)
            n_ref[...] = jnp.zeros_like(n_ref)
            m_ref[...] = jnp.zeros_like(m_ref)

    z = gs_ref[...] + bias_ref[...]
    lane = lax.broadcasted_iota(jnp.int32, z.shape, 1)
    log_f = jnp.minimum(z, 0.0) - jnp.log1p(jnp.exp(-jnp.abs(z)))
    g = jnp.where((lane >= N_HEADS) & (lane < 2 * N_HEADS), log_f, z)
    if s_valid < s_total:
        pad = _pad_rows(z.shape, t, Lt, s_valid)
        g = jnp.where(pad, jnp.where(lane < N_HEADS, NEG_BIG, 0.0), g)
    g_t = _lanes_to_rows(g, 0)
    bt_cols, bt_rows = _chunk_cumsum(g, g_t, L)
    scale = HEAD_DIM ** -0.5
    assert G == 1 or n_sub == 1
    n_r = G * L
    mask = _tri_blocks(n_r, L)

    def per_seq(x):
        return [x[j * L:(j + 1) * L] for j in range(G)]

    def expand(pieces):
        if G == 1:
            return pieces[0]
        return jnp.concatenate([jnp.broadcast_to(x, (L, x.shape[1])) for x in pieces], axis=0)

    for c in range(n_sub):
        r = slice(c * L, c * L + n_r)
        P = []
        for h in range(N_HEADS):
            sl = slice(h * HEAD_DIM, (h + 1) * HEAD_DIM)
            p = dict(h=h, sl=sl, q=q_ref[r, sl], k=k_ref[r, sl] * scale, v=v_ref[r, sl],
                     bt_c=bt_cols[r, N_HEADS + h:N_HEADS + h + 1], it_c=g[r, h:h + 1])
            bt_r = bt_rows[N_HEADS + h:N_HEADS + h + 1, r]
            it_r = g_t[h:h + 1, r]
            p["log_d"] = jnp.where(mask, p["bt_c"] - bt_r + it_r, -jnp.inf)
            p["c_prev"] = [c_ref[s, h] for s in range(G)]
            p["n_prev"] = [n_ref[s, h:h + 1, :] for s in range(G)]
            p["m_prev"] = [m_ref[s, h:h + 1, 0:1] for s in range(G)]
            P.append(p)
        for p in P:
            p["qk"] = _dot_nt(p["q"], p["k"])
        for p in P:
            qs = per_seq(p["q"])
            qc = [_dot(qs[s], p["c_prev"][s]) for s in range(G)]
            p["qc"] = qc[0] if G == 1 else jnp.concatenate(qc, axis=0)
        for p in P:
            inter = p["bt_c"] + expand(p["m_prev"])
            p["m_t"] = jnp.maximum(inter, jnp.max(p["log_d"], axis=-1, keepdims=True))
            p["inter_w"] = jnp.exp(inter - p["m_t"])
            p["sd"] = p["qk"] * jnp.exp(p["log_d"] - p["m_t"])
        for p in P:
            p["sv"] = _dot(p["sd"], p["v"])
        for p in P:
            bt_last = [x[L - 1:L, :] for x in per_seq(p["bt_c"])]
            p["m_new"] = [x[L - 1:L, :] for x in per_seq(p["m_t"])]
            w = jnp.exp(expand(bt_last) - p["bt_c"] + p["it_c"] - expand(p["m_new"]))
            p["decay"] = [jnp.exp(bt_last[s] + p["m_prev"][s] - p["m_new"][s]) for s in range(G)]
            p["kw"] = p["k"] * w
        for p in P:
            kws, vs = per_seq(p["kw"]), per_seq(p["v"])
            p["kv"] = [_dot_tn(kws[s], vs[s]) for s in range(G)]
            p["n_new"] = [p["decay"][s] * p["n_prev"][s] + jnp.sum(kws[s], axis=0, keepdims=True)
                          for s in range(G)]
        for p in P:
            num = p["sv"] + p["qc"] * p["inter_w"]
            den = (jnp.sum(p["sd"], axis=-1, keepdims=True)
                   + p["inter_w"] * jnp.sum(p["q"] * expand(p["n_prev"]), axis=-1, keepdims=True))
            hh = num / jnp.maximum(jnp.abs(den), jnp.exp(-p["m_t"]))
            hn = hh * lax.rsqrt(jnp.mean(hh * hh, axis=-1, keepdims=True) + EPS) * norm_ref[:, p["sl"]]
            p["h_out"] = hn * _sigmoid(og_ref[r, p["sl"]])
        for p in P:
            h = p["h"]
            h_ref[r, p["sl"]] = p["h_out"]
            for s in range(G):
                n_ref[s, h:h + 1, :] = p["n_new"][s]
                m_ref[s, h:h + 1, :] = jnp.broadcast_to(p["m_new"][s], (1, LANES))
                c_ref[s, h] = p["decay"][s] * p["c_prev"][s] + p["kv"][s]


def _mlstm(proj, gs, bias_row, norm_row, init, *, n_batch, seq, s_valid, row_off, L, n_sub, G):
    Lt = n_sub * L
    nt = seq // Lt
    rows = G * Lt
    has_init = init is not None
    blk = _seq_tile_specs(G, Lt, nt, row_off)

    def col(c):
        return _proj_spec(rows, c, blk)

    in_specs = [col(COL_MQ), col(COL_MK), col(COL_MV), col(COL_MO),
                pl.BlockSpec((rows, LANES), lambda b, t: (blk(b, t), 0)),
                pl.BlockSpec((1, LANES), lambda b, t: (0, 0)),
                pl.BlockSpec((1, BRANCH_W), lambda b, t: (0, 0))]
    args = [proj, proj, proj, proj, gs, bias_row, norm_row]
    c_spec = pl.BlockSpec((G, N_HEADS, HEAD_DIM, HEAD_DIM), lambda b, t: (b, 0, 0, 0))
    n_spec = pl.BlockSpec((G, N_HEADS, HEAD_DIM), lambda b, t: (b, 0, 0))
    m_spec = pl.BlockSpec((G, N_HEADS, LANES), lambda b, t: (b, 0, 0))
    if has_init:
        in_specs += [c_spec, n_spec, m_spec]
        args += list(init)
    return pl.pallas_call(
        functools.partial(_mlstm_kernel, L=L, n_sub=n_sub, G=G, s_valid=s_valid, s_total=seq,
                          has_init=has_init),
        out_shape=(jax.ShapeDtypeStruct((n_batch * seq, BRANCH_W), F32),
                   jax.ShapeDtypeStruct((n_batch, N_HEADS, HEAD_DIM, HEAD_DIM), F32),
                   jax.ShapeDtypeStruct((n_batch, N_HEADS, HEAD_DIM), F32),
                   jax.ShapeDtypeStruct((n_batch, N_HEADS, LANES), F32)),
        grid=(n_batch // G, nt),
        in_specs=in_specs,
        out_specs=(pl.BlockSpec((rows, BRANCH_W), lambda b, t: (b * nt + t, 0)),
                   c_spec, n_spec, m_spec),
        compiler_params=_params(("parallel", "arbitrary")),
        name="mlstm",
    )(*args)


def _gdn_kernel(*refs, L, n_sub, G, s_valid, s_total, has_init):
    if has_init:
        (q_ref, k_ref, v_ref, og_ref, gs_ref, cw_ref, arow_ref, dtrow_ref, norm_ref,
         conv0_ref, s0_ref, o_ref, s_ref, conv1_ref, ext_ref, conv_ref) = refs
    else:
        (q_ref, k_ref, v_ref, og_ref, gs_ref, cw_ref, arow_ref, dtrow_ref, norm_ref,
         o_ref, s_ref, conv1_ref, ext_ref, conv_ref) = refs
    t = pl.program_id(1)
    Lt = n_sub * L
    rows = G * Lt
    nt = s_total // Lt
    W = BRANCH_W

    @pl.when(t == 0)
    def _():
        if has_init:
            s_ref[...] = s0_ref[...]
            ext_ref[:, 0:SUBLANES, :] = conv0_ref[...]
        else:
            s_ref[...] = jnp.zeros_like(s_ref)
            ext_ref[:, 0:SUBLANES, :] = jnp.zeros((G, SUBLANES, 3 * W), F32)

    @pl.when(t > 0)
    def _():
        ext_ref[:, 0:SUBLANES, :] = ext_ref[:, Lt:Lt + SUBLANES, :]

    base = SUBLANES - (CONV_W - 1)
    for s in range(G):
        rs = slice(s * Lt, (s + 1) * Lt)
        ext_ref[s, SUBLANES:SUBLANES + Lt, 0:W] = q_ref[rs, :]
        ext_ref[s, SUBLANES:SUBLANES + Lt, W:2 * W] = k_ref[rs, :]
        ext_ref[s, SUBLANES:SUBLANES + Lt, 2 * W:3 * W] = v_ref[rs, :]
        acc = ext_ref[s, base:base + Lt, :] * cw_ref[0:1, :]
        for j in range(1, CONV_W):
            acc = acc + ext_ref[s, base + j:base + j + Lt, :] * cw_ref[j:j + 1, :]
        conv_ref[rs, :] = _silu(acc)

    @pl.when(t == nt - 1)
    def _():
        v_last = s_valid - (nt - 1) * Lt
        conv1_ref[...] = ext_ref[:, v_last:v_last + SUBLANES, :]

    z = gs_ref[...]
    lane = lax.broadcasted_iota(jnp.int32, z.shape, 1)
    gdec = arow_ref[...] * _softplus(z + dtrow_ref[...])
    gb = jnp.where((lane >= 2 * N_HEADS) & (lane < 3 * N_HEADS), gdec,
                   jnp.where((lane >= 3 * N_HEADS) & (lane < 4 * N_HEADS), _sigmoid(z), 0.0))
    if s_valid < s_total:
        gb = jnp.where(_pad_rows(z.shape, t, Lt, s_valid), 0.0, gb)
    gb_t = _lanes_to_rows(gb, 2 * N_HEADS)
    gam_cols, gam_rows = _chunk_cumsum(gb, gb_t, L)
    incl = _tri(L)
    strict = _tri(L, strict=True)
    eye = (lax.broadcasted_iota(jnp.int32, (L, L), 0)
           == lax.broadcasted_iota(jnp.int32, (L, L), 1)).astype(F32)

    P = []
    for s in range(G):
        for c in range(n_sub):
            for h in range(N_HEADS):
                r = slice(s * Lt + c * L, s * Lt + (c + 1) * L)
                sl = slice(h * HEAD_DIM, (h + 1) * HEAD_DIM)
                q = conv_ref[r, h * HEAD_DIM:(h + 1) * HEAD_DIM]
                k = conv_ref[r, W + h * HEAD_DIM:W + (h + 1) * HEAD_DIM]
                v = conv_ref[r, 2 * W + h * HEAD_DIM:2 * W + (h + 1) * HEAD_DIM]
                q = q * lax.rsqrt(jnp.sum(q * q, axis=-1, keepdims=True) + EPS) * (HEAD_DIM ** -0.5)
                k = k * lax.rsqrt(jnp.sum(k * k, axis=-1, keepdims=True) + EPS)
                gam_c = gam_cols[r, 2 * N_HEADS + h:2 * N_HEADS + h + 1]
                gam_r = gam_rows[h:h + 1, r]
                beta_c = gb[r, 3 * N_HEADS + h:3 * N_HEADS + h + 1]
                decay = jnp.exp(jnp.where(incl, gam_c - gam_r, -jnp.inf))
                P.append(dict(s=s, c=c, h=h, r=r, sl=sl, q=q, k=k, v=v, gam_c=gam_c,
                              beta_c=beta_c, decay=decay))
    for p in P:
        p["kk"] = _dot_nt(p["k"], p["k"])
    for p in P:
        p["qk"] = _dot_nt(p["q"], p["k"]) * p["decay"]
    for p in P:
        e_gam = jnp.exp(p["gam_c"])
        gam_last = p["gam_c"][L - 1:L, :]
        p["pw"] = -jnp.where(strict, p["beta_c"] * p["kk"] * p["decay"], 0.0)
        p["x"] = eye + p["pw"]
        p["rhs"] = jnp.concatenate([p["beta_c"] * p["v"], p["beta_c"] * p["k"] * e_gam], axis=-1)
        p["q_dec"] = p["q"] * e_gam
        p["k_dec"] = p["k"] * jnp.exp(gam_last - p["gam_c"])
        p["e_last"] = jnp.exp(gam_last)
    n = 2
    while n < L:
        for p in P:
            p["pw"] = _dot(p["pw"], p["pw"])
        for p in P:
            p["x"] = p["x"] + _dot(p["x"], p["pw"])
        n *= 2
    for p in P:
        sol = _dot(p["x"], p["rhs"])
        p["u"] = sol[:, :HEAD_DIM]
        p["w"] = sol[:, HEAD_DIM:]

    for c in range(n_sub):
        Pc = [p for p in P if p["c"] == c]
        for p in Pc:
            p["s_prev"] = s_ref[p["s"], p["h"]]
            p["ws"] = _dot(p["w"], p["s_prev"])
        for p in Pc:
            p["qs"] = _dot(p["q_dec"], p["s_prev"])
        for p in Pc:
            p["v_new"] = p["u"] - p["ws"]
        for p in Pc:
            p["o"] = p["qs"] + _dot(p["qk"], p["v_new"])
        for p in Pc:
            p["kv"] = _dot_tn(p["k_dec"], p["v_new"])
        for p in Pc:
            o = p["o"]
            on = o * lax.rsqrt(jnp.mean(o * o, axis=-1, keepdims=True) + EPS) * norm_ref[...]
            p["o_out"] = on * _silu(og_ref[p["r"], p["sl"]])
        for p in Pc:
            s_ref[p["s"], p["h"]] = p["e_last"] * p["s_prev"] + p["kv"]
            o_ref[p["r"], p["sl"]] = p["o_out"]


def _gdn(proj, gs, conv_w, a_row, dt_row, norm_row, init, *, n_batch, seq, s_valid, row_off,
         L, n_sub, G):
    Lt = n_sub * L
    nt = seq // Lt
    rows = G * Lt
    has_init = init is not None
    blk = _seq_tile_specs(G, Lt, nt, row_off)

    def col(c):
        return _proj_spec(rows, c, blk)

    def const(shape):
        return pl.BlockSpec(shape, lambda b, t: (0,) * len(shape))

    in_specs = [col(COL_GQ), col(COL_GK), col(COL_GV), col(COL_GO),
                pl.BlockSpec((rows, LANES), lambda b, t: (blk(b, t), 0)),
                const((CONV_W, 3 * BRANCH_W)), const((1, LANES)), const((1, LANES)),
                const((1, HEAD_DIM))]
    args = [proj, proj, proj, proj, gs, conv_w, a_row, dt_row, norm_row]
    s_spec = pl.BlockSpec((G, N_HEADS, HEAD_DIM, HEAD_DIM), lambda b, t: (b, 0, 0, 0))
    conv_spec = pl.BlockSpec((G, SUBLANES, 3 * BRANCH_W), lambda b, t: (b, 0, 0))
    if has_init:
        in_specs += [conv_spec, s_spec]
        args += list(init)
    return pl.pallas_call(
        functools.partial(_gdn_kernel, L=L, n_sub=n_sub, G=G, s_valid=s_valid, s_total=seq,
                          has_init=has_init),
        out_shape=(jax.ShapeDtypeStruct((n_batch * seq, BRANCH_W), F32),
                   jax.ShapeDtypeStruct((n_batch, N_HEADS, HEAD_DIM, HEAD_DIM), F32),
                   jax.ShapeDtypeStruct((n_batch, SUBLANES, 3 * BRANCH_W), F32)),
        grid=(n_batch // G, nt),
        in_specs=in_specs,
        out_specs=(pl.BlockSpec((rows, BRANCH_W), lambda b, t: (b * nt + t, 0)),
                   s_spec, conv_spec),
        scratch_shapes=[pltpu.VMEM((G, Lt + SUBLANES, 3 * BRANCH_W), F32),
                        pltpu.VMEM((rows, 3 * BRANCH_W), F32)],
        compiler_params=_params(("parallel", "arbitrary")),
        name="gdn",
    )(*args)


def _xattn_kernel(q_ref, mk_ref, mv_ref, o_ref, *, G, lt, mem_len, head_rows):
    scale = HEAD_DIM ** -0.5

    def head(ref, s, h):
        if head_rows:
            return ref[pl.ds(s * mem_len * N_HEADS + h, mem_len, stride=N_HEADS), :]
        return ref[s * mem_len:(s + 1) * mem_len, h * HEAD_DIM:(h + 1) * HEAD_DIM]

    P = []
    for s in range(G):
        for h in range(N_HEADS):
            P.append(dict(s=s, h=h, r=slice(s * lt, (s + 1) * lt),
                          sl=slice(h * HEAD_DIM, (h + 1) * HEAD_DIM)))
    for p in P:
        p["sc"] = _dot_nt(q_ref[p["r"], p["sl"]], head(mk_ref, p["s"], p["h"])) * scale
    for p in P:
        e = jnp.exp(p["sc"] - jnp.max(p["sc"], axis=-1, keepdims=True))
        p["p"] = e / jnp.sum(e, axis=-1, keepdims=True)
    for p in P:
        o_ref[p["r"], p["sl"]] = _dot(p["p"], head(mv_ref, p["s"], p["h"]))


def _xattn(proj, mk, mv, *, n_batch, seq, row_off, lt, G, mem_len, kv_specs, head_rows):
    nt = seq // lt
    blk = _seq_tile_specs(G, lt, nt, row_off)
    return pl.pallas_call(
        functools.partial(_xattn_kernel, G=G, lt=lt, mem_len=mem_len, head_rows=head_rows),
        out_shape=jax.ShapeDtypeStruct((n_batch * seq, BRANCH_W), F32),
        grid=(n_batch // G, nt),
        in_specs=[_proj_spec(G * lt, COL_XQ, blk)] + list(kv_specs),
        out_specs=pl.BlockSpec((G * lt, BRANCH_W), lambda b, t: (b * nt + t, 0)),
        compiler_params=_params(("parallel", "arbitrary")),
        name="xattn",
    )(proj, mk, mv)


def _lane_first_max(x, mask, lane_f):
    xm = jnp.where(mask, x, -jnp.inf)
    mx = jnp.max(xm, axis=-1, keepdims=True)
    idx = jnp.min(jnp.where(xm == mx, lane_f, float(LANES)), axis=-1, keepdims=True)
    return mx, idx


def _store_token_tiles(ref, x, tok_rows):
    n = x.shape[0]
    for j in range(x.shape[1] // LANES):
        ref[pl.ds(j, n, stride=tok_rows), :] = x[:, j * LANES:(j + 1) * LANES]


def _load_token_tiles(ref, n, tok_rows, n_chunks=SUBLANES):
    return jnp.concatenate([ref[pl.ds(j, n, stride=tok_rows), :] for j in range(n_chunks)], axis=1)


def _merge_kernel(hap_ref, has_ref, hbp_ref, hbs_ref, hcp_ref, hcs_ref, xp_ref, xs_ref,
                  ga0_ref, ga1_ref, gb0_ref, gb1_ref, gc0_ref, gc1_ref,
                  wa_ref, wb_ref, wc_ref, wo_ref, nf_ref, wr_ref, br_ref,
                  x1_ref, route_ref, count_ref, *, n_first):
    i = pl.program_id(0)

    @pl.when(i == 0)
    def _():
        count_ref[...] = jnp.zeros_like(count_ref)

    def gate(lo_ref, hi_ref):
        return jnp.concatenate([_sigmoid(lo_ref[...]), _sigmoid(hi_ref[...])], axis=1)

    def body(ha_ref, hb_ref, hc_ref, x_ref):
        mixed = (gate(ga0_ref, ga1_ref) * _dot(ha_ref[...], wa_ref[...])
                 + gate(gb0_ref, gb1_ref) * _dot(hb_ref[...], wb_ref[...])
                 + gate(gc0_ref, gc1_ref) * _dot(hc_ref[...], wc_ref[...]))
        x1 = x_ref[...] + _dot(mixed, wo_ref[...])
        h2 = _rms(x1, nf_ref[...])
        logits = _dot(h2, wr_ref[...]) + br_ref[...]
        lane = lax.broadcasted_iota(jnp.int32, logits.shape, 1)
        lane_f = lane.astype(F32)
        is_g = lane < N_GROUPS
        _, g_idx = _lane_first_max(logits, is_g, lane_f)
        e_lo = N_GROUPS + EXPERTS_PER_GROUP * g_idx
        in_grp = (lane_f >= e_lo) & (lane_f < e_lo + EXPERTS_PER_GROUP)
        _, i1 = _lane_first_max(logits, in_grp, lane_f)
        _, i2 = _lane_first_max(logits, in_grp & (lane_f != i1), lane_f)
        lo = jnp.minimum(i1, i2) - e_lo
        hi = jnp.maximum(i1, i2) - e_lo
        pair = lo * (7.0 - lo) * 0.5 + (hi - lo - 1.0)
        cls = g_idx * float(N_PAIRS) + pair
        _store_token_tiles(x1_ref, x1, TOKEN_ROWS)
        tm = logits.shape[0]
        onehot = (lane_f == cls).astype(F32)
        earlier = jnp.dot(_tri(tm, strict=True).astype(BF16), onehot.astype(BF16),
                          preferred_element_type=F32)
        rank = jnp.sum((earlier + count_ref[0:1, :]) * onehot, axis=-1, keepdims=True)
        count_ref[0:1, :] = count_ref[0:1, :] + jnp.sum(onehot, axis=0, keepdims=True)
        route_ref[...] = jnp.where(lane == ROUTE_CLS, cls, jnp.where(lane == ROUTE_RANK, rank, 0.0))

    @pl.when(i < n_first)
    def _():
        body(hap_ref, hbp_ref, hcp_ref, xp_ref)

    @pl.when(i >= n_first)
    def _():
        body(has_ref, hbs_ref, hcs_ref, xs_ref)


def _merge(ha, hb, hc, xs, proj, wa, wb, wc, wo, nf, wr, br, *, tm):
    d = xs[0].shape[1]
    n_first = xs[0].shape[0] // tm
    n_rows = xs[0].shape[0] + xs[1].shape[0]
    specs = []
    for width in (BRANCH_W, BRANCH_W, BRANCH_W, d):
        specs += list(_two_source_specs(n_first, tm, width))

    def const(a):
        return pl.BlockSpec(a.shape, lambda i: (0,) * a.ndim)

    n_gate = 3 * d // BRANCH_W
    specs += [_proj_spec(tm, COL_GATE + c, lambda i: i) for c in range(n_gate)]
    weights = [wa, wb, wc, wo, nf, wr, br]
    specs += [const(a) for a in weights]
    return pl.pallas_call(
        functools.partial(_merge_kernel, n_first=n_first),
        out_shape=(jax.ShapeDtypeStruct((n_rows * TOKEN_ROWS, LANES), F32),
                   jax.ShapeDtypeStruct((n_rows, LANES), F32),
                   jax.ShapeDtypeStruct((SUBLANES, LANES), F32)),
        grid=(n_rows // tm,),
        in_specs=specs,
        out_specs=(pl.BlockSpec((tm * TOKEN_ROWS, LANES), lambda i: (i, 0)),
                   pl.BlockSpec((tm, LANES), lambda i: (i, 0)),
                   pl.BlockSpec((SUBLANES, LANES), lambda i: (0, 0))),
        compiler_params=_params(("arbitrary",)),
        name="merge",
    )(*ha, *hb, *hc, *xs, *([proj] * n_gate), *weights)


GATHER_UNROLL = 8
N_DMA_PRIORITIES = 2
ROUTE_CLS, ROUTE_RANK = 0, 1


def _record_gather_start(src_ref, dst_ref, sem, ids_ref, base, n, rec):
    def issue(r2, carry):
        for prio in range(N_DMA_PRIORITIES):
            r = r2 * N_DMA_PRIORITIES + prio
            src = pl.multiple_of(ids_ref[base + r] * rec, rec)
            dst = pl.multiple_of(r * rec, rec)
            pltpu.make_async_copy(src_ref.at[pl.ds(src, rec)], dst_ref.at[pl.ds(dst, rec)],
                                  sem).start(priority=prio)
        return carry
    lax.fori_loop(0, n // N_DMA_PRIORITIES, issue, 0, unroll=GATHER_UNROLL // N_DMA_PRIORITIES)


def _record_gather_wait(src_ref, dst_ref, sem):
    pltpu.make_async_copy(src_ref.at[pl.ds(0, dst_ref.shape[0])], dst_ref, sem).wait()


def _sort_scatter_kernel(pos_ref, x_ref, init_hbm, out_hbm, sem, *, tm):
    del init_hbm
    base = pl.program_id(0) * tm

    def issue(r2, carry):
        for prio in range(N_DMA_PRIORITIES):
            r = r2 * N_DMA_PRIORITIES + prio
            src = pl.multiple_of(r * TOKEN_ROWS, TOKEN_ROWS)
            dst = pl.multiple_of(pos_ref[base + r] * TOKEN_ROWS, TOKEN_ROWS)
            pltpu.make_async_copy(x_ref.at[pl.ds(src, TOKEN_ROWS)],
                                  out_hbm.at[pl.ds(dst, TOKEN_ROWS)], sem).start(priority=prio)
        return carry
    lax.fori_loop(0, tm // N_DMA_PRIORITIES, issue, 0, unroll=GATHER_UNROLL // N_DMA_PRIORITIES)
    pltpu.make_async_copy(x_ref, out_hbm.at[pl.ds(0, tm * TOKEN_ROWS)], sem).wait()


def _sort_scatter(pos, x1rec, n_sorted, *, tm):
    n_tok = x1rec.shape[0] // TOKEN_ROWS
    grid_spec = pltpu.PrefetchScalarGridSpec(
        num_scalar_prefetch=1,
        grid=(n_tok // tm,),
        in_specs=[pl.BlockSpec((tm * TOKEN_ROWS, LANES), lambda i, *_: (i, 0)),
                  pl.BlockSpec(memory_space=pl.ANY)],
        out_specs=pl.BlockSpec(memory_space=pl.ANY),
        scratch_shapes=[pltpu.SemaphoreType.DMA(())],
    )
    return pl.pallas_call(
        functools.partial(_sort_scatter_kernel, tm=tm),
        out_shape=jax.ShapeDtypeStruct((n_sorted * TOKEN_ROWS, LANES), F32),
        grid_spec=grid_spec,
        input_output_aliases={2: 0},
        compiler_params=_params(("arbitrary",)),
        name="sort_scatter",
    )(pos, x1rec, jnp.zeros((n_sorted * TOKEN_ROWS, LANES), F32))


def _moe_kernel(ea_ref, eb_ref, nused_ref, x_ref, wga_ref, wua_ref, wda_ref,
                wgb_ref, wub_ref, wdb_ref, nf_ref, fin_ref, wr_ref, br_ref, y_ref,
                wg_bf, wu_bf, wd_bf, *, tm, d):
    i = pl.program_id(0)
    n_used = nused_ref[0]

    prev = jnp.maximum(i - 1, 0)
    for slot, e_ref, srcs in ((0, ea_ref, (wga_ref, wua_ref, wda_ref)),
                              (1, eb_ref, (wgb_ref, wub_ref, wdb_ref))):
        @pl.when((i < n_used) & ((i == 0) | (e_ref[i] != e_ref[prev])))
        def _():
            for dst, src in zip((wg_bf, wu_bf, wd_bf), srcs):
                dst[slot] = src[0].astype(BF16)

    @pl.when(i < n_used)
    def _():
        x1 = _load_token_tiles(x_ref, tm, TOKEN_ROWS)
        h2 = _rms(x1, nf_ref[...]).astype(BF16)
        logits = jnp.dot(h2, wr_ref[...], preferred_element_type=F32) + br_ref[...]
        lane = lax.broadcasted_iota(jnp.int32, logits.shape, 1)

        def pick(l):
            return jnp.sum(jnp.where(lane == l, logits, 0.0), axis=-1, keepdims=True)

        ea, eb = ea_ref[i], eb_ref[i]
        is_g = lane < N_GROUPS
        g_max = jnp.max(jnp.where(is_g, logits, -jnp.inf), axis=-1, keepdims=True)
        g_den = jnp.sum(jnp.where(is_g, jnp.exp(logits - g_max), 0.0), axis=-1, keepdims=True)
        p_grp = jnp.exp(pick(ea // EXPERTS_PER_GROUP) - g_max) / g_den
        v_a, v_b = pick(N_GROUPS + ea), pick(N_GROUPS + eb)
        combs = (p_grp / (1.0 + jnp.exp(v_b - v_a)), p_grp / (1.0 + jnp.exp(v_a - v_b)))
        acc = jnp.zeros((tm, d), F32)
        for slot, comb in enumerate(combs):
            a = jnp.dot(h2, wg_bf[slot], preferred_element_type=F32)
            u = jnp.dot(h2, wu_bf[slot], preferred_element_type=F32)
            act = _silu(a) * u * comb
            acc = acc + _dot(act, wd_bf[slot])
        _store_token_tiles(y_ref, _rms(x1 + acc, fin_ref[...]), SUBLANES)

    @pl.when(i >= n_used)
    def _():
        y_ref[...] = jnp.zeros_like(y_ref)


def _moe(tile_ea, tile_eb, n_used, x_sorted, wg, wu, wd, nf, fin, wr, br, *, tm, n_tiles):
    d = nf.shape[1]
    f = wg.shape[2]
    assert d == SUBLANES * LANES

    def expert(shape, which):
        return pl.BlockSpec((1,) + shape, lambda i, ea, eb, nu: ((ea, eb)[which][i], 0, 0))

    def const(a):
        return pl.BlockSpec(a.shape, lambda i, *_: (0,) * a.ndim)

    grid_spec = pltpu.PrefetchScalarGridSpec(
        num_scalar_prefetch=3,
        grid=(n_tiles,),
        in_specs=[pl.BlockSpec((tm * TOKEN_ROWS, LANES), lambda i, *_: (i, 0)),
                  expert((d, f), 0), expert((d, f), 0), expert((f, d), 0),
                  expert((d, f), 1), expert((d, f), 1), expert((f, d), 1),
                  const(nf), const(fin), const(wr), const(br)],
        out_specs=pl.BlockSpec((tm * SUBLANES, LANES), lambda i, *_: (i, 0)),
        scratch_shapes=[pltpu.VMEM((2, d, f), BF16), pltpu.VMEM((2, d, f), BF16),
                        pltpu.VMEM((2, f, d), BF16)],
    )
    return pl.pallas_call(
        functools.partial(_moe_kernel, tm=tm, d=d),
        out_shape=jax.ShapeDtypeStruct((n_tiles * tm * SUBLANES, LANES), F32),
        grid_spec=grid_spec,
        compiler_params=_params(("arbitrary",)),
        name="moe",
    )(tile_ea, tile_eb, n_used, x_sorted, wg, wu, wd, wg, wu, wd, nf, fin, wr, br)


def _unsort_kernel(pos_ref, y_hbm, o_ref, buf, sem, *, tm, base):
    i = pl.program_id(0)
    slot = i % 2

    @pl.when(i == 0)
    def _():
        _record_gather_start(y_hbm, buf.at[0], sem.at[0], pos_ref, base, tm, SUBLANES)

    @pl.when(i + 1 < pl.num_programs(0))
    def _():
        _record_gather_start(y_hbm, buf.at[1 - slot], sem.at[1 - slot], pos_ref,
                             base + (i + 1) * tm, tm, SUBLANES)

    _record_gather_wait(y_hbm, buf.at[slot], sem.at[slot])
    o_ref[...] = _load_token_tiles(buf.at[slot], tm, SUBLANES)


def _unsort(pos, y_rec, *, base, n_rows, tm):
    d = SUBLANES * LANES
    grid_spec = pltpu.PrefetchScalarGridSpec(
        num_scalar_prefetch=1,
        grid=(n_rows // tm,),
        in_specs=[pl.BlockSpec(memory_space=pl.ANY)],
        out_specs=pl.BlockSpec((tm, d), lambda i, *_: (i, 0)),
        scratch_shapes=[pltpu.VMEM((2, tm * SUBLANES, LANES), F32),
                        pltpu.SemaphoreType.DMA((2,))],
    )
    return pl.pallas_call(
        functools.partial(_unsort_kernel, tm=tm, base=base),
        out_shape=jax.ShapeDtypeStruct((n_rows, d), F32),
        grid_spec=grid_spec,
        compiler_params=_params(("arbitrary",)),
        name="unsort",
    )(pos, y_rec)


def _sort_plan(cls, rank, counts, *, tm, n_tiles):
    onehot = (cls[:, None] == jnp.arange(N_CLASSES, dtype=jnp.int32)[None, :]).astype(jnp.int32)
    tiles_per = (counts + tm - 1) // tm
    tile_end = jnp.cumsum(tiles_per)
    tile_start = tile_end - tiles_per
    pos = (jnp.sum(tile_start[None, :] * onehot, axis=1) * tm + rank).astype(jnp.int32)
    n_used = tile_end[-1].astype(jnp.int32)
    tile = jnp.minimum(jnp.arange(n_tiles, dtype=jnp.int32), n_used - 1)
    tile_cls = jnp.sum((tile[:, None] >= tile_end[None, :]).astype(jnp.int32), axis=1)
    grp = tile_cls // N_PAIRS
    pair = tile_cls % N_PAIRS
    lo = jnp.asarray(PAIR_LO, jnp.int32)[pair]
    hi = jnp.asarray(PAIR_HI, jnp.int32)[pair]
    ea = (grp * EXPERTS_PER_GROUP + lo).astype(jnp.int32)
    eb = (grp * EXPERTS_PER_GROUP + hi).astype(jnp.int32)
    return pos, ea, eb, n_used.reshape(1)


def _lane_row(values, start):
    row = jnp.zeros((LANES,), F32)
    return row.at[start:start + values.shape[0]].set(values.astype(F32)).reshape(1, LANES)


def _layer(x_prompt, x_sample, mem_prompt, cache_mem_k, cache_mem_v,
           state_mlstm_C, state_mlstm_n, state_mlstm_m, state_gdn_S, state_gdn_conv,
           norm_mix, w_in, b_igate, b_fgate, mlstm_norm, conv_w, a_log, dt_bias, gdn_norm,
           mem_norm, w_mem_kv, w_br_mlstm, w_br_gdn, w_br_xattn, w_out, norm_ffn,
           w_router_group, b_router_group, w_router_expert, b_router_expert,
           w_exp_gate, w_exp_up, w_exp_down, final_norm, *, cfg):
    bp, sp, d = x_prompt.shape
    bs, ss, _ = x_sample.shape
    mem_len = mem_prompt.shape[1]
    ss_pad = SUBLANES
    assert ss <= ss_pad and d % LANES == 0
    n_p = bp * sp
    n_s = bs * ss_pad
    W = BRANCH_W
    gs_ = cfg["sample_group"]

    sizes = (W, W, W, N_HEADS, N_HEADS, W, 3 * W, N_HEADS, N_HEADS, W, W, d, d, d)
    offs = [0]
    for s in sizes:
        offs.append(offs[-1] + s)
    (m_q, m_k, m_v, m_i, m_f, m_o, g_qkv, g_a, g_b, g_out, x_q, gate_a, gate_b, gate_c) = [
        w_in[:, offs[j]:offs[j + 1]] for j in range(len(sizes))]
    w_big = _col_blocks(
        jnp.concatenate([gate_a, gate_b, gate_c, m_q, m_k, m_v, m_o, g_qkv, g_out, x_q],
                        axis=1).astype(BF16), PROJ_GROUP * W)
    w_small = jnp.concatenate(
        [m_i, m_f, g_a, g_b, jnp.zeros((d, LANES - 4 * N_HEADS), w_in.dtype)], axis=1).astype(BF16)
    w_router = jnp.concatenate(
        [w_router_group, w_router_expert,
         jnp.zeros((d, LANES - N_GROUPS - N_EXPERTS), w_in.dtype)], axis=1).astype(BF16)
    b_router = _lane_row(jnp.concatenate([b_router_group, b_router_expert]), 0)
    gate_bias = _lane_row(jnp.concatenate([b_igate, b_fgate]), 0)
    a_row = _lane_row(-jnp.exp(a_log.astype(F32)), 2 * N_HEADS)
    dt_row = _lane_row(dt_bias, 2 * N_HEADS)

    xp2 = x_prompt.reshape(n_p, d)
    xs2 = jnp.pad(x_sample, ((0, 0), (0, ss_pad - ss), (0, 0))).reshape(n_s, d)

    proj, gs = _norm_proj(xp2, xs2, norm_mix.reshape(1, d), w_big, w_small, tm=cfg["tm_proj"])

    mem2 = mem_prompt.reshape(bp * mem_len, d)
    mem_kv, _ = _norm_proj(mem2, None, mem_norm.reshape(1, d),
                           _col_blocks(w_mem_kv.astype(BF16), W),
                           jnp.zeros((d, LANES), BF16), tm=min(cfg["tm_proj"], bp * mem_len))

    mnorm = mlstm_norm.reshape(1, W)
    gnorm = gdn_norm.reshape(1, HEAD_DIM)
    s_off = n_p // (gs_ * ss_pad)
    ha_p, c_p, nn_p, mm_p = _mlstm(proj, gs, gate_bias, mnorm, None, n_batch=bp, seq=sp,
                                   s_valid=sp, row_off=0, L=cfg["chunk_m"], n_sub=cfg["sub_m"], G=1)
    m0 = jnp.broadcast_to(state_mlstm_m[:, :, None], (bs, N_HEADS, LANES))
    ha_s, c_s, nn_s, mm_s = _mlstm(proj, gs, gate_bias, mnorm,
                                   (state_mlstm_C, state_mlstm_n, m0), n_batch=bs, seq=ss_pad,
                                   s_valid=ss, row_off=s_off, L=ss_pad, n_sub=1, G=gs_)
    hb_p, s_p, cv_p = _gdn(proj, gs, conv_w, a_row, dt_row, gnorm, None, n_batch=bp, seq=sp,
                           s_valid=sp, row_off=0, L=cfg["chunk_g"], n_sub=cfg["sub_g"], G=1)
    conv0 = jnp.pad(state_gdn_conv, ((0, 0), (SUBLANES - (CONV_W - 1), 0), (0, 0)))
    hb_s, s_s, cv_s = _gdn(proj, gs, conv_w, a_row, dt_row, gnorm, (conv0, state_gdn_S),
                           n_batch=bs, seq=ss_pad, s_valid=ss, row_off=s_off, L=ss_pad,
                           n_sub=1, G=gs_)
    lt = min(sp, cfg["lt_x"])
    hc_p = _xattn(proj, mem_kv, mem_kv, n_batch=bp, seq=sp, row_off=0, lt=lt, G=1,
                  mem_len=mem_len, head_rows=False,
                  kv_specs=[pl.BlockSpec((None, mem_len, W), lambda b, t, c=c: (c, b, 0))
                            for c in range(2)])
    kv_rows = gs_ * mem_len * N_HEADS
    hc_s = _xattn(proj, cache_mem_k.reshape(bs * mem_len * N_HEADS, HEAD_DIM),
                  cache_mem_v.reshape(bs * mem_len * N_HEADS, HEAD_DIM),
                  n_batch=bs, seq=ss_pad, row_off=s_off, lt=ss_pad, G=gs_, mem_len=mem_len,
                  head_rows=True,
                  kv_specs=[pl.BlockSpec((kv_rows, HEAD_DIM), lambda b, t: (b, 0))] * 2)

    tm = cfg["tm"]
    x1rec, route, counts = _merge((ha_p, ha_s), (hb_p, hb_s), (hc_p, hc_s), (xp2, xs2), proj,
                          w_br_mlstm.astype(BF16), w_br_gdn.astype(BF16), w_br_xattn.astype(BF16),
                          w_out.astype(BF16), norm_ffn.reshape(1, d), w_router, b_router, tm=tm)

    tm_moe = cfg["tm_moe"]
    n_all = n_p + n_s
    n_tiles = (n_all + N_CLASSES * (tm_moe - 1)) // tm_moe + 1
    pos, ea, eb, n_used = _sort_plan(route[:, ROUTE_CLS].astype(jnp.int32),
                                     route[:, ROUTE_RANK].astype(jnp.int32),
                                     counts[0, :N_CLASSES].astype(jnp.int32),
                                     tm=tm_moe, n_tiles=n_tiles)
    tm_sort = cfg["tm_sort"]
    x_sorted = _sort_scatter(pos, x1rec, n_tiles * tm_moe, tm=tm_sort)
    y_sorted = _moe(ea, eb, n_used, x_sorted, w_exp_gate, w_exp_up, w_exp_down,
                    norm_ffn.reshape(1, d),
                    final_norm.reshape(1, d), w_router, b_router, tm=tm_moe, n_tiles=n_tiles)
    y_p = _unsort(pos, y_sorted, base=0, n_rows=n_p, tm=tm_sort)
    y_s = _unsort(pos, y_sorted, base=n_p, n_rows=n_s, tm=tm_sort)

    y_prompt = y_p.reshape(bp, sp, d)
    y_sample = y_s.reshape(bs, ss_pad, d)[:, :ss]
    mk_p = mem_kv[0].reshape(bp, mem_len, N_HEADS, HEAD_DIM)
    mv_p = mem_kv[1].reshape(bp, mem_len, N_HEADS, HEAD_DIM)
    tail = slice(SUBLANES - (CONV_W - 1), SUBLANES)
    return (y_prompt, y_sample, mk_p, mv_p,
            c_p, nn_p, mm_p[:, :, 0], s_p, cv_p[:, tail],
            c_s, nn_s, mm_s[:, :, 0], s_s, cv_s[:, tail])


CONFIG = dict(tm_proj=1024, tm=512, tm_sort=1024, tm_moe=256, chunk_m=256, sub_m=1, chunk_g=64, sub_g=8,
              lt_x=1024, sample_group=16)


def kernel(x_prompt, x_sample, mem_prompt, cache_mem_k, cache_mem_v, state_mlstm_C, state_mlstm_n, state_mlstm_m, state_gdn_S, state_gdn_conv, norm_mix, w_in, b_igate, b_fgate, mlstm_norm, conv_w, a_log, dt_bias, gdn_norm, mem_norm, w_mem_kv, w_br_mlstm, w_br_gdn, w_br_xattn, w_out, norm_ffn, w_router_group, b_router_group, w_router_expert, b_router_expert, w_exp_gate, w_exp_up, w_exp_down, final_norm):
    depth = w_in.shape[0]
    assert depth == 1, "one decoder layer"
    outs = _layer(
        x_prompt, x_sample, mem_prompt, cache_mem_k[0], cache_mem_v[0],
        state_mlstm_C[0], state_mlstm_n[0], state_mlstm_m[0], state_gdn_S[0], state_gdn_conv[0],
        norm_mix[0], w_in[0], b_igate[0], b_fgate[0], mlstm_norm[0], conv_w[0], a_log[0],
        dt_bias[0], gdn_norm[0], mem_norm[0], w_mem_kv[0], w_br_mlstm[0], w_br_gdn[0],
        w_br_xattn[0], w_out[0], norm_ffn[0], w_router_group[0], b_router_group[0],
        w_router_expert[0], b_router_expert[0], w_exp_gate[0], w_exp_up[0], w_exp_down[0],
        final_norm, cfg=CONFIG)
    y_prompt, y_sample = outs[0], outs[1]
    return (y_prompt, y_sample) + tuple(o[None] for o in outs[2:])
```

```python
import functools

import jax
import jax.numpy as jnp
from jax import lax
from jax.experimental import pallas as pl
from jax.experimental.pallas import tpu as pltpu

F32 = jnp.float32
BF16 = jnp.bfloat16
EPS = 1e-6

N_HEADS = 4
HEAD_DIM = 128
BRANCH_W = N_HEADS * HEAD_DIM
CONV_W = 4
N_GROUPS = 4
EXPERTS_PER_GROUP = 4
N_EXPERTS = N_GROUPS * EXPERTS_PER_GROUP
N_PAIRS = 6
N_CLASSES = N_GROUPS * N_PAIRS
PAIR_LO = (0, 0, 0, 1, 1, 2)
PAIR_HI = (1, 2, 3, 2, 3, 3)
NEG_BIG = -1e30
LANES = 128
SUBLANES = 8
TOKEN_ROWS = 8
PROJ_GROUP = 5
VMEM_PHYSICAL = 64 * 1024 * 1024
VMEM_LIMIT = 48 * 1024 * 1024
VMEM_SLACK = 4 * 1024 * 1024

COL_GATE = 0
COL_MQ, COL_MK, COL_MV, COL_MO = 6, 7, 8, 9
COL_GQ, COL_GK, COL_GV, COL_GO, COL_XQ = 10, 11, 12, 13, 14


def _dot(a, b):
    return jnp.dot(a.astype(BF16), b.astype(BF16), preferred_element_type=F32)


def _dot_nt(a, b):
    return lax.dot_general(a.astype(BF16), b.astype(BF16), (((1,), (1,)), ((), ())),
                           preferred_element_type=F32)


def _dot_tn(a, b):
    return lax.dot_general(a.astype(BF16), b.astype(BF16), (((0,), (0,)), ((), ())),
                           preferred_element_type=F32)


def _dot_exact(a, b):
    return jnp.dot(a, b, preferred_element_type=F32, precision=lax.Precision.HIGHEST)


def _lanes_to_rows(x, lane0):
    r = lax.broadcasted_iota(jnp.int32, (SUBLANES, LANES), 0)
    c = lax.broadcasted_iota(jnp.int32, (SUBLANES, LANES), 1)
    sel = (c == r + lane0).astype(F32)
    return lax.dot_general(sel, x, (((1,), (1,)), ((), ())), preferred_element_type=F32,
                           precision=lax.Precision.HIGHEST)


def _rms(x, g):
    return x * lax.rsqrt(jnp.mean(x * x, axis=-1, keepdims=True) + EPS) * g


def _softplus(x):
    return jnp.maximum(x, 0.0) + jnp.log1p(jnp.exp(-jnp.abs(x)))


def _sigmoid(x):
    return 0.5 * jnp.tanh(0.5 * x) + 0.5


def _silu(x):
    return x * _sigmoid(x)


def _tri(n, strict=False):
    r = lax.broadcasted_iota(jnp.int32, (n, n), 0)
    c = lax.broadcasted_iota(jnp.int32, (n, n), 1)
    return (c < r) if strict else (c <= r)


def _tri_blocks(n, block, upper=False):
    shift = block.bit_length() - 1
    assert block == 1 << shift
    r = lax.broadcasted_iota(jnp.int32, (n, n), 0)
    c = lax.broadcasted_iota(jnp.int32, (n, n), 1)
    same = jnp.right_shift(r, shift) == jnp.right_shift(c, shift)
    return same & ((r <= c) if upper else (c <= r))


def _chunk_cumsum(x, x_t, L):
    rows = x.shape[0]
    blk = max(L, LANES)
    if rows % blk:
        blk = rows
    lower = _tri_blocks(blk, L).astype(F32)
    upper = _tri_blocks(blk, L, upper=True).astype(F32)
    cols = [_dot_exact(lower, x[c * blk:(c + 1) * blk]) for c in range(rows // blk)]
    rws = [_dot_exact(x_t[:, c * blk:(c + 1) * blk], upper) for c in range(rows // blk)]
    if len(cols) == 1:
        return cols[0], rws[0]
    return jnp.concatenate(cols, axis=0), jnp.concatenate(rws, axis=1)


def _pad_rows(shape, t, seq_rows, n_valid):
    assert seq_rows & (seq_rows - 1) == 0
    row = lax.broadcasted_iota(jnp.int32, shape, 0)
    return (row & (seq_rows - 1)) + t * seq_rows >= n_valid


def _params(sem, vmem_bytes=VMEM_LIMIT):
    return pltpu.CompilerParams(dimension_semantics=sem, vmem_limit_bytes=vmem_bytes)


def _two_source_specs(n_first, block, width):
    first = pl.BlockSpec((block, width), lambda i, *_: (jnp.minimum(i, n_first - 1), 0))
    second = pl.BlockSpec((block, width), lambda i, *_: (jnp.maximum(i - n_first, 0), 0))
    return first, second


def _norm_proj_kernel(xa_ref, xb_ref, g_ref, w_ref, ws_ref, o_ref, os_ref, hb_ref, *, n_first):
    i = pl.program_id(0)
    j = pl.program_id(1)

    @pl.when(j == 0)
    def _():
        @pl.when(i < n_first)
        def _():
            hb_ref[...] = _rms(xa_ref[...], g_ref[...]).astype(BF16)

        @pl.when(i >= n_first)
        def _():
            hb_ref[...] = _rms(xb_ref[...], g_ref[...]).astype(BF16)

        os_ref[...] = jnp.dot(hb_ref[...], ws_ref[...], preferred_element_type=F32)

    o_ref[...] = jnp.dot(hb_ref[...], w_ref[0], preferred_element_type=F32)


def _proj_spec(rows, c, row_block):
    return pl.BlockSpec((None, rows, BRANCH_W),
                        lambda *g: (c // PROJ_GROUP, row_block(*g), c % PROJ_GROUP))


def _col_blocks(w, tn):
    d, n = w.shape
    return w.reshape(d, n // tn, tn).transpose(1, 0, 2)


def _norm_proj(xa, xb, g, w, ws, *, tm):
    d = xa.shape[1]
    n_first = xa.shape[0] // tm
    n_rows = xa.shape[0] + (0 if xb is None else xb.shape[0])
    if xb is None:
        xb = xa
    tn = w.shape[2]
    n = w.shape[0] * tn
    sa, sb = _two_source_specs(n_first, tm, d)
    vmem = (4 * tm * d * 4 + 2 * d * tn * 2 + 2 * tm * tn * 4 + tm * d * 2
            + 2 * (tm + d) * ws.shape[1] * 4)
    vmem = min(max(vmem + VMEM_SLACK, VMEM_LIMIT), VMEM_PHYSICAL - VMEM_SLACK)
    return pl.pallas_call(
        functools.partial(_norm_proj_kernel, n_first=n_first),
        out_shape=(jax.ShapeDtypeStruct((n // tn, n_rows, tn), F32),
                   jax.ShapeDtypeStruct((n_rows, ws.shape[1]), F32)),
        grid=(n_rows // tm, n // tn),
        in_specs=[sa, sb,
                  pl.BlockSpec((1, d), lambda i, j: (0, 0)),
                  pl.BlockSpec((1, d, tn), lambda i, j: (j, 0, 0)),
                  pl.BlockSpec((d, ws.shape[1]), lambda i, j: (0, 0))],
        out_specs=(pl.BlockSpec((None, tm, tn), lambda i, j: (j, i, 0)),
                   pl.BlockSpec((tm, ws.shape[1]), lambda i, j: (i, 0))),
        scratch_shapes=[pltpu.VMEM((tm, d), BF16)],
        compiler_params=_params(("parallel", "arbitrary"), vmem),
        name="norm_proj",
    )(xa, xb, g, w, ws)


def _seq_tile_specs(G, Lt, nt, row_off):
    assert G == 1 or nt == 1

    def rows(b, t):
        return row_off + b * nt + t

    return rows


def _mlstm_kernel(*refs, L, n_sub, G, s_valid, s_total, has_init):
    if has_init:
        (q_ref, k_ref, v_ref, og_ref, gs_ref, bias_ref, norm_ref, c0_ref, n0_ref, m0_ref,
         h_ref, c_ref, n_ref, m_ref) = refs
    else:
        (q_ref, k_ref, v_ref, og_ref, gs_ref, bias_ref, norm_ref,
         h_ref, c_ref, n_ref, m_ref) = refs
    t = pl.program_id(1)
    Lt = n_sub * L
    rows = G * Lt

    @pl.when(t == 0)
    def _():
        if has_init:
            c_ref[...] = c0_ref[...]
            n_ref[...] = n0_ref[...]
            m_ref[...] = m0_ref[...]
        else:
            c_ref[...] = jnp.zeros_like(c_ref)
            n_ref[...] = jnp.zeros_like(n_ref)
            m_ref[...] = jnp.zeros_like(m_ref)

    z = gs_ref[...] + bias_ref[...]
    lane = lax.broadcasted_iota(jnp.int32, z.shape, 1)
    log_f = jnp.minimum(z, 0.0) - jnp.log1p(jnp.exp(-jnp.abs(z)))
    g = jnp.where((lane >= N_HEADS) & (lane < 2 * N_HEADS), log_f, z)
    if s_valid < s_total:
        pad = _pad_rows(z.shape, t, Lt, s_valid)
        g = jnp.where(pad, jnp.where(lane < N_HEADS, NEG_BIG, 0.0), g)
    g_t = _lanes_to_rows(g, 0)
    bt_cols, bt_rows = _chunk_cumsum(g, g_t, L)
    scale = HEAD_DIM ** -0.5
    assert G == 1 or n_sub == 1
    n_r = G * L
    mask = _tri_blocks(n_r, L)

    def per_seq(x):
        return [x[j * L:(j + 1) * L] for j in range(G)]

    def expand(pieces):
        if G == 1:
            return pieces[0]
        return jnp.concatenate([jnp.broadcast_to(x, (L, x.shape[1])) for x in pieces], axis=0)

    for c in range(n_sub):
        r = slice(c * L, c * L + n_r)
        P = []
        for h in range(N_HEADS):
            sl = slice(h * HEAD_DIM, (h + 1) * HEAD_DIM)
            p = dict(h=h, sl=sl, q=q_ref[r, sl], k=k_ref[r, sl] * scale, v=v_ref[r, sl],
                     bt_c=bt_cols[r, N_HEADS + h:N_HEADS + h + 1], it_c=g[r, h:h + 1])
            bt_r = bt_rows[N_HEADS + h:N_HEADS + h + 1, r]
            it_r = g_t[h:h + 1, r]
            p["log_d"] = jnp.where(mask, p["bt_c"] - bt_r + it_r, -jnp.inf)
            p["c_prev"] = [c_ref[s, h] for s in range(G)]
            p["n_prev"] = [n_ref[s, h:h + 1, :] for s in range(G)]
            p["m_prev"] = [m_ref[s, h:h + 1, 0:1] for s in range(G)]
            P.append(p)
        for p in P:
            p["qk"] = _dot_nt(p["q"], p["k"])
        for p in P:
            qs = per_seq(p["q"])
            qc = [_dot(qs[s], p["c_prev"][s]) for s in range(G)]
            p["qc"] = qc[0] if G == 1 else jnp.concatenate(qc, axis=0)
        for p in P:
            inter = p["bt_c"] + expand(p["m_prev"])
            p["m_t"] = jnp.maximum(inter, jnp.max(p["log_d"], axis=-1, keepdims=True))
            p["inter_w"] = jnp.exp(inter - p["m_t"])
            p["sd"] = p["qk"] * jnp.exp(p["log_d"] - p["m_t"])
        for p in P:
            p["sv"] = _dot(p["sd"], p["v"])
        for p in P:
            bt_last = [x[L - 1:L, :] for x in per_seq(p["bt_c"])]
            p["m_new"] = [x[L - 1:L, :] for x in per_seq(p["m_t"])]
            w = jnp.exp(expand(bt_last) - p["bt_c"] + p["it_c"] - expand(p["m_new"]))
            p["decay"] = [jnp.exp(bt_last[s] + p["m_prev"][s] - p["m_new"][s]) for s in range(G)]
            p["kw"] = p["k"] * w
        for p in P:
            kws, vs = per_seq(p["kw"]), per_seq(p["v"])
            p["kv"] = [_dot_tn(kws[s], vs[s]) for s in range(G)]
            p["n_new"] = [p["decay"][s] * p["n_prev"][s] + jnp.sum(kws[s], axis=0, keepdims=True)
                          for s in range(G)]
        for p in P:
            num = p["sv"] + p["qc"] * p["inter_w"]
            den = (jnp.sum(p["sd"], axis=-1, keepdims=True)
                   + p["inter_w"] * jnp.sum(p["q"] * expand(p["n_prev"]), axis=-1, keepdims=True))
            hh = num / jnp.maximum(jnp.abs(den), jnp.exp(-p["m_t"]))
            hn = hh * lax.rsqrt(jnp.mean(hh * hh, axis=-1, keepdims=True) + EPS) * norm_ref[:, p["sl"]]
            p["h_out"] = hn * _sigmoid(og_ref[r, p["sl"]])
        for p in P:
            h = p["h"]
            h_ref[r, p["sl"]] = p["h_out"]
            for s in range(G):
                n_ref[s, h:h + 1, :] = p["n_new"][s]
                m_ref[s, h:h + 1, :] = jnp.broadcast_to(p["m_new"][s], (1, LANES))
                c_ref[s, h] = p["decay"][s] * p["c_prev"][s] + p["kv"][s]


def _mlstm(proj, gs, bias_row, norm_row, init, *, n_batch, seq, s_valid, row_off, L, n_sub, G):
    Lt = n_sub * L
    nt = seq // Lt
    rows = G * Lt
    has_init = init is not None
    blk = _seq_tile_specs(G, Lt, nt, row_off)

    def col(c):
        return _proj_spec(rows, c, blk)

    in_specs = [col(COL_MQ), col(COL_MK), col(COL_MV), col(COL_MO),
                pl.BlockSpec((rows, LANES), lambda b, t: (blk(b, t), 0)),
                pl.BlockSpec((1, LANES), lambda b, t: (0, 0)),
                pl.BlockSpec((1, BRANCH_W), lambda b, t: (0, 0))]
    args = [proj, proj, proj, proj, gs, bias_row, norm_row]
    c_spec = pl.BlockSpec((G, N_HEADS, HEAD_DIM, HEAD_DIM), lambda b, t: (b, 0, 0, 0))
    n_spec = pl.BlockSpec((G, N_HEADS, HEAD_DIM), lambda b, t: (b, 0, 0))
    m_spec = pl.BlockSpec((G, N_HEADS, LANES), lambda b, t: (b, 0, 0))
    if has_init:
        in_specs += [c_spec, n_spec, m_spec]
        args += list(init)
    return pl.pallas_call(
        functools.partial(_mlstm_kernel, L=L, n_sub=n_sub, G=G, s_valid=s_valid, s_total=seq,
                          has_init=has_init),
        out_shape=(jax.ShapeDtypeStruct((n_batch * seq, BRANCH_W), F32),
                   jax.ShapeDtypeStruct((n_batch, N_HEADS, HEAD_DIM, HEAD_DIM), F32),
                   jax.ShapeDtypeStruct((n_batch, N_HEADS, HEAD_DIM), F32),
                   jax.ShapeDtypeStruct((n_batch, N_HEADS, LANES), F32)),
        grid=(n_batch // G, nt),
        in_specs=in_specs,
        out_specs=(pl.BlockSpec((rows, BRANCH_W), lambda b, t: (b * nt + t, 0)),
                   c_spec, n_spec, m_spec),
        compiler_params=_params(("parallel", "arbitrary")),
        name="mlstm",
    )(*args)


def _gdn_kernel(*refs, L, n_sub, G, s_valid, s_total, has_init):
    if has_init:
        (q_ref, k_ref, v_ref, og_ref, gs_ref, cw_ref, arow_ref, dtrow_ref, norm_ref,
         conv0_ref, s0_ref, o_ref, s_ref, conv1_ref, ext_ref, conv_ref) = refs
    else:
        (q_ref, k_ref, v_ref, og_ref, gs_ref, cw_ref, arow_ref, dtrow_ref, norm_ref,
         o_ref, s_ref, conv1_ref, ext_ref, conv_ref) = refs
    t = pl.program_id(1)
    Lt = n_sub * L
    rows = G * Lt
    nt = s_total // Lt
    W = BRANCH_W

    @pl.when(t == 0)
    def _():
        if has_init:
            s_ref[...] = s0_ref[...]
            ext_ref[:, 0:SUBLANES, :] = conv0_ref[...]
        else:
            s_ref[...] = jnp.zeros_like(s_ref)
            ext_ref[:, 0:SUBLANES, :] = jnp.zeros((G, SUBLANES, 3 * W), F32)

    @pl.when(t > 0)
    def _():
        ext_ref[:, 0:SUBLANES, :] = ext_ref[:, Lt:Lt + SUBLANES, :]

    base = SUBLANES - (CONV_W - 1)
    for s in range(G):
        rs = slice(s * Lt, (s + 1) * Lt)
        ext_ref[s, SUBLANES:SUBLANES + Lt, 0:W] = q_ref[rs, :]
        ext_ref[s, SUBLANES:SUBLANES + Lt, W:2 * W] = k_ref[rs, :]
        ext_ref[s, SUBLANES:SUBLANES + Lt, 2 * W:3 * W] = v_ref[rs, :]
        acc = ext_ref[s, base:base + Lt, :] * cw_ref[0:1, :]
        for j in range(1, CONV_W):
            acc = acc + ext_ref[s, base + j:base + j + Lt, :] * cw_ref[j:j + 1, :]
        conv_ref[rs, :] = _silu(acc)

    @pl.when(t == nt - 1)
    def _():
        v_last = s_valid - (nt - 1) * Lt
        conv1_ref[...] = ext_ref[:, v_last:v_last + SUBLANES, :]

    z = gs_ref[...]
    lane = lax.broadcasted_iota(jnp.int32, z.shape, 1)
    gdec = arow_ref[...] * _softplus(z + dtrow_ref[...])
    gb = jnp.where((lane >= 2 * N_HEADS) & (lane < 3 * N_HEADS), gdec,
                   jnp.where((lane >= 3 * N_HEADS) & (lane < 4 * N_HEADS), _sigmoid(z), 0.0))
    if s_valid < s_total:
        gb = jnp.where(_pad_rows(z.shape, t, Lt, s_valid), 0.0, gb)
    gb_t = _lanes_to_rows(gb, 2 * N_HEADS)
    gam_cols, gam_rows = _chunk_cumsum(gb, gb_t, L)
    incl = _tri(L)
    strict = _tri(L, strict=True)
    eye = (lax.broadcasted_iota(jnp.int32, (L, L), 0)
           == lax.broadcasted_iota(jnp.int32, (L, L), 1)).astype(F32)

    P = []
    for s in range(G):
        for c in range(n_sub):
            for h in range(N_HEADS):
                r = slice(s * Lt + c * L, s * Lt + (c + 1) * L)
                sl = slice(h * HEAD_DIM, (h + 1) * HEAD_DIM)
                q = conv_ref[r, h * HEAD_DIM:(h + 1) * HEAD_DIM]
                k = conv_ref[r, W + h * HEAD_DIM:W + (h + 1) * HEAD_DIM]
                v = conv_ref[r, 2 * W + h * HEAD_DIM:2 * W + (h + 1) * HEAD_DIM]
                q = q * lax.rsqrt(jnp.sum(q * q, axis=-1, keepdims=True) + EPS) * (HEAD_DIM ** -0.5)
                k = k * lax.rsqrt(jnp.sum(k * k, axis=-1, keepdims=True) + EPS)
                gam_c = gam_cols[r, 2 * N_HEADS + h:2 * N_HEADS + h + 1]
                gam_r = gam_rows[h:h + 1, r]
                beta_c = gb[r, 3 * N_HEADS + h:3 * N_HEADS + h + 1]
                decay = jnp.exp(jnp.where(incl, gam_c - gam_r, -jnp.inf))
                P.append(dict(s=s, c=c, h=h, r=r, sl=sl, q=q, k=k, v=v, gam_c=gam_c,
                              beta_c=beta_c, decay=decay))
    for p in P:
        p["kk"] = _dot_nt(p["k"], p["k"])
    for p in P:
        p["qk"] = _dot_nt(p["q"], p["k"]) * p["decay"]
    for p in P:
        e_gam = jnp.exp(p["gam_c"])
        gam_last = p["gam_c"][L - 1:L, :]
        p["pw"] = -jnp.where(strict, p["beta_c"] * p["kk"] * p["decay"], 0.0)
        p["x"] = eye + p["pw"]
        p["rhs"] = jnp.concatenate([p["beta_c"] * p["v"], p["beta_c"] * p["k"] * e_gam], axis=-1)
        p["q_dec"] = p["q"] * e_gam
        p["k_dec"] = p["k"] * jnp.exp(gam_last - p["gam_c"])
        p["e_last"] = jnp.exp(gam_last)
    n = 2
    while n < L:
        for p in P:
            p["pw"] = _dot(p["pw"], p["pw"])
        for p in P:
            p["x"] = p["x"] + _dot(p["x"], p["pw"])
        n *= 2
    for p in P:
        sol = _dot(p["x"], p["rhs"])
        p["u"] = sol[:, :HEAD_DIM]
        p["w"] = sol[:, HEAD_DIM:]

    for c in range(n_sub):
        Pc = [p for p in P if p["c"] == c]
        for p in Pc:
            p["s_prev"] = s_ref[p["s"], p["h"]]
            p["ws"] = _dot(p["w"], p["s_prev"])
        for p in Pc:
            p["qs"] = _dot(p["q_dec"], p["s_prev"])
        for p in Pc:
            p["v_new"] = p["u"] - p["ws"]
        for p in Pc:
            p["o"] = p["qs"] + _dot(p["qk"], p["v_new"])
        for p in Pc:
            p["kv"] = _dot_tn(p["k_dec"], p["v_new"])
        for p in Pc:
            o = p["o"]
            on = o * lax.rsqrt(jnp.mean(o * o, axis=-1, keepdims=True) + EPS) * norm_ref[...]
            p["o_out"] = on * _silu(og_ref[p["r"], p["sl"]])
        for p in Pc:
            s_ref[p["s"], p["h"]] = p["e_last"] * p["s_prev"] + p["kv"]
            o_ref[p["r"], p["sl"]] = p["o_out"]


def _gdn(proj, gs, conv_w, a_row, dt_row, norm_row, init, *, n_batch, seq, s_valid, row_off,
         L, n_sub, G):
    Lt = n_sub * L
    nt = seq // Lt
    rows = G * Lt
    has_init = init is not None
    blk = _seq_tile_specs(G, Lt, nt, row_off)

    def col(c):
        return _proj_spec(rows, c, blk)

    def const(shape):
        return pl.BlockSpec(shape, lambda b, t: (0,) * len(shape))

    in_specs = [col(COL_GQ), col(COL_GK), col(COL_GV), col(COL_GO),
                pl.BlockSpec((rows, LANES), lambda b, t: (blk(b, t), 0)),
                const((CONV_W, 3 * BRANCH_W)), const((1, LANES)), const((1, LANES)),
                const((1, HEAD_DIM))]
    args = [proj, proj, proj, proj, gs, conv_w, a_row, dt_row, norm_row]
    s_spec = pl.BlockSpec((G, N_HEADS, HEAD_DIM, HEAD_DIM), lambda b, t: (b, 0, 0, 0))
    conv_spec = pl.BlockSpec((G, SUBLANES, 3 * BRANCH_W), lambda b, t: (b, 0, 0))
    if has_init:
        in_specs += [conv_spec, s_spec]
        args += list(init)
    return pl.pallas_call(
        functools.partial(_gdn_kernel, L=L, n_sub=n_sub, G=G, s_valid=s_valid, s_total=seq,
                          has_init=has_init),
        out_shape=(jax.ShapeDtypeStruct((n_batch * seq, BRANCH_W), F32),
                   jax.ShapeDtypeStruct((n_batch, N_HEADS, HEAD_DIM, HEAD_DIM), F32),
                   jax.ShapeDtypeStruct((n_batch, SUBLANES, 3 * BRANCH_W), F32)),
        grid=(n_batch // G, nt),
        in_specs=in_specs,
        out_specs=(pl.BlockSpec((rows, BRANCH_W), lambda b, t: (b * nt + t, 0)),
                   s_spec, conv_spec),
        scratch_shapes=[pltpu.VMEM((G, Lt + SUBLANES, 3 * BRANCH_W), F32),
                        pltpu.VMEM((rows, 3 * BRANCH_W), F32)],
        compiler_params=_params(("parallel", "arbitrary")),
        name="gdn",
    )(*args)


def _xattn_kernel(q_ref, mk_ref, mv_ref, o_ref, *, G, lt, mem_len, head_rows):
    scale = HEAD_DIM ** -0.5

    def head(ref, s, h):
        if head_rows:
            return ref[pl.ds(s * mem_len * N_HEADS + h, mem_len, stride=N_HEADS), :]
        return ref[s * mem_len:(s + 1) * mem_len, h * HEAD_DIM:(h + 1) * HEAD_DIM]

    P = []
    for s in range(G):
        for h in range(N_HEADS):
            P.append(dict(s=s, h=h, r=slice(s * lt, (s + 1) * lt),
                          sl=slice(h * HEAD_DIM, (h + 1) * HEAD_DIM)))
    for p in P:
        p["sc"] = _dot_nt(q_ref[p["r"], p["sl"]], head(mk_ref, p["s"], p["h"])) * scale
    for p in P:
        e = jnp.exp(p["sc"] - jnp.max(p["sc"], axis=-1, keepdims=True))
        p["p"] = e / jnp.sum(e, axis=-1, keepdims=True)
    for p in P:
        o_ref[p["r"], p["sl"]] = _dot(p["p"], head(mv_ref, p["s"], p["h"]))


def _xattn(proj, mk, mv, *, n_batch, seq, row_off, lt, G, mem_len, kv_specs, head_rows):
    nt = seq // lt
    blk = _seq_tile_specs(G, lt, nt, row_off)
    return pl.pallas_call(
        functools.partial(_xattn_kernel, G=G, lt=lt, mem_len=mem_len, head_rows=head_rows),
        out_shape=jax.ShapeDtypeStruct((n_batch * seq, BRANCH_W), F32),
        grid=(n_batch // G, nt),
        in_specs=[_proj_spec(G * lt, COL_XQ, blk)] + list(kv_specs),
        out_specs=pl.BlockSpec((G * lt, BRANCH_W), lambda b, t: (b * nt + t, 0)),
        compiler_params=_params(("parallel", "arbitrary")),
        name="xattn",
    )(proj, mk, mv)


def _lane_first_max(x, mask, lane_f):
    xm = jnp.where(mask, x, -jnp.inf)
    mx = jnp.max(xm, axis=-1, keepdims=True)
    idx = jnp.min(jnp.where(xm == mx, lane_f, float(LANES)), axis=-1, keepdims=True)
    return mx, idx


def _store_token_tiles(ref, x, tok_rows):
    n = x.shape[0]
    for j in range(x.shape[1] // LANES):
        ref[pl.ds(j, n, stride=tok_rows), :] = x[:, j * LANES:(j + 1) * LANES]


def _load_token_tiles(ref, n, tok_rows, n_chunks=SUBLANES):
    return jnp.concatenate([ref[pl.ds(j, n, stride=tok_rows), :] for j in range(n_chunks)], axis=1)


def _merge_kernel(hap_ref, has_ref, hbp_ref, hbs_ref, hcp_ref, hcs_ref, xp_ref, xs_ref,
                  ga0_ref, ga1_ref, gb0_ref, gb1_ref, gc0_ref, gc1_ref,
                  wa_ref, wb_ref, wc_ref, wo_ref, nf_ref, wr_ref, br_ref,
                  x1_ref, route_ref, count_ref, *, n_first):
    i = pl.program_id(0)

    @pl.when(i == 0)
    def _():
        count_ref[...] = jnp.zeros_like(count_ref)

    def gate(lo_ref, hi_ref):
        return jnp.concatenate([_sigmoid(lo_ref[...]), _sigmoid(hi_ref[...])], axis=1)

    def body(ha_ref, hb_ref, hc_ref, x_ref):
        mixed = (gate(ga0_ref, ga1_ref) * _dot(ha_ref[...], wa_ref[...])
                 + gate(gb0_ref, gb1_ref) * _dot(hb_ref[...], wb_ref[...])
                 + gate(gc0_ref, gc1_ref) * _dot(hc_ref[...], wc_ref[...]))
        x1 = x_ref[...] + _dot(mixed, wo_ref[...])
        h2 = _rms(x1, nf_ref[...])
        logits = _dot(h2, wr_ref[...]) + br_ref[...]
        lane = lax.broadcasted_iota(jnp.int32, logits.shape, 1)
        lane_f = lane.astype(F32)
        is_g = lane < N_GROUPS
        _, g_idx = _lane_first_max(logits, is_g, lane_f)
        e_lo = N_GROUPS + EXPERTS_PER_GROUP * g_idx
        in_grp = (lane_f >= e_lo) & (lane_f < e_lo + EXPERTS_PER_GROUP)
        _, i1 = _lane_first_max(logits, in_grp, lane_f)
        _, i2 = _lane_first_max(logits, in_grp & (lane_f != i1), lane_f)
        lo = jnp.minimum(i1, i2) - e_lo
        hi = jnp.maximum(i1, i2) - e_lo
        pair = lo * (7.0 - lo) * 0.5 + (hi - lo - 1.0)
        cls = g_idx * float(N_PAIRS) + pair
        _store_token_tiles(x1_ref, x1, TOKEN_ROWS)
        tm = logits.shape[0]
        onehot = (lane_f == cls).astype(F32)
        earlier = jnp.dot(_tri(tm, strict=True).astype(BF16), onehot.astype(BF16),
                          preferred_element_type=F32)
        rank = jnp.sum((earlier + count_ref[0:1, :]) * onehot, axis=-1, keepdims=True)
        count_ref[0:1, :] = count_ref[0:1, :] + jnp.sum(onehot, axis=0, keepdims=True)
        route_ref[...] = jnp.where(lane == ROUTE_CLS, cls, jnp.where(lane == ROUTE_RANK, rank, 0.0))

    @pl.when(i < n_first)
    def _():
        body(hap_ref, hbp_ref, hcp_ref, xp_ref)

    @pl.when(i >= n_first)
    def _():
        body(has_ref, hbs_ref, hcs_ref, xs_ref)


def _merge(ha, hb, hc, xs, proj, wa, wb, wc, wo, nf, wr, br, *, tm):
    d = xs[0].shape[1]
    n_first = xs[0].shape[0] // tm
    n_rows = xs[0].shape[0] + xs[1].shape[0]
    specs = []
    for width in (BRANCH_W, BRANCH_W, BRANCH_W, d):
        specs += list(_two_source_specs(n_first, tm, width))

    def const(a):
        return pl.BlockSpec(a.shape, lambda i: (0,) * a.ndim)

    n_gate = 3 * d // BRANCH_W
    specs += [_proj_spec(tm, COL_GATE + c, lambda i: i) for c in range(n_gate)]
    weights = [wa, wb, wc, wo, nf, wr, br]
    specs += [const(a) for a in weights]
    return pl.pallas_call(
        functools.partial(_merge_kernel, n_first=n_first),
        out_shape=(jax.ShapeDtypeStruct((n_rows * TOKEN_ROWS, LANES), F32),
                   jax.ShapeDtypeStruct((n_rows, LANES), F32),
                   jax.ShapeDtypeStruct((SUBLANES, LANES), F32)),
        grid=(n_rows // tm,),
        in_specs=specs,
        out_specs=(pl.BlockSpec((tm * TOKEN_ROWS, LANES), lambda i: (i, 0)),
                   pl.BlockSpec((tm, LANES), lambda i: (i, 0)),
                   pl.BlockSpec((SUBLANES, LANES), lambda i: (0, 0))),
        compiler_params=_params(("arbitrary",)),
        name="merge",
    )(*ha, *hb, *hc, *xs, *([proj] * n_gate), *weights)


GATHER_UNROLL = 8
N_DMA_PRIORITIES = 2
ROUTE_CLS, ROUTE_RANK = 0, 1


def _record_gather_start(src_ref, dst_ref, sem, ids_ref, base, n, rec):
    def issue(r2, carry):
        for prio in range(N_DMA_PRIORITIES):
            r = r2 * N_DMA_PRIORITIES + prio
            src = pl.multiple_of(ids_ref[base + r] * rec, rec)
            dst = pl.multiple_of(r * rec, rec)
            pltpu.make_async_copy(src_ref.at[pl.ds(src, rec)], dst_ref.at[pl.ds(dst, rec)],
                                  sem).start(priority=prio)
        return carry
    lax.fori_loop(0, n // N_DMA_PRIORITIES, issue, 0, unroll=GATHER_UNROLL // N_DMA_PRIORITIES)


def _record_gather_wait(src_ref, dst_ref, sem):
    pltpu.make_async_copy(src_ref.at[pl.ds(0, dst_ref.shape[0])], dst_ref, sem).wait()


def _sort_scatter_kernel(pos_ref, x_ref, init_hbm, out_hbm, sem, *, tm):
    del init_hbm
    base = pl.program_id(0) * tm

    def issue(r2, carry):
        for prio in range(N_DMA_PRIORITIES):
            r = r2 * N_DMA_PRIORITIES + prio
            src = pl.multiple_of(r * TOKEN_ROWS, TOKEN_ROWS)
            dst = pl.multiple_of(pos_ref[base + r] * TOKEN_ROWS, TOKEN_ROWS)
            pltpu.make_async_copy(x_ref.at[pl.ds(src, TOKEN_ROWS)],
                                  out_hbm.at[pl.ds(dst, TOKEN_ROWS)], sem).start(priority=prio)
        return carry
    lax.fori_loop(0, tm // N_DMA_PRIORITIES, issue, 0, unroll=GATHER_UNROLL // N_DMA_PRIORITIES)
    pltpu.make_async_copy(x_ref, out_hbm.at[pl.ds(0, tm * TOKEN_ROWS)], sem).wait()


def _sort_scatter(pos, x1rec, n_sorted, *, tm):
    n_tok = x1rec.shape[0] // TOKEN_ROWS
    grid_spec = pltpu.PrefetchScalarGridSpec(
        num_scalar_prefetch=1,
        grid=(n_tok // tm,),
        in_specs=[pl.BlockSpec((tm * TOKEN_ROWS, LANES), lambda i, *_: (i, 0)),
                  pl.BlockSpec(memory_space=pl.ANY)],
        out_specs=pl.BlockSpec(memory_space=pl.ANY),
        scratch_shapes=[pltpu.SemaphoreType.DMA(())],
    )
    return pl.pallas_call(
        functools.partial(_sort_scatter_kernel, tm=tm),
        out_shape=jax.ShapeDtypeStruct((n_sorted * TOKEN_ROWS, LANES), F32),
        grid_spec=grid_spec,
        input_output_aliases={2: 0},
        compiler_params=_params(("arbitrary",)),
        name="sort_scatter",
    )(pos, x1rec, jnp.zeros((n_sorted * TOKEN_ROWS, LANES), F32))


def _moe_kernel(ea_ref, eb_ref, nused_ref, x_ref, wga_ref, wua_ref, wda_ref,
                wgb_ref, wub_ref, wdb_ref, nf_ref, fin_ref, wr_ref, br_ref, y_ref,
                wg_bf, wu_bf, wd_bf, *, tm, d):
    i = pl.program_id(0)
    n_used = nused_ref[0]

    prev = jnp.maximum(i - 1, 0)
    for slot, e_ref, srcs in ((0, ea_ref, (wga_ref, wua_ref, wda_ref)),
                              (1, eb_ref, (wgb_ref, wub_ref, wdb_ref))):
        @pl.when((i < n_used) & ((i == 0) | (e_ref[i] != e_ref[prev])))
        def _():
            for dst, src in zip((wg_bf, wu_bf, wd_bf), srcs):
                dst[slot] = src[0].astype(BF16)

    @pl.when(i < n_used)
    def _():
        x1 = _load_token_tiles(x_ref, tm, TOKEN_ROWS)
        h2 = _rms(x1, nf_ref[...]).astype(BF16)
        logits = jnp.dot(h2, wr_ref[...], preferred_element_type=F32) + br_ref[...]
        lane = lax.broadcasted_iota(jnp.int32, logits.shape, 1)

        def pick(l):
            return jnp.sum(jnp.where(lane == l, logits, 0.0), axis=-1, keepdims=True)

        ea, eb = ea_ref[i], eb_ref[i]
        is_g = lane < N_GROUPS
        g_max = jnp.max(jnp.where(is_g, logits, -jnp.inf), axis=-1, keepdims=True)
        g_den = jnp.sum(jnp.where(is_g, jnp.exp(logits - g_max), 0.0), axis=-1, keepdims=True)
        p_grp = jnp.exp(pick(ea // EXPERTS_PER_GROUP) - g_max) / g_den
        v_a, v_b = pick(N_GROUPS + ea), pick(N_GROUPS + eb)
        combs = (p_grp / (1.0 + jnp.exp(v_b - v_a)), p_grp / (1.0 + jnp.exp(v_a - v_b)))
        acc = jnp.zeros((tm, d), F32)
        for slot, comb in enumerate(combs):
            a = jnp.dot(h2, wg_bf[slot], preferred_element_type=F32)
            u = jnp.dot(h2, wu_bf[slot], preferred_element_type=F32)
            act = _silu(a) * u * comb
            acc = acc + _dot(act, wd_bf[slot])
        _store_token_tiles(y_ref, _rms(x1 + acc, fin_ref[...]), SUBLANES)

    @pl.when(i >= n_used)
    def _():
        y_ref[...] = jnp.zeros_like(y_ref)


def _moe(tile_ea, tile_eb, n_used, x_sorted, wg, wu, wd, nf, fin, wr, br, *, tm, n_tiles):
    d = nf.shape[1]
    f = wg.shape[2]
    assert d == SUBLANES * LANES

    def expert(shape, which):
        return pl.BlockSpec((1,) + shape, lambda i, ea, eb, nu: ((ea, eb)[which][i], 0, 0))

    def const(a):
        return pl.BlockSpec(a.shape, lambda i, *_: (0,) * a.ndim)

    grid_spec = pltpu.PrefetchScalarGridSpec(
        num_scalar_prefetch=3,
        grid=(n_tiles,),
        in_specs=[pl.BlockSpec((tm * TOKEN_ROWS, LANES), lambda i, *_: (i, 0)),
                  expert((d, f), 0), expert((d, f), 0), expert((f, d), 0),
                  expert((d, f), 1), expert((d, f), 1), expert((f, d), 1),
                  const(nf), const(fin), const(wr), const(br)],
        out_specs=pl.BlockSpec((tm * SUBLANES, LANES), lambda i, *_: (i, 0)),
        scratch_shapes=[pltpu.VMEM((2, d, f), BF16), pltpu.VMEM((2, d, f), BF16),
                        pltpu.VMEM((2, f, d), BF16)],
    )
    return pl.pallas_call(
        functools.partial(_moe_kernel, tm=tm, d=d),
        out_shape=jax.ShapeDtypeStruct((n_tiles * tm * SUBLANES, LANES), F32),
        grid_spec=grid_spec,
        compiler_params=_params(("arbitrary",)),
        name="moe",
    )(tile_ea, tile_eb, n_used, x_sorted, wg, wu, wd, wg, wu, wd, nf, fin, wr, br)


def _unsort_kernel(pos_ref, y_hbm, o_ref, buf, sem, *, tm, base):
    i = pl.program_id(0)
    slot = i % 2

    @pl.when(i == 0)
    def _():
        _record_gather_start(y_hbm, buf.at[0], sem.at[0], pos_ref, base, tm, SUBLANES)

    @pl.when(i + 1 < pl.num_programs(0))
    def _():
        _record_gather_start(y_hbm, buf.at[1 - slot], sem.at[1 - slot], pos_ref,
                             base + (i + 1) * tm, tm, SUBLANES)

    _record_gather_wait(y_hbm, buf.at[slot], sem.at[slot])
    o_ref[...] = _load_token_tiles(buf.at[slot], tm, SUBLANES)


def _unsort(pos, y_rec, *, base, n_rows, tm):
    d = SUBLANES * LANES
    grid_spec = pltpu.PrefetchScalarGridSpec(
        num_scalar_prefetch=1,
        grid=(n_rows // tm,),
        in_specs=[pl.BlockSpec(memory_space=pl.ANY)],
        out_specs=pl.BlockSpec((tm, d), lambda i, *_: (i, 0)),
        scratch_shapes=[pltpu.VMEM((2, tm * SUBLANES, LANES), F32),
                        pltpu.SemaphoreType.DMA((2,))],
    )
    return pl.pallas_call(
        functools.partial(_unsort_kernel, tm=tm, base=base),
        out_shape=jax.ShapeDtypeStruct((n_rows, d), F32),
        grid_spec=grid_spec,
        compiler_params=_params(("arbitrary",)),
        name="unsort",
    )(pos, y_rec)


def _sort_plan(cls, rank, counts, *, tm, n_tiles):
    onehot = (cls[:, None] == jnp.arange(N_CLASSES, dtype=jnp.int32)[None, :]).astype(jnp.int32)
    tiles_per = (counts + tm - 1) // tm
    tile_end = jnp.cumsum(tiles_per)
    tile_start = tile_end - tiles_per
    pos = (jnp.sum(tile_start[None, :] * onehot, axis=1) * tm + rank).astype(jnp.int32)
    n_used = tile_end[-1].astype(jnp.int32)
    tile = jnp.minimum(jnp.arange(n_tiles, dtype=jnp.int32), n_used - 1)
    tile_cls = jnp.sum((tile[:, None] >= tile_end[None, :]).astype(jnp.int32), axis=1)
    grp = tile_cls // N_PAIRS
    pair = tile_cls % N_PAIRS
    lo = jnp.asarray(PAIR_LO, jnp.int32)[pair]
    hi = jnp.asarray(PAIR_HI, jnp.int32)[pair]
    ea = (grp * EXPERTS_PER_GROUP + lo).astype(jnp.int32)
    eb = (grp * EXPERTS_PER_GROUP + hi).astype(jnp.int32)
    return pos, ea, eb, n_used.reshape(1)


def _lane_row(values, start):
    row = jnp.zeros((LANES,), F32)
    return row.at[start:start + values.shape[0]].set(values.astype(F32)).reshape(1, LANES)


def _layer(x_prompt, x_sample, mem_prompt, cache_mem_k, cache_mem_v,
           state_mlstm_C, state_mlstm_n, state_mlstm_m, state_gdn_S, state_gdn_conv,
           norm_mix, w_in, b_igate, b_fgate, mlstm_norm, conv_w, a_log, dt_bias, gdn_norm,
           mem_norm, w_mem_kv, w_br_mlstm, w_br_gdn, w_br_xattn, w_out, norm_ffn,
           w_router_group, b_router_group, w_router_expert, b_router_expert,
           w_exp_gate, w_exp_up, w_exp_down, final_norm, *, cfg):
    bp, sp, d = x_prompt.shape
    bs, ss, _ = x_sample.shape
    mem_len = mem_prompt.shape[1]
    ss_pad = SUBLANES
    assert ss <= ss_pad and d % LANES == 0
    n_p = bp * sp
    n_s = bs * ss_pad
    W = BRANCH_W
    gs_ = cfg["sample_group"]

    sizes = (W, W, W, N_HEADS, N_HEADS, W, 3 * W, N_HEADS, N_HEADS, W, W, d, d, d)
    offs = [0]
    for s in sizes:
        offs.append(offs[-1] + s)
    (m_q, m_k, m_v, m_i, m_f, m_o, g_qkv, g_a, g_b, g_out, x_q, gate_a, gate_b, gate_c) = [
        w_in[:, offs[j]:offs[j + 1]] for j in range(len(sizes))]
    w_big = _col_blocks(
        jnp.concatenate([gate_a, gate_b, gate_c, m_q, m_k, m_v, m_o, g_qkv, g_out, x_q],
                        axis=1).astype(BF16), PROJ_GROUP * W)
    w_small = jnp.concatenate(
        [m_i, m_f, g_a, g_b, jnp.zeros((d, LANES - 4 * N_HEADS), w_in.dtype)], axis=1).astype(BF16)
    w_router = jnp.concatenate(
        [w_router_group, w_router_expert,
         jnp.zeros((d, LANES - N_GROUPS - N_EXPERTS), w_in.dtype)], axis=1).astype(BF16)
    b_router = _lane_row(jnp.concatenate([b_router_group, b_router_expert]), 0)
    gate_bias = _lane_row(jnp.concatenate([b_igate, b_fgate]), 0)
    a_row = _lane_row(-jnp.exp(a_log.astype(F32)), 2 * N_HEADS)
    dt_row = _lane_row(dt_bias, 2 * N_HEADS)

    xp2 = x_prompt.reshape(n_p, d)
    xs2 = jnp.pad(x_sample, ((0, 0), (0, ss_pad - ss), (0, 0))).reshape(n_s, d)

    proj, gs = _norm_proj(xp2, xs2, norm_mix.reshape(1, d), w_big, w_small, tm=cfg["tm_proj"])

    mem2 = mem_prompt.reshape(bp * mem_len, d)
    mem_kv, _ = _norm_proj(mem2, None, mem_norm.reshape(1, d),
                           _col_blocks(w_mem_kv.astype(BF16), W),
                           jnp.zeros((d, LANES), BF16), tm=min(cfg["tm_proj"], bp * mem_len))

    mnorm = mlstm_norm.reshape(1, W)
    gnorm = gdn_norm.reshape(1, HEAD_DIM)
    s_off = n_p // (gs_ * ss_pad)
    ha_p, c_p, nn_p, mm_p = _mlstm(proj, gs, gate_bias, mnorm, None, n_batch=bp, seq=sp,
                                   s_valid=sp, row_off=0, L=cfg["chunk_m"], n_sub=cfg["sub_m"], G=1)
    m0 = jnp.broadcast_to(state_mlstm_m[:, :, None], (bs, N_HEADS, LANES))
    ha_s, c_s, nn_s, mm_s = _mlstm(proj, gs, gate_bias, mnorm,
                                   (state_mlstm_C, state_mlstm_n, m0), n_batch=bs, seq=ss_pad,
                                   s_valid=ss, row_off=s_off, L=ss_pad, n_sub=1, G=gs_)
    hb_p, s_p, cv_p = _gdn(proj, gs, conv_w, a_row, dt_row, gnorm, None, n_batch=bp, seq=sp,
                           s_valid=sp, row_off=0, L=cfg["chunk_g"], n_sub=cfg["sub_g"], G=1)
    conv0 = jnp.pad(state_gdn_conv, ((0, 0), (SUBLANES - (CONV_W - 1), 0), (0, 0)))
    hb_s, s_s, cv_s = _gdn(proj, gs, conv_w, a_row, dt_row, gnorm, (conv0, state_gdn_S),
                           n_batch=bs, seq=ss_pad, s_valid=ss, row_off=s_off, L=ss_pad,
                           n_sub=1, G=gs_)
    lt = min(sp, cfg["lt_x"])
    hc_p = _xattn(proj, mem_kv, mem_kv, n_batch=bp, seq=sp, row_off=0, lt=lt, G=1,
                  mem_len=mem_len, head_rows=False,
                  kv_specs=[pl.BlockSpec((None, mem_len, W), lambda b, t, c=c: (c, b, 0))
                            for c in range(2)])
    kv_rows = gs_ * mem_len * N_HEADS
    hc_s = _xattn(proj, cache_mem_k.reshape(bs * mem_len * N_HEADS, HEAD_DIM),
                  cache_mem_v.reshape(bs * mem_len * N_HEADS, HEAD_DIM),
                  n_batch=bs, seq=ss_pad, row_off=s_off, lt=ss_pad, G=gs_, mem_len=mem_len,
                  head_rows=True,
                  kv_specs=[pl.BlockSpec((kv_rows, HEAD_DIM), lambda b, t: (b, 0))] * 2)

    tm = cfg["tm"]
    x1rec, route, counts = _merge((ha_p, ha_s), (hb_p, hb_s), (hc_p, hc_s), (xp2, xs2), proj,
                          w_br_mlstm.astype(BF16), w_br_gdn.astype(BF16), w_br_xattn.astype(BF16),
                          w_out.astype(BF16), norm_ffn.reshape(1, d), w_router, b_router, tm=tm)

    tm_moe = cfg["tm_moe"]
    n_all = n_p + n_s
    n_tiles = (n_all + N_CLASSES * (tm_moe - 1)) // tm_moe + 1
    pos, ea, eb, n_used = _sort_plan(route[:, ROUTE_CLS].astype(jnp.int32),
                                     route[:, ROUTE_RANK].astype(jnp.int32),
                                     counts[0, :N_CLASSES].astype(jnp.int32),
                                     tm=tm_moe, n_tiles=n_tiles)
    tm_sort = cfg["tm_sort"]
    x_sorted = _sort_scatter(pos, x1rec, n_tiles * tm_moe, tm=tm_sort)
    y_sorted = _moe(ea, eb, n_used, x_sorted, w_exp_gate, w_exp_up, w_exp_down,
                    norm_ffn.reshape(1, d),
                    final_norm.reshape(1, d), w_router, b_router, tm=tm_moe, n_tiles=n_tiles)
    y_p = _unsort(pos, y_sorted, base=0, n_rows=n_p, tm=tm_sort)
    y_s = _unsort(pos, y_sorted, base=n_p, n_rows=n_s, tm=tm_sort)

    y_prompt = y_p.reshape(bp, sp, d)
    y_sample = y_s.reshape(bs, ss_pad, d)[:, :ss]
    mk_p = mem_kv[0].reshape(bp, mem_len, N_HEADS, HEAD_DIM)
    mv_p = mem_kv[1].reshape(bp, mem_len, N_HEADS, HEAD_DIM)
    tail = slice(SUBLANES - (CONV_W - 1), SUBLANES)
    return (y_prompt, y_sample, mk_p, mv_p,
            c_p, nn_p, mm_p[:, :, 0], s_p, cv_p[:, tail],
            c_s, nn_s, mm_s[:, :, 0], s_s, cv_s[:, tail])


CONFIG = dict(tm_proj=1024, tm=512, tm_sort=1024, tm_moe=256, chunk_m=256, sub_m=1, chunk_g=64, sub_g=8,
              lt_x=2048, sample_group=16)


def kernel(x_prompt, x_sample, mem_prompt, cache_mem_k, cache_mem_v, state_mlstm_C, state_mlstm_n, state_mlstm_m, state_gdn_S, state_gdn_conv, norm_mix, w_in, b_igate, b_fgate, mlstm_norm, conv_w, a_log, dt_bias, gdn_norm, mem_norm, w_mem_kv, w_br_mlstm, w_br_gdn, w_br_xattn, w_out, norm_ffn, w_router_group, b_router_group, w_router_expert, b_router_expert, w_exp_gate, w_exp_up, w_exp_down, final_norm):
    depth = w_in.shape[0]
    assert depth == 1, "one decoder layer"
    outs = _layer(
        x_prompt, x_sample, mem_prompt, cache_mem_k[0], cache_mem_v[0],
        state_mlstm_C[0], state_mlstm_n[0], state_mlstm_m[0], state_gdn_S[0], state_gdn_conv[0],
        norm_mix[0], w_in[0], b_igate[0], b_fgate[0], mlstm_norm[0], conv_w[0], a_log[0],
        dt_bias[0], gdn_norm[0], mem_norm[0], w_mem_kv[0], w_br_mlstm[0], w_br_gdn[0],
        w_br_xattn[0], w_out[0], norm_ffn[0], w_router_group[0], b_router_group[0],
        w_router_expert[0], b_router_expert[0], w_exp_gate[0], w_exp_up[0], w_exp_down[0],
        final_norm, cfg=CONFIG)
    y_prompt, y_sample = outs[0], outs[1]
    return (y_prompt, y_sample) + tuple(o[None] for o in outs[2:])
```

```python
import functools

import jax
import jax.numpy as jnp
from jax import lax
from jax.experimental import pallas as pl
from jax.experimental.pallas import tpu as pltpu

F32 = jnp.float32
BF16 = jnp.bfloat16
EPS = 1e-6

N_HEADS = 4
HEAD_DIM = 128
BRANCH_W = N_HEADS * HEAD_DIM
CONV_W = 4
N_GROUPS = 4
EXPERTS_PER_GROUP = 4
N_EXPERTS = N_GROUPS * EXPERTS_PER_GROUP
N_PAIRS = 6
N_CLASSES = N_GROUPS * N_PAIRS
PAIR_LO = (0, 0, 0, 1, 1, 2)
PAIR_HI = (1, 2, 3, 2, 3, 3)
NEG_BIG = -1e30
LANES = 128
SUBLANES = 8
TOKEN_ROWS = 8
PROJ_GROUP = 3
VMEM_PHYSICAL = 64 * 1024 * 1024
VMEM_LIMIT = 48 * 1024 * 1024
VMEM_SLACK = 4 * 1024 * 1024

COL_MQ, COL_MK, COL_MV, COL_MO = 0, 1, 2, 3
COL_GQ, COL_GK, COL_GV, COL_GO, COL_XQ = 4, 5, 6, 7, 8


def _dot(a, b):
    return jnp.dot(a.astype(BF16), b.astype(BF16), preferred_element_type=F32)


def _dot_nt(a, b):
    return lax.dot_general(a.astype(BF16), b.astype(BF16), (((1,), (1,)), ((), ())),
                           preferred_element_type=F32)


def _dot_tn(a, b):
    return lax.dot_general(a.astype(BF16), b.astype(BF16), (((0,), (0,)), ((), ())),
                           preferred_element_type=F32)


def _dot_exact(a, b):
    return jnp.dot(a, b, preferred_element_type=F32, precision=lax.Precision.HIGHEST)


def _lanes_to_rows(x, lane0):
    r = lax.broadcasted_iota(jnp.int32, (SUBLANES, LANES), 0)
    c = lax.broadcasted_iota(jnp.int32, (SUBLANES, LANES), 1)
    sel = (c == r + lane0).astype(F32)
    return lax.dot_general(sel, x, (((1,), (1,)), ((), ())), preferred_element_type=F32,
                           precision=lax.Precision.HIGHEST)


def _rms(x, g):
    return x * lax.rsqrt(jnp.mean(x * x, axis=-1, keepdims=True) + EPS) * g


def _softplus(x):
    return jnp.maximum(x, 0.0) + jnp.log1p(jnp.exp(-jnp.abs(x)))


def _sigmoid(x):
    return 0.5 * jnp.tanh(0.5 * x) + 0.5


def _silu(x):
    return x * _sigmoid(x)


def _tri(n, strict=False):
    r = lax.broadcasted_iota(jnp.int32, (n, n), 0)
    c = lax.broadcasted_iota(jnp.int32, (n, n), 1)
    return (c < r) if strict else (c <= r)


def _tri_blocks(n, block, upper=False):
    shift = block.bit_length() - 1
    assert block == 1 << shift
    r = lax.broadcasted_iota(jnp.int32, (n, n), 0)
    c = lax.broadcasted_iota(jnp.int32, (n, n), 1)
    same = jnp.right_shift(r, shift) == jnp.right_shift(c, shift)
    return same & ((r <= c) if upper else (c <= r))


def _chunk_cumsum(x, x_t, L):
    rows = x.shape[0]
    blk = max(L, LANES)
    if rows % blk:
        blk = rows
    lower = _tri_blocks(blk, L).astype(F32)
    upper = _tri_blocks(blk, L, upper=True).astype(F32)
    cols = [_dot_exact(lower, x[c * blk:(c + 1) * blk]) for c in range(rows // blk)]
    rws = [_dot_exact(x_t[:, c * blk:(c + 1) * blk], upper) for c in range(rows // blk)]
    if len(cols) == 1:
        return cols[0], rws[0]
    return jnp.concatenate(cols, axis=0), jnp.concatenate(rws, axis=1)


def _pad_rows(shape, t, seq_rows, n_valid):
    assert seq_rows & (seq_rows - 1) == 0
    row = lax.broadcasted_iota(jnp.int32, shape, 0)
    return (row & (seq_rows - 1)) + t * seq_rows >= n_valid


def _params(sem, vmem_bytes=VMEM_LIMIT):
    return pltpu.CompilerParams(dimension_semantics=sem, vmem_limit_bytes=vmem_bytes)


def _two_source_specs(n_first, block, width):
    first = pl.BlockSpec((block, width), lambda i, *_: (jnp.minimum(i, n_first - 1), 0))
    second = pl.BlockSpec((block, width), lambda i, *_: (jnp.maximum(i - n_first, 0), 0))
    return first, second


def _norm_proj_kernel(xa_ref, xb_ref, g_ref, w_ref, ws_ref, o_ref, os_ref, hb_ref, *, n_first):
    i = pl.program_id(0)
    j = pl.program_id(1)

    @pl.when(j == 0)
    def _():
        @pl.when(i < n_first)
        def _():
            hb_ref[...] = _rms(xa_ref[...], g_ref[...]).astype(BF16)

        @pl.when(i >= n_first)
        def _():
            hb_ref[...] = _rms(xb_ref[...], g_ref[...]).astype(BF16)

        os_ref[...] = jnp.dot(hb_ref[...], ws_ref[...], preferred_element_type=F32)

    o_ref[...] = jnp.dot(hb_ref[...], w_ref[0], preferred_element_type=F32)


def _proj_spec(rows, c, row_block):
    return pl.BlockSpec((None, rows, BRANCH_W),
                        lambda *g: (c // PROJ_GROUP, row_block(*g), c % PROJ_GROUP))


def _col_blocks(w, tn):
    d, n = w.shape
    return w.reshape(d, n // tn, tn).transpose(1, 0, 2)


def _norm_proj(xa, xb, g, w, ws, *, tm):
    d = xa.shape[1]
    n_first = xa.shape[0] // tm
    n_rows = xa.shape[0] + (0 if xb is None else xb.shape[0])
    if xb is None:
        xb = xa
    tn = w.shape[2]
    n = w.shape[0] * tn
    sa, sb = _two_source_specs(n_first, tm, d)
    vmem = (4 * tm * d * 4 + 2 * d * tn * 2 + 2 * tm * tn * 4 + tm * d * 2
            + 2 * (tm + d) * ws.shape[1] * 4)
    vmem = min(max(vmem + VMEM_SLACK, VMEM_LIMIT), VMEM_PHYSICAL - VMEM_SLACK)
    return pl.pallas_call(
        functools.partial(_norm_proj_kernel, n_first=n_first),
        out_shape=(jax.ShapeDtypeStruct((n // tn, n_rows, tn), F32),
                   jax.ShapeDtypeStruct((n_rows, ws.shape[1]), F32)),
        grid=(n_rows // tm, n // tn),
        in_specs=[sa, sb,
                  pl.BlockSpec((1, d), lambda i, j: (0, 0)),
                  pl.BlockSpec((1, d, tn), lambda i, j: (j, 0, 0)),
                  pl.BlockSpec((d, ws.shape[1]), lambda i, j: (0, 0))],
        out_specs=(pl.BlockSpec((None, tm, tn), lambda i, j: (j, i, 0)),
                   pl.BlockSpec((tm, ws.shape[1]), lambda i, j: (i, 0))),
        scratch_shapes=[pltpu.VMEM((tm, d), BF16)],
        compiler_params=_params(("parallel", "arbitrary"), vmem),
        name="norm_proj",
    )(xa, xb, g, w, ws)


def _seq_tile_specs(G, Lt, nt, row_off):
    assert G == 1 or nt == 1

    def rows(b, t):
        return row_off + b * nt + t

    return rows


def _mlstm_kernel(*refs, L, n_sub, G, s_valid, s_total, has_init):
    if has_init:
        (q_ref, k_ref, v_ref, og_ref, gs_ref, bias_ref, norm_ref, c0_ref, n0_ref, m0_ref,
         h_ref, c_ref, n_ref, m_ref) = refs
    else:
        (q_ref, k_ref, v_ref, og_ref, gs_ref, bias_ref, norm_ref,
         h_ref, c_ref, n_ref, m_ref) = refs
    t = pl.program_id(1)
    Lt = n_sub * L
    rows = G * Lt

    @pl.when(t == 0)
    def _():
        if has_init:
            c_ref[...] = c0_ref[...]
            n_ref[...] = n0_ref[...]
            m_ref[...] = m0_ref[...]
        else:
            c_ref[...] = jnp.zeros_like(c_ref)
            n_ref[...] = jnp.zeros_like(n_ref)
            m_ref[...] = jnp.zeros_like(m_ref)

    z = gs_ref[...] + bias_ref[...]
    lane = lax.broadcasted_iota(jnp.int32, z.shape, 1)
    log_f = jnp.minimum(z, 0.0) - jnp.log1p(jnp.exp(-jnp.abs(z)))
    g = jnp.where((lane >= N_HEADS) & (lane < 2 * N_HEADS), log_f, z)
    if s_valid < s_total:
        pad = _pad_rows(z.shape, t, Lt, s_valid)
        g = jnp.where(pad, jnp.where(lane < N_HEADS, NEG_BIG, 0.0), g)
    g_t = _lanes_to_rows(g, 0)
    bt_cols, bt_rows = _chunk_cumsum(g, g_t, L)
    scale = HEAD_DIM ** -0.5
    assert G == 1 or n_sub == 1
    n_r = G * L
    mask = _tri_blocks(n_r, L)

    def per_seq(x):
        return [x[j * L:(j + 1) * L] for j in range(G)]

    def expand(pieces):
        if G == 1:
            return pieces[0]
        return jnp.concatenate([jnp.broadcast_to(x, (L, x.shape[1])) for x in pieces], axis=0)

    for c in range(n_sub):
        r = slice(c * L, c * L + n_r)
        P = []
        for h in range(N_HEADS):
            sl = slice(h * HEAD_DIM, (h + 1) * HEAD_DIM)
            p = dict(h=h, sl=sl, q=q_ref[r, sl], k=k_ref[r, sl] * scale, v=v_ref[r, sl],
                     bt_c=bt_cols[r, N_HEADS + h:N_HEADS + h + 1], it_c=g[r, h:h + 1])
            bt_r = bt_rows[N_HEADS + h:N_HEADS + h + 1, r]
            it_r = g_t[h:h + 1, r]
            p["log_d"] = jnp.where(mask, p["bt_c"] - bt_r + it_r, -jnp.inf)
            p["c_prev"] = [c_ref[s, h] for s in range(G)]
            p["n_prev"] = [n_ref[s, h:h + 1, :] for s in range(G)]
            p["m_prev"] = [m_ref[s, h:h + 1, 0:1] for s in range(G)]
            P.append(p)
        for p in P:
            p["qk"] = _dot_nt(p["q"], p["k"])
        for p in P:
            qs = per_seq(p["q"])
            qc = [_dot(qs[s], p["c_prev"][s]) for s in range(G)]
            p["qc"] = qc[0] if G == 1 else jnp.concatenate(qc, axis=0)
        for p in P:
            inter = p["bt_c"] + expand(p["m_prev"])
            p["m_t"] = jnp.maximum(inter, jnp.max(p["log_d"], axis=-1, keepdims=True))
            p["inter_w"] = jnp.exp(inter - p["m_t"])
            p["sd"] = p["qk"] * jnp.exp(p["log_d"] - p["m_t"])
        for p in P:
            p["sv"] = _dot(p["sd"], p["v"])
        for p in P:
            bt_last = [x[L - 1:L, :] for x in per_seq(p["bt_c"])]
            p["m_new"] = [x[L - 1:L, :] for x in per_seq(p["m_t"])]
            w = jnp.exp(expand(bt_last) - p["bt_c"] + p["it_c"] - expand(p["m_new"]))
            p["decay"] = [jnp.exp(bt_last[s] + p["m_prev"][s] - p["m_new"][s]) for s in range(G)]
            p["kw"] = p["k"] * w
        for p in P:
            kws, vs = per_seq(p["kw"]), per_seq(p["v"])
            p["kv"] = [_dot_tn(kws[s], vs[s]) for s in range(G)]
            p["n_new"] = [p["decay"][s] * p["n_prev"][s] + jnp.sum(kws[s], axis=0, keepdims=True)
                          for s in range(G)]
        for p in P:
            num = p["sv"] + p["qc"] * p["inter_w"]
            den = (jnp.sum(p["sd"], axis=-1, keepdims=True)
                   + p["inter_w"] * jnp.sum(p["q"] * expand(p["n_prev"]), axis=-1, keepdims=True))
            hh = num / jnp.maximum(jnp.abs(den), jnp.exp(-p["m_t"]))
            hn = hh * lax.rsqrt(jnp.mean(hh * hh, axis=-1, keepdims=True) + EPS) * norm_ref[:, p["sl"]]
            p["h_out"] = hn * _sigmoid(og_ref[r, p["sl"]])
        for p in P:
            h = p["h"]
            h_ref[r, p["sl"]] = p["h_out"]
            for s in range(G):
                n_ref[s, h:h + 1, :] = p["n_new"][s]
                m_ref[s, h:h + 1, :] = jnp.broadcast_to(p["m_new"][s], (1, LANES))
                c_ref[s, h] = p["decay"][s] * p["c_prev"][s] + p["kv"][s]


def _mlstm(proj, gs, bias_row, norm_row, init, *, n_batch, seq, s_valid, row_off, L, n_sub, G):
    Lt = n_sub * L
    nt = seq // Lt
    rows = G * Lt
    has_init = init is not None
    blk = _seq_tile_specs(G, Lt, nt, row_off)

    def col(c):
        return _proj_spec(rows, c, blk)

    in_specs = [col(COL_MQ), col(COL_MK), col(COL_MV), col(COL_MO),
                pl.BlockSpec((rows, LANES), lambda b, t: (blk(b, t), 0)),
                pl.BlockSpec((1, LANES), lambda b, t: (0, 0)),
                pl.BlockSpec((1, BRANCH_W), lambda b, t: (0, 0))]
    args = [proj, proj, proj, proj, gs, bias_row, norm_row]
    c_spec = pl.BlockSpec((G, N_HEADS, HEAD_DIM, HEAD_DIM), lambda b, t: (b, 0, 0, 0))
    n_spec = pl.BlockSpec((G, N_HEADS, HEAD_DIM), lambda b, t: (b, 0, 0))
    m_spec = pl.BlockSpec((G, N_HEADS, LANES), lambda b, t: (b, 0, 0))
    if has_init:
        in_specs += [c_spec, n_spec, m_spec]
        args += list(init)
    return pl.pallas_call(
        functools.partial(_mlstm_kernel, L=L, n_sub=n_sub, G=G, s_valid=s_valid, s_total=seq,
                          has_init=has_init),
        out_shape=(jax.ShapeDtypeStruct((n_batch * seq, BRANCH_W), F32),
                   jax.ShapeDtypeStruct((n_batch, N_HEADS, HEAD_DIM, HEAD_DIM), F32),
                   jax.ShapeDtypeStruct((n_batch, N_HEADS, HEAD_DIM), F32),
                   jax.ShapeDtypeStruct((n_batch, N_HEADS, LANES), F32)),
        grid=(n_batch // G, nt),
        in_specs=in_specs,
        out_specs=(pl.BlockSpec((rows, BRANCH_W), lambda b, t: (b * nt + t, 0)),
                   c_spec, n_spec, m_spec),
        compiler_params=_params(("parallel", "arbitrary")),
        name="mlstm",
    )(*args)


def _gdn_kernel(*refs, L, n_sub, G, s_valid, s_total, has_init):
    if has_init:
        (q_ref, k_ref, v_ref, og_ref, gs_ref, cw_ref, arow_ref, dtrow_ref, norm_ref,
         conv0_ref, s0_ref, o_ref, s_ref, conv1_ref, ext_ref, conv_ref) = refs
    else:
        (q_ref, k_ref, v_ref, og_ref, gs_ref, cw_ref, arow_ref, dtrow_ref, norm_ref,
         o_ref, s_ref, conv1_ref, ext_ref, conv_ref) = refs
    t = pl.program_id(1)
    Lt = n_sub * L
    rows = G * Lt
    nt = s_total // Lt
    W = BRANCH_W

    @pl.when(t == 0)
    def _():
        if has_init:
            s_ref[...] = s0_ref[...]
            ext_ref[:, 0:SUBLANES, :] = conv0_ref[...]
        else:
            s_ref[...] = jnp.zeros_like(s_ref)
            ext_ref[:, 0:SUBLANES, :] = jnp.zeros((G, SUBLANES, 3 * W), F32)

    @pl.when(t > 0)
    def _():
        ext_ref[:, 0:SUBLANES, :] = ext_ref[:, Lt:Lt + SUBLANES, :]

    base = SUBLANES - (CONV_W - 1)
    for s in range(G):
        rs = slice(s * Lt, (s + 1) * Lt)
        ext_ref[s, SUBLANES:SUBLANES + Lt, 0:W] = q_ref[rs, :]
        ext_ref[s, SUBLANES:SUBLANES + Lt, W:2 * W] = k_ref[rs, :]
        ext_ref[s, SUBLANES:SUBLANES + Lt, 2 * W:3 * W] = v_ref[rs, :]
        acc = ext_ref[s, base:base + Lt, :] * cw_ref[0:1, :]
        for j in range(1, CONV_W):
            acc = acc + ext_ref[s, base + j:base + j + Lt, :] * cw_ref[j:j + 1, :]
        conv_ref[rs, :] = _silu(acc)

    @pl.when(t == nt - 1)
    def _():
        v_last = s_valid - (nt - 1) * Lt
        conv1_ref[...] = ext_ref[:, v_last:v_last + SUBLANES, :]

    z = gs_ref[...]
    lane = lax.broadcasted_iota(jnp.int32, z.shape, 1)
    gdec = arow_ref[...] * _softplus(z + dtrow_ref[...])
    gb = jnp.where((lane >= 2 * N_HEADS) & (lane < 3 * N_HEADS), gdec,
                   jnp.where((lane >= 3 * N_HEADS) & (lane < 4 * N_HEADS), _sigmoid(z), 0.0))
    if s_valid < s_total:
        gb = jnp.where(_pad_rows(z.shape, t, Lt, s_valid), 0.0, gb)
    gb_t = _lanes_to_rows(gb, 2 * N_HEADS)
    gam_cols, gam_rows = _chunk_cumsum(gb, gb_t, L)
    incl = _tri(L)
    strict = _tri(L, strict=True)
    eye = (lax.broadcasted_iota(jnp.int32, (L, L), 0)
           == lax.broadcasted_iota(jnp.int32, (L, L), 1)).astype(F32)

    P = []
    for s in range(G):
        for c in range(n_sub):
            for h in range(N_HEADS):
                r = slice(s * Lt + c * L, s * Lt + (c + 1) * L)
                sl = slice(h * HEAD_DIM, (h + 1) * HEAD_DIM)
                q = conv_ref[r, h * HEAD_DIM:(h + 1) * HEAD_DIM]
                k = conv_ref[r, W + h * HEAD_DIM:W + (h + 1) * HEAD_DIM]
                v = conv_ref[r, 2 * W + h * HEAD_DIM:2 * W + (h + 1) * HEAD_DIM]
                q = q * lax.rsqrt(jnp.sum(q * q, axis=-1, keepdims=True) + EPS) * (HEAD_DIM ** -0.5)
                k = k * lax.rsqrt(jnp.sum(k * k, axis=-1, keepdims=True) + EPS)
                gam_c = gam_cols[r, 2 * N_HEADS + h:2 * N_HEADS + h + 1]
                gam_r = gam_rows[h:h + 1, r]
                beta_c = gb[r, 3 * N_HEADS + h:3 * N_HEADS + h + 1]
                decay = jnp.exp(jnp.where(incl, gam_c - gam_r, -jnp.inf))
                P.append(dict(s=s, c=c, h=h, r=r, sl=sl, q=q, k=k, v=v, gam_c=gam_c,
                              beta_c=beta_c, decay=decay))
    for p in P:
        p["kk"] = _dot_nt(p["k"], p["k"])
    for p in P:
        p["qk"] = _dot_nt(p["q"], p["k"]) * p["decay"]
    for p in P:
        e_gam = jnp.exp(p["gam_c"])
        gam_last = p["gam_c"][L - 1:L, :]
        p["pw"] = -jnp.where(strict, p["beta_c"] * p["kk"] * p["decay"], 0.0)
        p["x"] = eye + p["pw"]
        p["rhs"] = jnp.concatenate([p["beta_c"] * p["v"], p["beta_c"] * p["k"] * e_gam], axis=-1)
        p["q_dec"] = p["q"] * e_gam
        p["k_dec"] = p["k"] * jnp.exp(gam_last - p["gam_c"])
        p["e_last"] = jnp.exp(gam_last)
    n = 2
    while n < L:
        for p in P:
            p["pw"] = _dot(p["pw"], p["pw"])
        for p in P:
            p["x"] = p["x"] + _dot(p["x"], p["pw"])
        n *= 2
    for p in P:
        sol = _dot(p["x"], p["rhs"])
        p["u"] = sol[:, :HEAD_DIM]
        p["w"] = sol[:, HEAD_DIM:]

    for c in range(n_sub):
        Pc = [p for p in P if p["c"] == c]
        for p in Pc:
            p["s_prev"] = s_ref[p["s"], p["h"]]
            p["ws"] = _dot(p["w"], p["s_prev"])
        for p in Pc:
            p["qs"] = _dot(p["q_dec"], p["s_prev"])
        for p in Pc:
            p["v_new"] = p["u"] - p["ws"]
        for p in Pc:
            p["o"] = p["qs"] + _dot(p["qk"], p["v_new"])
        for p in Pc:
            p["kv"] = _dot_tn(p["k_dec"], p["v_new"])
        for p in Pc:
            o = p["o"]
            on = o * lax.rsqrt(jnp.mean(o * o, axis=-1, keepdims=True) + EPS) * norm_ref[...]
            p["o_out"] = on * _silu(og_ref[p["r"], p["sl"]])
        for p in Pc:
            s_ref[p["s"], p["h"]] = p["e_last"] * p["s_prev"] + p["kv"]
            o_ref[p["r"], p["sl"]] = p["o_out"]


def _gdn(proj, gs, conv_w, a_row, dt_row, norm_row, init, *, n_batch, seq, s_valid, row_off,
         L, n_sub, G):
    Lt = n_sub * L
    nt = seq // Lt
    rows = G * Lt
    has_init = init is not None
    blk = _seq_tile_specs(G, Lt, nt, row_off)

    def col(c):
        return _proj_spec(rows, c, blk)

    def const(shape):
        return pl.BlockSpec(shape, lambda b, t: (0,) * len(shape))

    in_specs = [col(COL_GQ), col(COL_GK), col(COL_GV), col(COL_GO),
                pl.BlockSpec((rows, LANES), lambda b, t: (blk(b, t), 0)),
                const((CONV_W, 3 * BRANCH_W)), const((1, LANES)), const((1, LANES)),
                const((1, HEAD_DIM))]
    args = [proj, proj, proj, proj, gs, conv_w, a_row, dt_row, norm_row]
    s_spec = pl.BlockSpec((G, N_HEADS, HEAD_DIM, HEAD_DIM), lambda b, t: (b, 0, 0, 0))
    conv_spec = pl.BlockSpec((G, SUBLANES, 3 * BRANCH_W), lambda b, t: (b, 0, 0))
    if has_init:
        in_specs += [conv_spec, s_spec]
        args += list(init)
    return pl.pallas_call(
        functools.partial(_gdn_kernel, L=L, n_sub=n_sub, G=G, s_valid=s_valid, s_total=seq,
                          has_init=has_init),
        out_shape=(jax.ShapeDtypeStruct((n_batch * seq, BRANCH_W), F32),
                   jax.ShapeDtypeStruct((n_batch, N_HEADS, HEAD_DIM, HEAD_DIM), F32),
                   jax.ShapeDtypeStruct((n_batch, SUBLANES, 3 * BRANCH_W), F32)),
        grid=(n_batch // G, nt),
        in_specs=in_specs,
        out_specs=(pl.BlockSpec((rows, BRANCH_W), lambda b, t: (b * nt + t, 0)),
                   s_spec, conv_spec),
        scratch_shapes=[pltpu.VMEM((G, Lt + SUBLANES, 3 * BRANCH_W), F32),
                        pltpu.VMEM((rows, 3 * BRANCH_W), F32)],
        compiler_params=_params(("parallel", "arbitrary")),
        name="gdn",
    )(*args)


def _xattn_kernel(q_ref, mk_ref, mv_ref, o_ref, *, G, lt, mem_len, head_rows):
    scale = HEAD_DIM ** -0.5

    def head(ref, s, h):
        if head_rows:
            return ref[pl.ds(s * mem_len * N_HEADS + h, mem_len, stride=N_HEADS), :]
        return ref[s * mem_len:(s + 1) * mem_len, h * HEAD_DIM:(h + 1) * HEAD_DIM]

    P = []
    for s in range(G):
        for h in range(N_HEADS):
            P.append(dict(s=s, h=h, r=slice(s * lt, (s + 1) * lt),
                          sl=slice(h * HEAD_DIM, (h + 1) * HEAD_DIM)))
    for p in P:
        p["sc"] = _dot_nt(q_ref[p["r"], p["sl"]], head(mk_ref, p["s"], p["h"])) * scale
    for p in P:
        e = jnp.exp(p["sc"] - jnp.max(p["sc"], axis=-1, keepdims=True))
        p["p"] = e / jnp.sum(e, axis=-1, keepdims=True)
    for p in P:
        o_ref[p["r"], p["sl"]] = _dot(p["p"], head(mv_ref, p["s"], p["h"]))


def _xattn(proj, mk, mv, *, n_batch, seq, row_off, lt, G, mem_len, kv_specs, head_rows):
    nt = seq // lt
    blk = _seq_tile_specs(G, lt, nt, row_off)
    return pl.pallas_call(
        functools.partial(_xattn_kernel, G=G, lt=lt, mem_len=mem_len, head_rows=head_rows),
        out_shape=jax.ShapeDtypeStruct((n_batch * seq, BRANCH_W), F32),
        grid=(n_batch // G, nt),
        in_specs=[_proj_spec(G * lt, COL_XQ, blk)] + list(kv_specs),
        out_specs=pl.BlockSpec((G * lt, BRANCH_W), lambda b, t: (b * nt + t, 0)),
        compiler_params=_params(("parallel", "arbitrary")),
        name="xattn",
    )(proj, mk, mv)


def _lane_first_max(x, mask, lane_f):
    xm = jnp.where(mask, x, -jnp.inf)
    mx = jnp.max(xm, axis=-1, keepdims=True)
    idx = jnp.min(jnp.where(xm == mx, lane_f, float(LANES)), axis=-1, keepdims=True)
    return mx, idx


def _store_token_tiles(ref, x, tok_rows):
    n = x.shape[0]
    for j in range(x.shape[1] // LANES):
        ref[pl.ds(j, n, stride=tok_rows), :] = x[:, j * LANES:(j + 1) * LANES]


def _load_token_tiles(ref, n, tok_rows, n_chunks=SUBLANES):
    return jnp.concatenate([ref[pl.ds(j, n, stride=tok_rows), :] for j in range(n_chunks)], axis=1)


def _merge_kernel(hap_ref, has_ref, hbp_ref, hbs_ref, hcp_ref, hcs_ref, xp_ref, xs_ref,
                  nm_ref, wg_ref, wa_ref, wb_ref, wc_ref, wo_ref, nf_ref, wr_ref, br_ref,
                  x1_ref, route_ref, count_ref, *, n_first):
    i = pl.program_id(0)

    @pl.when(i == 0)
    def _():
        count_ref[...] = jnp.zeros_like(count_ref)

    def body(ha_ref, hb_ref, hc_ref, x_ref):
        d = x_ref.shape[1]
        x = x_ref[...]
        gates = _sigmoid(jnp.dot(_rms(x, nm_ref[...]).astype(BF16), wg_ref[...],
                                 preferred_element_type=F32))
        mixed = (gates[:, 0:d] * _dot(ha_ref[...], wa_ref[...])
                 + gates[:, d:2 * d] * _dot(hb_ref[...], wb_ref[...])
                 + gates[:, 2 * d:3 * d] * _dot(hc_ref[...], wc_ref[...]))
        x1 = x + _dot(mixed, wo_ref[...])
        h2 = _rms(x1, nf_ref[...])
        logits = _dot(h2, wr_ref[...]) + br_ref[...]
        lane = lax.broadcasted_iota(jnp.int32, logits.shape, 1)
        lane_f = lane.astype(F32)
        is_g = lane < N_GROUPS
        _, g_idx = _lane_first_max(logits, is_g, lane_f)
        e_lo = N_GROUPS + EXPERTS_PER_GROUP * g_idx
        in_grp = (lane_f >= e_lo) & (lane_f < e_lo + EXPERTS_PER_GROUP)
        _, i1 = _lane_first_max(logits, in_grp, lane_f)
        _, i2 = _lane_first_max(logits, in_grp & (lane_f != i1), lane_f)
        lo = jnp.minimum(i1, i2) - e_lo
        hi = jnp.maximum(i1, i2) - e_lo
        pair = lo * (7.0 - lo) * 0.5 + (hi - lo - 1.0)
        cls = g_idx * float(N_PAIRS) + pair
        _store_token_tiles(x1_ref, x1, TOKEN_ROWS)
        tm = logits.shape[0]
        onehot = (lane_f == cls).astype(F32)
        earlier = jnp.dot(_tri(tm, strict=True).astype(BF16), onehot.astype(BF16),
                          preferred_element_type=F32)
        rank = jnp.sum((earlier + count_ref[0:1, :]) * onehot, axis=-1, keepdims=True)
        count_ref[0:1, :] = count_ref[0:1, :] + jnp.sum(onehot, axis=0, keepdims=True)
        route_ref[...] = jnp.where(lane == ROUTE_CLS, cls, jnp.where(lane == ROUTE_RANK, rank, 0.0))

    @pl.when(i < n_first)
    def _():
        body(hap_ref, hbp_ref, hcp_ref, xp_ref)

    @pl.when(i >= n_first)
    def _():
        body(has_ref, hbs_ref, hcs_ref, xs_ref)


def _merge(ha, hb, hc, xs, nm, wg, wa, wb, wc, wo, nf, wr, br, *, tm):
    d = xs[0].shape[1]
    n_first = xs[0].shape[0] // tm
    n_rows = xs[0].shape[0] + xs[1].shape[0]
    specs = []
    for width in (BRANCH_W, BRANCH_W, BRANCH_W, d):
        specs += list(_two_source_specs(n_first, tm, width))

    def const(a):
        return pl.BlockSpec(a.shape, lambda i: (0,) * a.ndim)

    weights = [nm, wg, wa, wb, wc, wo, nf, wr, br]
    specs += [const(a) for a in weights]
    return pl.pallas_call(
        functools.partial(_merge_kernel, n_first=n_first),
        out_shape=(jax.ShapeDtypeStruct((n_rows * TOKEN_ROWS, LANES), F32),
                   jax.ShapeDtypeStruct((n_rows, LANES), F32),
                   jax.ShapeDtypeStruct((SUBLANES, LANES), F32)),
        grid=(n_rows // tm,),
        in_specs=specs,
        out_specs=(pl.BlockSpec((tm * TOKEN_ROWS, LANES), lambda i: (i, 0)),
                   pl.BlockSpec((tm, LANES), lambda i: (i, 0)),
                   pl.BlockSpec((SUBLANES, LANES), lambda i: (0, 0))),
        compiler_params=_params(("arbitrary",)),
        name="merge",
    )(*ha, *hb, *hc, *xs, *weights)


GATHER_UNROLL = 8
N_DMA_PRIORITIES = 2
ROUTE_CLS, ROUTE_RANK = 0, 1


def _record_gather_start(src_ref, dst_ref, sem, ids_ref, base, n, rec):
    def issue(r2, carry):
        for prio in range(N_DMA_PRIORITIES):
            r = r2 * N_DMA_PRIORITIES + prio
            src = pl.multiple_of(ids_ref[base + r] * rec, rec)
            dst = pl.multiple_of(r * rec, rec)
            pltpu.make_async_copy(src_ref.at[pl.ds(src, rec)], dst_ref.at[pl.ds(dst, rec)],
                                  sem).start(priority=prio)
        return carry
    lax.fori_loop(0, n // N_DMA_PRIORITIES, issue, 0, unroll=GATHER_UNROLL // N_DMA_PRIORITIES)


def _record_gather_wait(src_ref, dst_ref, sem):
    pltpu.make_async_copy(src_ref.at[pl.ds(0, dst_ref.shape[0])], dst_ref, sem).wait()


def _sort_scatter_kernel(pos_ref, x_ref, init_hbm, out_hbm, sem, *, tm):
    del init_hbm
    base = pl.program_id(0) * tm

    def issue(r2, carry):
        for prio in range(N_DMA_PRIORITIES):
            r = r2 * N_DMA_PRIORITIES + prio
            src = pl.multiple_of(r * TOKEN_ROWS, TOKEN_ROWS)
            dst = pl.multiple_of(pos_ref[base + r] * TOKEN_ROWS, TOKEN_ROWS)
            pltpu.make_async_copy(x_ref.at[pl.ds(src, TOKEN_ROWS)],
                                  out_hbm.at[pl.ds(dst, TOKEN_ROWS)], sem).start(priority=prio)
        return carry
    lax.fori_loop(0, tm // N_DMA_PRIORITIES, issue, 0, unroll=GATHER_UNROLL // N_DMA_PRIORITIES)
    pltpu.make_async_copy(x_ref, out_hbm.at[pl.ds(0, tm * TOKEN_ROWS)], sem).wait()


def _sort_scatter(pos, x1rec, n_sorted, *, tm):
    n_tok = x1rec.shape[0] // TOKEN_ROWS
    grid_spec = pltpu.PrefetchScalarGridSpec(
        num_scalar_prefetch=1,
        grid=(n_tok // tm,),
        in_specs=[pl.BlockSpec((tm * TOKEN_ROWS, LANES), lambda i, *_: (i, 0)),
                  pl.BlockSpec(memory_space=pl.ANY)],
        out_specs=pl.BlockSpec(memory_space=pl.ANY),
        scratch_shapes=[pltpu.SemaphoreType.DMA(())],
    )
    return pl.pallas_call(
        functools.partial(_sort_scatter_kernel, tm=tm),
        out_shape=jax.ShapeDtypeStruct((n_sorted * TOKEN_ROWS, LANES), F32),
        grid_spec=grid_spec,
        input_output_aliases={2: 0},
        compiler_params=_params(("arbitrary",)),
        name="sort_scatter",
    )(pos, x1rec, jnp.zeros((n_sorted * TOKEN_ROWS, LANES), F32))


def _moe_kernel(ea_ref, eb_ref, nused_ref, x_ref, wga_ref, wua_ref, wda_ref,
                wgb_ref, wub_ref, wdb_ref, nf_ref, fin_ref, wr_ref, br_ref, y_ref,
                wg_bf, wu_bf, wd_bf, *, tm, d):
    i = pl.program_id(0)
    n_used = nused_ref[0]

    prev = jnp.maximum(i - 1, 0)
    for slot, e_ref, srcs in ((0, ea_ref, (wga_ref, wua_ref, wda_ref)),
                              (1, eb_ref, (wgb_ref, wub_ref, wdb_ref))):
        @pl.when((i < n_used) & ((i == 0) | (e_ref[i] != e_ref[prev])))
        def _():
            for dst, src in zip((wg_bf, wu_bf, wd_bf), srcs):
                dst[slot] = src[0].astype(BF16)

    @pl.when(i < n_used)
    def _():
        x1 = _load_token_tiles(x_ref, tm, TOKEN_ROWS)
        h2 = _rms(x1, nf_ref[...]).astype(BF16)
        logits = jnp.dot(h2, wr_ref[...], preferred_element_type=F32) + br_ref[...]
        lane = lax.broadcasted_iota(jnp.int32, logits.shape, 1)

        def pick(l):
            return jnp.sum(jnp.where(lane == l, logits, 0.0), axis=-1, keepdims=True)

        ea, eb = ea_ref[i], eb_ref[i]
        is_g = lane < N_GROUPS
        g_max = jnp.max(jnp.where(is_g, logits, -jnp.inf), axis=-1, keepdims=True)
        g_den = jnp.sum(jnp.where(is_g, jnp.exp(logits - g_max), 0.0), axis=-1, keepdims=True)
        p_grp = jnp.exp(pick(ea // EXPERTS_PER_GROUP) - g_max) / g_den
        v_a, v_b = pick(N_GROUPS + ea), pick(N_GROUPS + eb)
        combs = (p_grp / (1.0 + jnp.exp(v_b - v_a)), p_grp / (1.0 + jnp.exp(v_a - v_b)))
        acc = jnp.zeros((tm, d), F32)
        for slot, comb in enumerate(combs):
            a = jnp.dot(h2, wg_bf[slot], preferred_element_type=F32)
            u = jnp.dot(h2, wu_bf[slot], preferred_element_type=F32)
            act = _silu(a) * u * comb
            acc = acc + _dot(act, wd_bf[slot])
        _store_token_tiles(y_ref, _rms(x1 + acc, fin_ref[...]), SUBLANES)

    @pl.when(i >= n_used)
    def _():
        y_ref[...] = jnp.zeros_like(y_ref)


def _moe(tile_ea, tile_eb, n_used, x_sorted, wg, wu, wd, nf, fin, wr, br, *, tm, n_tiles):
    d = nf.shape[1]
    f = wg.shape[2]
    assert d == SUBLANES * LANES

    def expert(shape, which):
        return pl.BlockSpec((1,) + shape, lambda i, ea, eb, nu: ((ea, eb)[which][i], 0, 0))

    def const(a):
        return pl.BlockSpec(a.shape, lambda i, *_: (0,) * a.ndim)

    grid_spec = pltpu.PrefetchScalarGridSpec(
        num_scalar_prefetch=3,
        grid=(n_tiles,),
        in_specs=[pl.BlockSpec((tm * TOKEN_ROWS, LANES), lambda i, *_: (i, 0)),
                  expert((d, f), 0), expert((d, f), 0), expert((f, d), 0),
                  expert((d, f), 1), expert((d, f), 1), expert((f, d), 1),
                  const(nf), const(fin), const(wr), const(br)],
        out_specs=pl.BlockSpec((tm * SUBLANES, LANES), lambda i, *_: (i, 0)),
        scratch_shapes=[pltpu.VMEM((2, d, f), BF16), pltpu.VMEM((2, d, f), BF16),
                        pltpu.VMEM((2, f, d), BF16)],
    )
    return pl.pallas_call(
        functools.partial(_moe_kernel, tm=tm, d=d),
        out_shape=jax.ShapeDtypeStruct((n_tiles * tm * SUBLANES, LANES), F32),
        grid_spec=grid_spec,
        compiler_params=_params(("arbitrary",)),
        name="moe",
    )(tile_ea, tile_eb, n_used, x_sorted, wg, wu, wd, wg, wu, wd, nf, fin, wr, br)


def _unsort_kernel(pos_ref, y_hbm, o_ref, buf, sem, *, tm, base):
    i = pl.program_id(0)
    slot = i % 2

    @pl.when(i == 0)
    def _():
        _record_gather_start(y_hbm, buf.at[0], sem.at[0], pos_ref, base, tm, SUBLANES)

    @pl.when(i + 1 < pl.num_programs(0))
    def _():
        _record_gather_start(y_hbm, buf.at[1 - slot], sem.at[1 - slot], pos_ref,
                             base + (i + 1) * tm, tm, SUBLANES)

    _record_gather_wait(y_hbm, buf.at[slot], sem.at[slot])
    o_ref[...] = _load_token_tiles(buf.at[slot], tm, SUBLANES)


def _unsort(pos, y_rec, *, base, n_rows, tm):
    d = SUBLANES * LANES
    grid_spec = pltpu.PrefetchScalarGridSpec(
        num_scalar_prefetch=1,
        grid=(n_rows // tm,),
        in_specs=[pl.BlockSpec(memory_space=pl.ANY)],
        out_specs=pl.BlockSpec((tm, d), lambda i, *_: (i, 0)),
        scratch_shapes=[pltpu.VMEM((2, tm * SUBLANES, LANES), F32),
                        pltpu.SemaphoreType.DMA((2,))],
    )
    return pl.pallas_call(
        functools.partial(_unsort_kernel, tm=tm, base=base),
        out_shape=jax.ShapeDtypeStruct((n_rows, d), F32),
        grid_spec=grid_spec,
        compiler_params=_params(("arbitrary",)),
        name="unsort",
    )(pos, y_rec)


def _sort_plan(cls, rank, counts, *, tm, n_tiles):
    onehot = (cls[:, None] == jnp.arange(N_CLASSES, dtype=jnp.int32)[None, :]).astype(jnp.int32)
    tiles_per = (counts + tm - 1) // tm
    tile_end = jnp.cumsum(tiles_per)
    tile_start = tile_end - tiles_per
    pos = (jnp.sum(tile_start[None, :] * onehot, axis=1) * tm + rank).astype(jnp.int32)
    n_used = tile_end[-1].astype(jnp.int32)
    tile = jnp.minimum(jnp.arange(n_tiles, dtype=jnp.int32), n_used - 1)
    tile_cls = jnp.sum((tile[:, None] >= tile_end[None, :]).astype(jnp.int32), axis=1)
    grp = tile_cls // N_PAIRS
    pair = tile_cls % N_PAIRS
    lo = jnp.asarray(PAIR_LO, jnp.int32)[pair]
    hi = jnp.asarray(PAIR_HI, jnp.int32)[pair]
    ea = (grp * EXPERTS_PER_GROUP + lo).astype(jnp.int32)
    eb = (grp * EXPERTS_PER_GROUP + hi).astype(jnp.int32)
    return pos, ea, eb, n_used.reshape(1)


def _lane_row(values, start):
    row = jnp.zeros((LANES,), F32)
    return row.at[start:start + values.shape[0]].set(values.astype(F32)).reshape(1, LANES)


def _layer(x_prompt, x_sample, mem_prompt, cache_mem_k, cache_mem_v,
           state_mlstm_C, state_mlstm_n, state_mlstm_m, state_gdn_S, state_gdn_conv,
           norm_mix, w_in, b_igate, b_fgate, mlstm_norm, conv_w, a_log, dt_bias, gdn_norm,
           mem_norm, w_mem_kv, w_br_mlstm, w_br_gdn, w_br_xattn, w_out, norm_ffn,
           w_router_group, b_router_group, w_router_expert, b_router_expert,
           w_exp_gate, w_exp_up, w_exp_down, final_norm, *, cfg):
    bp, sp, d = x_prompt.shape
    bs, ss, _ = x_sample.shape
    mem_len = mem_prompt.shape[1]
    ss_pad = SUBLANES
    assert ss <= ss_pad and d % LANES == 0
    n_p = bp * sp
    n_s = bs * ss_pad
    W = BRANCH_W
    gs_ = cfg["sample_group"]

    sizes = (W, W, W, N_HEADS, N_HEADS, W, 3 * W, N_HEADS, N_HEADS, W, W, d, d, d)
    offs = [0]
    for s in sizes:
        offs.append(offs[-1] + s)
    (m_q, m_k, m_v, m_i, m_f, m_o, g_qkv, g_a, g_b, g_out, x_q, gate_a, gate_b, gate_c) = [
        w_in[:, offs[j]:offs[j + 1]] for j in range(len(sizes))]
    w_big = _col_blocks(
        jnp.concatenate([m_q, m_k, m_v, m_o, g_qkv, g_out, x_q],
                        axis=1).astype(BF16), PROJ_GROUP * W)
    w_gates = jnp.concatenate([gate_a, gate_b, gate_c], axis=1).astype(BF16)
    w_small = jnp.concatenate(
        [m_i, m_f, g_a, g_b, jnp.zeros((d, LANES - 4 * N_HEADS), w_in.dtype)], axis=1).astype(BF16)
    w_router = jnp.concatenate(
        [w_router_group, w_router_expert,
         jnp.zeros((d, LANES - N_GROUPS - N_EXPERTS), w_in.dtype)], axis=1).astype(BF16)
    b_router = _lane_row(jnp.concatenate([b_router_group, b_router_expert]), 0)
    gate_bias = _lane_row(jnp.concatenate([b_igate, b_fgate]), 0)
    a_row = _lane_row(-jnp.exp(a_log.astype(F32)), 2 * N_HEADS)
    dt_row = _lane_row(dt_bias, 2 * N_HEADS)

    xp2 = x_prompt.reshape(n_p, d)
    xs2 = jnp.pad(x_sample, ((0, 0), (0, ss_pad - ss), (0, 0))).reshape(n_s, d)

    proj, gs = _norm_proj(xp2, xs2, norm_mix.reshape(1, d), w_big, w_small, tm=cfg["tm_proj"])

    mem2 = mem_prompt.reshape(bp * mem_len, d)
    mem_kv, _ = _norm_proj(mem2, None, mem_norm.reshape(1, d),
                           _col_blocks(w_mem_kv.astype(BF16), W),
                           jnp.zeros((d, LANES), BF16), tm=min(cfg["tm_proj"], bp * mem_len))

    mnorm = mlstm_norm.reshape(1, W)
    gnorm = gdn_norm.reshape(1, HEAD_DIM)
    s_off = n_p // (gs_ * ss_pad)
    ha_p, c_p, nn_p, mm_p = _mlstm(proj, gs, gate_bias, mnorm, None, n_batch=bp, seq=sp,
                                   s_valid=sp, row_off=0, L=cfg["chunk_m"], n_sub=cfg["sub_m"], G=1)
    m0 = jnp.broadcast_to(state_mlstm_m[:, :, None], (bs, N_HEADS, LANES))
    ha_s, c_s, nn_s, mm_s = _mlstm(proj, gs, gate_bias, mnorm,
                                   (state_mlstm_C, state_mlstm_n, m0), n_batch=bs, seq=ss_pad,
                                   s_valid=ss, row_off=s_off, L=ss_pad, n_sub=1, G=gs_)
    hb_p, s_p, cv_p = _gdn(proj, gs, conv_w, a_row, dt_row, gnorm, None, n_batch=bp, seq=sp,
                           s_valid=sp, row_off=0, L=cfg["chunk_g"], n_sub=cfg["sub_g"], G=1)
    conv0 = jnp.pad(state_gdn_conv, ((0, 0), (SUBLANES - (CONV_W - 1), 0), (0, 0)))
    hb_s, s_s, cv_s = _gdn(proj, gs, conv_w, a_row, dt_row, gnorm, (conv0, state_gdn_S),
                           n_batch=bs, seq=ss_pad, s_valid=ss, row_off=s_off, L=ss_pad,
                           n_sub=1, G=gs_)
    lt = min(sp, cfg["lt_x"])
    hc_p = _xattn(proj, mem_kv, mem_kv, n_batch=bp, seq=sp, row_off=0, lt=lt, G=1,
                  mem_len=mem_len, head_rows=False,
                  kv_specs=[pl.BlockSpec((None, mem_len, W), lambda b, t, c=c: (c, b, 0))
                            for c in range(2)])
    kv_rows = gs_ * mem_len * N_HEADS
    hc_s = _xattn(proj, cache_mem_k.reshape(bs * mem_len * N_HEADS, HEAD_DIM),
                  cache_mem_v.reshape(bs * mem_len * N_HEADS, HEAD_DIM),
                  n_batch=bs, seq=ss_pad, row_off=s_off, lt=ss_pad, G=gs_, mem_len=mem_len,
                  head_rows=True,
                  kv_specs=[pl.BlockSpec((kv_rows, HEAD_DIM), lambda b, t: (b, 0))] * 2)

    tm = cfg["tm"]
    x1rec, route, counts = _merge((ha_p, ha_s), (hb_p, hb_s), (hc_p, hc_s), (xp2, xs2),
                          norm_mix.reshape(1, d), w_gates, w_br_mlstm.astype(BF16), w_br_gdn.astype(BF16), w_br_xattn.astype(BF16),
                          w_out.astype(BF16), norm_ffn.reshape(1, d), w_router, b_router, tm=tm)

    tm_moe = cfg["tm_moe"]
    n_all = n_p + n_s
    n_tiles = (n_all + N_CLASSES * (tm_moe - 1)) // tm_moe + 1
    pos, ea, eb, n_used = _sort_plan(route[:, ROUTE_CLS].astype(jnp.int32),
                                     route[:, ROUTE_RANK].astype(jnp.int32),
                                     counts[0, :N_CLASSES].astype(jnp.int32),
                                     tm=tm_moe, n_tiles=n_tiles)
    tm_sort = cfg["tm_sort"]
    x_sorted = _sort_scatter(pos, x1rec, n_tiles * tm_moe, tm=tm_sort)
    y_sorted = _moe(ea, eb, n_used, x_sorted, w_exp_gate, w_exp_up, w_exp_down,
                    norm_ffn.reshape(1, d),
                    final_norm.reshape(1, d), w_router, b_router, tm=tm_moe, n_tiles=n_tiles)
    y_p = _unsort(pos, y_sorted, base=0, n_rows=n_p, tm=tm_sort)
    y_s = _unsort(pos, y_sorted, base=n_p, n_rows=n_s, tm=tm_sort)

    y_prompt = y_p.reshape(bp, sp, d)
    y_sample = y_s.reshape(bs, ss_pad, d)[:, :ss]
    mk_p = mem_kv[0].reshape(bp, mem_len, N_HEADS, HEAD_DIM)
    mv_p = mem_kv[1].reshape(bp, mem_len, N_HEADS, HEAD_DIM)
    tail = slice(SUBLANES - (CONV_W - 1), SUBLANES)
    return (y_prompt, y_sample, mk_p, mv_p,
            c_p, nn_p, mm_p[:, :, 0], s_p, cv_p[:, tail],
            c_s, nn_s, mm_s[:, :, 0], s_s, cv_s[:, tail])


CONFIG = dict(tm_proj=1024, tm=512, tm_sort=1024, tm_moe=256, chunk_m=256, sub_m=1, chunk_g=64, sub_g=8,
              lt_x=2048, sample_group=16)


def kernel(x_prompt, x_sample, mem_prompt, cache_mem_k, cache_mem_v, state_mlstm_C, state_mlstm_n, state_mlstm_m, state_gdn_S, state_gdn_conv, norm_mix, w_in, b_igate, b_fgate, mlstm_norm, conv_w, a_log, dt_bias, gdn_norm, mem_norm, w_mem_kv, w_br_mlstm, w_br_gdn, w_br_xattn, w_out, norm_ffn, w_router_group, b_router_group, w_router_expert, b_router_expert, w_exp_gate, w_exp_up, w_exp_down, final_norm):
    depth = w_in.shape[0]
    assert depth == 1, "one decoder layer"
    outs = _layer(
        x_prompt, x_sample, mem_prompt, cache_mem_k[0], cache_mem_v[0],
        state_mlstm_C[0], state_mlstm_n[0], state_mlstm_m[0], state_gdn_S[0], state_gdn_conv[0],
        norm_mix[0], w_in[0], b_igate[0], b_fgate[0], mlstm_norm[0], conv_w[0], a_log[0],
        dt_bias[0], gdn_norm[0], mem_norm[0], w_mem_kv[0], w_br_mlstm[0], w_br_gdn[0],
        w_br_xattn[0], w_out[0], norm_ffn[0], w_router_group[0], b_router_group[0],
        w_router_expert[0], b_router_expert[0], w_exp_gate[0], w_exp_up[0], w_exp_down[0],
        final_norm, cfg=CONFIG)
    y_prompt, y_sample = outs[0], outs[1]
    return (y_prompt, y_sample) + tuple(o[None] for o in outs[2:])
```
